```python
import math
import jax, jax.numpy as jnp
from jax import lax
import numpy as np

D_MODEL = 1024
BATCH = 2
SEQ = 8192
DEPTH = 2
DEC_BATCH = 2
DEC_SEQ = 16384
PAST_LEN = 128

HEAD_DIM = 64
A_HEADS = 16
A_KV_HEADS = 4
A_GROUP = A_HEADS // A_KV_HEADS
Q_BLOCK = 128
GRID_W = 64
ROPE_THETA = 10000.0
ROPE_AXIS_DIM = HEAD_DIM // 2
B_HEADS = 16
DILATED_CONFIGS = ((128, 1), (512, 4), (2048, 16))
B_GROUPS = len(DILATED_CONFIGS)
NUM_BUCKETS = 32
MAX_DISTANCE = 1024
N_EXPERTS = 32
TOP_K = 4
D_FF_EXPERT = D_MODEL
SWIGLU_ALPHA = 1.702
SWIGLU_LIMIT = 7.0
EXPERT_BLOCK = 128
RMS_EPS = 1e-6
NEG_INF = -1e30

kernel_name = 'hybrid_dilated_gqa_moe_encoder'


def _rms(x, g):
    xf = x.astype(jnp.float32)
    y = xf * lax.rsqrt(jnp.mean(xf * xf, axis=-1, keepdims=True) + RMS_EPS)
    return (y * g.astype(jnp.float32)).astype(x.dtype)


def _axial_rope_tables(seq_len):
    rows = seq_len // GRID_W
    row = jnp.repeat(jnp.arange(rows, dtype=jnp.float32), GRID_W)
    col = jnp.tile(jnp.arange(GRID_W, dtype=jnp.float32), rows)
    n_freq = ROPE_AXIS_DIM // 2
    inv_freq = 1.0 / (ROPE_THETA ** (jnp.arange(n_freq, dtype=jnp.float32) / n_freq))
    ang_r = row[:, None] * inv_freq[None, :]
    ang_c = col[:, None] * inv_freq[None, :]
    ang = jnp.concatenate([ang_r, ang_r, ang_c, ang_c], axis=-1)
    return jnp.cos(ang), jnp.sin(ang)


def _apply_axial_rope(x, cos, sin):
    xf = x.astype(jnp.float32)
    r1, r2, c1, c2 = jnp.split(xf, 4, axis=-1)
    rot = jnp.concatenate([-r2, r1, -c2, c1], axis=-1)
    return (xf * cos[None, :, None, :] + rot * sin[None, :, None, :]).astype(x.dtype)


def _t5_bucket(rel):
    nb = NUM_BUCKETS // 2
    max_exact = nb // 2
    ret = (rel > 0).astype(np.int32) * nb
    n = np.abs(rel)
    large = max_exact + (np.log(np.maximum(n, 1) / max_exact) / np.log(MAX_DISTANCE / max_exact)
                         * (nb - max_exact)).astype(np.int32)
    large = np.minimum(large, nb - 1)
    return (ret + np.where(n < max_exact, n, large)).astype(np.int32)


def _mixer_a(h, w_qkv, gq, gk, w_o):
    Bb, S, _ = h.shape
    qkv = h @ w_qkv
    nq_cols = A_HEADS * HEAD_DIM
    nk_cols = A_KV_HEADS * HEAD_DIM
    q = qkv[..., :nq_cols].reshape(Bb, S, A_HEADS, HEAD_DIM)
    k = qkv[..., nq_cols:nq_cols + nk_cols].reshape(Bb, S, A_KV_HEADS, HEAD_DIM)
    v = qkv[..., nq_cols + nk_cols:].reshape(Bb, S, A_KV_HEADS, HEAD_DIM)
    cos, sin = _axial_rope_tables(S)
    q = _apply_axial_rope(_rms(q, gq), cos, sin)
    k = _apply_axial_rope(_rms(k, gk), cos, sin)
    nq = S // Q_BLOCK
    qb = q.reshape(Bb, nq, Q_BLOCK, A_KV_HEADS, A_GROUP, HEAD_DIM).transpose(1, 0, 2, 3, 4, 5)
    scale = HEAD_DIM ** -0.5

    def block(qi):
        s = jnp.einsum('bqhge,bkhe->bhgqk', qi, k).astype(jnp.float32) * scale
        p = jax.nn.softmax(s, axis=-1)
        return jnp.einsum('bhgqk,bkhe->bqhge', p.astype(v.dtype), v)

    o = lax.map(block, qb)
    o = o.transpose(1, 0, 2, 3, 4, 5).reshape(Bb, S, A_HEADS * HEAD_DIM)
    return o @ w_o


def _dilated_group(q, k, v, bias_tab, window, dilation):
    Bb, S, H, hd = q.shape
    radius = window // (2 * dilation)
    blk = radius
    L = S // dilation
    nb = -(-L // blk)
    Lp = nb * blk

    def sub(t):
        return t.reshape(Bb, L, dilation, H, hd).transpose(0, 2, 1, 3, 4)

    qs, ks, vs = sub(q), sub(k), sub(v)
    qs = jnp.pad(qs, ((0, 0), (0, 0), (0, Lp - L), (0, 0), (0, 0))).reshape(Bb, dilation, nb, blk, H, hd)

    def windows(t):
        tp = jnp.pad(t, ((0, 0), (0, 0), (blk, Lp - L + blk), (0, 0), (0, 0)))
        tp = tp.reshape(Bb, dilation, nb + 2, blk, H, hd)
        return jnp.concatenate([tp[:, :, :-2], tp[:, :, 1:-1], tp[:, :, 2:]], axis=3)

    kw, vw = windows(ks), windows(vs)
    qq = np.arange(blk)[:, None]
    kk = np.arange(3 * blk)[None, :]
    rel = kk - blk - qq
    bucket = _t5_bucket(rel * dilation)
    key_pos = np.arange(nb)[:, None] * blk - blk + np.arange(3 * blk)[None, :]
    mask = (np.abs(rel) <= radius)[None] & ((key_pos >= 0) & (key_pos < L))[:, None, :]
    bias = bias_tab[jnp.asarray(bucket)].transpose(2, 0, 1).astype(jnp.float32)
    s = jnp.einsum('bdnqhe,bdnkhe->bdnhqk', qs, kw).astype(jnp.float32) * (hd ** -0.5) + bias
    s = jnp.where(jnp.asarray(mask)[:, None], s, NEG_INF)
    lse = jax.nn.logsumexp(s, axis=-1)
    p = jnp.exp(s - lse[..., None])
    o = jnp.einsum('bdnhqk,bdnkhe->bdnqhe', p.astype(v.dtype), vw)
    o = o.reshape(Bb, dilation, Lp, H, hd)[:, :, :L].transpose(0, 2, 1, 3, 4).reshape(Bb, S, H, hd)
    lse = lse.transpose(0, 1, 2, 4, 3).reshape(Bb, dilation, Lp, H)[:, :, :L]
    lse = lse.transpose(0, 2, 1, 3).reshape(Bb, S, H)
    return o, lse


def _mixer_b(h, w_qkv, gq, gk, w_o, rel_bias):
    Bb, S, _ = h.shape
    qkv = (h @ w_qkv).reshape(Bb, S, B_GROUPS, 3, B_HEADS, HEAD_DIM)
    outs, lses = [], []
    for g, (window, dilation) in enumerate(DILATED_CONFIGS):
        q = _rms(qkv[:, :, g, 0], gq[g])
        k = _rms(qkv[:, :, g, 1], gk[g])
        v = qkv[:, :, g, 2]
        o, lse = _dilated_group(q, k, v, rel_bias[:, g * B_HEADS:(g + 1) * B_HEADS], window, dilation)
        outs.append(o)
        lses.append(lse)
    wts = jax.nn.softmax(jnp.stack(lses), axis=0)
    o = jnp.sum(wts[..., None] * jnp.stack(outs).astype(jnp.float32), axis=0).astype(h.dtype)
    return o.reshape(Bb, S, B_HEADS * HEAD_DIM) @ w_o


def _moe(h, w_router, b_router, w_gu, b_gu, w_dn, b_dn):
    Bb, S, D = h.shape
    T = Bb * S
    xt = h.reshape(T, D)
    logits = (xt @ w_router + b_router).astype(jnp.float32)
    top_val, top_idx = lax.top_k(logits, TOP_K)
    gate = jax.nn.softmax(top_val, axis=-1).astype(h.dtype)
    TK = T * TOP_K
    flat_e = top_idx.reshape(TK)
    flat_tok = jnp.arange(TK, dtype=jnp.int32) // TOP_K
    order = jnp.argsort(flat_e)
    e_sorted = flat_e[order]
    counts = jnp.bincount(flat_e, length=N_EXPERTS)
    padded = (counts + EXPERT_BLOCK - 1) // EXPERT_BLOCK * EXPERT_BLOCK
    start = jnp.cumsum(counts) - counts
    pend = jnp.cumsum(padded)
    pstart = pend - padded
    dest = pstart[e_sorted] + jnp.arange(TK, dtype=jnp.int32) - start[e_sorted]
    n_rows = (TK + N_EXPERTS * (EXPERT_BLOCK - 1) + EXPERT_BLOCK - 1) // EXPERT_BLOCK * EXPERT_BLOCK
    n_blk = n_rows // EXPERT_BLOCK
    row_tok = jnp.full((n_rows,), T, jnp.int32).at[dest].set(flat_tok[order])
    row_gate = jnp.zeros((n_rows,), h.dtype).at[dest].set(gate.reshape(TK)[order])
    blk_start = jnp.arange(n_blk, dtype=jnp.int32) * EXPERT_BLOCK
    blk_e = jnp.minimum(jnp.sum(blk_start[:, None] >= pend[None, :], axis=1), N_EXPERTS - 1)
    x_pad = jnp.concatenate([xt, jnp.zeros((1, D), xt.dtype)], axis=0)
    xs = x_pad[row_tok].reshape(n_blk, EXPERT_BLOCK, D)

    def expert_block(args):
        xb, e = args
        gu = xb @ w_gu[e] + b_gu[e]
        g = jnp.minimum(gu[:, :D_FF_EXPERT], SWIGLU_LIMIT)
        u = jnp.clip(gu[:, D_FF_EXPERT:], -SWIGLU_LIMIT, SWIGLU_LIMIT)
        act = g * jax.nn.sigmoid(SWIGLU_ALPHA * g)
        return ((u + 1.0) * act) @ w_dn[e] + b_dn[e]

    ys = lax.map(expert_block, (xs, blk_e)).reshape(n_rows, D)
    out = jnp.zeros((T + 1, D), h.dtype).at[row_tok].add(ys * row_gate[:, None])[:T]
    return out.reshape(Bb, S, D)


def _trunk(x, c, norm_mix_g, norm_ffn_g, w_ada, b_ada, w_qkv_a, q_norm_a, k_norm_a, w_o_a,
           w_qkv_b, q_norm_b, k_norm_b, w_o_b, rel_bias, w_router, b_router,
           w_gate_up, b_gate_up, w_down, b_down):
    for i in range(DEPTH):
        mod = (jax.nn.silu(c) @ w_ada[i] + b_ada[i])[:, None, :]
        shift_m, scale_m, gate_m, shift_f, scale_f, gate_f = jnp.split(mod, 6, axis=-1)
        hm = _rms(x, norm_mix_g[i]) * (1.0 + scale_m) + shift_m
        j = i // 2
        if i % 2 == 0:
            mix = _mixer_a(hm, w_qkv_a[j], q_norm_a[j], k_norm_a[j], w_o_a[j])
        else:
            mix = _mixer_b(hm, w_qkv_b[j], q_norm_b[j], k_norm_b[j], w_o_b[j], rel_bias)
        x = x + gate_m * mix
        hf = _rms(x, norm_ffn_g[i]) * (1.0 + scale_f) + shift_f
        x = x + gate_f * _moe(hf, w_router[i], b_router[i], w_gate_up[i], b_gate_up[i],
                              w_down[i], b_down[i])
    return x


def setup_inputs(seed: int = 0) -> dict:
    key = jax.random.key(seed)
    ks = jax.random.split(key, 24)
    n_a = (DEPTH + 1) // 2
    n_b = DEPTH // 2
    D = D_MODEL
    f32 = jnp.float32
    qkv_a = (A_HEADS + 2 * A_KV_HEADS) * HEAD_DIM
    qkv_b = B_GROUPS * 3 * B_HEADS * HEAD_DIM
    nrm = lambda k, s: jax.random.normal(k, s, f32)
    return {
        'x_prompt': nrm(ks[0], (BATCH, SEQ, D)),
        'x_sample': nrm(ks[1], (DEC_BATCH, DEC_SEQ, D)),
        'c_prompt': nrm(ks[2], (BATCH, D)),
        'c_sample': nrm(ks[3], (DEC_BATCH, D)),
        'norm_mix_g': 1.0 + 0.02 * nrm(ks[4], (DEPTH, D)),
        'norm_ffn_g': 1.0 + 0.02 * nrm(ks[5], (DEPTH, D)),
        'w_ada': nrm(ks[6], (DEPTH, D, 6 * D)) * (0.5 * D ** -0.5),
        'b_ada': 0.02 * nrm(ks[7], (DEPTH, 6 * D)),
        'w_qkv_a': nrm(ks[8], (n_a, D, qkv_a)) * D ** -0.5,
        'q_norm_a': 1.0 + 0.02 * nrm(ks[9], (n_a, HEAD_DIM)),
        'k_norm_a': 1.0 + 0.02 * nrm(ks[10], (n_a, HEAD_DIM)),
        'w_o_a': nrm(ks[11], (n_a, A_HEADS * HEAD_DIM, D)) * (A_HEADS * HEAD_DIM) ** -0.5,
        'w_qkv_b': nrm(ks[12], (n_b, D, qkv_b)) * D ** -0.5,
        'q_norm_b': 1.0 + 0.02 * nrm(ks[13], (n_b, B_GROUPS, HEAD_DIM)),
        'k_norm_b': 1.0 + 0.02 * nrm(ks[14], (n_b, B_GROUPS, HEAD_DIM)),
        'w_o_b': nrm(ks[15], (n_b, B_HEADS * HEAD_DIM, D)) * (B_HEADS * HEAD_DIM) ** -0.5,
        'rel_bias': 0.5 * nrm(ks[16], (NUM_BUCKETS, B_GROUPS * B_HEADS)),
        'w_router': nrm(ks[17], (DEPTH, D, N_EXPERTS)) * D ** -0.5,
        'b_router': 0.01 * nrm(ks[18], (DEPTH, N_EXPERTS)),
        'w_gate_up': nrm(ks[19], (DEPTH, N_EXPERTS, D, 2 * D_FF_EXPERT)) * D ** -0.5,
        'b_gate_up': 0.01 * nrm(ks[20], (DEPTH, N_EXPERTS, 2 * D_FF_EXPERT)),
        'w_down': nrm(ks[21], (DEPTH, N_EXPERTS, D_FF_EXPERT, D)) * D_FF_EXPERT ** -0.5,
        'b_down': 0.01 * nrm(ks[22], (DEPTH, N_EXPERTS, D)),
    }


def reference(x_prompt, x_sample, c_prompt, c_sample, norm_mix_g, norm_ffn_g, w_ada, b_ada,
              w_qkv_a, q_norm_a, k_norm_a, w_o_a, w_qkv_b, q_norm_b, k_norm_b, w_o_b, rel_bias,
              w_router, b_router, w_gate_up, b_gate_up, w_down, b_down):
    y_prompt = _trunk(x_prompt, c_prompt, norm_mix_g, norm_ffn_g, w_ada, b_ada, w_qkv_a, q_norm_a,
                      k_norm_a, w_o_a, w_qkv_b, q_norm_b, k_norm_b, w_o_b, rel_bias, w_router,
                      b_router, w_gate_up, b_gate_up, w_down, b_down)
    y_sample = _trunk(x_sample, c_sample, norm_mix_g, norm_ffn_g, w_ada, b_ada, w_qkv_a, q_norm_a,
                      k_norm_a, w_o_a, w_qkv_b, q_norm_b, k_norm_b, w_o_b, rel_bias, w_router,
                      b_router, w_gate_up, b_gate_up, w_down, b_down)
    return (y_prompt, y_sample)
```

```python
import functools

import numpy as np
import jax
import jax.numpy as jnp
from jax import lax
from jax.experimental import pallas as pl
from jax.experimental.pallas import tpu as pltpu

D_MODEL = 1024
HEAD_DIM = 64
A_HEADS = 16
A_KV_HEADS = 4
A_GROUP = A_HEADS // A_KV_HEADS
GRID_W = 64
ROPE_THETA = 10000.0
B_HEADS = 16
DILATED_CONFIGS = ((128, 1), (512, 4), (2048, 16))
NUM_BUCKETS = 32
MAX_DISTANCE = 1024
N_EXPERTS = 32
TOP_K = 4
SWIGLU_ALPHA = 1.702
SWIGLU_LIMIT = 7.0
RMS_EPS = 1e-6
NEG_INF = -1e30
DEPTH = 2

LANES = 128
SUBLANES = 8
VMEM_LIMIT = 56 * 1024 * 1024

ROW_TILE = 512
FA_TQ = 256
FA_TK = 512
DIL_TQ = 512
DIL_SUB = 128
DIL_RADIUS = 64
ROUTE_TILE = 256
EXPERT_BLOCK = 256
ROW_WORDS = D_MODEL // LANES

_F32 = jnp.float32
_BF16 = jnp.bfloat16
_NT = (((1,), (1,)), ((), ()))


def _cparams(sem):
    return pltpu.CompilerParams(dimension_semantics=sem, vmem_limit_bytes=VMEM_LIMIT)


def _rms_mod(x, g, scale, shift):
    var = jnp.mean(x * x, axis=-1, keepdims=True)
    y = x * lax.rsqrt(var + RMS_EPS) * g
    return y * (1.0 + scale) + shift


def _head_rms(y, bd, gvec):
    ss = jnp.dot((y * y).astype(_BF16), bd, preferred_element_type=_F32)
    return y * lax.rsqrt(ss * (1.0 / HEAD_DIM) + RMS_EPS) * gvec


def _ada_kernel(c_ref, w_ref, b_ref, o_ref):
    c = c_ref[...]
    s = c * jax.nn.sigmoid(c)
    o_ref[...] = jnp.dot(s, w_ref[...], preferred_element_type=_F32,
                         precision=lax.Precision.HIGHEST) + b_ref[...]


def _ada_mod(c, w, b):
    bsz = c.shape[0]
    cp = jnp.pad(c, ((0, SUBLANES - bsz), (0, 0)))
    n = w.shape[1]
    out = pl.pallas_call(
        _ada_kernel,
        grid=(n // D_MODEL,),
        in_specs=[pl.BlockSpec((SUBLANES, D_MODEL), lambda j: (0, 0)),
                  pl.BlockSpec((D_MODEL, D_MODEL), lambda j: (0, j)),
                  pl.BlockSpec((1, D_MODEL), lambda j: (0, j))],
        out_specs=pl.BlockSpec((SUBLANES, D_MODEL), lambda j: (0, j)),
        out_shape=jax.ShapeDtypeStruct((SUBLANES, n), _F32),
        compiler_params=_cparams(("arbitrary",)),
        name="ada_mod",
    )(cp, w, b.reshape(1, n))
    return out[:bsz]


def _proj_a_kernel(x_ref, g_ref, sc_ref, sh_ref, wq_ref, wkt_ref, wv_ref, gq_ref, gk_ref,
                   cos_ref, sina_ref, sinb_ref, cost_ref, sint_ref, bd_ref,
                   q_ref, kt_ref, v_ref):
    h = _rms_mod(x_ref[0], g_ref[...], sc_ref[0], sh_ref[0]).astype(_BF16)
    bd = bd_ref[...]
    cos = cos_ref[...]
    sina = sina_ref[...]
    sinb = sinb_ref[...]
    accq = jnp.dot(h, wq_ref[...], preferred_element_type=_F32)
    for c in range(A_HEADS * HEAD_DIM // 256):
        y = _head_rms(accq[:, 256 * c:256 * (c + 1)], bd, gq_ref[:, 256 * c:256 * (c + 1)])
        for hh in range(2):
            yy = y[:, LANES * hh:LANES * (hh + 1)]
            r = (yy * cos + pltpu.roll(yy, LANES - 16, 1) * sina + pltpu.roll(yy, 16, 1) * sinb)
            q_ref[0, :, 256 * c + LANES * hh:256 * c + LANES * (hh + 1)] = r.astype(_BF16)
    acct = lax.dot_general(wkt_ref[...], h, _NT, preferred_element_type=_F32)
    cost = cost_ref[...]
    sint = sint_ref[...]
    gk = gk_ref[...]
    for j in range(A_KV_HEADS):
        slab = acct[HEAD_DIM * j:HEAD_DIM * (j + 1), :]
        ss = jnp.sum(slab * slab, axis=0, keepdims=True)
        y = slab * lax.rsqrt(ss * (1.0 / HEAD_DIM) + RMS_EPS) * gk
        rot = jnp.concatenate([-y[16:32], y[0:16], -y[48:64], y[32:48]], axis=0)
        kt_ref[0, HEAD_DIM * j:HEAD_DIM * (j + 1), :] = (y * cost + rot * sint).astype(_BF16)
    accv = jnp.dot(h, wv_ref[...], preferred_element_type=_F32)
    for j in range(A_KV_HEADS):
        v_ref[0, j] = accv[:, HEAD_DIM * j:HEAD_DIM * (j + 1)].astype(_BF16)


def _rope_tables(seq_len):
    rows = seq_len // GRID_W
    row = jnp.repeat(jnp.arange(rows, dtype=_F32), GRID_W)
    col = jnp.tile(jnp.arange(GRID_W, dtype=_F32), rows)
    n_freq = HEAD_DIM // 4
    inv_freq = 1.0 / (ROPE_THETA ** (jnp.arange(n_freq, dtype=_F32) / n_freq))
    ang_r = row[:, None] * inv_freq[None, :]
    ang_c = col[:, None] * inv_freq[None, :]
    ang = jnp.concatenate([ang_r, ang_r, ang_c, ang_c], axis=-1)
    cos, sin = jnp.cos(ang), jnp.sin(ang)
    first = (np.arange(HEAD_DIM) % 32) < 16
    cos2 = jnp.tile(cos, (1, 2))
    sina = jnp.tile(jnp.where(first[None, :], -sin, 0.0), (1, 2))
    sinb = jnp.tile(jnp.where(first[None, :], 0.0, sin), (1, 2))
    return cos2, sina, sinb, cos.T, sin.T


def _block_diag_ones():
    i = np.arange(256)
    return jnp.asarray((i[:, None] // HEAD_DIM) == (i[None, :] // HEAD_DIM), dtype=_BF16)


def _proj_a(x, g, scale, shift, wq, wkt, wv, gq_t, gk_col, tables, bd):
    bsz, seq, _ = x.shape
    tm = min(ROW_TILE, seq)
    cos2, sina, sinb, cost, sint = tables
    nq = A_HEADS * HEAD_DIM
    nk = A_KV_HEADS * HEAD_DIM
    const = lambda b, i: (0, 0)
    return pl.pallas_call(
        _proj_a_kernel,
        grid=(bsz, seq // tm),
        in_specs=[
            pl.BlockSpec((1, tm, D_MODEL), lambda b, i: (b, i, 0)),
            pl.BlockSpec((1, D_MODEL), const),
            pl.BlockSpec((1, 1, D_MODEL), lambda b, i: (b, 0, 0)),
            pl.BlockSpec((1, 1, D_MODEL), lambda b, i: (b, 0, 0)),
            pl.BlockSpec((D_MODEL, nq), const),
            pl.BlockSpec((nk, D_MODEL), const),
            pl.BlockSpec((D_MODEL, nk), const),
            pl.BlockSpec((1, nq), const),
            pl.BlockSpec((HEAD_DIM, 1), const),
            pl.BlockSpec((tm, LANES), lambda b, i: (i, 0)),
            pl.BlockSpec((tm, LANES), lambda b, i: (i, 0)),
            pl.BlockSpec((tm, LANES), lambda b, i: (i, 0)),
            pl.BlockSpec((HEAD_DIM, tm), lambda b, i: (0, i)),
            pl.BlockSpec((HEAD_DIM, tm), lambda b, i: (0, i)),
            pl.BlockSpec((256, 256), const),
        ],
        out_specs=[
            pl.BlockSpec((1, tm, nq), lambda b, i: (b, i, 0)),
            pl.BlockSpec((1, nk, tm), lambda b, i: (b, 0, i)),
            pl.BlockSpec((1, A_KV_HEADS, tm, HEAD_DIM), lambda b, i: (b, 0, i, 0)),
        ],
        out_shape=[
            jax.ShapeDtypeStruct((bsz, seq, nq), _BF16),
            jax.ShapeDtypeStruct((bsz, nk, seq), _BF16),
            jax.ShapeDtypeStruct((bsz, A_KV_HEADS, seq, HEAD_DIM), _BF16),
        ],
        compiler_params=_cparams(("parallel", "parallel")),
        name="proj_a",
    )(x, g, scale, shift, wq, wkt, wv, gq_t, gk_col, cos2, sina, sinb, cost, sint, bd)


def _flash_kernel(q_ref, kt_ref, v_ref, o_ref, q4_ref, m_ref, l_ref, acc_ref, *, tq):
    ki = pl.program_id(3)

    @pl.when(ki == 0)
    def _():
        q = q_ref[0]
        head_of_lane = lax.broadcasted_iota(jnp.int32, q.shape, 1) // HEAD_DIM
        for hq in range(A_GROUP):
            q4_ref[tq * hq:tq * (hq + 1), :] = jnp.where(head_of_lane == hq, q, jnp.zeros_like(q))
        m_ref[...] = jnp.full(m_ref.shape, -jnp.inf, _F32)
        l_ref[...] = jnp.zeros(l_ref.shape, _F32)
        acc_ref[...] = jnp.zeros(acc_ref.shape, _F32)

    kt = kt_ref[0]
    kt4 = jnp.concatenate([kt] * A_GROUP, axis=0)
    s = jnp.dot(q4_ref[...], kt4, preferred_element_type=_F32)
    m_prev = m_ref[...]
    m_new = jnp.maximum(m_prev, jnp.max(s, axis=1, keepdims=True))
    alpha = jnp.exp(m_prev - m_new)
    p = jnp.exp(s - m_new)
    l_ref[...] = alpha * l_ref[...] + jnp.sum(p, axis=1, keepdims=True)
    acc_ref[...] = alpha * acc_ref[...] + jnp.dot(p.astype(_BF16), v_ref[0, 0],
                                                  preferred_element_type=_F32)
    m_ref[...] = m_new

    @pl.when(ki == pl.num_programs(3) - 1)
    def _():
        o = (acc_ref[...] / l_ref[...]).astype(_BF16)
        r = lax.broadcasted_iota(jnp.int32, (HEAD_DIM, 256), 0)
        c = lax.broadcasted_iota(jnp.int32, (HEAD_DIM, 256), 1)
        out = jnp.zeros((tq, 256), _F32)
        for hq in range(A_GROUP):
            place = (c == r + HEAD_DIM * hq).astype(_BF16)
            out = out + jnp.dot(o[tq * hq:tq * (hq + 1)], place, preferred_element_type=_F32)
        o_ref[0] = out.astype(_BF16)


def _flash_a(q, kt, v):
    bsz, seq, nq = q.shape
    tq = min(FA_TQ, seq)
    tk = min(FA_TK, seq)
    return pl.pallas_call(
        functools.partial(_flash_kernel, tq=tq),
        grid=(bsz, A_KV_HEADS, seq // tq, seq // tk),
        in_specs=[
            pl.BlockSpec((1, tq, 256), lambda b, h, i, j: (b, i, h)),
            pl.BlockSpec((1, HEAD_DIM, tk), lambda b, h, i, j: (b, h, j)),
            pl.BlockSpec((1, 1, tk, HEAD_DIM), lambda b, h, i, j: (b, h, j, 0)),
        ],
        out_specs=pl.BlockSpec((1, tq, 256), lambda b, h, i, j: (b, i, h)),
        out_shape=jax.ShapeDtypeStruct((bsz, seq, nq), _BF16),
        scratch_shapes=[
            pltpu.VMEM((A_GROUP * tq, 256), _BF16),
            pltpu.VMEM((A_GROUP * tq, 1), _F32),
            pltpu.VMEM((A_GROUP * tq, 1), _F32),
            pltpu.VMEM((A_GROUP * tq, HEAD_DIM), _F32),
        ],
        compiler_params=_cparams(("parallel", "parallel", "parallel", "arbitrary")),
        name="flash_a",
    )(q, kt, v)


def _oproj_kernel(a_ref, w_ref, x_ref, gate_ref, o_ref):
    y = jnp.dot(a_ref[0], w_ref[...], preferred_element_type=_F32)
    o_ref[0] = x_ref[0] + gate_ref[0] * y


def _oproj(a, w, x, gate):
    bsz, seq, _ = x.shape
    tm = min(ROW_TILE, seq)
    return pl.pallas_call(
        _oproj_kernel,
        grid=(bsz, seq // tm),
        in_specs=[
            pl.BlockSpec((1, tm, D_MODEL), lambda b, i: (b, i, 0)),
            pl.BlockSpec((D_MODEL, D_MODEL), lambda b, i: (0, 0)),
            pl.BlockSpec((1, tm, D_MODEL), lambda b, i: (b, i, 0)),
            pl.BlockSpec((1, 1, D_MODEL), lambda b, i: (b, 0, 0)),
        ],
        out_specs=pl.BlockSpec((1, tm, D_MODEL), lambda b, i: (b, i, 0)),
        out_shape=jax.ShapeDtypeStruct(x.shape, _F32),
        compiler_params=_cparams(("parallel", "parallel")),
        name="oproj",
    )(a, w, x, gate)


def _proj_b_kernel(x_ref, g_ref, sc_ref, sh_ref, w_ref, gv_ref, bd_ref, o_ref, scr_ref, *, dil, tm):
    h = _rms_mod(x_ref[0], g_ref[...], sc_ref[0], sh_ref[0]).astype(_BF16)
    bd = bd_ref[...]
    acc = jnp.dot(h, w_ref[...], preferred_element_type=_F32)
    nqk = 2 * B_HEADS * HEAD_DIM
    n = acc.shape[1]

    def emit(col, y):
        if dil == 1:
            o_ref[0, 0, :, col:col + y.shape[1]] = y.astype(_BF16)
        else:
            for c in range(y.shape[1] // LANES):
                scr_ref[col // LANES + c] = y[:, LANES * c:LANES * (c + 1)]

    for c in range(nqk // 256):
        sl = slice(256 * c, 256 * (c + 1))
        emit(256 * c, _head_rms(acc[:, sl], bd, gv_ref[:, sl]))
    emit(nqk, acc[:, nqk:])
    if dil > 1:
        for r in range(dil):
            for c in range(n // LANES):
                o_ref[0, r, :, LANES * c:LANES * (c + 1)] = (
                    scr_ref[c, pl.ds(r, tm // dil, stride=dil), :].astype(_BF16))


def _proj_b(x, g, scale, shift, w, gvec, bd, dil):
    bsz, seq, _ = x.shape
    tm = min(ROW_TILE, seq)
    n = w.shape[1]
    const = lambda b, i: (0, 0)
    return pl.pallas_call(
        functools.partial(_proj_b_kernel, dil=dil, tm=tm),
        grid=(bsz, seq // tm),
        in_specs=[
            pl.BlockSpec((1, tm, D_MODEL), lambda b, i: (b, i, 0)),
            pl.BlockSpec((1, D_MODEL), const),
            pl.BlockSpec((1, 1, D_MODEL), lambda b, i: (b, 0, 0)),
            pl.BlockSpec((1, 1, D_MODEL), lambda b, i: (b, 0, 0)),
            pl.BlockSpec((D_MODEL, n), const),
            pl.BlockSpec((1, 2 * B_HEADS * HEAD_DIM), const),
            pl.BlockSpec((256, 256), const),
        ],
        out_specs=pl.BlockSpec((1, dil, tm // dil, n), lambda b, i: (b, 0, i, 0)),
        out_shape=jax.ShapeDtypeStruct((bsz, dil, seq // dil, n), _BF16),
        scratch_shapes=[pltpu.VMEM((n // LANES, tm, LANES), _F32)],
        compiler_params=_cparams(("parallel", "parallel")),
        name="proj_b",
    )(x, g, scale, shift, w, gvec, bd)


def _dilated_kernel(main_ref, kp_ref, vp_ref, kn_ref, vn_ref, bias_ref, o_ref, lse_ref,
                    kf_ref, vf_ref, *, tq, length):
    nh = B_HEADS * HEAD_DIM
    l0 = pl.program_id(2) * tq
    kf_ref[0:DIL_RADIUS] = kp_ref[0, 0]
    kf_ref[DIL_RADIUS:DIL_RADIUS + tq] = main_ref[0, 0, :, nh:2 * nh]
    kf_ref[DIL_RADIUS + tq:] = kn_ref[0, 0]
    vf_ref[0:DIL_RADIUS] = vp_ref[0, 0]
    vf_ref[DIL_RADIUS:DIL_RADIUS + tq] = main_ref[0, 0, :, 2 * nh:3 * nh]
    vf_ref[DIL_RADIUS + tq:] = vn_ref[0, 0]
    width = DIL_SUB + 2 * DIL_RADIUS
    low_half = lax.broadcasted_iota(jnp.int32, (DIL_SUB, LANES), 1) < HEAD_DIM
    col = lax.broadcasted_iota(jnp.int32, (DIL_SUB, width), 1)

    def sub_block(u, carry):
        r0 = pl.multiple_of(u * DIL_SUB, DIL_SUB)
        kpos = col + (l0 + r0 - DIL_RADIUS)
        valid = (kpos >= 0) & (kpos < length)
        for m in range(B_HEADS // 2):
            sl = slice(LANES * m, LANES * (m + 1))
            qp = main_ref[0, 0, pl.ds(r0, DIL_SUB), sl]
            kw = kf_ref[pl.ds(r0, width), sl]
            vw = vf_ref[pl.ds(r0, width), sl]
            res = []
            for hh in range(2):
                keep = low_half if hh == 0 else jnp.logical_not(low_half)
                qm = jnp.where(keep, qp, jnp.zeros_like(qp))
                s = lax.dot_general(qm, kw, _NT, preferred_element_type=_F32) + bias_ref[2 * m + hh]
                s = jnp.where(valid, s, NEG_INF)
                mx = jnp.max(s, axis=1, keepdims=True)
                p = jnp.exp(s - mx)
                den = jnp.sum(p, axis=1, keepdims=True)
                pv = jnp.dot(p.astype(_BF16), vw, preferred_element_type=_F32)
                res.append((pv / den, mx + jnp.log(den)))
            o_ref[0, 0, pl.ds(r0, DIL_SUB), sl] = jnp.where(low_half, res[0][0], res[1][0])
            lse_ref[0, 0, pl.ds(r0, DIL_SUB), sl] = jnp.where(
                low_half, jnp.broadcast_to(res[0][1], (DIL_SUB, LANES)),
                jnp.broadcast_to(res[1][1], (DIL_SUB, LANES)))
        return carry

    lax.fori_loop(0, tq // DIL_SUB, sub_block, 0)


def _dilated_attn(qkv, bias):
    bsz, dil, length, n = qkv.shape
    nh = B_HEADS * HEAD_DIM
    tq = min(DIL_TQ, length)
    nb = tq // DIL_RADIUS
    last = length // DIL_RADIUS - 1
    prev_ix = lambda i: jnp.maximum(i * nb - 1, 0)
    next_ix = lambda i: jnp.minimum((i + 1) * nb, last)
    halo = (1, 1, DIL_RADIUS, nh)
    width = DIL_SUB + 2 * DIL_RADIUS
    out = jax.ShapeDtypeStruct((bsz, dil, length, nh), _F32)
    return pl.pallas_call(
        functools.partial(_dilated_kernel, tq=tq, length=length),
        grid=(bsz, dil, length // tq),
        in_specs=[
            pl.BlockSpec((1, 1, tq, n), lambda b, r, i: (b, r, i, 0)),
            pl.BlockSpec(halo, lambda b, r, i: (b, r, prev_ix(i), 1)),
            pl.BlockSpec(halo, lambda b, r, i: (b, r, prev_ix(i), 2)),
            pl.BlockSpec(halo, lambda b, r, i: (b, r, next_ix(i), 1)),
            pl.BlockSpec(halo, lambda b, r, i: (b, r, next_ix(i), 2)),
            pl.BlockSpec((B_HEADS, DIL_SUB, width), lambda b, r, i: (0, 0, 0)),
        ],
        out_specs=[pl.BlockSpec((1, 1, tq, nh), lambda b, r, i: (b, r, i, 0)),
                   pl.BlockSpec((1, 1, tq, nh), lambda b, r, i: (b, r, i, 0))],
        out_shape=[out, out],
        scratch_shapes=[pltpu.VMEM((tq + 2 * DIL_RADIUS, nh), _BF16),
                        pltpu.VMEM((tq + 2 * DIL_RADIUS, nh), _BF16)],
        compiler_params=_cparams(("parallel", "parallel", "parallel")),
        name="dilated_attn",
    )(qkv, qkv, qkv, qkv, qkv, bias)


def _t5_bucket(rel):
    nb = NUM_BUCKETS // 2
    max_exact = nb // 2
    ret = (rel > 0).astype(np.int32) * nb
    n = np.abs(rel)
    large = max_exact + (np.log(np.maximum(n, 1) / max_exact) / np.log(MAX_DISTANCE / max_exact)
                         * (nb - max_exact)).astype(np.int32)
    large = np.minimum(large, nb - 1)
    return (ret + np.where(n < max_exact, n, large)).astype(np.int32)


def _dilated_bias(rel_bias_g, dil):
    width = DIL_SUB + 2 * DIL_RADIUS
    rel = np.arange(width)[None, :] - DIL_RADIUS - np.arange(DIL_SUB)[:, None]
    bucket = _t5_bucket(rel * dil)
    band = np.abs(rel) <= DIL_RADIUS
    bias = rel_bias_g[jnp.asarray(bucket)].transpose(2, 0, 1).astype(_F32)
    return jnp.where(jnp.asarray(band)[None], bias, NEG_INF)


def _merge_oproj_kernel(o0_ref, l0_ref, o1_ref, l1_ref, o2_ref, l2_ref, w_ref, x_ref, gate_ref,
                        out_ref, so1, sl1, so2, sl2, *, tm):
    d1 = DILATED_CONFIGS[1][1]
    d2 = DILATED_CONFIGS[2][1]
    nc = B_HEADS * HEAD_DIM // LANES

    def to_token_order(src_ref, scr_ref, dil):
        for r in range(dil):
            for c in range(nc):
                scr_ref[c, pl.ds(r, tm // dil, stride=dil), :] = (
                    src_ref[0, r, :, LANES * c:LANES * (c + 1)])
        return jnp.concatenate([scr_ref[c] for c in range(nc)], axis=1)

    o1 = to_token_order(o1_ref, so1, d1)
    l1 = to_token_order(l1_ref, sl1, d1)
    o2 = to_token_order(o2_ref, so2, d2)
    l2 = to_token_order(l2_ref, sl2, d2)
    l0 = l0_ref[0, 0]
    mx = jnp.maximum(jnp.maximum(l0, l1), l2)
    e0, e1, e2 = jnp.exp(l0 - mx), jnp.exp(l1 - mx), jnp.exp(l2 - mx)
    o = (e0 * o0_ref[0, 0] + e1 * o1 + e2 * o2) / (e0 + e1 + e2)
    y = jnp.dot(o.astype(_BF16), w_ref[...], preferred_element_type=_F32)
    out_ref[0] = x_ref[0] + gate_ref[0] * y


def _merge_oproj(outs, lses, w, x, gate):
    bsz, seq, _ = x.shape
    tm = min(ROUTE_TILE, seq)
    nh = B_HEADS * HEAD_DIM
    specs = []
    for (_, dil) in DILATED_CONFIGS:
        spec = pl.BlockSpec((1, dil, tm // dil, nh), lambda b, i: (b, 0, i, 0))
        specs += [spec, spec]
    return pl.pallas_call(
        functools.partial(_merge_oproj_kernel, tm=tm),
        grid=(bsz, seq // tm),
        in_specs=specs + [
            pl.BlockSpec((nh, D_MODEL), lambda b, i: (0, 0)),
            pl.BlockSpec((1, tm, D_MODEL), lambda b, i: (b, i, 0)),
            pl.BlockSpec((1, 1, D_MODEL), lambda b, i: (b, 0, 0)),
        ],
        out_specs=pl.BlockSpec((1, tm, D_MODEL), lambda b, i: (b, i, 0)),
        out_shape=jax.ShapeDtypeStruct(x.shape, _F32),
        scratch_shapes=[pltpu.VMEM((nh // LANES, tm, LANES), _F32)] * 4,
        compiler_params=_cparams(("parallel", "parallel")),
        name="merge_oproj",
    )(outs[0], lses[0], outs[1], lses[1], outs[2], lses[2], w, x, gate)


def _router_kernel(x_ref, g_ref, sc_ref, sh_ref, wr_ref, br_ref,
                   hf_ref, idx_ref, gate_ref, rank_ref, cnt_ref, carry_ref, tri_ref, *, tm):
    first = (pl.program_id(0) == 0) & (pl.program_id(1) == 0)

    @pl.when(first)
    def _():
        carry_ref[...] = jnp.zeros(carry_ref.shape, _F32)
        r = lax.broadcasted_iota(jnp.int32, (tm, tm), 0)
        c = lax.broadcasted_iota(jnp.int32, (tm, tm), 1)
        tri_ref[...] = (r < c).astype(_BF16)

    hf = _rms_mod(x_ref[0], g_ref[...], sc_ref[0], sh_ref[0])
    for j in range(ROW_WORDS):
        hf_ref[pl.ds(j, tm, stride=ROW_WORDS), :] = hf[:, LANES * j:LANES * (j + 1)]
    logits = lax.dot_general(wr_ref[...], hf, _NT, preferred_element_type=_F32,
                             precision=lax.Precision.HIGHEST) + br_ref[...]
    eio = lax.broadcasted_iota(jnp.int32, logits.shape, 0).astype(_F32)
    cur = logits
    vals, sel = [], []
    for _ in range(TOP_K):
        mx = jnp.max(cur, axis=0, keepdims=True)
        ix = jnp.min(jnp.where(cur == mx, eio, float(N_EXPERTS)), axis=0, keepdims=True)
        vals.append(mx)
        sel.append(ix)
        cur = jnp.where(eio == ix, -jnp.inf, cur)
    ex = [jnp.exp(v - vals[0]) for v in vals]
    den = ex[0] + ex[1] + ex[2] + ex[3]
    member = jnp.zeros(logits.shape, _F32)
    for ix in sel:
        member = member + (eio == ix).astype(_F32)
    before = carry_ref[:, 0:1] + jnp.dot(member.astype(_BF16), tri_ref[...],
                                         preferred_element_type=_F32)
    ranks = [jnp.sum(jnp.where(eio == ix, before, 0.0), axis=0, keepdims=True) for ix in sel]
    idx_ref[...] = jnp.concatenate(sel, axis=0).astype(jnp.int32)
    gate_ref[...] = jnp.concatenate([e / den for e in ex], axis=0)
    rank_ref[...] = jnp.concatenate(ranks, axis=0).astype(jnp.int32)
    carry_ref[...] = carry_ref[...] + jnp.sum(member, axis=1, keepdims=True)
    cnt_ref[...] = carry_ref[...]


def _router(x, g, scale, shift, wr_t, br):
    bsz, seq, _ = x.shape
    tm = min(ROUTE_TILE, seq)
    ns = seq // tm
    tok = bsz * seq
    const = lambda b, i: (0, 0)
    flat = lambda b, i: (0, b * ns + i)
    return pl.pallas_call(
        functools.partial(_router_kernel, tm=tm),
        grid=(bsz, ns),
        in_specs=[
            pl.BlockSpec((1, tm, D_MODEL), lambda b, i: (b, i, 0)),
            pl.BlockSpec((1, D_MODEL), const),
            pl.BlockSpec((1, 1, D_MODEL), lambda b, i: (b, 0, 0)),
            pl.BlockSpec((1, 1, D_MODEL), lambda b, i: (b, 0, 0)),
            pl.BlockSpec((N_EXPERTS, D_MODEL), const),
            pl.BlockSpec((N_EXPERTS, 1), const),
        ],
        out_specs=[
            pl.BlockSpec((tm * ROW_WORDS, LANES), lambda b, i: (b * ns + i, 0)),
            pl.BlockSpec((TOP_K, tm), flat),
            pl.BlockSpec((TOP_K, tm), flat),
            pl.BlockSpec((TOP_K, tm), flat),
            pl.BlockSpec((N_EXPERTS, LANES), const),
        ],
        out_shape=[
            jax.ShapeDtypeStruct((tok * ROW_WORDS, LANES), _F32),
            jax.ShapeDtypeStruct((TOP_K, tok), jnp.int32),
            jax.ShapeDtypeStruct((TOP_K, tok), _F32),
            jax.ShapeDtypeStruct((TOP_K, tok), jnp.int32),
            jax.ShapeDtypeStruct((N_EXPERTS, LANES), _F32),
        ],
        scratch_shapes=[pltpu.VMEM((N_EXPERTS, LANES), _F32), pltpu.VMEM((tm, tm), _BF16)],
        compiler_params=_cparams(("arbitrary", "arbitrary")),
        name="router",
    )(x, g, scale, shift, wr_t, br)


def _row_copy(src_ref, src_row, dst_ref, dst_row, sem):
    return pltpu.make_async_copy(
        src_ref.at[pl.ds(pl.multiple_of(src_row * ROW_WORDS, ROW_WORDS), ROW_WORDS), :],
        dst_ref.at[pl.ds(pl.multiple_of(dst_row * ROW_WORDS, ROW_WORDS), ROW_WORDS), :], sem)


def _dispatch_kernel(dest_hbm, hf_ref, init_ref, xs_ref, dest_smem, isem, sem, *, tm):
    del init_ref
    i = pl.program_id(0)
    n = tm * TOP_K
    cp = pltpu.make_async_copy(dest_hbm.at[pl.ds(pl.multiple_of(i * n, n), n)], dest_smem, isem)
    cp.start()
    cp.wait()

    def issue(r, carry):
        for k in range(TOP_K):
            _row_copy(hf_ref, r, xs_ref, dest_smem[r * TOP_K + k], sem).start()
        return carry

    lax.fori_loop(0, tm, issue, 0)

    def drain(r, carry):
        for k in range(TOP_K):
            _row_copy(hf_ref, 0, xs_ref, 0, sem).wait()
        return carry

    lax.fori_loop(0, tm, drain, 0)


def _dispatch(dest_flat, hf, n_rows):
    tok = hf.shape[0] // ROW_WORDS
    tm = min(ROUTE_TILE, tok)
    init = jnp.zeros((n_rows * ROW_WORDS, LANES), _F32)
    return pl.pallas_call(
        functools.partial(_dispatch_kernel, tm=tm),
        grid=(tok // tm,),
        in_specs=[
            pl.BlockSpec(memory_space=pl.ANY),
            pl.BlockSpec((tm * ROW_WORDS, LANES), lambda i: (i, 0)),
            pl.BlockSpec(memory_space=pl.ANY),
        ],
        out_specs=pl.BlockSpec(memory_space=pl.ANY),
        out_shape=jax.ShapeDtypeStruct(init.shape, _F32),
        scratch_shapes=[pltpu.SMEM((tm * TOP_K,), jnp.int32),
                        pltpu.SemaphoreType.DMA, pltpu.SemaphoreType.DMA],
        input_output_aliases={2: 0},
        compiler_params=_cparams(("arbitrary",)),
        name="moe_dispatch",
    )(dest_flat, hf, init)


def _expert_kernel(be_ref, nu_ref, xs_ref, wgu_ref, bgu_ref, wdn_ref, bdn_ref, ys_ref, *, blk):
    del be_ref
    i = pl.program_id(0)
    dff = wdn_ref.shape[1]

    @pl.when(i < nu_ref[0])
    def _():
        x = jnp.concatenate([xs_ref[pl.ds(j, blk, stride=ROW_WORDS), :] for j in range(ROW_WORDS)],
                            axis=1).astype(_BF16)
        gu = jnp.dot(x, wgu_ref[0], preferred_element_type=_F32) + bgu_ref[0]
        g = jnp.minimum(gu[:, :dff], SWIGLU_LIMIT)
        u = jnp.clip(gu[:, dff:], -SWIGLU_LIMIT, SWIGLU_LIMIT)
        act = g * jax.nn.sigmoid(SWIGLU_ALPHA * g)
        mid = ((u + 1.0) * act).astype(_BF16)
        y = jnp.dot(mid, wdn_ref[0], preferred_element_type=_F32) + bdn_ref[0]
        for j in range(ROW_WORDS):
            ys_ref[pl.ds(j, blk, stride=ROW_WORDS), :] = y[:, LANES * j:LANES * (j + 1)]

    @pl.when(i >= nu_ref[0])
    def _():
        ys_ref[...] = jnp.zeros(ys_ref.shape, _F32)


def _experts(blk_e, n_used, xs, wgu, bgu, wdn, bdn):
    blk = EXPERT_BLOCK
    n_blk = xs.shape[0] // (blk * ROW_WORDS)
    dff = wdn.shape[1]
    grid_spec = pltpu.PrefetchScalarGridSpec(
        num_scalar_prefetch=2,
        grid=(n_blk,),
        in_specs=[
            pl.BlockSpec((blk * ROW_WORDS, LANES), lambda i, be, nu: (i, 0)),
            pl.BlockSpec((1, D_MODEL, 2 * dff), lambda i, be, nu: (be[i], 0, 0)),
            pl.BlockSpec((1, 1, 2 * dff), lambda i, be, nu: (be[i], 0, 0)),
            pl.BlockSpec((1, dff, D_MODEL), lambda i, be, nu: (be[i], 0, 0)),
            pl.BlockSpec((1, 1, D_MODEL), lambda i, be, nu: (be[i], 0, 0)),
        ],
        out_specs=pl.BlockSpec((blk * ROW_WORDS, LANES), lambda i, be, nu: (i, 0)),
    )
    return pl.pallas_call(
        functools.partial(_expert_kernel, blk=blk),
        grid_spec=grid_spec,
        out_shape=jax.ShapeDtypeStruct(xs.shape, _F32),
        compiler_params=_cparams(("arbitrary",)),
        name="moe_experts",
    )(blk_e, n_used, xs, wgu, bgu, wdn, bdn)


def _combine_kernel(dest_hbm, gate_hbm, ys_ref, x_ref, gf_ref, o_ref,
                    dest_smem, gate_smem, buf_ref, mo_ref, isem, sem, *, tm, ns):
    i = pl.program_id(0) * ns + pl.program_id(1)
    n = tm * TOP_K
    off = pl.multiple_of(i * n, n)
    cp_d = pltpu.make_async_copy(dest_hbm.at[pl.ds(off, n)], dest_smem, isem.at[0])
    cp_g = pltpu.make_async_copy(gate_hbm.at[pl.ds(off, n)], gate_smem, isem.at[1])
    cp_d.start()
    cp_g.start()
    cp_d.wait()
    cp_g.wait()

    def issue(r, carry):
        for k in range(TOP_K):
            _row_copy(ys_ref, dest_smem[r * TOP_K + k], buf_ref, k * tm + r, sem).start()
        return carry

    lax.fori_loop(0, tm, issue, 0)

    def drain(r, carry):
        for k in range(TOP_K):
            _row_copy(ys_ref, 0, buf_ref, 0, sem).wait()
        return carry

    lax.fori_loop(0, tm, drain, 0)

    def mix(r, carry):
        acc = jnp.zeros((ROW_WORDS, LANES), _F32)
        for k in range(TOP_K):
            row = pl.multiple_of((k * tm + r) * ROW_WORDS, ROW_WORDS)
            acc = acc + gate_smem[r * TOP_K + k] * buf_ref[pl.ds(row, ROW_WORDS), :]
        mo_ref[pl.ds(pl.multiple_of(r * ROW_WORDS, ROW_WORDS), ROW_WORDS), :] = acc
        return carry

    lax.fori_loop(0, tm, mix, 0)
    moe = jnp.concatenate([mo_ref[pl.ds(j, tm, stride=ROW_WORDS), :] for j in range(ROW_WORDS)],
                          axis=1)
    o_ref[0] = x_ref[0] + gf_ref[0] * moe


def _combine(dest_flat, gate_flat, ys, x, gate_f):
    bsz, seq, _ = x.shape
    tm = min(ROUTE_TILE, seq)
    ns = seq // tm
    return pl.pallas_call(
        functools.partial(_combine_kernel, tm=tm, ns=ns),
        grid=(bsz, ns),
        in_specs=[
            pl.BlockSpec(memory_space=pl.ANY),
            pl.BlockSpec(memory_space=pl.ANY),
            pl.BlockSpec(memory_space=pl.ANY),
            pl.BlockSpec((1, tm, D_MODEL), lambda b, i: (b, i, 0)),
            pl.BlockSpec((1, 1, D_MODEL), lambda b, i: (b, 0, 0)),
        ],
        out_specs=pl.BlockSpec((1, tm, D_MODEL), lambda b, i: (b, i, 0)),
        out_shape=jax.ShapeDtypeStruct(x.shape, _F32),
        scratch_shapes=[
            pltpu.SMEM((tm * TOP_K,), jnp.int32),
            pltpu.SMEM((tm * TOP_K,), _F32),
            pltpu.VMEM((TOP_K * tm * ROW_WORDS, LANES), _F32),
            pltpu.VMEM((tm * ROW_WORDS, LANES), _F32),
            pltpu.SemaphoreType.DMA((2,)),
            pltpu.SemaphoreType.DMA,
        ],
        compiler_params=_cparams(("arbitrary", "arbitrary")),
        name="moe_combine",
    )(dest_flat, gate_flat, ys, x, gate_f)


def _moe(x, g, scale, shift, gate_f, wr_t, br, wgu, bgu, wdn, bdn):
    bsz, seq, _ = x.shape
    tok = bsz * seq
    blk = EXPERT_BLOCK
    hf, idx, gate, rank, cnt = _router(x, g, scale, shift, wr_t, br)
    counts = cnt[:, 0].astype(jnp.int32)
    padded = (counts + blk - 1) // blk * blk
    pend = jnp.cumsum(padded)
    pstart = pend - padded
    dest = (pstart[idx] + rank).T.reshape(tok * TOP_K)
    gate_flat = gate.T.reshape(tok * TOP_K)
    n_blk = (tok * TOP_K + N_EXPERTS * (blk - 1) + blk - 1) // blk
    blk_start = jnp.arange(n_blk, dtype=jnp.int32) * blk
    blk_e = jnp.minimum(jnp.sum(blk_start[:, None] >= pend[None, :], axis=1),
                        N_EXPERTS - 1).astype(jnp.int32)
    n_used = (pend[-1:] // blk).astype(jnp.int32)
    xs = _dispatch(dest, hf, n_blk * blk)
    ys = _experts(blk_e, n_used, xs, wgu, bgu, wdn, bdn)
    return _combine(dest, gate_flat, ys, x, gate_f)


def _prepare(norm_mix_g, norm_ffn_g, w_qkv_a, q_norm_a, k_norm_a, w_o_a, w_qkv_b, q_norm_b, k_norm_b,
             w_o_b, rel_bias, w_router, b_router, w_gate_up, b_gate_up, w_down, b_down):
    nq = A_HEADS * HEAD_DIM
    nk = A_KV_HEADS * HEAD_DIM
    nh = B_HEADS * HEAD_DIM
    scale = HEAD_DIM ** -0.5
    p = {}
    wa = w_qkv_a[0]
    p["wq_a"] = wa[:, :nq].astype(_BF16)
    p["wkt_a"] = wa[:, nq:nq + nk].T.astype(_BF16)
    p["wv_a"] = wa[:, nq + nk:].astype(_BF16)
    p["gq_a"] = (jnp.tile(q_norm_a[0], A_HEADS) * scale).reshape(1, nq)
    p["gk_a"] = k_norm_a[0].reshape(HEAD_DIM, 1)
    p["wo_a"] = w_o_a[0].astype(_BF16)
    wb = w_qkv_b[0].astype(_BF16)
    p["w_b"] = [wb[:, 3 * nh * gi:3 * nh * (gi + 1)] for gi in range(len(DILATED_CONFIGS))]
    p["gv_b"] = [jnp.concatenate([jnp.tile(q_norm_b[0, gi], B_HEADS) * scale,
                                  jnp.tile(k_norm_b[0, gi], B_HEADS)]).reshape(1, 2 * nh)
                 for gi in range(len(DILATED_CONFIGS))]
    p["bias_b"] = [_dilated_bias(rel_bias[:, gi * B_HEADS:(gi + 1) * B_HEADS], dil)
                   for gi, (_, dil) in enumerate(DILATED_CONFIGS)]
    p["wo_b"] = w_o_b[0].astype(_BF16)
    p["g_mix"] = [norm_mix_g[i].reshape(1, D_MODEL) for i in range(DEPTH)]
    p["g_ffn"] = [norm_ffn_g[i].reshape(1, D_MODEL) for i in range(DEPTH)]
    p["wr_t"] = [w_router[i].T for i in range(DEPTH)]
    p["br"] = [b_router[i].reshape(N_EXPERTS, 1) for i in range(DEPTH)]
    p["wgu"] = [w_gate_up[i].astype(_BF16) for i in range(DEPTH)]
    p["bgu"] = [b_gate_up[i].reshape(N_EXPERTS, 1, -1) for i in range(DEPTH)]
    p["wdn"] = [w_down[i].astype(_BF16) for i in range(DEPTH)]
    p["bdn"] = [b_down[i].reshape(N_EXPERTS, 1, -1) for i in range(DEPTH)]
    p["bd"] = _block_diag_ones()
    return p


def _trunk(x, c, w_ada, b_ada, p):
    bsz, seq, _ = x.shape
    tables = _rope_tables(seq)
    for i in range(DEPTH):
        mod = _ada_mod(c, w_ada[i], b_ada[i])
        shift_m, scale_m, gate_m, shift_f, scale_f, gate_f = [
            mod[:, D_MODEL * k:D_MODEL * (k + 1)].reshape(bsz, 1, D_MODEL) for k in range(6)]
        if i % 2 == 0:
            q, kt, v = _proj_a(x, p["g_mix"][i], scale_m, shift_m, p["wq_a"], p["wkt_a"], p["wv_a"],
                               p["gq_a"], p["gk_a"], tables, p["bd"])
            a = _flash_a(q, kt, v)
            x = _oproj(a, p["wo_a"], x, gate_m)
        else:
            outs, lses = [], []
            for gi, (_, dil) in enumerate(DILATED_CONFIGS):
                qkv = _proj_b(x, p["g_mix"][i], scale_m, shift_m, p["w_b"][gi], p["gv_b"][gi],
                              p["bd"], dil)
                o, lse = _dilated_attn(qkv, p["bias_b"][gi])
                outs.append(o)
                lses.append(lse)
            x = _merge_oproj(outs, lses, p["wo_b"], x, gate_m)
        x = _moe(x, p["g_ffn"][i], scale_f, shift_f, gate_f, p["wr_t"][i], p["br"][i],
                 p["wgu"][i], p["bgu"][i], p["wdn"][i], p["bdn"][i])
    return x


def kernel(x_prompt, x_sample, c_prompt, c_sample, norm_mix_g, norm_ffn_g, w_ada, b_ada, w_qkv_a,
           q_norm_a, k_norm_a, w_o_a, w_qkv_b, q_norm_b, k_norm_b, w_o_b, rel_bias, w_router,
           b_router, w_gate_up, b_gate_up, w_down, b_down):
    p = _prepare(norm_mix_g, norm_ffn_g, w_qkv_a, q_norm_a, k_norm_a, w_o_a, w_qkv_b, q_norm_b,
                 k_norm_b, w_o_b, rel_bias, w_router, b_router, w_gate_up, b_gate_up, w_down, b_down)
    y_prompt = _trunk(x_prompt, c_prompt, w_ada, b_ada, p)
    y_sample = _trunk(x_sample, c_sample, w_ada, b_ada, p)
    return (y_prompt, y_sample)
```

```python
import functools

import numpy as np
import jax
import jax.numpy as jnp
from jax import lax
from jax.experimental import pallas as pl
from jax.experimental.pallas import tpu as pltpu

D_MODEL = 1024
HEAD_DIM = 64
A_HEADS = 16
A_KV_HEADS = 4
A_GROUP = A_HEADS // A_KV_HEADS
GRID_W = 64
ROPE_THETA = 10000.0
B_HEADS = 16
DILATED_CONFIGS = ((128, 1), (512, 4), (2048, 16))
NUM_BUCKETS = 32
MAX_DISTANCE = 1024
N_EXPERTS = 32
TOP_K = 4
SWIGLU_ALPHA = 1.702
SWIGLU_LIMIT = 7.0
RMS_EPS = 1e-6
NEG_INF = -1e30
DEPTH = 2

LANES = 128
SUBLANES = 8
VMEM_LIMIT = 56 * 1024 * 1024

ROW_TILE = 512
FA_TQ = 256
FA_TK = 512
DIL_TQ = 512
DIL_SUB = 128
DIL_RADIUS = 64
ROUTE_TILE = 256
EXPERT_BLOCK = 256
ROW_WORDS = D_MODEL // LANES

_F32 = jnp.float32
_BF16 = jnp.bfloat16
_NT = (((1,), (1,)), ((), ()))


def _cparams(sem):
    return pltpu.CompilerParams(dimension_semantics=sem, vmem_limit_bytes=VMEM_LIMIT)


def _rms_mod(x, g, scale, shift):
    var = jnp.mean(x * x, axis=-1, keepdims=True)
    y = x * lax.rsqrt(var + RMS_EPS) * g
    return y * (1.0 + scale) + shift


def _head_rms(y, bd, gvec):
    ss = jnp.dot((y * y).astype(_BF16), bd, preferred_element_type=_F32)
    return y * lax.rsqrt(ss * (1.0 / HEAD_DIM) + RMS_EPS) * gvec


def _ada_kernel(c_ref, w_ref, b_ref, o_ref):
    c = c_ref[...]
    s = c * jax.nn.sigmoid(c)
    o_ref[...] = jnp.dot(s, w_ref[...], preferred_element_type=_F32,
                         precision=lax.Precision.HIGHEST) + b_ref[...]


def _ada_mod(c, w, b):
    bsz = c.shape[0]
    cp = jnp.pad(c, ((0, SUBLANES - bsz), (0, 0)))
    n = w.shape[1]
    out = pl.pallas_call(
        _ada_kernel,
        grid=(n // D_MODEL,),
        in_specs=[pl.BlockSpec((SUBLANES, D_MODEL), lambda j: (0, 0)),
                  pl.BlockSpec((D_MODEL, D_MODEL), lambda j: (0, j)),
                  pl.BlockSpec((1, D_MODEL), lambda j: (0, j))],
        out_specs=pl.BlockSpec((SUBLANES, D_MODEL), lambda j: (0, j)),
        out_shape=jax.ShapeDtypeStruct((SUBLANES, n), _F32),
        compiler_params=_cparams(("arbitrary",)),
        name="ada_mod",
    )(cp, w, b.reshape(1, n))
    return out[:bsz]


def _proj_a_kernel(x_ref, g_ref, sc_ref, sh_ref, wqt_ref, wk_ref, wvt_ref, gq_ref, gk_ref,
                   cos_ref, sina_ref, sinb_ref, cost_ref, sint_ref, bd_ref,
                   qt_ref, k_ref, vt_ref):
    h = _rms_mod(x_ref[0], g_ref[...], sc_ref[0], sh_ref[0]).astype(_BF16)
    acct = lax.dot_general(wqt_ref[...], h, _NT, preferred_element_type=_F32)
    cost = cost_ref[...]
    sint = sint_ref[...]
    gq = gq_ref[...]
    for j in range(A_HEADS):
        slab = acct[HEAD_DIM * j:HEAD_DIM * (j + 1), :]
        ss = jnp.sum(slab * slab, axis=0, keepdims=True)
        y = slab * lax.rsqrt(ss * (1.0 / HEAD_DIM) + RMS_EPS) * gq
        rot = jnp.concatenate([-y[16:32], y[0:16], -y[48:64], y[32:48]], axis=0)
        qt_ref[0, HEAD_DIM * j:HEAD_DIM * (j + 1), :] = (y * cost + rot * sint).astype(_BF16)
    acck = jnp.dot(h, wk_ref[...], preferred_element_type=_F32)
    y = _head_rms(acck, bd_ref[...], gk_ref[...])
    cos = cos_ref[...]
    sina = sina_ref[...]
    sinb = sinb_ref[...]
    for hh in range(2):
        yy = y[:, LANES * hh:LANES * (hh + 1)]
        r = yy * cos + pltpu.roll(yy, LANES - 16, 1) * sina + pltpu.roll(yy, 16, 1) * sinb
        for e in range(2):
            k_ref[0, 2 * hh + e] = r[:, HEAD_DIM * e:HEAD_DIM * (e + 1)].astype(_BF16)
    accv = lax.dot_general(wvt_ref[...], h, _NT, preferred_element_type=_F32)
    for j in range(A_KV_HEADS):
        vt_ref[0, j, 0] = accv[HEAD_DIM * j:HEAD_DIM * (j + 1), :].astype(_BF16)


def _rope_tables(seq_len):
    rows = seq_len // GRID_W
    row = jnp.repeat(jnp.arange(rows, dtype=_F32), GRID_W)
    col = jnp.tile(jnp.arange(GRID_W, dtype=_F32), rows)
    n_freq = HEAD_DIM // 4
    inv_freq = 1.0 / (ROPE_THETA ** (jnp.arange(n_freq, dtype=_F32) / n_freq))
    ang_r = row[:, None] * inv_freq[None, :]
    ang_c = col[:, None] * inv_freq[None, :]
    ang = jnp.concatenate([ang_r, ang_r, ang_c, ang_c], axis=-1)
    cos, sin = jnp.cos(ang), jnp.sin(ang)
    first = (np.arange(HEAD_DIM) % 32) < 16
    cos2 = jnp.tile(cos, (1, 2))
    sina = jnp.tile(jnp.where(first[None, :], -sin, 0.0), (1, 2))
    sinb = jnp.tile(jnp.where(first[None, :], 0.0, sin), (1, 2))
    return cos2, sina, sinb, cos.T, sin.T


def _block_diag_ones():
    i = np.arange(256)
    return jnp.asarray((i[:, None] // HEAD_DIM) == (i[None, :] // HEAD_DIM), dtype=_BF16)


def _proj_a(x, g, scale, shift, wqt, wk, wvt, gq_col, gk_t, tables, bd):
    bsz, seq, _ = x.shape
    tm = FA_TK
    assert seq % tm == 0
    cos2, sina, sinb, cost, sint = tables
    nq = A_HEADS * HEAD_DIM
    nk = A_KV_HEADS * HEAD_DIM
    const = lambda b, i: (0, 0)
    return pl.pallas_call(
        _proj_a_kernel,
        grid=(bsz, seq // tm),
        in_specs=[
            pl.BlockSpec((1, tm, D_MODEL), lambda b, i: (b, i, 0)),
            pl.BlockSpec((1, D_MODEL), const),
            pl.BlockSpec((1, 1, D_MODEL), lambda b, i: (b, 0, 0)),
            pl.BlockSpec((1, 1, D_MODEL), lambda b, i: (b, 0, 0)),
            pl.BlockSpec((nq, D_MODEL), const),
            pl.BlockSpec((D_MODEL, nk), const),
            pl.BlockSpec((nk, D_MODEL), const),
            pl.BlockSpec((HEAD_DIM, 1), const),
            pl.BlockSpec((1, nk), const),
            pl.BlockSpec((tm, LANES), lambda b, i: (i, 0)),
            pl.BlockSpec((tm, LANES), lambda b, i: (i, 0)),
            pl.BlockSpec((tm, LANES), lambda b, i: (i, 0)),
            pl.BlockSpec((HEAD_DIM, tm), lambda b, i: (0, i)),
            pl.BlockSpec((HEAD_DIM, tm), lambda b, i: (0, i)),
            pl.BlockSpec((256, 256), const),
        ],
        out_specs=[
            pl.BlockSpec((1, nq, tm), lambda b, i: (b, 0, i)),
            pl.BlockSpec((1, A_KV_HEADS, tm, HEAD_DIM), lambda b, i: (b, 0, i, 0)),
            pl.BlockSpec((1, A_KV_HEADS, 1, HEAD_DIM, tm), lambda b, i: (b, 0, i, 0, 0)),
        ],
        out_shape=[
            jax.ShapeDtypeStruct((bsz, nq, seq), _BF16),
            jax.ShapeDtypeStruct((bsz, A_KV_HEADS, seq, HEAD_DIM), _BF16),
            jax.ShapeDtypeStruct((bsz, A_KV_HEADS, seq // tm, HEAD_DIM, tm), _BF16),
        ],
        compiler_params=_cparams(("parallel", "parallel")),
        name="proj_a",
    )(x, g, scale, shift, wqt, wk, wvt, gq_col, gk_t, cos2, sina, sinb, cost, sint, bd)


def _flash_kernel(qt_ref, k_ref, vt_ref, o_ref, s_ref, m_ref, l_ref, acc_ref, *, tq, tk, nchunks):
    qt4 = jnp.concatenate([qt_ref[0, HEAD_DIM * hq:HEAD_DIM * (hq + 1), :] for hq in range(A_GROUP)],
                          axis=1)
    m_ref[...] = jnp.full(m_ref.shape, -jnp.inf, _F32)
    l_ref[...] = jnp.zeros(l_ref.shape, _F32)
    acc_ref[...] = jnp.zeros(acc_ref.shape, _F32)

    def scores(kc, slot):
        kb = k_ref[0, 0, pl.ds(pl.multiple_of(kc * tk, tk), tk), :]
        s_ref[slot] = jnp.dot(kb, qt4, preferred_element_type=_F32)

    def update(kc, slot):
        s = s_ref[slot]
        m_prev = m_ref[...]
        m_new = jnp.maximum(m_prev, jnp.max(s, axis=0, keepdims=True))
        alpha = jnp.exp2(m_prev - m_new)
        p = jnp.exp2(s - m_new)
        l_ref[...] = alpha * l_ref[...] + jnp.sum(p, axis=0, keepdims=True)
        acc_ref[...] = alpha * acc_ref[...] + jnp.dot(vt_ref[0, 0, kc], p.astype(_BF16),
                                                      preferred_element_type=_F32)
        m_ref[...] = m_new

    def pair(kc, last):
        scores(kc + 1, 1)
        update(kc, 0)
        if not last:
            scores(kc + 2, 0)
        update(kc + 1, 1)

    scores(0, 0)

    def body(c2, carry):
        pair(2 * c2, False)
        return carry

    lax.fori_loop(0, nchunks // 2 - 1, body, 0)
    pair(nchunks - 2, True)
    o = acc_ref[...] / l_ref[...]
    o = jnp.concatenate([o[:, tq * hq:tq * (hq + 1)] for hq in range(A_GROUP)], axis=0)
    o_ref[0] = o.T.astype(_BF16)


def _flash_a(qt, k, vt):
    bsz, nq, seq = qt.shape
    tq = min(FA_TQ, seq)
    tk = vt.shape[-1]
    nchunks = seq // tk
    assert nchunks % 2 == 0 and seq % tq == 0
    return pl.pallas_call(
        functools.partial(_flash_kernel, tq=tq, tk=tk, nchunks=nchunks),
        grid=(bsz, A_KV_HEADS, seq // tq),
        in_specs=[
            pl.BlockSpec((1, A_GROUP * HEAD_DIM, tq), lambda b, h, i: (b, h, i)),
            pl.BlockSpec((1, 1, seq, HEAD_DIM), lambda b, h, i: (b, h, 0, 0)),
            pl.BlockSpec((1, 1, nchunks, HEAD_DIM, tk), lambda b, h, i: (b, h, 0, 0, 0)),
        ],
        out_specs=pl.BlockSpec((1, tq, A_GROUP * HEAD_DIM), lambda b, h, i: (b, i, h)),
        out_shape=jax.ShapeDtypeStruct((bsz, seq, nq), _BF16),
        scratch_shapes=[
            pltpu.VMEM((2, tk, A_GROUP * tq), _F32),
            pltpu.VMEM((1, A_GROUP * tq), _F32),
            pltpu.VMEM((1, A_GROUP * tq), _F32),
            pltpu.VMEM((HEAD_DIM, A_GROUP * tq), _F32),
        ],
        compiler_params=_cparams(("parallel", "parallel", "parallel")),
        name="flash_a",
    )(qt, k, vt)


def _oproj_kernel(a_ref, w_ref, x_ref, gate_ref, o_ref):
    y = jnp.dot(a_ref[0], w_ref[...], preferred_element_type=_F32)
    o_ref[0] = x_ref[0] + gate_ref[0] * y


def _oproj(a, w, x, gate):
    bsz, seq, _ = x.shape
    tm = min(ROW_TILE, seq)
    return pl.pallas_call(
        _oproj_kernel,
        grid=(bsz, seq // tm),
        in_specs=[
            pl.BlockSpec((1, tm, D_MODEL), lambda b, i: (b, i, 0)),
            pl.BlockSpec((D_MODEL, D_MODEL), lambda b, i: (0, 0)),
            pl.BlockSpec((1, tm, D_MODEL), lambda b, i: (b, i, 0)),
            pl.BlockSpec((1, 1, D_MODEL), lambda b, i: (b, 0, 0)),
        ],
        out_specs=pl.BlockSpec((1, tm, D_MODEL), lambda b, i: (b, i, 0)),
        out_shape=jax.ShapeDtypeStruct(x.shape, _F32),
        compiler_params=_cparams(("parallel", "parallel")),
        name="oproj",
    )(a, w, x, gate)


def _proj_b_kernel(x_ref, g_ref, sc_ref, sh_ref, w_ref, gv_ref, bd_ref, o_ref, scr_ref, *, dil, tm):
    h = _rms_mod(x_ref[0], g_ref[...], sc_ref[0], sh_ref[0]).astype(_BF16)
    bd = bd_ref[...]
    acc = jnp.dot(h, w_ref[...], preferred_element_type=_F32)
    nqk = 2 * B_HEADS * HEAD_DIM
    n = acc.shape[1]

    def emit(col, y):
        if dil == 1:
            o_ref[0, 0, :, col:col + y.shape[1]] = y.astype(_BF16)
        else:
            for c in range(y.shape[1] // LANES):
                scr_ref[col // LANES + c] = y[:, LANES * c:LANES * (c + 1)]

    for c in range(nqk // 256):
        sl = slice(256 * c, 256 * (c + 1))
        emit(256 * c, _head_rms(acc[:, sl], bd, gv_ref[:, sl]))
    emit(nqk, acc[:, nqk:])
    if dil > 1:
        for r in range(dil):
            for c in range(n // LANES):
                o_ref[0, r, :, LANES * c:LANES * (c + 1)] = (
                    scr_ref[c, pl.ds(r, tm // dil, stride=dil), :].astype(_BF16))


def _proj_b(x, g, scale, shift, w, gvec, bd, dil):
    bsz, seq, _ = x.shape
    tm = min(ROW_TILE, seq)
    n = w.shape[1]
    const = lambda b, i: (0, 0)
    return pl.pallas_call(
        functools.partial(_proj_b_kernel, dil=dil, tm=tm),
        grid=(bsz, seq // tm),
        in_specs=[
            pl.BlockSpec((1, tm, D_MODEL), lambda b, i: (b, i, 0)),
            pl.BlockSpec((1, D_MODEL), const),
            pl.BlockSpec((1, 1, D_MODEL), lambda b, i: (b, 0, 0)),
            pl.BlockSpec((1, 1, D_MODEL), lambda b, i: (b, 0, 0)),
            pl.BlockSpec((D_MODEL, n), const),
            pl.BlockSpec((1, 2 * B_HEADS * HEAD_DIM), const),
            pl.BlockSpec((256, 256), const),
        ],
        out_specs=pl.BlockSpec((1, dil, tm // dil, n), lambda b, i: (b, 0, i, 0)),
        out_shape=jax.ShapeDtypeStruct((bsz, dil, seq // dil, n), _BF16),
        scratch_shapes=[pltpu.VMEM((n // LANES, tm, LANES), _F32)],
        compiler_params=_cparams(("parallel", "parallel")),
        name="proj_b",
    )(x, g, scale, shift, w, gvec, bd)


def _dilated_kernel(main_ref, kp_ref, vp_ref, kn_ref, vn_ref, bias_ref, o_ref, lse_ref,
                    kf_ref, vf_ref, *, tq, length):
    nh = B_HEADS * HEAD_DIM
    l0 = pl.program_id(2) * tq
    kf_ref[0:DIL_RADIUS] = kp_ref[0, 0]
    kf_ref[DIL_RADIUS:DIL_RADIUS + tq] = main_ref[0, 0, :, nh:2 * nh]
    kf_ref[DIL_RADIUS + tq:] = kn_ref[0, 0]
    vf_ref[0:DIL_RADIUS] = vp_ref[0, 0]
    vf_ref[DIL_RADIUS:DIL_RADIUS + tq] = main_ref[0, 0, :, 2 * nh:3 * nh]
    vf_ref[DIL_RADIUS + tq:] = vn_ref[0, 0]
    width = DIL_SUB + 2 * DIL_RADIUS
    low_half = lax.broadcasted_iota(jnp.int32, (DIL_SUB, LANES), 1) < HEAD_DIM
    col = lax.broadcasted_iota(jnp.int32, (DIL_SUB, width), 1)

    def sub_block(u, carry):
        r0 = pl.multiple_of(u * DIL_SUB, DIL_SUB)
        kpos = col + (l0 + r0 - DIL_RADIUS)
        valid = (kpos >= 0) & (kpos < length)
        for m in range(B_HEADS // 2):
            sl = slice(LANES * m, LANES * (m + 1))
            qp = main_ref[0, 0, pl.ds(r0, DIL_SUB), sl]
            kw = kf_ref[pl.ds(r0, width), sl]
            vw = vf_ref[pl.ds(r0, width), sl]
            res = []
            for hh in range(2):
                keep = low_half if hh == 0 else jnp.logical_not(low_half)
                qm = jnp.where(keep, qp, jnp.zeros_like(qp))
                s = lax.dot_general(qm, kw, _NT, preferred_element_type=_F32) + bias_ref[2 * m + hh]
                s = jnp.where(valid, s, NEG_INF)
                mx = jnp.max(s, axis=1, keepdims=True)
                p = jnp.exp(s - mx)
                den = jnp.sum(p, axis=1, keepdims=True)
                pv = jnp.dot(p.astype(_BF16), vw, preferred_element_type=_F32)
                res.append((pv / den, mx + jnp.log(den)))
            o_ref[0, 0, pl.ds(r0, DIL_SUB), sl] = jnp.where(low_half, res[0][0], res[1][0])
            lse_ref[0, 0, pl.ds(r0, DIL_SUB), sl] = jnp.where(
                low_half, jnp.broadcast_to(res[0][1], (DIL_SUB, LANES)),
                jnp.broadcast_to(res[1][1], (DIL_SUB, LANES)))
        return carry

    lax.fori_loop(0, tq // DIL_SUB, sub_block, 0)


def _dilated_attn(qkv, bias):
    bsz, dil, length, n = qkv.shape
    nh = B_HEADS * HEAD_DIM
    tq = min(DIL_TQ, length)
    nb = tq // DIL_RADIUS
    last = length // DIL_RADIUS - 1
    prev_ix = lambda i: jnp.maximum(i * nb - 1, 0)
    next_ix = lambda i: jnp.minimum((i + 1) * nb, last)
    halo = (1, 1, DIL_RADIUS, nh)
    width = DIL_SUB + 2 * DIL_RADIUS
    out = jax.ShapeDtypeStruct((bsz, dil, length, nh), _F32)
    return pl.pallas_call(
        functools.partial(_dilated_kernel, tq=tq, length=length),
        grid=(bsz, dil, length // tq),
        in_specs=[
            pl.BlockSpec((1, 1, tq, n), lambda b, r, i: (b, r, i, 0)),
            pl.BlockSpec(halo, lambda b, r, i: (b, r, prev_ix(i), 1)),
            pl.BlockSpec(halo, lambda b, r, i: (b, r, prev_ix(i), 2)),
            pl.BlockSpec(halo, lambda b, r, i: (b, r, next_ix(i), 1)),
            pl.BlockSpec(halo, lambda b, r, i: (b, r, next_ix(i), 2)),
            pl.BlockSpec((B_HEADS, DIL_SUB, width), lambda b, r, i: (0, 0, 0)),
        ],
        out_specs=[pl.BlockSpec((1, 1, tq, nh), lambda b, r, i: (b, r, i, 0)),
                   pl.BlockSpec((1, 1, tq, nh), lambda b, r, i: (b, r, i, 0))],
        out_shape=[out, out],
        scratch_shapes=[pltpu.VMEM((tq + 2 * DIL_RADIUS, nh), _BF16),
                        pltpu.VMEM((tq + 2 * DIL_RADIUS, nh), _BF16)],
        compiler_params=_cparams(("parallel", "parallel", "parallel")),
        name="dilated_attn",
    )(qkv, qkv, qkv, qkv, qkv, bias)


def _t5_bucket(rel):
    nb = NUM_BUCKETS // 2
    max_exact = nb // 2
    ret = (rel > 0).astype(np.int32) * nb
    n = np.abs(rel)
    large = max_exact + (np.log(np.maximum(n, 1) / max_exact) / np.log(MAX_DISTANCE / max_exact)
                         * (nb - max_exact)).astype(np.int32)
    large = np.minimum(large, nb - 1)
    return (ret + np.where(n < max_exact, n, large)).astype(np.int32)


def _dilated_bias(rel_bias_g, dil):
    width = DIL_SUB + 2 * DIL_RADIUS
    rel = np.arange(width)[None, :] - DIL_RADIUS - np.arange(DIL_SUB)[:, None]
    bucket = _t5_bucket(rel * dil)
    band = np.abs(rel) <= DIL_RADIUS
    bias = rel_bias_g[jnp.asarray(bucket)].transpose(2, 0, 1).astype(_F32)
    return jnp.where(jnp.asarray(band)[None], bias, NEG_INF)


def _merge_oproj_kernel(o0_ref, l0_ref, o1_ref, l1_ref, o2_ref, l2_ref, w_ref, x_ref, gate_ref,
                        out_ref, so1, sl1, so2, sl2, *, tm):
    d1 = DILATED_CONFIGS[1][1]
    d2 = DILATED_CONFIGS[2][1]
    nc = B_HEADS * HEAD_DIM // LANES

    def to_token_order(src_ref, scr_ref, dil):
        for r in range(dil):
            for c in range(nc):
                scr_ref[c, pl.ds(r, tm // dil, stride=dil), :] = (
                    src_ref[0, r, :, LANES * c:LANES * (c + 1)])
        return jnp.concatenate([scr_ref[c] for c in range(nc)], axis=1)

    o1 = to_token_order(o1_ref, so1, d1)
    l1 = to_token_order(l1_ref, sl1, d1)
    o2 = to_token_order(o2_ref, so2, d2)
    l2 = to_token_order(l2_ref, sl2, d2)
    l0 = l0_ref[0, 0]
    mx = jnp.maximum(jnp.maximum(l0, l1), l2)
    e0, e1, e2 = jnp.exp(l0 - mx), jnp.exp(l1 - mx), jnp.exp(l2 - mx)
    o = (e0 * o0_ref[0, 0] + e1 * o1 + e2 * o2) / (e0 + e1 + e2)
    y = jnp.dot(o.astype(_BF16), w_ref[...], preferred_element_type=_F32)
    out_ref[0] = x_ref[0] + gate_ref[0] * y


def _merge_oproj(outs, lses, w, x, gate):
    bsz, seq, _ = x.shape
    tm = min(ROUTE_TILE, seq)
    nh = B_HEADS * HEAD_DIM
    specs = []
    for (_, dil) in DILATED_CONFIGS:
        spec = pl.BlockSpec((1, dil, tm // dil, nh), lambda b, i: (b, 0, i, 0))
        specs += [spec, spec]
    return pl.pallas_call(
        functools.partial(_merge_oproj_kernel, tm=tm),
        grid=(bsz, seq // tm),
        in_specs=specs + [
            pl.BlockSpec((nh, D_MODEL), lambda b, i: (0, 0)),
            pl.BlockSpec((1, tm, D_MODEL), lambda b, i: (b, i, 0)),
            pl.BlockSpec((1, 1, D_MODEL), lambda b, i: (b, 0, 0)),
        ],
        out_specs=pl.BlockSpec((1, tm, D_MODEL), lambda b, i: (b, i, 0)),
        out_shape=jax.ShapeDtypeStruct(x.shape, _F32),
        scratch_shapes=[pltpu.VMEM((nh // LANES, tm, LANES), _F32)] * 4,
        compiler_params=_cparams(("parallel", "parallel")),
        name="merge_oproj",
    )(outs[0], lses[0], outs[1], lses[1], outs[2], lses[2], w, x, gate)


def _router_kernel(x_ref, g_ref, sc_ref, sh_ref, wr_ref, br_ref,
                   hf_ref, idx_ref, gate_ref, rank_ref, cnt_ref, carry_ref, tri_ref, *, tm):
    first = (pl.program_id(0) == 0) & (pl.program_id(1) == 0)

    @pl.when(first)
    def _():
        carry_ref[...] = jnp.zeros(carry_ref.shape, _F32)
        r = lax.broadcasted_iota(jnp.int32, (tm, tm), 0)
        c = lax.broadcasted_iota(jnp.int32, (tm, tm), 1)
        tri_ref[...] = (r < c).astype(_BF16)

    hf = _rms_mod(x_ref[0], g_ref[...], sc_ref[0], sh_ref[0])
    for j in range(ROW_WORDS):
        hf_ref[pl.ds(j, tm, stride=ROW_WORDS), :] = hf[:, LANES * j:LANES * (j + 1)]
    logits = lax.dot_general(wr_ref[...], hf, _NT, preferred_element_type=_F32,
                             precision=lax.Precision.HIGHEST) + br_ref[...]
    eio = lax.broadcasted_iota(jnp.int32, logits.shape, 0).astype(_F32)
    cur = logits
    vals, sel = [], []
    for _ in range(TOP_K):
        mx = jnp.max(cur, axis=0, keepdims=True)
        ix = jnp.min(jnp.where(cur == mx, eio, float(N_EXPERTS)), axis=0, keepdims=True)
        vals.append(mx)
        sel.append(ix)
        cur = jnp.where(eio == ix, -jnp.inf, cur)
    ex = [jnp.exp(v - vals[0]) for v in vals]
    den = ex[0] + ex[1] + ex[2] + ex[3]
    member = jnp.zeros(logits.shape, _F32)
    for ix in sel:
        member = member + (eio == ix).astype(_F32)
    before = carry_ref[:, 0:1] + jnp.dot(member.astype(_BF16), tri_ref[...],
                                         preferred_element_type=_F32)
    ranks = [jnp.sum(jnp.where(eio == ix, before, 0.0), axis=0, keepdims=True) for ix in sel]
    idx_ref[...] = jnp.concatenate(sel, axis=0).astype(jnp.int32)
    gate_ref[...] = jnp.concatenate([e / den for e in ex], axis=0)
    rank_ref[...] = jnp.concatenate(ranks, axis=0).astype(jnp.int32)
    carry_ref[...] = carry_ref[...] + jnp.sum(member, axis=1, keepdims=True)
    cnt_ref[...] = carry_ref[...]


def _router(x, g, scale, shift, wr_t, br):
    bsz, seq, _ = x.shape
    tm = min(ROUTE_TILE, seq)
    ns = seq // tm
    tok = bsz * seq
    const = lambda b, i: (0, 0)
    flat = lambda b, i: (0, b * ns + i)
    return pl.pallas_call(
        functools.partial(_router_kernel, tm=tm),
        grid=(bsz, ns),
        in_specs=[
            pl.BlockSpec((1, tm, D_MODEL), lambda b, i: (b, i, 0)),
            pl.BlockSpec((1, D_MODEL), const),
            pl.BlockSpec((1, 1, D_MODEL), lambda b, i: (b, 0, 0)),
            pl.BlockSpec((1, 1, D_MODEL), lambda b, i: (b, 0, 0)),
            pl.BlockSpec((N_EXPERTS, D_MODEL), const),
            pl.BlockSpec((N_EXPERTS, 1), const),
        ],
        out_specs=[
            pl.BlockSpec((tm * ROW_WORDS, LANES), lambda b, i: (b * ns + i, 0)),
            pl.BlockSpec((TOP_K, tm), flat),
            pl.BlockSpec((TOP_K, tm), flat),
            pl.BlockSpec((TOP_K, tm), flat),
            pl.BlockSpec((N_EXPERTS, LANES), const),
        ],
        out_shape=[
            jax.ShapeDtypeStruct((tok * ROW_WORDS, LANES), _F32),
            jax.ShapeDtypeStruct((TOP_K, tok), jnp.int32),
            jax.ShapeDtypeStruct((TOP_K, tok), _F32),
            jax.ShapeDtypeStruct((TOP_K, tok), jnp.int32),
            jax.ShapeDtypeStruct((N_EXPERTS, LANES), _F32),
        ],
        scratch_shapes=[pltpu.VMEM((N_EXPERTS, LANES), _F32), pltpu.VMEM((tm, tm), _BF16)],
        compiler_params=_cparams(("arbitrary", "arbitrary")),
        name="router",
    )(x, g, scale, shift, wr_t, br)


def _row_copy(src_ref, src_row, dst_ref, dst_row, sem):
    return pltpu.make_async_copy(
        src_ref.at[pl.ds(pl.multiple_of(src_row * ROW_WORDS, ROW_WORDS), ROW_WORDS), :],
        dst_ref.at[pl.ds(pl.multiple_of(dst_row * ROW_WORDS, ROW_WORDS), ROW_WORDS), :], sem)


def _dispatch_kernel(dest_hbm, hf_ref, init_ref, xs_ref, dest_smem, isem, sem, *, tm):
    del init_ref
    i = pl.program_id(0)
    n = tm * TOP_K
    cp = pltpu.make_async_copy(dest_hbm.at[pl.ds(pl.multiple_of(i * n, n), n)], dest_smem, isem)
    cp.start()
    cp.wait()

    def issue(r, carry):
        for k in range(TOP_K):
            _row_copy(hf_ref, r, xs_ref, dest_smem[r * TOP_K + k], sem).start()
        return carry

    lax.fori_loop(0, tm, issue, 0)

    def drain(r, carry):
        for k in range(TOP_K):
            _row_copy(hf_ref, 0, xs_ref, 0, sem).wait()
        return carry

    lax.fori_loop(0, tm, drain, 0)


def _dispatch(dest_flat, hf, n_rows):
    tok = hf.shape[0] // ROW_WORDS
    tm = min(ROUTE_TILE, tok)
    init = jnp.zeros((n_rows * ROW_WORDS, LANES), _F32)
    return pl.pallas_call(
        functools.partial(_dispatch_kernel, tm=tm),
        grid=(tok // tm,),
        in_specs=[
            pl.BlockSpec(memory_space=pl.ANY),
            pl.BlockSpec((tm * ROW_WORDS, LANES), lambda i: (i, 0)),
            pl.BlockSpec(memory_space=pl.ANY),
        ],
        out_specs=pl.BlockSpec(memory_space=pl.ANY),
        out_shape=jax.ShapeDtypeStruct(init.shape, _F32),
        scratch_shapes=[pltpu.SMEM((tm * TOP_K,), jnp.int32),
                        pltpu.SemaphoreType.DMA, pltpu.SemaphoreType.DMA],
        input_output_aliases={2: 0},
        compiler_params=_cparams(("arbitrary",)),
        name="moe_dispatch",
    )(dest_flat, hf, init)


def _expert_kernel(be_ref, nu_ref, xs_ref, wgu_ref, bgu_ref, wdn_ref, bdn_ref, ys_ref, *, blk):
    del be_ref
    i = pl.program_id(0)
    dff = wdn_ref.shape[1]

    @pl.when(i < nu_ref[0])
    def _():
        x = jnp.concatenate([xs_ref[pl.ds(j, blk, stride=ROW_WORDS), :] for j in range(ROW_WORDS)],
                            axis=1).astype(_BF16)
        gu = jnp.dot(x, wgu_ref[0], preferred_element_type=_F32) + bgu_ref[0]
        g = jnp.minimum(gu[:, :dff], SWIGLU_LIMIT)
        u = jnp.clip(gu[:, dff:], -SWIGLU_LIMIT, SWIGLU_LIMIT)
        act = g * jax.nn.sigmoid(SWIGLU_ALPHA * g)
        mid = ((u + 1.0) * act).astype(_BF16)
        y = jnp.dot(mid, wdn_ref[0], preferred_element_type=_F32) + bdn_ref[0]
        for j in range(ROW_WORDS):
            ys_ref[pl.ds(j, blk, stride=ROW_WORDS), :] = y[:, LANES * j:LANES * (j + 1)]

    @pl.when(i >= nu_ref[0])
    def _():
        ys_ref[...] = jnp.zeros(ys_ref.shape, _F32)


def _experts(blk_e, n_used, xs, wgu, bgu, wdn, bdn):
    blk = EXPERT_BLOCK
    n_blk = xs.shape[0] // (blk * ROW_WORDS)
    dff = wdn.shape[1]
    grid_spec = pltpu.PrefetchScalarGridSpec(
        num_scalar_prefetch=2,
        grid=(n_blk,),
        in_specs=[
            pl.BlockSpec((blk * ROW_WORDS, LANES), lambda i, be, nu: (i, 0)),
            pl.BlockSpec((1, D_MODEL, 2 * dff), lambda i, be, nu: (be[i], 0, 0)),
            pl.BlockSpec((1, 1, 2 * dff), lambda i, be, nu: (be[i], 0, 0)),
            pl.BlockSpec((1, dff, D_MODEL), lambda i, be, nu: (be[i], 0, 0)),
            pl.BlockSpec((1, 1, D_MODEL), lambda i, be, nu: (be[i], 0, 0)),
        ],
        out_specs=pl.BlockSpec((blk * ROW_WORDS, LANES), lambda i, be, nu: (i, 0)),
    )
    return pl.pallas_call(
        functools.partial(_expert_kernel, blk=blk),
        grid_spec=grid_spec,
        out_shape=jax.ShapeDtypeStruct(xs.shape, _F32),
        compiler_params=_cparams(("arbitrary",)),
        name="moe_experts",
    )(blk_e, n_used, xs, wgu, bgu, wdn, bdn)


def _combine_kernel(dest_hbm, gate_hbm, ys_ref, x_ref, gf_ref, o_ref,
                    dest_smem, gate_smem, buf_ref, mo_ref, isem, sem, *, tm, ns):
    i = pl.program_id(0) * ns + pl.program_id(1)
    n = tm * TOP_K
    off = pl.multiple_of(i * n, n)
    cp_d = pltpu.make_async_copy(dest_hbm.at[pl.ds(off, n)], dest_smem, isem.at[0])
    cp_g = pltpu.make_async_copy(gate_hbm.at[pl.ds(off, n)], gate_smem, isem.at[1])
    cp_d.start()
    cp_g.start()
    cp_d.wait()
    cp_g.wait()

    def issue(r, carry):
        for k in range(TOP_K):
            _row_copy(ys_ref, dest_smem[r * TOP_K + k], buf_ref, k * tm + r, sem).start()
        return carry

    lax.fori_loop(0, tm, issue, 0)

    def drain(r, carry):
        for k in range(TOP_K):
            _row_copy(ys_ref, 0, buf_ref, 0, sem).wait()
        return carry

    lax.fori_loop(0, tm, drain, 0)

    def mix(r, carry):
        acc = jnp.zeros((ROW_WORDS, LANES), _F32)
        for k in range(TOP_K):
            row = pl.multiple_of((k * tm + r) * ROW_WORDS, ROW_WORDS)
            acc = acc + gate_smem[r * TOP_K + k] * buf_ref[pl.ds(row, ROW_WORDS), :]
        mo_ref[pl.ds(pl.multiple_of(r * ROW_WORDS, ROW_WORDS), ROW_WORDS), :] = acc
        return carry

    lax.fori_loop(0, tm, mix, 0)
    moe = jnp.concatenate([mo_ref[pl.ds(j, tm, stride=ROW_WORDS), :] for j in range(ROW_WORDS)],
                          axis=1)
    o_ref[0] = x_ref[0] + gf_ref[0] * moe


def _combine(dest_flat, gate_flat, ys, x, gate_f):
    bsz, seq, _ = x.shape
    tm = min(ROUTE_TILE, seq)
    ns = seq // tm
    return pl.pallas_call(
        functools.partial(_combine_kernel, tm=tm, ns=ns),
        grid=(bsz, ns),
        in_specs=[
            pl.BlockSpec(memory_space=pl.ANY),
            pl.BlockSpec(memory_space=pl.ANY),
            pl.BlockSpec(memory_space=pl.ANY),
            pl.BlockSpec((1, tm, D_MODEL), lambda b, i: (b, i, 0)),
            pl.BlockSpec((1, 1, D_MODEL), lambda b, i: (b, 0, 0)),
        ],
        out_specs=pl.BlockSpec((1, tm, D_MODEL), lambda b, i: (b, i, 0)),
        out_shape=jax.ShapeDtypeStruct(x.shape, _F32),
        scratch_shapes=[
            pltpu.SMEM((tm * TOP_K,), jnp.int32),
            pltpu.SMEM((tm * TOP_K,), _F32),
            pltpu.VMEM((TOP_K * tm * ROW_WORDS, LANES), _F32),
            pltpu.VMEM((tm * ROW_WORDS, LANES), _F32),
            pltpu.SemaphoreType.DMA((2,)),
            pltpu.SemaphoreType.DMA,
        ],
        compiler_params=_cparams(("arbitrary", "arbitrary")),
        name="moe_combine",
    )(dest_flat, gate_flat, ys, x, gate_f)


def _moe(x, g, scale, shift, gate_f, wr_t, br, wgu, bgu, wdn, bdn):
    bsz, seq, _ = x.shape
    tok = bsz * seq
    blk = EXPERT_BLOCK
    hf, idx, gate, rank, cnt = _router(x, g, scale, shift, wr_t, br)
    counts = cnt[:, 0].astype(jnp.int32)
    padded = (counts + blk - 1) // blk * blk
    pend = jnp.cumsum(padded)
    pstart = pend - padded
    eids = jnp.arange(N_EXPERTS, dtype=jnp.int32)[:, None, None]
    base = jnp.sum(jnp.where(idx[None] == eids, pstart[:, None, None], 0), axis=0)
    dest = (base + rank).T.reshape(tok * TOP_K)
    gate_flat = gate.T.reshape(tok * TOP_K)
    n_blk = (tok * TOP_K + N_EXPERTS * (blk - 1) + blk - 1) // blk
    blk_start = jnp.arange(n_blk, dtype=jnp.int32) * blk
    blk_e = jnp.minimum(jnp.sum(blk_start[:, None] >= pend[None, :], axis=1),
                        N_EXPERTS - 1).astype(jnp.int32)
    n_used = (pend[-1:] // blk).astype(jnp.int32)
    xs = _dispatch(dest, hf, n_blk * blk)
    ys = _experts(blk_e, n_used, xs, wgu, bgu, wdn, bdn)
    return _combine(dest, gate_flat, ys, x, gate_f)


def _prepare(norm_mix_g, norm_ffn_g, w_qkv_a, q_norm_a, k_norm_a, w_o_a, w_qkv_b, q_norm_b, k_norm_b,
             w_o_b, rel_bias, w_router, b_router, w_gate_up, b_gate_up, w_down, b_down):
    nq = A_HEADS * HEAD_DIM
    nk = A_KV_HEADS * HEAD_DIM
    nh = B_HEADS * HEAD_DIM
    scale = HEAD_DIM ** -0.5
    p = {}
    wa = w_qkv_a[0]
    p["wqt_a"] = wa[:, :nq].T.astype(_BF16)
    p["wk_a"] = wa[:, nq:nq + nk].astype(_BF16)
    p["wvt_a"] = wa[:, nq + nk:].T.astype(_BF16)
    p["gq_a"] = (q_norm_a[0] * (scale * np.log2(np.e))).reshape(HEAD_DIM, 1)
    p["gk_a"] = jnp.tile(k_norm_a[0], A_KV_HEADS).reshape(1, nk)
    p["wo_a"] = w_o_a[0].astype(_BF16)
    wb = w_qkv_b[0].astype(_BF16)
    p["w_b"] = [wb[:, 3 * nh * gi:3 * nh * (gi + 1)] for gi in range(len(DILATED_CONFIGS))]
    p["gv_b"] = [jnp.concatenate([jnp.tile(q_norm_b[0, gi], B_HEADS) * scale,
                                  jnp.tile(k_norm_b[0, gi], B_HEADS)]).reshape(1, 2 * nh)
                 for gi in range(len(DILATED_CONFIGS))]
    p["bias_b"] = [_dilated_bias(rel_bias[:, gi * B_HEADS:(gi + 1) * B_HEADS], dil)
                   for gi, (_, dil) in enumerate(DILATED_CONFIGS)]
    p["wo_b"] = w_o_b[0].astype(_BF16)
    p["g_mix"] = [norm_mix_g[i].reshape(1, D_MODEL) for i in range(DEPTH)]
    p["g_ffn"] = [norm_ffn_g[i].reshape(1, D_MODEL) for i in range(DEPTH)]
    p["wr_t"] = [w_router[i].T for i in range(DEPTH)]
    p["br"] = [b_router[i].reshape(N_EXPERTS, 1) for i in range(DEPTH)]
    p["wgu"] = [w_gate_up[i].astype(_BF16) for i in range(DEPTH)]
    p["bgu"] = [b_gate_up[i].reshape(N_EXPERTS, 1, -1) for i in range(DEPTH)]
    p["wdn"] = [w_down[i].astype(_BF16) for i in range(DEPTH)]
    p["bdn"] = [b_down[i].reshape(N_EXPERTS, 1, -1) for i in range(DEPTH)]
    p["bd"] = _block_diag_ones()
    return p


def _trunk(x, c, w_ada, b_ada, p):
    bsz, seq, _ = x.shape
    tables = _rope_tables(seq)
    for i in range(DEPTH):
        mod = _ada_mod(c, w_ada[i], b_ada[i])
        shift_m, scale_m, gate_m, shift_f, scale_f, gate_f = [
            mod[:, D_MODEL * k:D_MODEL * (k + 1)].reshape(bsz, 1, D_MODEL) for k in range(6)]
        if i % 2 == 0:
            qt, k, vt = _proj_a(x, p["g_mix"][i], scale_m, shift_m, p["wqt_a"], p["wk_a"],
                                p["wvt_a"], p["gq_a"], p["gk_a"], tables, p["bd"])
            a = _flash_a(qt, k, vt)
            x = _oproj(a, p["wo_a"], x, gate_m)
        else:
            outs, lses = [], []
            for gi, (_, dil) in enumerate(DILATED_CONFIGS):
                qkv = _proj_b(x, p["g_mix"][i], scale_m, shift_m, p["w_b"][gi], p["gv_b"][gi],
                              p["bd"], dil)
                o, lse = _dilated_attn(qkv, p["bias_b"][gi])
                outs.append(o)
                lses.append(lse)
            x = _merge_oproj(outs, lses, p["wo_b"], x, gate_m)
        x = _moe(x, p["g_ffn"][i], scale_f, shift_f, gate_f, p["wr_t"][i], p["br"][i],
                 p["wgu"][i], p["bgu"][i], p["wdn"][i], p["bdn"][i])
    return x


def kernel(x_prompt, x_sample, c_prompt, c_sample, norm_mix_g, norm_ffn_g, w_ada, b_ada, w_qkv_a,
           q_norm_a, k_norm_a, w_o_a, w_qkv_b, q_norm_b, k_norm_b, w_o_b, rel_bias, w_router,
           b_router, w_gate_up, b_gate_up, w_down, b_down):
    p = _prepare(norm_mix_g, norm_ffn_g, w_qkv_a, q_norm_a, k_norm_a, w_o_a, w_qkv_b, q_norm_b,
                 k_norm_b, w_o_b, rel_bias, w_router, b_router, w_gate_up, b_gate_up, w_down, b_down)
    y_prompt = _trunk(x_prompt, c_prompt, w_ada, b_ada, p)
    y_sample = _trunk(x_sample, c_sample, w_ada, b_ada, p)
    return (y_prompt, y_sample)
```

```python
import functools

import numpy as np
import jax
import jax.numpy as jnp
from jax import lax
from jax.experimental import pallas as pl
from jax.experimental.pallas import tpu as pltpu

D_MODEL = 1024
HEAD_DIM = 64
A_HEADS = 16
A_KV_HEADS = 4
A_GROUP = A_HEADS // A_KV_HEADS
GRID_W = 64
ROPE_THETA = 10000.0
B_HEADS = 16
DILATED_CONFIGS = ((128, 1), (512, 4), (2048, 16))
NUM_BUCKETS = 32
MAX_DISTANCE = 1024
N_EXPERTS = 32
TOP_K = 4
SWIGLU_ALPHA = 1.702
SWIGLU_LIMIT = 7.0
RMS_EPS = 1e-6
NEG_INF = -1e30
DEPTH = 2

LANES = 128
SUBLANES = 8
VMEM_LIMIT = 56 * 1024 * 1024

ROW_TILE = 512
FA_TQ = 256
FA_TK = 512
V_ROWS = HEAD_DIM + 16
DIL_TQ = 512
DIL_SUB = 128
DIL_RADIUS = 64
ROUTE_TILE = 256
DISPATCH_TILE = 256
EXPERT_BLOCK = 256
ROW_WORDS = D_MODEL // LANES

_F32 = jnp.float32
_BF16 = jnp.bfloat16
_NT = (((1,), (1,)), ((), ()))


def _cparams(sem):
    return pltpu.CompilerParams(dimension_semantics=sem, vmem_limit_bytes=VMEM_LIMIT)


def _rms_mod(x, g, scale, shift):
    var = jnp.mean(x * x, axis=-1, keepdims=True)
    y = x * lax.rsqrt(var + RMS_EPS) * g
    return y * (1.0 + scale) + shift


def _head_rms(y, bd, gvec):
    ss = jnp.dot((y * y).astype(_BF16), bd, preferred_element_type=_F32)
    return y * lax.rsqrt(ss * (1.0 / HEAD_DIM) + RMS_EPS) * gvec


def _ada_kernel(c_ref, w_ref, b_ref, o_ref):
    c = c_ref[...]
    s = c * jax.nn.sigmoid(c)
    o_ref[...] = jnp.dot(s, w_ref[...], preferred_element_type=_F32,
                         precision=lax.Precision.HIGHEST) + b_ref[...]


def _ada_mod(c, w, b):
    bsz = c.shape[0]
    cp = jnp.pad(c, ((0, SUBLANES - bsz), (0, 0)))
    n = w.shape[1]
    out = pl.pallas_call(
        _ada_kernel,
        grid=(n // D_MODEL,),
        in_specs=[pl.BlockSpec((SUBLANES, D_MODEL), lambda j: (0, 0)),
                  pl.BlockSpec((D_MODEL, D_MODEL), lambda j: (0, j)),
                  pl.BlockSpec((1, D_MODEL), lambda j: (0, j))],
        out_specs=pl.BlockSpec((SUBLANES, D_MODEL), lambda j: (0, j)),
        out_shape=jax.ShapeDtypeStruct((SUBLANES, n), _F32),
        compiler_params=_cparams(("arbitrary",)),
        name="ada_mod",
    )(cp, w, b.reshape(1, n))
    return out[:bsz]


def _proj_a_kernel(x_ref, g_ref, sc_ref, sh_ref, wqt_ref, wk_ref, wvt_ref, gq_ref, gk_ref,
                   cos_ref, sina_ref, sinb_ref, cost_ref, sint_ref, bd_ref,
                   qt_ref, k_ref, vt_ref):
    h = _rms_mod(x_ref[0], g_ref[...], sc_ref[0], sh_ref[0]).astype(_BF16)
    acct = lax.dot_general(wqt_ref[...], h, _NT, preferred_element_type=_F32)
    cost = cost_ref[...]
    sint = sint_ref[...]
    gq = gq_ref[...]
    for j in range(A_HEADS):
        slab = acct[HEAD_DIM * j:HEAD_DIM * (j + 1), :]
        ss = jnp.sum(slab * slab, axis=0, keepdims=True)
        y = slab * lax.rsqrt(ss * (1.0 / HEAD_DIM) + RMS_EPS) * gq
        rot = jnp.concatenate([-y[16:32], y[0:16], -y[48:64], y[32:48]], axis=0)
        qt_ref[0, HEAD_DIM * j:HEAD_DIM * (j + 1), :] = (y * cost + rot * sint).astype(_BF16)
    acck = jnp.dot(h, wk_ref[...], preferred_element_type=_F32)
    y = _head_rms(acck, bd_ref[...], gk_ref[...])
    cos = cos_ref[...]
    sina = sina_ref[...]
    sinb = sinb_ref[...]
    for hh in range(2):
        yy = y[:, LANES * hh:LANES * (hh + 1)]
        r = yy * cos + pltpu.roll(yy, LANES - 16, 1) * sina + pltpu.roll(yy, 16, 1) * sinb
        for e in range(2):
            k_ref[0, 2 * hh + e] = r[:, HEAD_DIM * e:HEAD_DIM * (e + 1)].astype(_BF16)
    accv = lax.dot_general(wvt_ref[...], h, _NT, preferred_element_type=_F32)
    ones = jnp.ones((V_ROWS - HEAD_DIM, accv.shape[1]), _BF16)
    for j in range(A_KV_HEADS):
        vt_ref[0, j, 0, 0:HEAD_DIM, :] = accv[HEAD_DIM * j:HEAD_DIM * (j + 1), :].astype(_BF16)
        vt_ref[0, j, 0, HEAD_DIM:V_ROWS, :] = ones


def _rope_tables(seq_len):
    rows = seq_len // GRID_W
    row = jnp.repeat(jnp.arange(rows, dtype=_F32), GRID_W)
    col = jnp.tile(jnp.arange(GRID_W, dtype=_F32), rows)
    n_freq = HEAD_DIM // 4
    inv_freq = 1.0 / (ROPE_THETA ** (jnp.arange(n_freq, dtype=_F32) / n_freq))
    ang_r = row[:, None] * inv_freq[None, :]
    ang_c = col[:, None] * inv_freq[None, :]
    ang = jnp.concatenate([ang_r, ang_r, ang_c, ang_c], axis=-1)
    cos, sin = jnp.cos(ang), jnp.sin(ang)
    first = (np.arange(HEAD_DIM) % 32) < 16
    cos2 = jnp.tile(cos, (1, 2))
    sina = jnp.tile(jnp.where(first[None, :], -sin, 0.0), (1, 2))
    sinb = jnp.tile(jnp.where(first[None, :], 0.0, sin), (1, 2))
    return cos2, sina, sinb, cos.T, sin.T


def _block_diag_ones():
    i = np.arange(256)
    return jnp.asarray((i[:, None] // HEAD_DIM) == (i[None, :] // HEAD_DIM), dtype=_BF16)


def _proj_a(x, g, scale, shift, wqt, wk, wvt, gq_col, gk_t, tables, bd):
    bsz, seq, _ = x.shape
    tm = FA_TK
    assert seq % tm == 0
    cos2, sina, sinb, cost, sint = tables
    nq = A_HEADS * HEAD_DIM
    nk = A_KV_HEADS * HEAD_DIM
    const = lambda b, i: (0, 0)
    return pl.pallas_call(
        _proj_a_kernel,
        grid=(bsz, seq // tm),
        in_specs=[
            pl.BlockSpec((1, tm, D_MODEL), lambda b, i: (b, i, 0)),
            pl.BlockSpec((1, D_MODEL), const),
            pl.BlockSpec((1, 1, D_MODEL), lambda b, i: (b, 0, 0)),
            pl.BlockSpec((1, 1, D_MODEL), lambda b, i: (b, 0, 0)),
            pl.BlockSpec((nq, D_MODEL), const),
            pl.BlockSpec((D_MODEL, nk), const),
            pl.BlockSpec((nk, D_MODEL), const),
            pl.BlockSpec((HEAD_DIM, 1), const),
            pl.BlockSpec((1, nk), const),
            pl.BlockSpec((tm, LANES), lambda b, i: (i, 0)),
            pl.BlockSpec((tm, LANES), lambda b, i: (i, 0)),
            pl.BlockSpec((tm, LANES), lambda b, i: (i, 0)),
            pl.BlockSpec((HEAD_DIM, tm), lambda b, i: (0, i)),
            pl.BlockSpec((HEAD_DIM, tm), lambda b, i: (0, i)),
            pl.BlockSpec((256, 256), const),
        ],
        out_specs=[
            pl.BlockSpec((1, nq, tm), lambda b, i: (b, 0, i)),
            pl.BlockSpec((1, A_KV_HEADS, tm, HEAD_DIM), lambda b, i: (b, 0, i, 0)),
            pl.BlockSpec((1, A_KV_HEADS, 1, V_ROWS, tm), lambda b, i: (b, 0, i, 0, 0)),
        ],
        out_shape=[
            jax.ShapeDtypeStruct((bsz, nq, seq), _BF16),
            jax.ShapeDtypeStruct((bsz, A_KV_HEADS, seq, HEAD_DIM), _BF16),
            jax.ShapeDtypeStruct((bsz, A_KV_HEADS, seq // tm, V_ROWS, tm), _BF16),
        ],
        compiler_params=_cparams(("parallel", "parallel")),
        name="proj_a",
    )(x, g, scale, shift, wqt, wk, wvt, gq_col, gk_t, cos2, sina, sinb, cost, sint, bd)


def _flash_kernel(qt_ref, k_ref, vt_ref, o_ref, s_ref, m_ref, acc_ref, *, tq, tk, nchunks):
    qt4 = jnp.concatenate([qt_ref[0, HEAD_DIM * hq:HEAD_DIM * (hq + 1), :] for hq in range(A_GROUP)],
                          axis=1)
    m_ref[...] = jnp.full(m_ref.shape, -jnp.inf, _F32)
    acc_ref[...] = jnp.zeros(acc_ref.shape, _F32)

    def scores(kc, slot):
        kb = k_ref[0, 0, pl.ds(pl.multiple_of(kc * tk, tk), tk), :]
        s_ref[slot] = jnp.dot(kb, qt4, preferred_element_type=_F32)

    def update(kc, slot):
        s = s_ref[slot]
        m_prev = m_ref[...]
        m_new = jnp.maximum(m_prev, jnp.max(s, axis=0, keepdims=True))
        alpha = jnp.exp2(m_prev - m_new)
        p = jnp.exp2(s - m_new).astype(_BF16)
        acc_ref[...] = alpha * acc_ref[...] + jnp.dot(vt_ref[0, 0, kc], p,
                                                      preferred_element_type=_F32)
        m_ref[...] = m_new

    def pair(kc, last):
        scores(kc + 1, 1)
        update(kc, 0)
        if not last:
            scores(kc + 2, 0)
        update(kc + 1, 1)

    scores(0, 0)

    def body(c2, carry):
        pair(2 * c2, False)
        return carry

    lax.fori_loop(0, nchunks // 2 - 1, body, 0)
    pair(nchunks - 2, True)
    o = acc_ref[0:HEAD_DIM, :] / acc_ref[HEAD_DIM:HEAD_DIM + 1, :]
    o = jnp.concatenate([o[:, tq * hq:tq * (hq + 1)] for hq in range(A_GROUP)], axis=0)
    o_ref[0] = o.T.astype(_BF16)


def _flash_a(qt, k, vt):
    bsz, nq, seq = qt.shape
    tq = min(FA_TQ, seq)
    tk = vt.shape[-1]
    nchunks = seq // tk
    assert nchunks % 2 == 0 and seq % tq == 0
    return pl.pallas_call(
        functools.partial(_flash_kernel, tq=tq, tk=tk, nchunks=nchunks),
        grid=(bsz, A_KV_HEADS, seq // tq),
        in_specs=[
            pl.BlockSpec((1, A_GROUP * HEAD_DIM, tq), lambda b, h, i: (b, h, i)),
            pl.BlockSpec((1, 1, seq, HEAD_DIM), lambda b, h, i: (b, h, 0, 0)),
            pl.BlockSpec((1, 1, nchunks, V_ROWS, tk), lambda b, h, i: (b, h, 0, 0, 0)),
        ],
        out_specs=pl.BlockSpec((1, tq, A_GROUP * HEAD_DIM), lambda b, h, i: (b, i, h)),
        out_shape=jax.ShapeDtypeStruct((bsz, seq, nq), _BF16),
        scratch_shapes=[
            pltpu.VMEM((2, tk, A_GROUP * tq), _F32),
            pltpu.VMEM((1, A_GROUP * tq), _F32),
            pltpu.VMEM((V_ROWS, A_GROUP * tq), _F32),
        ],
        compiler_params=_cparams(("parallel", "parallel", "parallel")),
        name="flash_a",
    )(qt, k, vt)


def _oproj_kernel(a_ref, w_ref, x_ref, gate_ref, o_ref):
    y = jnp.dot(a_ref[0], w_ref[...], preferred_element_type=_F32)
    o_ref[0] = x_ref[0] + gate_ref[0] * y


def _oproj(a, w, x, gate):
    bsz, seq, _ = x.shape
    tm = min(ROW_TILE, seq)
    return pl.pallas_call(
        _oproj_kernel,
        grid=(bsz, seq // tm),
        in_specs=[
            pl.BlockSpec((1, tm, D_MODEL), lambda b, i: (b, i, 0)),
            pl.BlockSpec((D_MODEL, D_MODEL), lambda b, i: (0, 0)),
            pl.BlockSpec((1, tm, D_MODEL), lambda b, i: (b, i, 0)),
            pl.BlockSpec((1, 1, D_MODEL), lambda b, i: (b, 0, 0)),
        ],
        out_specs=pl.BlockSpec((1, tm, D_MODEL), lambda b, i: (b, i, 0)),
        out_shape=jax.ShapeDtypeStruct(x.shape, _F32),
        compiler_params=_cparams(("parallel", "parallel")),
        name="oproj",
    )(a, w, x, gate)


def _proj_b_kernel(x_ref, g_ref, sc_ref, sh_ref, w_ref, gv_ref, bd_ref, o_ref, scr_ref, *, dil, tm):
    h = _rms_mod(x_ref[0], g_ref[...], sc_ref[0], sh_ref[0]).astype(_BF16)
    bd = bd_ref[...]
    acc = jnp.dot(h, w_ref[...], preferred_element_type=_F32)
    nqk = 2 * B_HEADS * HEAD_DIM
    n = acc.shape[1]

    def emit(col, y):
        if dil == 1:
            o_ref[0, 0, :, col:col + y.shape[1]] = y.astype(_BF16)
        else:
            for c in range(y.shape[1] // LANES):
                scr_ref[col // LANES + c] = y[:, LANES * c:LANES * (c + 1)]

    for c in range(nqk // 256):
        sl = slice(256 * c, 256 * (c + 1))
        emit(256 * c, _head_rms(acc[:, sl], bd, gv_ref[:, sl]))
    emit(nqk, acc[:, nqk:])
    if dil > 1:
        for r in range(dil):
            for c in range(n // LANES):
                o_ref[0, r, :, LANES * c:LANES * (c + 1)] = (
                    scr_ref[c, pl.ds(r, tm // dil, stride=dil), :].astype(_BF16))


def _proj_b(x, g, scale, shift, w, gvec, bd, dil):
    bsz, seq, _ = x.shape
    tm = min(ROW_TILE, seq)
    n = w.shape[1]
    const = lambda b, i: (0, 0)
    return pl.pallas_call(
        functools.partial(_proj_b_kernel, dil=dil, tm=tm),
        grid=(bsz, seq // tm),
        in_specs=[
            pl.BlockSpec((1, tm, D_MODEL), lambda b, i: (b, i, 0)),
            pl.BlockSpec((1, D_MODEL), const),
            pl.BlockSpec((1, 1, D_MODEL), lambda b, i: (b, 0, 0)),
            pl.BlockSpec((1, 1, D_MODEL), lambda b, i: (b, 0, 0)),
            pl.BlockSpec((D_MODEL, n), const),
            pl.BlockSpec((1, 2 * B_HEADS * HEAD_DIM), const),
            pl.BlockSpec((256, 256), const),
        ],
        out_specs=pl.BlockSpec((1, dil, tm // dil, n), lambda b, i: (b, 0, i, 0)),
        out_shape=jax.ShapeDtypeStruct((bsz, dil, seq // dil, n), _BF16),
        scratch_shapes=[pltpu.VMEM((n // LANES, tm, LANES), _F32)],
        compiler_params=_cparams(("parallel", "parallel")),
        name="proj_b",
    )(x, g, scale, shift, w, gvec, bd)


def _dilated_kernel(main_ref, kp_ref, vp_ref, kn_ref, vn_ref, bias_ref, o_ref, lse_ref,
                    kf_ref, vf_ref, *, tq, length):
    nh = B_HEADS * HEAD_DIM
    l0 = pl.program_id(2) * tq
    kf_ref[0:DIL_RADIUS] = kp_ref[0, 0]
    kf_ref[DIL_RADIUS:DIL_RADIUS + tq] = main_ref[0, 0, :, nh:2 * nh]
    kf_ref[DIL_RADIUS + tq:] = kn_ref[0, 0]
    vf_ref[0:DIL_RADIUS] = vp_ref[0, 0]
    vf_ref[DIL_RADIUS:DIL_RADIUS + tq] = main_ref[0, 0, :, 2 * nh:3 * nh]
    vf_ref[DIL_RADIUS + tq:] = vn_ref[0, 0]
    width = DIL_SUB + 2 * DIL_RADIUS
    low_half = lax.broadcasted_iota(jnp.int32, (DIL_SUB, LANES), 1) < HEAD_DIM
    col = lax.broadcasted_iota(jnp.int32, (DIL_SUB, width), 1)

    def sub_block(u, carry):
        r0 = pl.multiple_of(u * DIL_SUB, DIL_SUB)
        kpos = col + (l0 + r0 - DIL_RADIUS)
        valid = (kpos >= 0) & (kpos < length)
        for m in range(B_HEADS // 2):
            sl = slice(LANES * m, LANES * (m + 1))
            qp = main_ref[0, 0, pl.ds(r0, DIL_SUB), sl]
            kw = kf_ref[pl.ds(r0, width), sl]
            vw = vf_ref[pl.ds(r0, width), sl]
            res = []
            for hh in range(2):
                keep = low_half if hh == 0 else jnp.logical_not(low_half)
                qm = jnp.where(keep, qp, jnp.zeros_like(qp))
                s = lax.dot_general(qm, kw, _NT, preferred_element_type=_F32) + bias_ref[2 * m + hh]
                s = jnp.where(valid, s, NEG_INF)
                mx = jnp.max(s, axis=1, keepdims=True)
                p = jnp.exp(s - mx)
                den = jnp.sum(p, axis=1, keepdims=True)
                pv = jnp.dot(p.astype(_BF16), vw, preferred_element_type=_F32)
                res.append((pv / den, mx + jnp.log(den)))
            o_ref[0, 0, pl.ds(r0, DIL_SUB), sl] = jnp.where(low_half, res[0][0], res[1][0])
            lse_ref[0, 0, pl.ds(r0, DIL_SUB), sl] = jnp.where(
                low_half, jnp.broadcast_to(res[0][1], (DIL_SUB, LANES)),
                jnp.broadcast_to(res[1][1], (DIL_SUB, LANES)))
        return carry

    lax.fori_loop(0, tq // DIL_SUB, sub_block, 0)


def _dilated_attn(qkv, bias):
    bsz, dil, length, n = qkv.shape
    nh = B_HEADS * HEAD_DIM
    tq = min(DIL_TQ, length)
    nb = tq // DIL_RADIUS
    last = length // DIL_RADIUS - 1
    prev_ix = lambda i: jnp.maximum(i * nb - 1, 0)
    next_ix = lambda i: jnp.minimum((i + 1) * nb, last)
    halo = (1, 1, DIL_RADIUS, nh)
    width = DIL_SUB + 2 * DIL_RADIUS
    out = jax.ShapeDtypeStruct((bsz, dil, length, nh), _F32)
    return pl.pallas_call(
        functools.partial(_dilated_kernel, tq=tq, length=length),
        grid=(bsz, dil, length // tq),
        in_specs=[
            pl.BlockSpec((1, 1, tq, n), lambda b, r, i: (b, r, i, 0)),
            pl.BlockSpec(halo, lambda b, r, i: (b, r, prev_ix(i), 1)),
            pl.BlockSpec(halo, lambda b, r, i: (b, r, prev_ix(i), 2)),
            pl.BlockSpec(halo, lambda b, r, i: (b, r, next_ix(i), 1)),
            pl.BlockSpec(halo, lambda b, r, i: (b, r, next_ix(i), 2)),
            pl.BlockSpec((B_HEADS, DIL_SUB, width), lambda b, r, i: (0, 0, 0)),
        ],
        out_specs=[pl.BlockSpec((1, 1, tq, nh), lambda b, r, i: (b, r, i, 0)),
                   pl.BlockSpec((1, 1, tq, nh), lambda b, r, i: (b, r, i, 0))],
        out_shape=[out, out],
        scratch_shapes=[pltpu.VMEM((tq + 2 * DIL_RADIUS, nh), _BF16),
                        pltpu.VMEM((tq + 2 * DIL_RADIUS, nh), _BF16)],
        compiler_params=_cparams(("parallel", "parallel", "parallel")),
        name="dilated_attn",
    )(qkv, qkv, qkv, qkv, qkv, bias)


def _t5_bucket(rel):
    nb = NUM_BUCKETS // 2
    max_exact = nb // 2
    ret = (rel > 0).astype(np.int32) * nb
    n = np.abs(rel)
    large = max_exact + (np.log(np.maximum(n, 1) / max_exact) / np.log(MAX_DISTANCE / max_exact)
                         * (nb - max_exact)).astype(np.int32)
    large = np.minimum(large, nb - 1)
    return (ret + np.where(n < max_exact, n, large)).astype(np.int32)


def _dilated_bias(rel_bias_g, dil):
    width = DIL_SUB + 2 * DIL_RADIUS
    rel = np.arange(width)[None, :] - DIL_RADIUS - np.arange(DIL_SUB)[:, None]
    bucket = _t5_bucket(rel * dil)
    band = np.abs(rel) <= DIL_RADIUS
    onehot = (bucket.reshape(-1, 1) == np.arange(NUM_BUCKETS)[None, :]).astype(np.float32)
    bias = jnp.dot(jnp.asarray(onehot), rel_bias_g.astype(_F32), precision=lax.Precision.HIGHEST)
    bias = bias.reshape(DIL_SUB, width, B_HEADS).transpose(2, 0, 1)
    return jnp.where(jnp.asarray(band)[None], bias, NEG_INF)


def _merge_oproj_kernel(o0_ref, l0_ref, o1_ref, l1_ref, o2_ref, l2_ref, w_ref, x_ref, gate_ref,
                        out_ref, so1, sl1, so2, sl2, *, tm):
    d1 = DILATED_CONFIGS[1][1]
    d2 = DILATED_CONFIGS[2][1]
    nc = B_HEADS * HEAD_DIM // LANES

    def to_token_order(src_ref, scr_ref, dil):
        for r in range(dil):
            for c in range(nc):
                scr_ref[c, pl.ds(r, tm // dil, stride=dil), :] = (
                    src_ref[0, r, :, LANES * c:LANES * (c + 1)])
        return jnp.concatenate([scr_ref[c] for c in range(nc)], axis=1)

    o1 = to_token_order(o1_ref, so1, d1)
    l1 = to_token_order(l1_ref, sl1, d1)
    o2 = to_token_order(o2_ref, so2, d2)
    l2 = to_token_order(l2_ref, sl2, d2)
    l0 = l0_ref[0, 0]
    mx = jnp.maximum(jnp.maximum(l0, l1), l2)
    e0, e1, e2 = jnp.exp(l0 - mx), jnp.exp(l1 - mx), jnp.exp(l2 - mx)
    o = (e0 * o0_ref[0, 0] + e1 * o1 + e2 * o2) / (e0 + e1 + e2)
    y = jnp.dot(o.astype(_BF16), w_ref[...], preferred_element_type=_F32)
    out_ref[0] = x_ref[0] + gate_ref[0] * y


def _merge_oproj(outs, lses, w, x, gate):
    bsz, seq, _ = x.shape
    tm = min(ROUTE_TILE, seq)
    nh = B_HEADS * HEAD_DIM
    specs = []
    for (_, dil) in DILATED_CONFIGS:
        spec = pl.BlockSpec((1, dil, tm // dil, nh), lambda b, i: (b, 0, i, 0))
        specs += [spec, spec]
    return pl.pallas_call(
        functools.partial(_merge_oproj_kernel, tm=tm),
        grid=(bsz, seq // tm),
        in_specs=specs + [
            pl.BlockSpec((nh, D_MODEL), lambda b, i: (0, 0)),
            pl.BlockSpec((1, tm, D_MODEL), lambda b, i: (b, i, 0)),
            pl.BlockSpec((1, 1, D_MODEL), lambda b, i: (b, 0, 0)),
        ],
        out_specs=pl.BlockSpec((1, tm, D_MODEL), lambda b, i: (b, i, 0)),
        out_shape=jax.ShapeDtypeStruct(x.shape, _F32),
        scratch_shapes=[pltpu.VMEM((nh // LANES, tm, LANES), _F32)] * 4,
        compiler_params=_cparams(("parallel", "parallel")),
        name="merge_oproj",
    )(outs[0], lses[0], outs[1], lses[1], outs[2], lses[2], w, x, gate)


def _router_kernel(x_ref, g_ref, sc_ref, sh_ref, wr_ref, br_ref,
                   hf_ref, idx_ref, gate_ref, rank_ref, cnt_ref, carry_ref, tri_ref, *, tm):
    first = (pl.program_id(0) == 0) & (pl.program_id(1) == 0)

    @pl.when(first)
    def _():
        carry_ref[...] = jnp.zeros(carry_ref.shape, _F32)
        r = lax.broadcasted_iota(jnp.int32, (tm, tm), 0)
        c = lax.broadcasted_iota(jnp.int32, (tm, tm), 1)
        tri_ref[...] = (r < c).astype(_BF16)

    hf = _rms_mod(x_ref[0], g_ref[...], sc_ref[0], sh_ref[0])
    for j in range(ROW_WORDS):
        hf_ref[pl.ds(j, tm, stride=ROW_WORDS), :] = hf[:, LANES * j:LANES * (j + 1)]
    logits = lax.dot_general(wr_ref[...], hf, _NT, preferred_element_type=_F32,
                             precision=lax.Precision.HIGHEST) + br_ref[...]
    eio = lax.broadcasted_iota(jnp.int32, logits.shape, 0).astype(_F32)
    cur = logits
    vals, sel = [], []
    for _ in range(TOP_K):
        mx = jnp.max(cur, axis=0, keepdims=True)
        ix = jnp.min(jnp.where(cur == mx, eio, float(N_EXPERTS)), axis=0, keepdims=True)
        vals.append(mx)
        sel.append(ix)
        cur = jnp.where(eio == ix, -jnp.inf, cur)
    ex = [jnp.exp(v - vals[0]) for v in vals]
    den = ex[0] + ex[1] + ex[2] + ex[3]
    member = jnp.zeros(logits.shape, _F32)
    for ix in sel:
        member = member + (eio == ix).astype(_F32)
    before = carry_ref[:, 0:1] + jnp.dot(member.astype(_BF16), tri_ref[...],
                                         preferred_element_type=_F32)
    ranks = [jnp.sum(jnp.where(eio == ix, before, 0.0), axis=0, keepdims=True) for ix in sel]
    idx_ref[...] = jnp.concatenate(sel, axis=0).astype(jnp.int32)
    gate_ref[...] = jnp.concatenate([e / den for e in ex], axis=0)
    rank_ref[...] = jnp.concatenate(ranks, axis=0).astype(jnp.int32)
    carry_ref[...] = carry_ref[...] + jnp.sum(member, axis=1, keepdims=True)
    cnt_ref[...] = carry_ref[...]


def _router(x, g, scale, shift, wr_t, br):
    bsz, seq, _ = x.shape
    tm = min(ROUTE_TILE, seq)
    ns = seq // tm
    tok = bsz * seq
    const = lambda b, i: (0, 0)
    flat = lambda b, i: (0, b * ns + i)
    return pl.pallas_call(
        functools.partial(_router_kernel, tm=tm),
        grid=(bsz, ns),
        in_specs=[
            pl.BlockSpec((1, tm, D_MODEL), lambda b, i: (b, i, 0)),
            pl.BlockSpec((1, D_MODEL), const),
            pl.BlockSpec((1, 1, D_MODEL), lambda b, i: (b, 0, 0)),
            pl.BlockSpec((1, 1, D_MODEL), lambda b, i: (b, 0, 0)),
            pl.BlockSpec((N_EXPERTS, D_MODEL), const),
            pl.BlockSpec((N_EXPERTS, 1), const),
        ],
        out_specs=[
            pl.BlockSpec((tm * ROW_WORDS, LANES), lambda b, i: (b * ns + i, 0)),
            pl.BlockSpec((TOP_K, tm), flat),
            pl.BlockSpec((TOP_K, tm), flat),
            pl.BlockSpec((TOP_K, tm), flat),
            pl.BlockSpec((N_EXPERTS, LANES), const),
        ],
        out_shape=[
            jax.ShapeDtypeStruct((tok * ROW_WORDS, LANES), _F32),
            jax.ShapeDtypeStruct((TOP_K, tok), jnp.int32),
            jax.ShapeDtypeStruct((TOP_K, tok), _F32),
            jax.ShapeDtypeStruct((TOP_K, tok), jnp.int32),
            jax.ShapeDtypeStruct((N_EXPERTS, LANES), _F32),
        ],
        scratch_shapes=[pltpu.VMEM((N_EXPERTS, LANES), _F32), pltpu.VMEM((tm, tm), _BF16)],
        compiler_params=_cparams(("arbitrary", "arbitrary")),
        name="router",
    )(x, g, scale, shift, wr_t, br)


def _row_copy(src_ref, src_row, dst_ref, dst_row, sem):
    return pltpu.make_async_copy(
        src_ref.at[pl.ds(pl.multiple_of(src_row * ROW_WORDS, ROW_WORDS), ROW_WORDS), :],
        dst_ref.at[pl.ds(pl.multiple_of(dst_row * ROW_WORDS, ROW_WORDS), ROW_WORDS), :], sem)


def _dispatch_kernel(dest_hbm, hf_ref, init_ref, xs_ref, dest_smem, isem, sem, *, tm):
    del init_ref
    i = pl.program_id(0)
    n = tm * TOP_K
    cp = pltpu.make_async_copy(dest_hbm.at[pl.ds(pl.multiple_of(i * n, n), n)], dest_smem, isem)
    cp.start()
    cp.wait()

    def issue(r, carry):
        for k in range(TOP_K):
            _row_copy(hf_ref, r, xs_ref, dest_smem[r * TOP_K + k], sem).start(priority=k % 2)
        return carry

    lax.fori_loop(0, tm, issue, 0)

    def drain(r, carry):
        for k in range(TOP_K):
            _row_copy(hf_ref, 0, xs_ref, 0, sem).wait()
        return carry

    lax.fori_loop(0, tm, drain, 0)


def _dispatch(dest_flat, hf, n_rows):
    tok = hf.shape[0] // ROW_WORDS
    tm = min(DISPATCH_TILE, tok)
    assert tok % tm == 0 and (tm * TOP_K) % 1024 == 0
    init = jnp.zeros((n_rows * ROW_WORDS, LANES), _F32)
    return pl.pallas_call(
        functools.partial(_dispatch_kernel, tm=tm),
        grid=(tok // tm,),
        in_specs=[
            pl.BlockSpec(memory_space=pl.ANY),
            pl.BlockSpec((tm * ROW_WORDS, LANES), lambda i: (i, 0)),
            pl.BlockSpec(memory_space=pl.ANY),
        ],
        out_specs=pl.BlockSpec(memory_space=pl.ANY),
        out_shape=jax.ShapeDtypeStruct(init.shape, _F32),
        scratch_shapes=[pltpu.SMEM((tm * TOP_K,), jnp.int32),
                        pltpu.SemaphoreType.DMA, pltpu.SemaphoreType.DMA],
        input_output_aliases={2: 0},
        compiler_params=_cparams(("arbitrary",)),
        name="moe_dispatch",
    )(dest_flat, hf, init)


def _expert_kernel(be_ref, nu_ref, xs_ref, wgu_ref, bgu_ref, wdn_ref, bdn_ref, ys_ref, *, blk):
    del be_ref
    i = pl.program_id(0)
    dff = wdn_ref.shape[1]

    @pl.when(i < nu_ref[0])
    def _():
        x = jnp.concatenate([xs_ref[pl.ds(j, blk, stride=ROW_WORDS), :] for j in range(ROW_WORDS)],
                            axis=1).astype(_BF16)
        gu = jnp.dot(x, wgu_ref[0], preferred_element_type=_F32) + bgu_ref[0]
        g = jnp.minimum(gu[:, :dff], SWIGLU_LIMIT)
        u = jnp.clip(gu[:, dff:], -SWIGLU_LIMIT, SWIGLU_LIMIT)
        act = g * jax.nn.sigmoid(SWIGLU_ALPHA * g)
        mid = ((u + 1.0) * act).astype(_BF16)
        y = jnp.dot(mid, wdn_ref[0], preferred_element_type=_F32) + bdn_ref[0]
        for j in range(ROW_WORDS):
            ys_ref[pl.ds(j, blk, stride=ROW_WORDS), :] = y[:, LANES * j:LANES * (j + 1)]

    @pl.when(i >= nu_ref[0])
    def _():
        ys_ref[...] = jnp.zeros(ys_ref.shape, _F32)


def _experts(blk_e, n_used, xs, wgu, bgu, wdn, bdn):
    blk = EXPERT_BLOCK
    n_blk = xs.shape[0] // (blk * ROW_WORDS)
    dff = wdn.shape[1]
    grid_spec = pltpu.PrefetchScalarGridSpec(
        num_scalar_prefetch=2,
        grid=(n_blk,),
        in_specs=[
            pl.BlockSpec((blk * ROW_WORDS, LANES), lambda i, be, nu: (i, 0)),
            pl.BlockSpec((1, D_MODEL, 2 * dff), lambda i, be, nu: (be[i], 0, 0)),
            pl.BlockSpec((1, 1, 2 * dff), lambda i, be, nu: (be[i], 0, 0)),
            pl.BlockSpec((1, dff, D_MODEL), lambda i, be, nu: (be[i], 0, 0)),
            pl.BlockSpec((1, 1, D_MODEL), lambda i, be, nu: (be[i], 0, 0)),
        ],
        out_specs=pl.BlockSpec((blk * ROW_WORDS, LANES), lambda i, be, nu: (i, 0)),
    )
    return pl.pallas_call(
        functools.partial(_expert_kernel, blk=blk),
        grid_spec=grid_spec,
        out_shape=jax.ShapeDtypeStruct(xs.shape, _F32),
        compiler_params=_cparams(("arbitrary",)),
        name="moe_experts",
    )(blk_e, n_used, xs, wgu, bgu, wdn, bdn)


def _combine_kernel(dest_hbm, gate_hbm, ys_ref, x_ref, gf_ref, o_ref,
                    dest_smem, gate_smem, buf_ref, mo_ref, isem, sem, *, tm, ns, nsteps):
    g = pl.program_id(0) * ns + pl.program_id(1)
    n = tm * TOP_K

    def start_gathers(step, slot):
        off = pl.multiple_of(step * n, n)
        cp_d = pltpu.make_async_copy(dest_hbm.at[pl.ds(off, n)], dest_smem.at[slot], isem.at[0])
        cp_g = pltpu.make_async_copy(gate_hbm.at[pl.ds(off, n)], gate_smem.at[slot], isem.at[1])
        cp_d.start()
        cp_g.start()
        cp_d.wait()
        cp_g.wait()

        def issue(r, carry):
            for k in range(TOP_K):
                _row_copy(ys_ref, dest_smem[slot, r * TOP_K + k], buf_ref.at[slot], k * tm + r,
                          sem.at[slot]).start(priority=k % 2)
            return carry

        lax.fori_loop(0, tm, issue, 0)

    @pl.when(g == 0)
    def _():
        start_gathers(0, 0)

    slot = g % 2

    @pl.when(g + 1 < nsteps)
    def _():
        start_gathers(g + 1, 1 - slot)

    def drain(r, carry):
        for k in range(TOP_K):
            _row_copy(ys_ref, 0, buf_ref.at[slot], 0, sem.at[slot]).wait()
        return carry

    lax.fori_loop(0, tm, drain, 0)

    def mix(r, carry):
        acc = jnp.zeros((ROW_WORDS, LANES), _F32)
        for k in range(TOP_K):
            row = pl.multiple_of((k * tm + r) * ROW_WORDS, ROW_WORDS)
            acc = acc + gate_smem[slot, r * TOP_K + k] * buf_ref[slot, pl.ds(row, ROW_WORDS), :]
        mo_ref[pl.ds(pl.multiple_of(r * ROW_WORDS, ROW_WORDS), ROW_WORDS), :] = acc
        return carry

    lax.fori_loop(0, tm, mix, 0)
    moe = jnp.concatenate([mo_ref[pl.ds(j, tm, stride=ROW_WORDS), :] for j in range(ROW_WORDS)],
                          axis=1)
    o_ref[0] = x_ref[0] + gf_ref[0] * moe


def _combine(dest_flat, gate_flat, ys, x, gate_f):
    bsz, seq, _ = x.shape
    tm = min(ROUTE_TILE, seq)
    ns = seq // tm
    return pl.pallas_call(
        functools.partial(_combine_kernel, tm=tm, ns=ns, nsteps=bsz * ns),
        grid=(bsz, ns),
        in_specs=[
            pl.BlockSpec(memory_space=pl.ANY),
            pl.BlockSpec(memory_space=pl.ANY),
            pl.BlockSpec(memory_space=pl.ANY),
            pl.BlockSpec((1, tm, D_MODEL), lambda b, i: (b, i, 0)),
            pl.BlockSpec((1, 1, D_MODEL), lambda b, i: (b, 0, 0)),
        ],
        out_specs=pl.BlockSpec((1, tm, D_MODEL), lambda b, i: (b, i, 0)),
        out_shape=jax.ShapeDtypeStruct(x.shape, _F32),
        scratch_shapes=[
            pltpu.SMEM((2, tm * TOP_K), jnp.int32),
            pltpu.SMEM((2, tm * TOP_K), _F32),
            pltpu.VMEM((2, TOP_K * tm * ROW_WORDS, LANES), _F32),
            pltpu.VMEM((tm * ROW_WORDS, LANES), _F32),
            pltpu.SemaphoreType.DMA((2,)),
            pltpu.SemaphoreType.DMA((2,)),
        ],
        compiler_params=_cparams(("arbitrary", "arbitrary")),
        name="moe_combine",
    )(dest_flat, gate_flat, ys, x, gate_f)


def _moe(x, g, scale, shift, gate_f, wr_t, br, wgu, bgu, wdn, bdn):
    bsz, seq, _ = x.shape
    tok = bsz * seq
    blk = EXPERT_BLOCK
    hf, idx, gate, rank, cnt = _router(x, g, scale, shift, wr_t, br)
    counts = cnt[:, 0].astype(jnp.int32)
    padded = (counts + blk - 1) // blk * blk
    pend = jnp.cumsum(padded)
    pstart = pend - padded
    eids = jnp.arange(N_EXPERTS, dtype=jnp.int32)[:, None, None]
    base = jnp.sum(jnp.where(idx[None] == eids, pstart[:, None, None], 0), axis=0)
    dest = (base + rank).T.reshape(tok * TOP_K)
    gate_flat = gate.T.reshape(tok * TOP_K)
    n_blk = (tok * TOP_K + N_EXPERTS * (blk - 1) + blk - 1) // blk
    blk_start = jnp.arange(n_blk, dtype=jnp.int32) * blk
    blk_e = jnp.minimum(jnp.sum(blk_start[:, None] >= pend[None, :], axis=1),
                        N_EXPERTS - 1).astype(jnp.int32)
    n_used = (pend[-1:] // blk).astype(jnp.int32)
    xs = _dispatch(dest, hf, n_blk * blk)
    ys = _experts(blk_e, n_used, xs, wgu, bgu, wdn, bdn)
    return _combine(dest, gate_flat, ys, x, gate_f)


def _prepare(norm_mix_g, norm_ffn_g, w_qkv_a, q_norm_a, k_norm_a, w_o_a, w_qkv_b, q_norm_b, k_norm_b,
             w_o_b, rel_bias, w_router, b_router, w_gate_up, b_gate_up, w_down, b_down):
    nq = A_HEADS * HEAD_DIM
    nk = A_KV_HEADS * HEAD_DIM
    nh = B_HEADS * HEAD_DIM
    scale = HEAD_DIM ** -0.5
    p = {}
    wa = w_qkv_a[0]
    p["wqt_a"] = wa[:, :nq].T.astype(_BF16)
    p["wk_a"] = wa[:, nq:nq + nk].astype(_BF16)
    p["wvt_a"] = wa[:, nq + nk:].T.astype(_BF16)
    p["gq_a"] = (q_norm_a[0] * (scale * np.log2(np.e))).reshape(HEAD_DIM, 1)
    p["gk_a"] = jnp.tile(k_norm_a[0], A_KV_HEADS).reshape(1, nk)
    p["wo_a"] = w_o_a[0].astype(_BF16)
    wb = w_qkv_b[0].astype(_BF16)
    p["w_b"] = [wb[:, 3 * nh * gi:3 * nh * (gi + 1)] for gi in range(len(DILATED_CONFIGS))]
    p["gv_b"] = [jnp.concatenate([jnp.tile(q_norm_b[0, gi], B_HEADS) * scale,
                                  jnp.tile(k_norm_b[0, gi], B_HEADS)]).reshape(1, 2 * nh)
                 for gi in range(len(DILATED_CONFIGS))]
    p["bias_b"] = [_dilated_bias(rel_bias[:, gi * B_HEADS:(gi + 1) * B_HEADS], dil)
                   for gi, (_, dil) in enumerate(DILATED_CONFIGS)]
    p["wo_b"] = w_o_b[0].astype(_BF16)
    p["g_mix"] = [norm_mix_g[i].reshape(1, D_MODEL) for i in range(DEPTH)]
    p["g_ffn"] = [norm_ffn_g[i].reshape(1, D_MODEL) for i in range(DEPTH)]
    p["wr_t"] = [w_router[i].T for i in range(DEPTH)]
    p["br"] = [b_router[i].reshape(N_EXPERTS, 1) for i in range(DEPTH)]
    p["wgu"] = [w_gate_up[i].astype(_BF16) for i in range(DEPTH)]
    p["bgu"] = [b_gate_up[i].reshape(N_EXPERTS, 1, -1) for i in range(DEPTH)]
    p["wdn"] = [w_down[i].astype(_BF16) for i in range(DEPTH)]
    p["bdn"] = [b_down[i].reshape(N_EXPERTS, 1, -1) for i in range(DEPTH)]
    p["bd"] = _block_diag_ones()
    return p


def _trunk(x, c, w_ada, b_ada, p):
    bsz, seq, _ = x.shape
    tables = _rope_tables(seq)
    for i in range(DEPTH):
        mod = _ada_mod(c, w_ada[i], b_ada[i])
        shift_m, scale_m, gate_m, shift_f, scale_f, gate_f = [
            mod[:, D_MODEL * k:D_MODEL * (k + 1)].reshape(bsz, 1, D_MODEL) for k in range(6)]
        if i % 2 == 0:
            qt, k, vt = _proj_a(x, p["g_mix"][i], scale_m, shift_m, p["wqt_a"], p["wk_a"],
                                p["wvt_a"], p["gq_a"], p["gk_a"], tables, p["bd"])
            a = _flash_a(qt, k, vt)
            x = _oproj(a, p["wo_a"], x, gate_m)
        else:
            outs, lses = [], []
            for gi, (_, dil) in enumerate(DILATED_CONFIGS):
                qkv = _proj_b(x, p["g_mix"][i], scale_m, shift_m, p["w_b"][gi], p["gv_b"][gi],
                              p["bd"], dil)
                o, lse = _dilated_attn(qkv, p["bias_b"][gi])
                outs.append(o)
                lses.append(lse)
            x = _merge_oproj(outs, lses, p["wo_b"], x, gate_m)
        x = _moe(x, p["g_ffn"][i], scale_f, shift_f, gate_f, p["wr_t"][i], p["br"][i],
                 p["wgu"][i], p["bgu"][i], p["wdn"][i], p["bdn"][i])
    return x


def kernel(x_prompt, x_sample, c_prompt, c_sample, norm_mix_g, norm_ffn_g, w_ada, b_ada, w_qkv_a,
           q_norm_a, k_norm_a, w_o_a, w_qkv_b, q_norm_b, k_norm_b, w_o_b, rel_bias, w_router,
           b_router, w_gate_up, b_gate_up, w_down, b_down):
    p = _prepare(norm_mix_g, norm_ffn_g, w_qkv_a, q_norm_a, k_norm_a, w_o_a, w_qkv_b, q_norm_b,
                 k_norm_b, w_o_b, rel_bias, w_router, b_router, w_gate_up, b_gate_up, w_down, b_down)
    y_prompt = _trunk(x_prompt, c_prompt, w_ada, b_ada, p)
    y_sample = _trunk(x_sample, c_sample, w_ada, b_ada, p)
    return (y_prompt, y_sample)
```

```python
import functools

import numpy as np
import jax
import jax.numpy as jnp
from jax import lax
from jax.experimental import pallas as pl
from jax.experimental.pallas import tpu as pltpu

D_MODEL = 1024
HEAD_DIM = 64
A_HEADS = 16
A_KV_HEADS = 4
A_GROUP = A_HEADS // A_KV_HEADS
GRID_W = 64
ROPE_THETA = 10000.0
B_HEADS = 16
DILATED_CONFIGS = ((128, 1), (512, 4), (2048, 16))
NUM_BUCKETS = 32
MAX_DISTANCE = 1024
N_EXPERTS = 32
TOP_K = 4
SWIGLU_ALPHA = 1.702
SWIGLU_LIMIT = 7.0
RMS_EPS = 1e-6
NEG_INF = -1e30
DEPTH = 2

LANES = 128
SUBLANES = 8
VMEM_LIMIT = 56 * 1024 * 1024

ROW_TILE = 512
FA_TQ = 256
FA_TK = 512
V_ROWS = HEAD_DIM + 16
DIL_TQ = 512
DIL_SUB = 128
DIL_RADIUS = 64
ROUTE_TILE = 256
DISPATCH_TILE = 256
EXPERT_BLOCK = 256
ROW_WORDS = D_MODEL // LANES

_F32 = jnp.float32
_BF16 = jnp.bfloat16
_NT = (((1,), (1,)), ((), ()))


def _cparams(sem):
    return pltpu.CompilerParams(dimension_semantics=sem, vmem_limit_bytes=VMEM_LIMIT)


def _rms_mod(x, g, scale, shift):
    var = jnp.mean(x * x, axis=-1, keepdims=True)
    y = x * lax.rsqrt(var + RMS_EPS) * g
    return y * (1.0 + scale) + shift


def _head_rms(y, bd, gvec):
    ss = jnp.dot((y * y).astype(_BF16), bd, preferred_element_type=_F32)
    return y * lax.rsqrt(ss * (1.0 / HEAD_DIM) + RMS_EPS) * gvec


def _ada_kernel(c_ref, w_ref, b_ref, o_ref):
    c = c_ref[...]
    s = c * jax.nn.sigmoid(c)
    o_ref[...] = jnp.dot(s, w_ref[...], preferred_element_type=_F32,
                         precision=lax.Precision.HIGHEST) + b_ref[...]


def _ada_mod(c, w, b):
    bsz = c.shape[0]
    cp = jnp.pad(c, ((0, SUBLANES - bsz), (0, 0)))
    n = w.shape[1]
    out = pl.pallas_call(
        _ada_kernel,
        grid=(n // D_MODEL,),
        in_specs=[pl.BlockSpec((SUBLANES, D_MODEL), lambda j: (0, 0)),
                  pl.BlockSpec((D_MODEL, D_MODEL), lambda j: (0, j)),
                  pl.BlockSpec((1, D_MODEL), lambda j: (0, j))],
        out_specs=pl.BlockSpec((SUBLANES, D_MODEL), lambda j: (0, j)),
        out_shape=jax.ShapeDtypeStruct((SUBLANES, n), _F32),
        compiler_params=_cparams(("arbitrary",)),
        name="ada_mod",
    )(cp, w, b.reshape(1, n))
    return out[:bsz]


def _proj_a_kernel(x_ref, g_ref, sc_ref, sh_ref, wqt_ref, wk_ref, wvt_ref, gq_ref, gk_ref,
                   cos_ref, sina_ref, sinb_ref, cost_ref, sint_ref, bd_ref,
                   qt_ref, k_ref, vt_ref):
    h = _rms_mod(x_ref[0], g_ref[...], sc_ref[0], sh_ref[0]).astype(_BF16)
    acct = lax.dot_general(wqt_ref[...], h, _NT, preferred_element_type=_F32)
    cost = cost_ref[...]
    sint = sint_ref[...]
    gq = gq_ref[...]
    for j in range(A_HEADS):
        slab = acct[HEAD_DIM * j:HEAD_DIM * (j + 1), :]
        ss = jnp.sum(slab * slab, axis=0, keepdims=True)
        y = slab * lax.rsqrt(ss * (1.0 / HEAD_DIM) + RMS_EPS) * gq
        rot = jnp.concatenate([-y[16:32], y[0:16], -y[48:64], y[32:48]], axis=0)
        qt_ref[0, HEAD_DIM * j:HEAD_DIM * (j + 1), :] = (y * cost + rot * sint).astype(_BF16)
    acck = jnp.dot(h, wk_ref[...], preferred_element_type=_F32)
    y = _head_rms(acck, bd_ref[...], gk_ref[...])
    cos = cos_ref[...]
    sina = sina_ref[...]
    sinb = sinb_ref[...]
    for hh in range(2):
        yy = y[:, LANES * hh:LANES * (hh + 1)]
        r = yy * cos + pltpu.roll(yy, LANES - 16, 1) * sina + pltpu.roll(yy, 16, 1) * sinb
        for e in range(2):
            k_ref[0, 2 * hh + e] = r[:, HEAD_DIM * e:HEAD_DIM * (e + 1)].astype(_BF16)
    accv = lax.dot_general(wvt_ref[...], h, _NT, preferred_element_type=_F32)
    ones = jnp.ones((V_ROWS - HEAD_DIM, accv.shape[1]), _BF16)
    for j in range(A_KV_HEADS):
        vt_ref[0, j, 0, 0:HEAD_DIM, :] = accv[HEAD_DIM * j:HEAD_DIM * (j + 1), :].astype(_BF16)
        vt_ref[0, j, 0, HEAD_DIM:V_ROWS, :] = ones


def _rope_tables(seq_len):
    rows = seq_len // GRID_W
    row = jnp.repeat(jnp.arange(rows, dtype=_F32), GRID_W)
    col = jnp.tile(jnp.arange(GRID_W, dtype=_F32), rows)
    n_freq = HEAD_DIM // 4
    inv_freq = 1.0 / (ROPE_THETA ** (jnp.arange(n_freq, dtype=_F32) / n_freq))
    ang_r = row[:, None] * inv_freq[None, :]
    ang_c = col[:, None] * inv_freq[None, :]
    ang = jnp.concatenate([ang_r, ang_r, ang_c, ang_c], axis=-1)
    cos, sin = jnp.cos(ang), jnp.sin(ang)
    first = (np.arange(HEAD_DIM) % 32) < 16
    cos2 = jnp.tile(cos, (1, 2))
    sina = jnp.tile(jnp.where(first[None, :], -sin, 0.0), (1, 2))
    sinb = jnp.tile(jnp.where(first[None, :], 0.0, sin), (1, 2))
    return cos2, sina, sinb, cos.T, sin.T


def _block_diag_ones():
    i = np.arange(256)
    return jnp.asarray((i[:, None] // HEAD_DIM) == (i[None, :] // HEAD_DIM), dtype=_BF16)


def _proj_a(x, g, scale, shift, wqt, wk, wvt, gq_col, gk_t, tables, bd):
    bsz, seq, _ = x.shape
    tm = FA_TK
    assert seq % tm == 0
    cos2, sina, sinb, cost, sint = tables
    nq = A_HEADS * HEAD_DIM
    nk = A_KV_HEADS * HEAD_DIM
    const = lambda b, i: (0, 0)
    return pl.pallas_call(
        _proj_a_kernel,
        grid=(bsz, seq // tm),
        in_specs=[
            pl.BlockSpec((1, tm, D_MODEL), lambda b, i: (b, i, 0)),
            pl.BlockSpec((1, D_MODEL), const),
            pl.BlockSpec((1, 1, D_MODEL), lambda b, i: (b, 0, 0)),
            pl.BlockSpec((1, 1, D_MODEL), lambda b, i: (b, 0, 0)),
            pl.BlockSpec((nq, D_MODEL), const),
            pl.BlockSpec((D_MODEL, nk), const),
            pl.BlockSpec((nk, D_MODEL), const),
            pl.BlockSpec((HEAD_DIM, 1), const),
            pl.BlockSpec((1, nk), const),
            pl.BlockSpec((tm, LANES), lambda b, i: (i, 0)),
            pl.BlockSpec((tm, LANES), lambda b, i: (i, 0)),
            pl.BlockSpec((tm, LANES), lambda b, i: (i, 0)),
            pl.BlockSpec((HEAD_DIM, tm), lambda b, i: (0, i)),
            pl.BlockSpec((HEAD_DIM, tm), lambda b, i: (0, i)),
            pl.BlockSpec((256, 256), const),
        ],
        out_specs=[
            pl.BlockSpec((1, nq, tm), lambda b, i: (b, 0, i)),
            pl.BlockSpec((1, A_KV_HEADS, tm, HEAD_DIM), lambda b, i: (b, 0, i, 0)),
            pl.BlockSpec((1, A_KV_HEADS, 1, V_ROWS, tm), lambda b, i: (b, 0, i, 0, 0)),
        ],
        out_shape=[
            jax.ShapeDtypeStruct((bsz, nq, seq), _BF16),
            jax.ShapeDtypeStruct((bsz, A_KV_HEADS, seq, HEAD_DIM), _BF16),
            jax.ShapeDtypeStruct((bsz, A_KV_HEADS, seq // tm, V_ROWS, tm), _BF16),
        ],
        compiler_params=_cparams(("parallel", "parallel")),
        name="proj_a",
    )(x, g, scale, shift, wqt, wk, wvt, gq_col, gk_t, cos2, sina, sinb, cost, sint, bd)


def _flash_kernel(qt_ref, k_ref, vt_ref, o_ref, s_ref, m_ref, acc_ref, *, tq, tk, nchunks):
    qt4 = jnp.concatenate([qt_ref[0, HEAD_DIM * hq:HEAD_DIM * (hq + 1), :] for hq in range(A_GROUP)],
                          axis=1)
    m_ref[...] = jnp.full(m_ref.shape, -jnp.inf, _F32)
    acc_ref[...] = jnp.zeros(acc_ref.shape, _F32)

    def scores(kc, slot):
        kb = k_ref[0, 0, pl.ds(pl.multiple_of(kc * tk, tk), tk), :]
        s_ref[slot] = jnp.dot(kb, qt4, preferred_element_type=_F32)

    def update(kc, slot):
        s = s_ref[slot]
        m_prev = m_ref[...]
        m_new = jnp.maximum(m_prev, jnp.max(s, axis=0, keepdims=True))
        alpha = jnp.exp2(m_prev - m_new)
        p = jnp.exp2(s - m_new).astype(_BF16)
        acc_ref[...] = alpha * acc_ref[...] + jnp.dot(vt_ref[0, 0, kc], p,
                                                      preferred_element_type=_F32)
        m_ref[...] = m_new

    def pair(kc, last):
        scores(kc + 1, 1)
        update(kc, 0)
        if not last:
            scores(kc + 2, 0)
        update(kc + 1, 1)

    scores(0, 0)

    def body(c2, carry):
        pair(2 * c2, False)
        return carry

    lax.fori_loop(0, nchunks // 2 - 1, body, 0)
    pair(nchunks - 2, True)
    o = acc_ref[0:HEAD_DIM, :] / acc_ref[HEAD_DIM:HEAD_DIM + 1, :]
    o = jnp.concatenate([o[:, tq * hq:tq * (hq + 1)] for hq in range(A_GROUP)], axis=0)
    o_ref[0] = o.T.astype(_BF16)


def _flash_a(qt, k, vt):
    bsz, nq, seq = qt.shape
    tq = min(FA_TQ, seq)
    tk = vt.shape[-1]
    nchunks = seq // tk
    assert nchunks % 2 == 0 and seq % tq == 0
    return pl.pallas_call(
        functools.partial(_flash_kernel, tq=tq, tk=tk, nchunks=nchunks),
        grid=(bsz, A_KV_HEADS, seq // tq),
        in_specs=[
            pl.BlockSpec((1, A_GROUP * HEAD_DIM, tq), lambda b, h, i: (b, h, i)),
            pl.BlockSpec((1, 1, seq, HEAD_DIM), lambda b, h, i: (b, h, 0, 0)),
            pl.BlockSpec((1, 1, nchunks, V_ROWS, tk), lambda b, h, i: (b, h, 0, 0, 0)),
        ],
        out_specs=pl.BlockSpec((1, tq, A_GROUP * HEAD_DIM), lambda b, h, i: (b, i, h)),
        out_shape=jax.ShapeDtypeStruct((bsz, seq, nq), _BF16),
        scratch_shapes=[
            pltpu.VMEM((2, tk, A_GROUP * tq), _F32),
            pltpu.VMEM((1, A_GROUP * tq), _F32),
            pltpu.VMEM((V_ROWS, A_GROUP * tq), _F32),
        ],
        compiler_params=_cparams(("parallel", "parallel", "parallel")),
        name="flash_a",
    )(qt, k, vt)


def _oproj_kernel(a_ref, w_ref, x_ref, gate_ref, o_ref):
    y = jnp.dot(a_ref[0], w_ref[...], preferred_element_type=_F32)
    o_ref[0] = x_ref[0] + gate_ref[0] * y


def _oproj(a, w, x, gate):
    bsz, seq, _ = x.shape
    tm = min(ROW_TILE, seq)
    return pl.pallas_call(
        _oproj_kernel,
        grid=(bsz, seq // tm),
        in_specs=[
            pl.BlockSpec((1, tm, D_MODEL), lambda b, i: (b, i, 0)),
            pl.BlockSpec((D_MODEL, D_MODEL), lambda b, i: (0, 0)),
            pl.BlockSpec((1, tm, D_MODEL), lambda b, i: (b, i, 0)),
            pl.BlockSpec((1, 1, D_MODEL), lambda b, i: (b, 0, 0)),
        ],
        out_specs=pl.BlockSpec((1, tm, D_MODEL), lambda b, i: (b, i, 0)),
        out_shape=jax.ShapeDtypeStruct(x.shape, _F32),
        compiler_params=_cparams(("parallel", "parallel")),
        name="oproj",
    )(a, w, x, gate)


def _proj_b_kernel(x_ref, g_ref, sc_ref, sh_ref, w_ref, gv_ref, bd_ref, o_ref, scr_ref, *, dil, tm):
    h = _rms_mod(x_ref[0], g_ref[...], sc_ref[0], sh_ref[0]).astype(_BF16)
    bd = bd_ref[...]
    acc = jnp.dot(h, w_ref[...], preferred_element_type=_F32)
    nqk = 2 * B_HEADS * HEAD_DIM
    n = acc.shape[1]

    def emit(col, y):
        if dil == 1:
            o_ref[0, 0, :, col:col + y.shape[1]] = y.astype(_BF16)
        else:
            for c in range(y.shape[1] // LANES):
                scr_ref[col // LANES + c] = y[:, LANES * c:LANES * (c + 1)]

    for c in range(nqk // 256):
        sl = slice(256 * c, 256 * (c + 1))
        emit(256 * c, _head_rms(acc[:, sl], bd, gv_ref[:, sl]))
    emit(nqk, acc[:, nqk:])
    if dil > 1:
        for r in range(dil):
            for c in range(n // LANES):
                o_ref[0, r, :, LANES * c:LANES * (c + 1)] = (
                    scr_ref[c, pl.ds(r, tm // dil, stride=dil), :].astype(_BF16))


def _proj_b(x, g, scale, shift, w, gvec, bd, dil):
    bsz, seq, _ = x.shape
    tm = min(ROW_TILE, seq)
    n = w.shape[1]
    const = lambda b, i: (0, 0)
    return pl.pallas_call(
        functools.partial(_proj_b_kernel, dil=dil, tm=tm),
        grid=(bsz, seq // tm),
        in_specs=[
            pl.BlockSpec((1, tm, D_MODEL), lambda b, i: (b, i, 0)),
            pl.BlockSpec((1, D_MODEL), const),
            pl.BlockSpec((1, 1, D_MODEL), lambda b, i: (b, 0, 0)),
            pl.BlockSpec((1, 1, D_MODEL), lambda b, i: (b, 0, 0)),
            pl.BlockSpec((D_MODEL, n), const),
            pl.BlockSpec((1, 2 * B_HEADS * HEAD_DIM), const),
            pl.BlockSpec((256, 256), const),
        ],
        out_specs=pl.BlockSpec((1, dil, tm // dil, n), lambda b, i: (b, 0, i, 0)),
        out_shape=jax.ShapeDtypeStruct((bsz, dil, seq // dil, n), _BF16),
        scratch_shapes=[pltpu.VMEM((n // LANES, tm, LANES), _F32)],
        compiler_params=_cparams(("parallel", "parallel")),
        name="proj_b",
    )(x, g, scale, shift, w, gvec, bd)


def _dilated_kernel(main_ref, kp_ref, vp_ref, kn_ref, vn_ref, bias_ref, o_ref, lse_ref,
                    kf_ref, vf_ref, *, tq, length):
    nh = B_HEADS * HEAD_DIM
    l0 = pl.program_id(2) * tq
    kf_ref[0:DIL_RADIUS] = kp_ref[0, 0]
    kf_ref[DIL_RADIUS:DIL_RADIUS + tq] = main_ref[0, 0, :, nh:2 * nh]
    kf_ref[DIL_RADIUS + tq:] = kn_ref[0, 0]
    vf_ref[0:DIL_RADIUS] = vp_ref[0, 0]
    vf_ref[DIL_RADIUS:DIL_RADIUS + tq] = main_ref[0, 0, :, 2 * nh:3 * nh]
    vf_ref[DIL_RADIUS + tq:] = vn_ref[0, 0]
    width = DIL_SUB + 2 * DIL_RADIUS
    low_half = lax.broadcasted_iota(jnp.int32, (DIL_SUB, LANES), 1) < HEAD_DIM
    col = lax.broadcasted_iota(jnp.int32, (DIL_SUB, width), 1)

    def sub_block(u, carry):
        r0 = pl.multiple_of(u * DIL_SUB, DIL_SUB)
        kpos = col + (l0 + r0 - DIL_RADIUS)
        valid = (kpos >= 0) & (kpos < length)
        for m in range(B_HEADS // 2):
            sl = slice(LANES * m, LANES * (m + 1))
            qp = main_ref[0, 0, pl.ds(r0, DIL_SUB), sl]
            kw = kf_ref[pl.ds(r0, width), sl]
            vw = vf_ref[pl.ds(r0, width), sl]
            res = []
            for hh in range(2):
                keep = low_half if hh == 0 else jnp.logical_not(low_half)
                qm = jnp.where(keep, qp, jnp.zeros_like(qp))
                s = lax.dot_general(qm, kw, _NT, preferred_element_type=_F32) + bias_ref[2 * m + hh]
                s = jnp.where(valid, s, NEG_INF)
                mx = jnp.max(s, axis=1, keepdims=True)
                p = jnp.exp(s - mx)
                den = jnp.sum(p, axis=1, keepdims=True)
                pv = jnp.dot(p.astype(_BF16), vw, preferred_element_type=_F32)
                res.append((pv / den, mx + jnp.log(den)))
            o_ref[0, 0, pl.ds(r0, DIL_SUB), sl] = jnp.where(low_half, res[0][0], res[1][0])
            lse_ref[0, 0, pl.ds(r0, DIL_SUB), sl] = jnp.where(
                low_half, jnp.broadcast_to(res[0][1], (DIL_SUB, LANES)),
                jnp.broadcast_to(res[1][1], (DIL_SUB, LANES)))
        return carry

    lax.fori_loop(0, tq // DIL_SUB, sub_block, 0)


def _dilated_attn(qkv, bias):
    bsz, dil, length, n = qkv.shape
    nh = B_HEADS * HEAD_DIM
    tq = min(DIL_TQ, length)
    nb = tq // DIL_RADIUS
    last = length // DIL_RADIUS - 1
    prev_ix = lambda i: jnp.maximum(i * nb - 1, 0)
    next_ix = lambda i: jnp.minimum((i + 1) * nb, last)
    halo = (1, 1, DIL_RADIUS, nh)
    width = DIL_SUB + 2 * DIL_RADIUS
    out = jax.ShapeDtypeStruct((bsz, dil, length, nh), _F32)
    return pl.pallas_call(
        functools.partial(_dilated_kernel, tq=tq, length=length),
        grid=(bsz, dil, length // tq),
        in_specs=[
            pl.BlockSpec((1, 1, tq, n), lambda b, r, i: (b, r, i, 0)),
            pl.BlockSpec(halo, lambda b, r, i: (b, r, prev_ix(i), 1)),
            pl.BlockSpec(halo, lambda b, r, i: (b, r, prev_ix(i), 2)),
            pl.BlockSpec(halo, lambda b, r, i: (b, r, next_ix(i), 1)),
            pl.BlockSpec(halo, lambda b, r, i: (b, r, next_ix(i), 2)),
            pl.BlockSpec((B_HEADS, DIL_SUB, width), lambda b, r, i: (0, 0, 0)),
        ],
        out_specs=[pl.BlockSpec((1, 1, tq, nh), lambda b, r, i: (b, r, i, 0)),
                   pl.BlockSpec((1, 1, tq, nh), lambda b, r, i: (b, r, i, 0))],
        out_shape=[out, out],
        scratch_shapes=[pltpu.VMEM((tq + 2 * DIL_RADIUS, nh), _BF16),
                        pltpu.VMEM((tq + 2 * DIL_RADIUS, nh), _BF16)],
        compiler_params=_cparams(("parallel", "parallel", "parallel")),
        name="dilated_attn",
    )(qkv, qkv, qkv, qkv, qkv, bias)


def _t5_bucket(rel):
    nb = NUM_BUCKETS // 2
    max_exact = nb // 2
    ret = (rel > 0).astype(np.int32) * nb
    n = np.abs(rel)
    large = max_exact + (np.log(np.maximum(n, 1) / max_exact) / np.log(MAX_DISTANCE / max_exact)
                         * (nb - max_exact)).astype(np.int32)
    large = np.minimum(large, nb - 1)
    return (ret + np.where(n < max_exact, n, large)).astype(np.int32)


def _dilated_bias(rel_bias_g, dil):
    width = DIL_SUB + 2 * DIL_RADIUS
    rel = np.arange(width)[None, :] - DIL_RADIUS - np.arange(DIL_SUB)[:, None]
    bucket = _t5_bucket(rel * dil)
    band = np.abs(rel) <= DIL_RADIUS
    onehot = (bucket.reshape(-1, 1) == np.arange(NUM_BUCKETS)[None, :]).astype(np.float32)
    bias = jnp.dot(jnp.asarray(onehot), rel_bias_g.astype(_F32), precision=lax.Precision.HIGHEST)
    bias = bias.reshape(DIL_SUB, width, B_HEADS).transpose(2, 0, 1)
    return jnp.where(jnp.asarray(band)[None], bias, NEG_INF)


def _merge_oproj_kernel(o0_ref, l0_ref, o1_ref, l1_ref, o2_ref, l2_ref, w_ref, x_ref, gate_ref,
                        out_ref, so1, sl1, so2, sl2, *, tm):
    d1 = DILATED_CONFIGS[1][1]
    d2 = DILATED_CONFIGS[2][1]
    nc = B_HEADS * HEAD_DIM // LANES

    def to_token_order(src_ref, scr_ref, dil):
        for r in range(dil):
            for c in range(nc):
                scr_ref[c, pl.ds(r, tm // dil, stride=dil), :] = (
                    src_ref[0, r, :, LANES * c:LANES * (c + 1)])
        return jnp.concatenate([scr_ref[c] for c in range(nc)], axis=1)

    o1 = to_token_order(o1_ref, so1, d1)
    l1 = to_token_order(l1_ref, sl1, d1)
    o2 = to_token_order(o2_ref, so2, d2)
    l2 = to_token_order(l2_ref, sl2, d2)
    l0 = l0_ref[0, 0]
    mx = jnp.maximum(jnp.maximum(l0, l1), l2)
    e0, e1, e2 = jnp.exp(l0 - mx), jnp.exp(l1 - mx), jnp.exp(l2 - mx)
    o = (e0 * o0_ref[0, 0] + e1 * o1 + e2 * o2) / (e0 + e1 + e2)
    y = jnp.dot(o.astype(_BF16), w_ref[...], preferred_element_type=_F32)
    out_ref[0] = x_ref[0] + gate_ref[0] * y


def _merge_oproj(outs, lses, w, x, gate):
    bsz, seq, _ = x.shape
    tm = min(ROUTE_TILE, seq)
    nh = B_HEADS * HEAD_DIM
    specs = []
    for (_, dil) in DILATED_CONFIGS:
        spec = pl.BlockSpec((1, dil, tm // dil, nh), lambda b, i: (b, 0, i, 0))
        specs += [spec, spec]
    return pl.pallas_call(
        functools.partial(_merge_oproj_kernel, tm=tm),
        grid=(bsz, seq // tm),
        in_specs=specs + [
            pl.BlockSpec((nh, D_MODEL), lambda b, i: (0, 0)),
            pl.BlockSpec((1, tm, D_MODEL), lambda b, i: (b, i, 0)),
            pl.BlockSpec((1, 1, D_MODEL), lambda b, i: (b, 0, 0)),
        ],
        out_specs=pl.BlockSpec((1, tm, D_MODEL), lambda b, i: (b, i, 0)),
        out_shape=jax.ShapeDtypeStruct(x.shape, _F32),
        scratch_shapes=[pltpu.VMEM((nh // LANES, tm, LANES), _F32)] * 4,
        compiler_params=_cparams(("parallel", "parallel")),
        name="merge_oproj",
    )(outs[0], lses[0], outs[1], lses[1], outs[2], lses[2], w, x, gate)


def _router_kernel(x_ref, g_ref, sc_ref, sh_ref, wr_ref, br_ref,
                   hf_ref, idx_ref, gate_ref, rank_ref, cnt_ref, carry_ref, tri_ref, *, tm):
    first = (pl.program_id(0) == 0) & (pl.program_id(1) == 0)

    @pl.when(first)
    def _():
        carry_ref[...] = jnp.zeros(carry_ref.shape, _F32)
        r = lax.broadcasted_iota(jnp.int32, (tm, tm), 0)
        c = lax.broadcasted_iota(jnp.int32, (tm, tm), 1)
        tri_ref[...] = (r < c).astype(_BF16)

    hf = _rms_mod(x_ref[0], g_ref[...], sc_ref[0], sh_ref[0])
    for j in range(ROW_WORDS):
        hf_ref[pl.ds(j, tm, stride=ROW_WORDS), :] = hf[:, LANES * j:LANES * (j + 1)]
    logits = lax.dot_general(wr_ref[...], hf, _NT, preferred_element_type=_F32,
                             precision=lax.Precision.HIGHEST) + br_ref[...]
    eio = lax.broadcasted_iota(jnp.int32, logits.shape, 0).astype(_F32)
    cur = logits
    vals, sel = [], []
    for _ in range(TOP_K):
        mx = jnp.max(cur, axis=0, keepdims=True)
        ix = jnp.min(jnp.where(cur == mx, eio, float(N_EXPERTS)), axis=0, keepdims=True)
        vals.append(mx)
        sel.append(ix)
        cur = jnp.where(eio == ix, -jnp.inf, cur)
    ex = [jnp.exp(v - vals[0]) for v in vals]
    den = ex[0] + ex[1] + ex[2] + ex[3]
    member = jnp.zeros(logits.shape, _F32)
    for ix in sel:
        member = member + (eio == ix).astype(_F32)
    before = carry_ref[:, 0:1] + jnp.dot(member.astype(_BF16), tri_ref[...],
                                         preferred_element_type=_F32)
    ranks = [jnp.sum(jnp.where(eio == ix, before, 0.0), axis=0, keepdims=True) for ix in sel]
    idx_ref[...] = jnp.concatenate(sel, axis=0).astype(jnp.int32)
    gate_ref[...] = jnp.concatenate([e / den for e in ex], axis=0)
    rank_ref[...] = jnp.concatenate(ranks, axis=0).astype(jnp.int32)
    carry_ref[...] = carry_ref[...] + jnp.sum(member, axis=1, keepdims=True)
    cnt_ref[...] = carry_ref[...]


def _router(x, g, scale, shift, wr_t, br):
    bsz, seq, _ = x.shape
    tm = min(ROUTE_TILE, seq)
    ns = seq // tm
    tok = bsz * seq
    const = lambda b, i: (0, 0)
    flat = lambda b, i: (0, b * ns + i)
    return pl.pallas_call(
        functools.partial(_router_kernel, tm=tm),
        grid=(bsz, ns),
        in_specs=[
            pl.BlockSpec((1, tm, D_MODEL), lambda b, i: (b, i, 0)),
            pl.BlockSpec((1, D_MODEL), const),
            pl.BlockSpec((1, 1, D_MODEL), lambda b, i: (b, 0, 0)),
            pl.BlockSpec((1, 1, D_MODEL), lambda b, i: (b, 0, 0)),
            pl.BlockSpec((N_EXPERTS, D_MODEL), const),
            pl.BlockSpec((N_EXPERTS, 1), const),
        ],
        out_specs=[
            pl.BlockSpec((tm * ROW_WORDS, LANES), lambda b, i: (b * ns + i, 0)),
            pl.BlockSpec((TOP_K, tm), flat),
            pl.BlockSpec((TOP_K, tm), flat),
            pl.BlockSpec((TOP_K, tm), flat),
            pl.BlockSpec((N_EXPERTS, LANES), const),
        ],
        out_shape=[
            jax.ShapeDtypeStruct((tok * ROW_WORDS, LANES), _F32),
            jax.ShapeDtypeStruct((TOP_K, tok), jnp.int32),
            jax.ShapeDtypeStruct((TOP_K, tok), _F32),
            jax.ShapeDtypeStruct((TOP_K, tok), jnp.int32),
            jax.ShapeDtypeStruct((N_EXPERTS, LANES), _F32),
        ],
        scratch_shapes=[pltpu.VMEM((N_EXPERTS, LANES), _F32), pltpu.VMEM((tm, tm), _BF16)],
        compiler_params=_cparams(("arbitrary", "arbitrary")),
        name="router",
    )(x, g, scale, shift, wr_t, br)


def _row_copy(src_ref, src_row, dst_ref, dst_row, sem):
    return pltpu.make_async_copy(
        src_ref.at[pl.ds(pl.multiple_of(src_row * ROW_WORDS, ROW_WORDS), ROW_WORDS), :],
        dst_ref.at[pl.ds(pl.multiple_of(dst_row * ROW_WORDS, ROW_WORDS), ROW_WORDS), :], sem)


def _dispatch_kernel(dest_hbm, hf_ref, init_ref, xs_ref, dest_smem, isem, sem, *, tm):
    del init_ref
    i = pl.program_id(0)
    n = tm * TOP_K
    cp = pltpu.make_async_copy(dest_hbm.at[pl.ds(pl.multiple_of(i * n, n), n)], dest_smem, isem)
    cp.start()
    cp.wait()

    def issue(r, carry):
        for k in range(TOP_K):
            _row_copy(hf_ref, r, xs_ref, dest_smem[r * TOP_K + k], sem).start(priority=k % 2)
        return carry

    lax.fori_loop(0, tm, issue, 0)

    def drain(r, carry):
        for k in range(TOP_K):
            _row_copy(hf_ref, 0, xs_ref, 0, sem).wait()
        return carry

    lax.fori_loop(0, tm, drain, 0)


def _dispatch(dest_flat, hf, n_rows):
    tok = hf.shape[0] // ROW_WORDS
    tm = min(DISPATCH_TILE, tok)
    assert tok % tm == 0 and (tm * TOP_K) % 1024 == 0
    init = jnp.zeros((n_rows * ROW_WORDS, LANES), _F32)
    return pl.pallas_call(
        functools.partial(_dispatch_kernel, tm=tm),
        grid=(tok // tm,),
        in_specs=[
            pl.BlockSpec(memory_space=pl.ANY),
            pl.BlockSpec((tm * ROW_WORDS, LANES), lambda i: (i, 0)),
            pl.BlockSpec(memory_space=pl.ANY),
        ],
        out_specs=pl.BlockSpec(memory_space=pl.ANY),
        out_shape=jax.ShapeDtypeStruct(init.shape, _F32),
        scratch_shapes=[pltpu.SMEM((tm * TOP_K,), jnp.int32),
                        pltpu.SemaphoreType.DMA, pltpu.SemaphoreType.DMA],
        input_output_aliases={2: 0},
        compiler_params=_cparams(("arbitrary",)),
        name="moe_dispatch",
    )(dest_flat, hf, init)


def _expert_kernel(be_ref, nu_ref, xs_ref, wgu_ref, bgu_ref, wdn_ref, bdn_ref, ys_ref, *, blk):
    del be_ref
    i = pl.program_id(0)
    dff = wdn_ref.shape[1]

    @pl.when(i < nu_ref[0])
    def _():
        x = jnp.concatenate([xs_ref[pl.ds(j, blk, stride=ROW_WORDS), :] for j in range(ROW_WORDS)],
                            axis=1).astype(_BF16)
        gu = jnp.dot(x, wgu_ref[0], preferred_element_type=_F32) + bgu_ref[0]
        g = jnp.minimum(gu[:, :dff], SWIGLU_LIMIT)
        u = jnp.clip(gu[:, dff:], -SWIGLU_LIMIT, SWIGLU_LIMIT)
        act = g * jax.nn.sigmoid(SWIGLU_ALPHA * g)
        mid = ((u + 1.0) * act).astype(_BF16)
        y = jnp.dot(mid, wdn_ref[0], preferred_element_type=_F32) + bdn_ref[0]
        for j in range(ROW_WORDS):
            ys_ref[pl.ds(j, blk, stride=ROW_WORDS), :] = y[:, LANES * j:LANES * (j + 1)]

    @pl.when(i >= nu_ref[0])
    def _():
        ys_ref[...] = jnp.zeros(ys_ref.shape, _F32)


def _experts(blk_e, n_used, xs, wgu, bgu, wdn, bdn):
    blk = EXPERT_BLOCK
    n_blk = xs.shape[0] // (blk * ROW_WORDS)
    dff = wdn.shape[1]
    grid_spec = pltpu.PrefetchScalarGridSpec(
        num_scalar_prefetch=2,
        grid=(n_blk,),
        in_specs=[
            pl.BlockSpec((blk * ROW_WORDS, LANES), lambda i, be, nu: (i, 0)),
            pl.BlockSpec((1, D_MODEL, 2 * dff), lambda i, be, nu: (be[i], 0, 0)),
            pl.BlockSpec((1, 1, 2 * dff), lambda i, be, nu: (be[i], 0, 0)),
            pl.BlockSpec((1, dff, D_MODEL), lambda i, be, nu: (be[i], 0, 0)),
            pl.BlockSpec((1, 1, D_MODEL), lambda i, be, nu: (be[i], 0, 0)),
        ],
        out_specs=pl.BlockSpec((blk * ROW_WORDS, LANES), lambda i, be, nu: (i, 0)),
    )
    return pl.pallas_call(
        functools.partial(_expert_kernel, blk=blk),
        grid_spec=grid_spec,
        out_shape=jax.ShapeDtypeStruct(xs.shape, _F32),
        compiler_params=_cparams(("arbitrary",)),
        name="moe_experts",
    )(blk_e, n_used, xs, wgu, bgu, wdn, bdn)


def _combine_kernel(dest_hbm, ys_ref, gate_ref, x_ref, gf_ref, o_ref,
                    dest_smem, buf_ref, isem, sem, *, tm, ns, nsteps):
    g = pl.program_id(0) * ns + pl.program_id(1)
    n = tm * TOP_K

    def start_gathers(step, slot):
        off = pl.multiple_of(step * n, n)
        cp = pltpu.make_async_copy(dest_hbm.at[pl.ds(off, n)], dest_smem.at[slot], isem)
        cp.start()
        cp.wait()

        def issue(r, carry):
            for k in range(TOP_K):
                _row_copy(ys_ref, dest_smem[slot, r * TOP_K + k], buf_ref.at[slot], k * tm + r,
                          sem.at[slot]).start(priority=k % 2)
            return carry

        lax.fori_loop(0, tm, issue, 0)

    def finish(slot):
        def drain(r, carry):
            for k in range(TOP_K):
                _row_copy(ys_ref, 0, buf_ref.at[slot], 0, sem.at[slot]).wait()
            return carry

        lax.fori_loop(0, tm, drain, 0)
        gate = gate_ref[...]
        cols = []
        for j in range(ROW_WORDS):
            acc = None
            for k in range(TOP_K):
                slab = buf_ref[slot, pl.ds(k * tm * ROW_WORDS + j, tm, stride=ROW_WORDS), :]
                term = gate[:, k:k + 1] * slab
                acc = term if acc is None else acc + term
            cols.append(acc)
        moe = jnp.concatenate(cols, axis=1)
        o_ref[0] = x_ref[0] + gf_ref[0] * moe

    @pl.when(g == 0)
    def _():
        start_gathers(0, 0)

    for slot in range(2):
        @pl.when(g % 2 == slot)
        def _():
            @pl.when(g + 1 < nsteps)
            def _():
                start_gathers(g + 1, 1 - slot)

            finish(slot)


def _combine(dest_flat, gate_tok, ys, x, gate_f):
    bsz, seq, _ = x.shape
    tm = min(ROUTE_TILE, seq)
    ns = seq // tm
    return pl.pallas_call(
        functools.partial(_combine_kernel, tm=tm, ns=ns, nsteps=bsz * ns),
        grid=(bsz, ns),
        in_specs=[
            pl.BlockSpec(memory_space=pl.ANY),
            pl.BlockSpec(memory_space=pl.ANY),
            pl.BlockSpec((tm, TOP_K), lambda b, i: (b * ns + i, 0)),
            pl.BlockSpec((1, tm, D_MODEL), lambda b, i: (b, i, 0)),
            pl.BlockSpec((1, 1, D_MODEL), lambda b, i: (b, 0, 0)),
        ],
        out_specs=pl.BlockSpec((1, tm, D_MODEL), lambda b, i: (b, i, 0)),
        out_shape=jax.ShapeDtypeStruct(x.shape, _F32),
        scratch_shapes=[
            pltpu.SMEM((2, tm * TOP_K), jnp.int32),
            pltpu.VMEM((2, TOP_K * tm * ROW_WORDS, LANES), _F32),
            pltpu.SemaphoreType.DMA,
            pltpu.SemaphoreType.DMA((2,)),
        ],
        compiler_params=_cparams(("arbitrary", "arbitrary")),
        name="moe_combine",
    )(dest_flat, ys, gate_tok, x, gate_f)


def _moe(x, g, scale, shift, gate_f, wr_t, br, wgu, bgu, wdn, bdn):
    bsz, seq, _ = x.shape
    tok = bsz * seq
    blk = EXPERT_BLOCK
    hf, idx, gate, rank, cnt = _router(x, g, scale, shift, wr_t, br)
    counts = cnt[:, 0].astype(jnp.int32)
    padded = (counts + blk - 1) // blk * blk
    pend = jnp.cumsum(padded)
    pstart = pend - padded
    eids = jnp.arange(N_EXPERTS, dtype=jnp.int32)[:, None, None]
    base = jnp.sum(jnp.where(idx[None] == eids, pstart[:, None, None], 0), axis=0)
    dest = (base + rank).T.reshape(tok * TOP_K)
    gate_tok = gate.T
    n_blk = (tok * TOP_K + N_EXPERTS * (blk - 1) + blk - 1) // blk
    blk_start = jnp.arange(n_blk, dtype=jnp.int32) * blk
    blk_e = jnp.minimum(jnp.sum(blk_start[:, None] >= pend[None, :], axis=1),
                        N_EXPERTS - 1).astype(jnp.int32)
    n_used = (pend[-1:] // blk).astype(jnp.int32)
    xs = _dispatch(dest, hf, n_blk * blk)
    ys = _experts(blk_e, n_used, xs, wgu, bgu, wdn, bdn)
    return _combine(dest, gate_tok, ys, x, gate_f)


def _prepare(norm_mix_g, norm_ffn_g, w_qkv_a, q_norm_a, k_norm_a, w_o_a, w_qkv_b, q_norm_b, k_norm_b,
             w_o_b, rel_bias, w_router, b_router, w_gate_up, b_gate_up, w_down, b_down):
    nq = A_HEADS * HEAD_DIM
    nk = A_KV_HEADS * HEAD_DIM
    nh = B_HEADS * HEAD_DIM
    scale = HEAD_DIM ** -0.5
    p = {}
    wa = w_qkv_a[0]
    p["wqt_a"] = wa[:, :nq].T.astype(_BF16)
    p["wk_a"] = wa[:, nq:nq + nk].astype(_BF16)
    p["wvt_a"] = wa[:, nq + nk:].T.astype(_BF16)
    p["gq_a"] = (q_norm_a[0] * (scale * np.log2(np.e))).reshape(HEAD_DIM, 1)
    p["gk_a"] = jnp.tile(k_norm_a[0], A_KV_HEADS).reshape(1, nk)
    p["wo_a"] = w_o_a[0].astype(_BF16)
    wb = w_qkv_b[0].astype(_BF16)
    p["w_b"] = [wb[:, 3 * nh * gi:3 * nh * (gi + 1)] for gi in range(len(DILATED_CONFIGS))]
    p["gv_b"] = [jnp.concatenate([jnp.tile(q_norm_b[0, gi], B_HEADS) * scale,
                                  jnp.tile(k_norm_b[0, gi], B_HEADS)]).reshape(1, 2 * nh)
                 for gi in range(len(DILATED_CONFIGS))]
    p["bias_b"] = [_dilated_bias(rel_bias[:, gi * B_HEADS:(gi + 1) * B_HEADS], dil)
                   for gi, (_, dil) in enumerate(DILATED_CONFIGS)]
    p["wo_b"] = w_o_b[0].astype(_BF16)
    p["g_mix"] = [norm_mix_g[i].reshape(1, D_MODEL) for i in range(DEPTH)]
    p["g_ffn"] = [norm_ffn_g[i].reshape(1, D_MODEL) for i in range(DEPTH)]
    p["wr_t"] = [w_router[i].T for i in range(DEPTH)]
    p["br"] = [b_router[i].reshape(N_EXPERTS, 1) for i in range(DEPTH)]
    p["wgu"] = [w_gate_up[i].astype(_BF16) for i in range(DEPTH)]
    p["bgu"] = [b_gate_up[i].reshape(N_EXPERTS, 1, -1) for i in range(DEPTH)]
    p["wdn"] = [w_down[i].astype(_BF16) for i in range(DEPTH)]
    p["bdn"] = [b_down[i].reshape(N_EXPERTS, 1, -1) for i in range(DEPTH)]
    p["bd"] = _block_diag_ones()
    return p


def _trunk(x, c, w_ada, b_ada, p):
    bsz, seq, _ = x.shape
    tables = _rope_tables(seq)
    for i in range(DEPTH):
        mod = _ada_mod(c, w_ada[i], b_ada[i])
        shift_m, scale_m, gate_m, shift_f, scale_f, gate_f = [
            mod[:, D_MODEL * k:D_MODEL * (k + 1)].reshape(bsz, 1, D_MODEL) for k in range(6)]
        if i % 2 == 0:
            qt, k, vt = _proj_a(x, p["g_mix"][i], scale_m, shift_m, p["wqt_a"], p["wk_a"],
                                p["wvt_a"], p["gq_a"], p["gk_a"], tables, p["bd"])
            a = _flash_a(qt, k, vt)
            x = _oproj(a, p["wo_a"], x, gate_m)
        else:
            outs, lses = [], []
            for gi, (_, dil) in enumerate(DILATED_CONFIGS):
                qkv = _proj_b(x, p["g_mix"][i], scale_m, shift_m, p["w_b"][gi], p["gv_b"][gi],
                              p["bd"], dil)
                o, lse = _dilated_attn(qkv, p["bias_b"][gi])
                outs.append(o)
                lses.append(lse)
            x = _merge_oproj(outs, lses, p["wo_b"], x, gate_m)
        x = _moe(x, p["g_ffn"][i], scale_f, shift_f, gate_f, p["wr_t"][i], p["br"][i],
                 p["wgu"][i], p["bgu"][i], p["wdn"][i], p["bdn"][i])
    return x


def kernel(x_prompt, x_sample, c_prompt, c_sample, norm_mix_g, norm_ffn_g, w_ada, b_ada, w_qkv_a,
           q_norm_a, k_norm_a, w_o_a, w_qkv_b, q_norm_b, k_norm_b, w_o_b, rel_bias, w_router,
           b_router, w_gate_up, b_gate_up, w_down, b_down):
    p = _prepare(norm_mix_g, norm_ffn_g, w_qkv_a, q_norm_a, k_norm_a, w_o_a, w_qkv_b, q_norm_b,
                 k_norm_b, w_o_b, rel_bias, w_router, b_router, w_gate_up, b_gate_up, w_down, b_down)
    y_prompt = _trunk(x_prompt, c_prompt, w_ada, b_ada, p)
    y_sample = _trunk(x_sample, c_sample, w_ada, b_ada, p)
    return (y_prompt, y_sample)
```

```python
import functools

import numpy as np
import jax
import jax.numpy as jnp
from jax import lax
from jax.experimental import pallas as pl
from jax.experimental.pallas import tpu as pltpu

D_MODEL = 1024
HEAD_DIM = 64
A_HEADS = 16
A_KV_HEADS = 4
A_GROUP = A_HEADS // A_KV_HEADS
GRID_W = 64
ROPE_THETA = 10000.0
B_HEADS = 16
DILATED_CONFIGS = ((128, 1), (512, 4), (2048, 16))
NUM_BUCKETS = 32
MAX_DISTANCE = 1024
N_EXPERTS = 32
TOP_K = 4
SWIGLU_ALPHA = 1.702
SWIGLU_LIMIT = 7.0
RMS_EPS = 1e-6
NEG_INF = -1e30
DEPTH = 2

LANES = 128
SUBLANES = 8
VMEM_LIMIT = 56 * 1024 * 1024

ROW_TILE = 512
FA_TQ = 256
FA_TK = 512
FA_UNROLL = 8
V_ROWS = HEAD_DIM + 16
DIL_TQ = 512
DIL_SUB = 128
DIL_RADIUS = 64
ROUTE_TILE = 256
DISPATCH_TILE = 256
EXPERT_BLOCK = 256
ROW_WORDS = D_MODEL // LANES

_F32 = jnp.float32
_BF16 = jnp.bfloat16
_NT = (((1,), (1,)), ((), ()))


def _cparams(sem):
    return pltpu.CompilerParams(dimension_semantics=sem, vmem_limit_bytes=VMEM_LIMIT)


def _rms_mod(x, g, scale, shift):
    var = jnp.mean(x * x, axis=-1, keepdims=True)
    y = x * lax.rsqrt(var + RMS_EPS) * g
    return y * (1.0 + scale) + shift


def _head_rms(y, bd, gvec):
    ss = jnp.dot((y * y).astype(_BF16), bd, preferred_element_type=_F32)
    return y * lax.rsqrt(ss * (1.0 / HEAD_DIM) + RMS_EPS) * gvec


def _ada_kernel(c_ref, w_ref, b_ref, o_ref):
    c = c_ref[...]
    s = c * jax.nn.sigmoid(c)
    o_ref[...] = jnp.dot(s, w_ref[...], preferred_element_type=_F32,
                         precision=lax.Precision.HIGHEST) + b_ref[...]


def _ada_mod(c, w, b):
    bsz = c.shape[0]
    cp = jnp.pad(c, ((0, SUBLANES - bsz), (0, 0)))
    n = w.shape[1]
    out = pl.pallas_call(
        _ada_kernel,
        grid=(n // D_MODEL,),
        in_specs=[pl.BlockSpec((SUBLANES, D_MODEL), lambda j: (0, 0)),
                  pl.BlockSpec((D_MODEL, D_MODEL), lambda j: (0, j)),
                  pl.BlockSpec((1, D_MODEL), lambda j: (0, j))],
        out_specs=pl.BlockSpec((SUBLANES, D_MODEL), lambda j: (0, j)),
        out_shape=jax.ShapeDtypeStruct((SUBLANES, n), _F32),
        compiler_params=_cparams(("arbitrary",)),
        name="ada_mod",
    )(cp, w, b.reshape(1, n))
    return out[:bsz]


def _proj_a_kernel(x_ref, g_ref, sc_ref, sh_ref, wqt_ref, wk_ref, wvt_ref, gq_ref, gk_ref,
                   cos_ref, sina_ref, sinb_ref, cost_ref, sint_ref, bd_ref,
                   qt_ref, k_ref, vt_ref):
    h = _rms_mod(x_ref[0], g_ref[...], sc_ref[0], sh_ref[0]).astype(_BF16)
    acct = lax.dot_general(wqt_ref[...], h, _NT, preferred_element_type=_F32)
    cost = cost_ref[...]
    sint = sint_ref[...]
    gq = gq_ref[...]
    for j in range(A_HEADS):
        slab = acct[HEAD_DIM * j:HEAD_DIM * (j + 1), :]
        ss = jnp.sum(slab * slab, axis=0, keepdims=True)
        y = slab * lax.rsqrt(ss * (1.0 / HEAD_DIM) + RMS_EPS) * gq
        rot = jnp.concatenate([-y[16:32], y[0:16], -y[48:64], y[32:48]], axis=0)
        qt_ref[0, HEAD_DIM * j:HEAD_DIM * (j + 1), :] = (y * cost + rot * sint).astype(_BF16)
    acck = jnp.dot(h, wk_ref[...], preferred_element_type=_F32)
    y = _head_rms(acck, bd_ref[...], gk_ref[...])
    cos = cos_ref[...]
    sina = sina_ref[...]
    sinb = sinb_ref[...]
    for hh in range(2):
        yy = y[:, LANES * hh:LANES * (hh + 1)]
        r = yy * cos + pltpu.roll(yy, LANES - 16, 1) * sina + pltpu.roll(yy, 16, 1) * sinb
        for e in range(2):
            k_ref[0, 2 * hh + e] = r[:, HEAD_DIM * e:HEAD_DIM * (e + 1)].astype(_BF16)
    accv = lax.dot_general(wvt_ref[...], h, _NT, preferred_element_type=_F32)
    ones = jnp.ones((V_ROWS - HEAD_DIM, accv.shape[1]), _BF16)
    for j in range(A_KV_HEADS):
        vt_ref[0, j, 0, 0:HEAD_DIM, :] = accv[HEAD_DIM * j:HEAD_DIM * (j + 1), :].astype(_BF16)
        vt_ref[0, j, 0, HEAD_DIM:V_ROWS, :] = ones


def _rope_tables(seq_len):
    rows = seq_len // GRID_W
    row = jnp.repeat(jnp.arange(rows, dtype=_F32), GRID_W)
    col = jnp.tile(jnp.arange(GRID_W, dtype=_F32), rows)
    n_freq = HEAD_DIM // 4
    inv_freq = 1.0 / (ROPE_THETA ** (jnp.arange(n_freq, dtype=_F32) / n_freq))
    ang_r = row[:, None] * inv_freq[None, :]
    ang_c = col[:, None] * inv_freq[None, :]
    ang = jnp.concatenate([ang_r, ang_r, ang_c, ang_c], axis=-1)
    cos, sin = jnp.cos(ang), jnp.sin(ang)
    first = (np.arange(HEAD_DIM) % 32) < 16
    cos2 = jnp.tile(cos, (1, 2))
    sina = jnp.tile(jnp.where(first[None, :], -sin, 0.0), (1, 2))
    sinb = jnp.tile(jnp.where(first[None, :], 0.0, sin), (1, 2))
    return cos2, sina, sinb, cos.T, sin.T


def _block_diag_ones():
    i = np.arange(256)
    return jnp.asarray((i[:, None] // HEAD_DIM) == (i[None, :] // HEAD_DIM), dtype=_BF16)


def _proj_a(x, g, scale, shift, wqt, wk, wvt, gq_col, gk_t, tables, bd):
    bsz, seq, _ = x.shape
    tm = FA_TK
    assert seq % tm == 0
    cos2, sina, sinb, cost, sint = tables
    nq = A_HEADS * HEAD_DIM
    nk = A_KV_HEADS * HEAD_DIM
    const = lambda b, i: (0, 0)
    return pl.pallas_call(
        _proj_a_kernel,
        grid=(bsz, seq // tm),
        in_specs=[
            pl.BlockSpec((1, tm, D_MODEL), lambda b, i: (b, i, 0)),
            pl.BlockSpec((1, D_MODEL), const),
            pl.BlockSpec((1, 1, D_MODEL), lambda b, i: (b, 0, 0)),
            pl.BlockSpec((1, 1, D_MODEL), lambda b, i: (b, 0, 0)),
            pl.BlockSpec((nq, D_MODEL), const),
            pl.BlockSpec((D_MODEL, nk), const),
            pl.BlockSpec((nk, D_MODEL), const),
            pl.BlockSpec((HEAD_DIM, 1), const),
            pl.BlockSpec((1, nk), const),
            pl.BlockSpec((tm, LANES), lambda b, i: (i, 0)),
            pl.BlockSpec((tm, LANES), lambda b, i: (i, 0)),
            pl.BlockSpec((tm, LANES), lambda b, i: (i, 0)),
            pl.BlockSpec((HEAD_DIM, tm), lambda b, i: (0, i)),
            pl.BlockSpec((HEAD_DIM, tm), lambda b, i: (0, i)),
            pl.BlockSpec((256, 256), const),
        ],
        out_specs=[
            pl.BlockSpec((1, nq, tm), lambda b, i: (b, 0, i)),
            pl.BlockSpec((1, A_KV_HEADS, tm, HEAD_DIM), lambda b, i: (b, 0, i, 0)),
            pl.BlockSpec((1, A_KV_HEADS, 1, V_ROWS, tm), lambda b, i: (b, 0, i, 0, 0)),
        ],
        out_shape=[
            jax.ShapeDtypeStruct((bsz, nq, seq), _BF16),
            jax.ShapeDtypeStruct((bsz, A_KV_HEADS, seq, HEAD_DIM), _BF16),
            jax.ShapeDtypeStruct((bsz, A_KV_HEADS, seq // tm, V_ROWS, tm), _BF16),
        ],
        compiler_params=_cparams(("parallel", "parallel")),
        name="proj_a",
    )(x, g, scale, shift, wqt, wk, wvt, gq_col, gk_t, cos2, sina, sinb, cost, sint, bd)


def _flash_kernel(qt_ref, k_ref, vt_ref, o_ref, s_ref, m_ref, acc_ref, *, tq, tk, nchunks):
    qt4 = jnp.concatenate([qt_ref[0, HEAD_DIM * hq:HEAD_DIM * (hq + 1), :] for hq in range(A_GROUP)],
                          axis=1)
    m_ref[...] = jnp.full(m_ref.shape, -jnp.inf, _F32)
    acc_ref[...] = jnp.zeros(acc_ref.shape, _F32)

    def scores(kc, slot):
        kb = k_ref[0, 0, pl.ds(pl.multiple_of(kc * tk, tk), tk), :]
        s_ref[slot] = jnp.dot(kb, qt4, preferred_element_type=_F32)

    def step(kc, slot, kc_next, slot_next):
        vb = vt_ref[0, 0, kc]
        if kc_next is not None:
            kb = k_ref[0, 0, pl.ds(pl.multiple_of(kc_next * tk, tk), tk), :]
        for hq in range(A_GROUP):
            sl = slice(tq * hq, tq * (hq + 1))
            if kc_next is not None:
                s_ref[slot_next, :, sl] = jnp.dot(kb, qt4[:, sl], preferred_element_type=_F32)
            s = s_ref[slot, :, sl]
            m_prev = m_ref[:, sl]
            m_new = jnp.maximum(m_prev, jnp.max(s, axis=0, keepdims=True))
            alpha = jnp.exp2(m_prev - m_new)
            p = jnp.exp2(s - m_new).astype(_BF16)
            acc_ref[:, sl] = alpha * acc_ref[:, sl] + jnp.dot(vb, p, preferred_element_type=_F32)
            m_ref[:, sl] = m_new

    def run(kc0, last):
        for j in range(FA_UNROLL):
            final = last and j == FA_UNROLL - 1
            step(kc0 + j, j % 2, None if final else kc0 + j + 1, (j + 1) % 2)

    scores(0, 0)

    def body(c, carry):
        run(FA_UNROLL * c, False)
        return carry

    lax.fori_loop(0, nchunks // FA_UNROLL - 1, body, 0)
    run(nchunks - FA_UNROLL, True)
    o = acc_ref[0:HEAD_DIM, :] / acc_ref[HEAD_DIM:HEAD_DIM + 1, :]
    o = jnp.concatenate([o[:, tq * hq:tq * (hq + 1)] for hq in range(A_GROUP)], axis=0)
    o_ref[0] = o.T.astype(_BF16)


def _flash_a(qt, k, vt):
    bsz, nq, seq = qt.shape
    tq = min(FA_TQ, seq)
    tk = vt.shape[-1]
    nchunks = seq // tk
    assert nchunks % FA_UNROLL == 0 and FA_UNROLL % 2 == 0 and seq % tq == 0
    return pl.pallas_call(
        functools.partial(_flash_kernel, tq=tq, tk=tk, nchunks=nchunks),
        grid=(bsz, A_KV_HEADS, seq // tq),
        in_specs=[
            pl.BlockSpec((1, A_GROUP * HEAD_DIM, tq), lambda b, h, i: (b, h, i)),
            pl.BlockSpec((1, 1, seq, HEAD_DIM), lambda b, h, i: (b, h, 0, 0)),
            pl.BlockSpec((1, 1, nchunks, V_ROWS, tk), lambda b, h, i: (b, h, 0, 0, 0)),
        ],
        out_specs=pl.BlockSpec((1, tq, A_GROUP * HEAD_DIM), lambda b, h, i: (b, i, h)),
        out_shape=jax.ShapeDtypeStruct((bsz, seq, nq), _BF16),
        scratch_shapes=[
            pltpu.VMEM((2, tk, A_GROUP * tq), _F32),
            pltpu.VMEM((1, A_GROUP * tq), _F32),
            pltpu.VMEM((V_ROWS, A_GROUP * tq), _F32),
        ],
        compiler_params=_cparams(("parallel", "parallel", "parallel")),
        name="flash_a",
    )(qt, k, vt)


def _oproj_kernel(a_ref, w_ref, x_ref, gate_ref, o_ref):
    y = jnp.dot(a_ref[0], w_ref[...], preferred_element_type=_F32)
    o_ref[0] = x_ref[0] + gate_ref[0] * y


def _oproj(a, w, x, gate):
    bsz, seq, _ = x.shape
    tm = min(ROW_TILE, seq)
    return pl.pallas_call(
        _oproj_kernel,
        grid=(bsz, seq // tm),
        in_specs=[
            pl.BlockSpec((1, tm, D_MODEL), lambda b, i: (b, i, 0)),
            pl.BlockSpec((D_MODEL, D_MODEL), lambda b, i: (0, 0)),
            pl.BlockSpec((1, tm, D_MODEL), lambda b, i: (b, i, 0)),
            pl.BlockSpec((1, 1, D_MODEL), lambda b, i: (b, 0, 0)),
        ],
        out_specs=pl.BlockSpec((1, tm, D_MODEL), lambda b, i: (b, i, 0)),
        out_shape=jax.ShapeDtypeStruct(x.shape, _F32),
        compiler_params=_cparams(("parallel", "parallel")),
        name="oproj",
    )(a, w, x, gate)


def _proj_b_kernel(x_ref, g_ref, sc_ref, sh_ref, w_ref, gv_ref, bd_ref, o_ref, scr_ref, *, dil, tm):
    h = _rms_mod(x_ref[0], g_ref[...], sc_ref[0], sh_ref[0]).astype(_BF16)
    bd = bd_ref[...]
    acc = jnp.dot(h, w_ref[...], preferred_element_type=_F32)
    nqk = 2 * B_HEADS * HEAD_DIM
    n = acc.shape[1]

    def emit(col, y):
        if dil == 1:
            o_ref[0, 0, :, col:col + y.shape[1]] = y.astype(_BF16)
        else:
            for c in range(y.shape[1] // LANES):
                scr_ref[col // LANES + c] = y[:, LANES * c:LANES * (c + 1)]

    for c in range(nqk // 256):
        sl = slice(256 * c, 256 * (c + 1))
        emit(256 * c, _head_rms(acc[:, sl], bd, gv_ref[:, sl]))
    emit(nqk, acc[:, nqk:])
    if dil > 1:
        for r in range(dil):
            for c in range(n // LANES):
                o_ref[0, r, :, LANES * c:LANES * (c + 1)] = (
                    scr_ref[c, pl.ds(r, tm // dil, stride=dil), :].astype(_BF16))


def _proj_b(x, g, scale, shift, w, gvec, bd, dil):
    bsz, seq, _ = x.shape
    tm = min(ROW_TILE, seq)
    n = w.shape[1]
    const = lambda b, i: (0, 0)
    return pl.pallas_call(
        functools.partial(_proj_b_kernel, dil=dil, tm=tm),
        grid=(bsz, seq // tm),
        in_specs=[
            pl.BlockSpec((1, tm, D_MODEL), lambda b, i: (b, i, 0)),
            pl.BlockSpec((1, D_MODEL), const),
            pl.BlockSpec((1, 1, D_MODEL), lambda b, i: (b, 0, 0)),
            pl.BlockSpec((1, 1, D_MODEL), lambda b, i: (b, 0, 0)),
            pl.BlockSpec((D_MODEL, n), const),
            pl.BlockSpec((1, 2 * B_HEADS * HEAD_DIM), const),
            pl.BlockSpec((256, 256), const),
        ],
        out_specs=pl.BlockSpec((1, dil, tm // dil, n), lambda b, i: (b, 0, i, 0)),
        out_shape=jax.ShapeDtypeStruct((bsz, dil, seq // dil, n), _BF16),
        scratch_shapes=[pltpu.VMEM((n // LANES, tm, LANES), _F32)],
        compiler_params=_cparams(("parallel", "parallel")),
        name="proj_b",
    )(x, g, scale, shift, w, gvec, bd)


def _dilated_kernel(main_ref, kp_ref, vp_ref, kn_ref, vn_ref, bias_ref, o_ref, lse_ref,
                    kf_ref, vf_ref, *, tq, length):
    nh = B_HEADS * HEAD_DIM
    l0 = pl.program_id(2) * tq
    kf_ref[0:DIL_RADIUS] = kp_ref[0, 0]
    kf_ref[DIL_RADIUS:DIL_RADIUS + tq] = main_ref[0, 0, :, nh:2 * nh]
    kf_ref[DIL_RADIUS + tq:] = kn_ref[0, 0]
    vf_ref[0:DIL_RADIUS] = vp_ref[0, 0]
    vf_ref[DIL_RADIUS:DIL_RADIUS + tq] = main_ref[0, 0, :, 2 * nh:3 * nh]
    vf_ref[DIL_RADIUS + tq:] = vn_ref[0, 0]
    width = DIL_SUB + 2 * DIL_RADIUS
    low_half = lax.broadcasted_iota(jnp.int32, (DIL_SUB, LANES), 1) < HEAD_DIM
    col = lax.broadcasted_iota(jnp.int32, (DIL_SUB, width), 1)

    def sub_block(u, carry):
        r0 = pl.multiple_of(u * DIL_SUB, DIL_SUB)
        kpos = col + (l0 + r0 - DIL_RADIUS)
        valid = (kpos >= 0) & (kpos < length)
        for m in range(B_HEADS // 2):
            sl = slice(LANES * m, LANES * (m + 1))
            qp = main_ref[0, 0, pl.ds(r0, DIL_SUB), sl]
            kw = kf_ref[pl.ds(r0, width), sl]
            vw = vf_ref[pl.ds(r0, width), sl]
            res = []
            for hh in range(2):
                keep = low_half if hh == 0 else jnp.logical_not(low_half)
                qm = jnp.where(keep, qp, jnp.zeros_like(qp))
                s = lax.dot_general(qm, kw, _NT, preferred_element_type=_F32) + bias_ref[2 * m + hh]
                s = jnp.where(valid, s, NEG_INF)
                mx = jnp.max(s, axis=1, keepdims=True)
                p = jnp.exp(s - mx)
                den = jnp.sum(p, axis=1, keepdims=True)
                pv = jnp.dot(p.astype(_BF16), vw, preferred_element_type=_F32)
                res.append((pv / den, mx + jnp.log(den)))
            o_ref[0, 0, pl.ds(r0, DIL_SUB), sl] = jnp.where(low_half, res[0][0], res[1][0])
            lse_ref[0, 0, pl.ds(r0, DIL_SUB), sl] = jnp.where(
                low_half, jnp.broadcast_to(res[0][1], (DIL_SUB, LANES)),
                jnp.broadcast_to(res[1][1], (DIL_SUB, LANES)))
        return carry

    lax.fori_loop(0, tq // DIL_SUB, sub_block, 0)


def _dilated_attn(qkv, bias):
    bsz, dil, length, n = qkv.shape
    nh = B_HEADS * HEAD_DIM
    tq = min(DIL_TQ, length)
    nb = tq // DIL_RADIUS
    last = length // DIL_RADIUS - 1
    prev_ix = lambda i: jnp.maximum(i * nb - 1, 0)
    next_ix = lambda i: jnp.minimum((i + 1) * nb, last)
    halo = (1, 1, DIL_RADIUS, nh)
    width = DIL_SUB + 2 * DIL_RADIUS
    out = jax.ShapeDtypeStruct((bsz, dil, length, nh), _F32)
    return pl.pallas_call(
        functools.partial(_dilated_kernel, tq=tq, length=length),
        grid=(bsz, dil, length // tq),
        in_specs=[
            pl.BlockSpec((1, 1, tq, n), lambda b, r, i: (b, r, i, 0)),
            pl.BlockSpec(halo, lambda b, r, i: (b, r, prev_ix(i), 1)),
            pl.BlockSpec(halo, lambda b, r, i: (b, r, prev_ix(i), 2)),
            pl.BlockSpec(halo, lambda b, r, i: (b, r, next_ix(i), 1)),
            pl.BlockSpec(halo, lambda b, r, i: (b, r, next_ix(i), 2)),
            pl.BlockSpec((B_HEADS, DIL_SUB, width), lambda b, r, i: (0, 0, 0)),
        ],
        out_specs=[pl.BlockSpec((1, 1, tq, nh), lambda b, r, i: (b, r, i, 0)),
                   pl.BlockSpec((1, 1, tq, nh), lambda b, r, i: (b, r, i, 0))],
        out_shape=[out, out],
        scratch_shapes=[pltpu.VMEM((tq + 2 * DIL_RADIUS, nh), _BF16),
                        pltpu.VMEM((tq + 2 * DIL_RADIUS, nh), _BF16)],
        compiler_params=_cparams(("parallel", "parallel", "parallel")),
        name="dilated_attn",
    )(qkv, qkv, qkv, qkv, qkv, bias)


def _t5_bucket(rel):
    nb = NUM_BUCKETS // 2
    max_exact = nb // 2
    ret = (rel > 0).astype(np.int32) * nb
    n = np.abs(rel)
    large = max_exact + (np.log(np.maximum(n, 1) / max_exact) / np.log(MAX_DISTANCE / max_exact)
                         * (nb - max_exact)).astype(np.int32)
    large = np.minimum(large, nb - 1)
    return (ret + np.where(n < max_exact, n, large)).astype(np.int32)


def _dilated_bias(rel_bias_g, dil):
    width = DIL_SUB + 2 * DIL_RADIUS
    rel = np.arange(width)[None, :] - DIL_RADIUS - np.arange(DIL_SUB)[:, None]
    bucket = _t5_bucket(rel * dil)
    band = np.abs(rel) <= DIL_RADIUS
    onehot = (bucket.reshape(-1, 1) == np.arange(NUM_BUCKETS)[None, :]).astype(np.float32)
    bias = jnp.dot(jnp.asarray(onehot), rel_bias_g.astype(_F32), precision=lax.Precision.HIGHEST)
    bias = bias.reshape(DIL_SUB, width, B_HEADS).transpose(2, 0, 1)
    return jnp.where(jnp.asarray(band)[None], bias, NEG_INF)


def _merge_oproj_kernel(o0_ref, l0_ref, o1_ref, l1_ref, o2_ref, l2_ref, w_ref, x_ref, gate_ref,
                        out_ref, so1, sl1, so2, sl2, *, tm):
    d1 = DILATED_CONFIGS[1][1]
    d2 = DILATED_CONFIGS[2][1]
    nc = B_HEADS * HEAD_DIM // LANES

    def to_token_order(src_ref, scr_ref, dil):
        for r in range(dil):
            for c in range(nc):
                scr_ref[c, pl.ds(r, tm // dil, stride=dil), :] = (
                    src_ref[0, r, :, LANES * c:LANES * (c + 1)])
        return jnp.concatenate([scr_ref[c] for c in range(nc)], axis=1)

    o1 = to_token_order(o1_ref, so1, d1)
    l1 = to_token_order(l1_ref, sl1, d1)
    o2 = to_token_order(o2_ref, so2, d2)
    l2 = to_token_order(l2_ref, sl2, d2)
    l0 = l0_ref[0, 0]
    mx = jnp.maximum(jnp.maximum(l0, l1), l2)
    e0, e1, e2 = jnp.exp(l0 - mx), jnp.exp(l1 - mx), jnp.exp(l2 - mx)
    o = (e0 * o0_ref[0, 0] + e1 * o1 + e2 * o2) / (e0 + e1 + e2)
    y = jnp.dot(o.astype(_BF16), w_ref[...], preferred_element_type=_F32)
    out_ref[0] = x_ref[0] + gate_ref[0] * y


def _merge_oproj(outs, lses, w, x, gate):
    bsz, seq, _ = x.shape
    tm = min(ROUTE_TILE, seq)
    nh = B_HEADS * HEAD_DIM
    specs = []
    for (_, dil) in DILATED_CONFIGS:
        spec = pl.BlockSpec((1, dil, tm // dil, nh), lambda b, i: (b, 0, i, 0))
        specs += [spec, spec]
    return pl.pallas_call(
        functools.partial(_merge_oproj_kernel, tm=tm),
        grid=(bsz, seq // tm),
        in_specs=specs + [
            pl.BlockSpec((nh, D_MODEL), lambda b, i: (0, 0)),
            pl.BlockSpec((1, tm, D_MODEL), lambda b, i: (b, i, 0)),
            pl.BlockSpec((1, 1, D_MODEL), lambda b, i: (b, 0, 0)),
        ],
        out_specs=pl.BlockSpec((1, tm, D_MODEL), lambda b, i: (b, i, 0)),
        out_shape=jax.ShapeDtypeStruct(x.shape, _F32),
        scratch_shapes=[pltpu.VMEM((nh // LANES, tm, LANES), _F32)] * 4,
        compiler_params=_cparams(("parallel", "parallel")),
        name="merge_oproj",
    )(outs[0], lses[0], outs[1], lses[1], outs[2], lses[2], w, x, gate)


def _router_kernel(x_ref, g_ref, sc_ref, sh_ref, wr_ref, br_ref,
                   hf_ref, idx_ref, gate_ref, rank_ref, cnt_ref, carry_ref, tri_ref, *, tm):
    first = (pl.program_id(0) == 0) & (pl.program_id(1) == 0)

    @pl.when(first)
    def _():
        carry_ref[...] = jnp.zeros(carry_ref.shape, _F32)
        r = lax.broadcasted_iota(jnp.int32, (tm, tm), 0)
        c = lax.broadcasted_iota(jnp.int32, (tm, tm), 1)
        tri_ref[...] = (r < c).astype(_BF16)

    hf = _rms_mod(x_ref[0], g_ref[...], sc_ref[0], sh_ref[0])
    for j in range(ROW_WORDS):
        hf_ref[pl.ds(j, tm, stride=ROW_WORDS), :] = hf[:, LANES * j:LANES * (j + 1)]
    logits = lax.dot_general(wr_ref[...], hf, _NT, preferred_element_type=_F32,
                             precision=lax.Precision.HIGHEST) + br_ref[...]
    eio = lax.broadcasted_iota(jnp.int32, logits.shape, 0).astype(_F32)
    cur = logits
    vals, sel = [], []
    for _ in range(TOP_K):
        mx = jnp.max(cur, axis=0, keepdims=True)
        ix = jnp.min(jnp.where(cur == mx, eio, float(N_EXPERTS)), axis=0, keepdims=True)
        vals.append(mx)
        sel.append(ix)
        cur = jnp.where(eio == ix, -jnp.inf, cur)
    ex = [jnp.exp(v - vals[0]) for v in vals]
    den = ex[0] + ex[1] + ex[2] + ex[3]
    member = jnp.zeros(logits.shape, _F32)
    for ix in sel:
        member = member + (eio == ix).astype(_F32)
    before = carry_ref[:, 0:1] + jnp.dot(member.astype(_BF16), tri_ref[...],
                                         preferred_element_type=_F32)
    ranks = [jnp.sum(jnp.where(eio == ix, before, 0.0), axis=0, keepdims=True) for ix in sel]
    idx_ref[...] = jnp.concatenate(sel, axis=0).astype(jnp.int32)
    gate_ref[...] = jnp.concatenate([e / den for e in ex], axis=0)
    rank_ref[...] = jnp.concatenate(ranks, axis=0).astype(jnp.int32)
    carry_ref[...] = carry_ref[...] + jnp.sum(member, axis=1, keepdims=True)
    cnt_ref[...] = carry_ref[...]


def _router(x, g, scale, shift, wr_t, br):
    bsz, seq, _ = x.shape
    tm = min(ROUTE_TILE, seq)
    ns = seq // tm
    tok = bsz * seq
    const = lambda b, i: (0, 0)
    flat = lambda b, i: (0, b * ns + i)
    return pl.pallas_call(
        functools.partial(_router_kernel, tm=tm),
        grid=(bsz, ns),
        in_specs=[
            pl.BlockSpec((1, tm, D_MODEL), lambda b, i: (b, i, 0)),
            pl.BlockSpec((1, D_MODEL), const),
            pl.BlockSpec((1, 1, D_MODEL), lambda b, i: (b, 0, 0)),
            pl.BlockSpec((1, 1, D_MODEL), lambda b, i: (b, 0, 0)),
            pl.BlockSpec((N_EXPERTS, D_MODEL), const),
            pl.BlockSpec((N_EXPERTS, 1), const),
        ],
        out_specs=[
            pl.BlockSpec((tm * ROW_WORDS, LANES), lambda b, i: (b * ns + i, 0)),
            pl.BlockSpec((TOP_K, tm), flat),
            pl.BlockSpec((TOP_K, tm), flat),
            pl.BlockSpec((TOP_K, tm), flat),
            pl.BlockSpec((N_EXPERTS, LANES), const),
        ],
        out_shape=[
            jax.ShapeDtypeStruct((tok * ROW_WORDS, LANES), _F32),
            jax.ShapeDtypeStruct((TOP_K, tok), jnp.int32),
            jax.ShapeDtypeStruct((TOP_K, tok), _F32),
            jax.ShapeDtypeStruct((TOP_K, tok), jnp.int32),
            jax.ShapeDtypeStruct((N_EXPERTS, LANES), _F32),
        ],
        scratch_shapes=[pltpu.VMEM((N_EXPERTS, LANES), _F32), pltpu.VMEM((tm, tm), _BF16)],
        compiler_params=_cparams(("arbitrary", "arbitrary")),
        name="router",
    )(x, g, scale, shift, wr_t, br)


def _row_copy(src_ref, src_row, dst_ref, dst_row, sem):
    return pltpu.make_async_copy(
        src_ref.at[pl.ds(pl.multiple_of(src_row * ROW_WORDS, ROW_WORDS), ROW_WORDS), :],
        dst_ref.at[pl.ds(pl.multiple_of(dst_row * ROW_WORDS, ROW_WORDS), ROW_WORDS), :], sem)


def _dispatch_kernel(dest_hbm, hf_ref, init_ref, xs_ref, dest_smem, isem, sem, *, tm):
    del init_ref
    i = pl.program_id(0)
    n = tm * TOP_K
    cp = pltpu.make_async_copy(dest_hbm.at[pl.ds(pl.multiple_of(i * n, n), n)], dest_smem, isem)
    cp.start()
    cp.wait()

    def issue(r, carry):
        for k in range(TOP_K):
            _row_copy(hf_ref, r, xs_ref, dest_smem[r * TOP_K + k], sem).start(priority=k % 2)
        return carry

    lax.fori_loop(0, tm, issue, 0, unroll=2)

    def drain(r, carry):
        for k in range(TOP_K):
            _row_copy(hf_ref, 0, xs_ref, 0, sem).wait()
        return carry

    lax.fori_loop(0, tm, drain, 0)


def _dispatch(dest_flat, hf, n_rows):
    tok = hf.shape[0] // ROW_WORDS
    tm = min(DISPATCH_TILE, tok)
    assert tok % tm == 0 and (tm * TOP_K) % 1024 == 0
    init = jnp.zeros((n_rows * ROW_WORDS, LANES), _F32)
    return pl.pallas_call(
        functools.partial(_dispatch_kernel, tm=tm),
        grid=(tok // tm,),
        in_specs=[
            pl.BlockSpec(memory_space=pl.ANY),
            pl.BlockSpec((tm * ROW_WORDS, LANES), lambda i: (i, 0)),
            pl.BlockSpec(memory_space=pl.ANY),
        ],
        out_specs=pl.BlockSpec(memory_space=pl.ANY),
        out_shape=jax.ShapeDtypeStruct(init.shape, _F32),
        scratch_shapes=[pltpu.SMEM((tm * TOP_K,), jnp.int32),
                        pltpu.SemaphoreType.DMA, pltpu.SemaphoreType.DMA],
        input_output_aliases={2: 0},
        compiler_params=_cparams(("arbitrary",)),
        name="moe_dispatch",
    )(dest_flat, hf, init)


def _expert_kernel(be_ref, nu_ref, xs_ref, wgu_ref, bgu_ref, wdn_ref, bdn_ref, ys_ref, *, blk):
    del be_ref
    i = pl.program_id(0)
    dff = wdn_ref.shape[1]

    @pl.when(i < nu_ref[0])
    def _():
        x = jnp.concatenate([xs_ref[pl.ds(j, blk, stride=ROW_WORDS), :] for j in range(ROW_WORDS)],
                            axis=1).astype(_BF16)
        gu = jnp.dot(x, wgu_ref[0], preferred_element_type=_F32) + bgu_ref[0]
        g = jnp.minimum(gu[:, :dff], SWIGLU_LIMIT)
        u = jnp.clip(gu[:, dff:], -SWIGLU_LIMIT, SWIGLU_LIMIT)
        act = g * jax.nn.sigmoid(SWIGLU_ALPHA * g)
        mid = ((u + 1.0) * act).astype(_BF16)
        y = jnp.dot(mid, wdn_ref[0], preferred_element_type=_F32) + bdn_ref[0]
        for j in range(ROW_WORDS):
            ys_ref[pl.ds(j, blk, stride=ROW_WORDS), :] = y[:, LANES * j:LANES * (j + 1)]

    @pl.when(i >= nu_ref[0])
    def _():
        ys_ref[...] = jnp.zeros(ys_ref.shape, _F32)


def _experts(blk_e, n_used, xs, wgu, bgu, wdn, bdn):
    blk = EXPERT_BLOCK
    n_blk = xs.shape[0] // (blk * ROW_WORDS)
    dff = wdn.shape[1]
    grid_spec = pltpu.PrefetchScalarGridSpec(
        num_scalar_prefetch=2,
        grid=(n_blk,),
        in_specs=[
            pl.BlockSpec((blk * ROW_WORDS, LANES), lambda i, be, nu: (i, 0)),
            pl.BlockSpec((1, D_MODEL, 2 * dff), lambda i, be, nu: (be[i], 0, 0)),
            pl.BlockSpec((1, 1, 2 * dff), lambda i, be, nu: (be[i], 0, 0)),
            pl.BlockSpec((1, dff, D_MODEL), lambda i, be, nu: (be[i], 0, 0)),
            pl.BlockSpec((1, 1, D_MODEL), lambda i, be, nu: (be[i], 0, 0)),
        ],
        out_specs=pl.BlockSpec((blk * ROW_WORDS, LANES), lambda i, be, nu: (i, 0)),
    )
    return pl.pallas_call(
        functools.partial(_expert_kernel, blk=blk),
        grid_spec=grid_spec,
        out_shape=jax.ShapeDtypeStruct(xs.shape, _F32),
        compiler_params=_cparams(("arbitrary",)),
        name="moe_experts",
    )(blk_e, n_used, xs, wgu, bgu, wdn, bdn)


def _combine_kernel(dest_hbm, ys_ref, gate_ref, x_ref, gf_ref, o_ref,
                    dest_smem0, dest_smem1, buf_ref, isem, sem, *, tm, ns, nsteps):
    g = pl.program_id(0) * ns + pl.program_id(1)
    n = tm * TOP_K
    dest_smem = (dest_smem0, dest_smem1)

    def start_gathers(step, slot):
        off = pl.multiple_of(step * n, n)
        idx_ref = dest_smem[slot]
        cp = pltpu.make_async_copy(dest_hbm.at[pl.ds(off, n)], idx_ref, isem)
        cp.start()
        cp.wait()

        def issue(r, carry):
            for k in range(TOP_K):
                _row_copy(ys_ref, idx_ref[r * TOP_K + k], buf_ref.at[slot], k * tm + r,
                          sem.at[slot]).start(priority=k % 2)
            return carry

        lax.fori_loop(0, tm, issue, 0, unroll=2)

    def finish(slot):
        def drain(r, carry):
            for k in range(TOP_K):
                _row_copy(ys_ref, 0, buf_ref.at[slot], 0, sem.at[slot]).wait()
            return carry

        lax.fori_loop(0, tm, drain, 0)
        gate = gate_ref[...]
        cols = []
        for j in range(ROW_WORDS):
            acc = None
            for k in range(TOP_K):
                slab = buf_ref[slot, pl.ds(k * tm * ROW_WORDS + j, tm, stride=ROW_WORDS), :]
                term = gate[:, k:k + 1] * slab
                acc = term if acc is None else acc + term
            cols.append(acc)
        moe = jnp.concatenate(cols, axis=1)
        o_ref[0] = x_ref[0] + gf_ref[0] * moe

    @pl.when(g == 0)
    def _():
        start_gathers(0, 0)

    for slot in range(2):
        @pl.when(g % 2 == slot)
        def _():
            @pl.when(g + 1 < nsteps)
            def _():
                start_gathers(g + 1, 1 - slot)

            finish(slot)


def _combine(dest_flat, gate_tok, ys, x, gate_f):
    bsz, seq, _ = x.shape
    tm = min(ROUTE_TILE, seq)
    ns = seq // tm
    return pl.pallas_call(
        functools.partial(_combine_kernel, tm=tm, ns=ns, nsteps=bsz * ns),
        grid=(bsz, ns),
        in_specs=[
            pl.BlockSpec(memory_space=pl.ANY),
            pl.BlockSpec(memory_space=pl.ANY),
            pl.BlockSpec((tm, TOP_K), lambda b, i: (b * ns + i, 0)),
            pl.BlockSpec((1, tm, D_MODEL), lambda b, i: (b, i, 0)),
            pl.BlockSpec((1, 1, D_MODEL), lambda b, i: (b, 0, 0)),
        ],
        out_specs=pl.BlockSpec((1, tm, D_MODEL), lambda b, i: (b, i, 0)),
        out_shape=jax.ShapeDtypeStruct(x.shape, _F32),
        scratch_shapes=[
            pltpu.SMEM((tm * TOP_K,), jnp.int32),
            pltpu.SMEM((tm * TOP_K,), jnp.int32),
            pltpu.VMEM((2, TOP_K * tm * ROW_WORDS, LANES), _F32),
            pltpu.SemaphoreType.DMA,
            pltpu.SemaphoreType.DMA((2,)),
        ],
        compiler_params=_cparams(("arbitrary", "arbitrary")),
        name="moe_combine",
    )(dest_flat, ys, gate_tok, x, gate_f)


def _moe(x, g, scale, shift, gate_f, wr_t, br, wgu, bgu, wdn, bdn):
    bsz, seq, _ = x.shape
    tok = bsz * seq
    blk = EXPERT_BLOCK
    hf, idx, gate, rank, cnt = _router(x, g, scale, shift, wr_t, br)
    counts = cnt[:, 0].astype(jnp.int32)
    padded = (counts + blk - 1) // blk * blk
    pend = jnp.cumsum(padded)
    pstart = pend - padded
    eids = jnp.arange(N_EXPERTS, dtype=jnp.int32)[:, None, None]
    base = jnp.sum(jnp.where(idx[None] == eids, pstart[:, None, None], 0), axis=0)
    dest = (base + rank).T.reshape(tok * TOP_K)
    gate_tok = gate.T
    n_blk = (tok * TOP_K + N_EXPERTS * (blk - 1) + blk - 1) // blk
    blk_start = jnp.arange(n_blk, dtype=jnp.int32) * blk
    blk_e = jnp.minimum(jnp.sum(blk_start[:, None] >= pend[None, :], axis=1),
                        N_EXPERTS - 1).astype(jnp.int32)
    n_used = (pend[-1:] // blk).astype(jnp.int32)
    xs = _dispatch(dest, hf, n_blk * blk)
    ys = _experts(blk_e, n_used, xs, wgu, bgu, wdn, bdn)
    return _combine(dest, gate_tok, ys, x, gate_f)


def _prepare(norm_mix_g, norm_ffn_g, w_qkv_a, q_norm_a, k_norm_a, w_o_a, w_qkv_b, q_norm_b, k_norm_b,
             w_o_b, rel_bias, w_router, b_router, w_gate_up, b_gate_up, w_down, b_down):
    nq = A_HEADS * HEAD_DIM
    nk = A_KV_HEADS * HEAD_DIM
    nh = B_HEADS * HEAD_DIM
    scale = HEAD_DIM ** -0.5
    p = {}
    wa = w_qkv_a[0]
    p["wqt_a"] = wa[:, :nq].T.astype(_BF16)
    p["wk_a"] = wa[:, nq:nq + nk].astype(_BF16)
    p["wvt_a"] = wa[:, nq + nk:].T.astype(_BF16)
    p["gq_a"] = (q_norm_a[0] * (scale * np.log2(np.e))).reshape(HEAD_DIM, 1)
    p["gk_a"] = jnp.tile(k_norm_a[0], A_KV_HEADS).reshape(1, nk)
    p["wo_a"] = w_o_a[0].astype(_BF16)
    wb = w_qkv_b[0].astype(_BF16)
    p["w_b"] = [wb[:, 3 * nh * gi:3 * nh * (gi + 1)] for gi in range(len(DILATED_CONFIGS))]
    p["gv_b"] = [jnp.concatenate([jnp.tile(q_norm_b[0, gi], B_HEADS) * scale,
                                  jnp.tile(k_norm_b[0, gi], B_HEADS)]).reshape(1, 2 * nh)
                 for gi in range(len(DILATED_CONFIGS))]
    p["bias_b"] = [_dilated_bias(rel_bias[:, gi * B_HEADS:(gi + 1) * B_HEADS], dil)
                   for gi, (_, dil) in enumerate(DILATED_CONFIGS)]
    p["wo_b"] = w_o_b[0].astype(_BF16)
    p["g_mix"] = [norm_mix_g[i].reshape(1, D_MODEL) for i in range(DEPTH)]
    p["g_ffn"] = [norm_ffn_g[i].reshape(1, D_MODEL) for i in range(DEPTH)]
    p["wr_t"] = [w_router[i].T for i in range(DEPTH)]
    p["br"] = [b_router[i].reshape(N_EXPERTS, 1) for i in range(DEPTH)]
    p["wgu"] = [w_gate_up[i].astype(_BF16) for i in range(DEPTH)]
    p["bgu"] = [b_gate_up[i].reshape(N_EXPERTS, 1, -1) for i in range(DEPTH)]
    p["wdn"] = [w_down[i].astype(_BF16) for i in range(DEPTH)]
    p["bdn"] = [b_down[i].reshape(N_EXPERTS, 1, -1) for i in range(DEPTH)]
    p["bd"] = _block_diag_ones()
    return p


def _trunk(x, c, w_ada, b_ada, p):
    bsz, seq, _ = x.shape
    tables = _rope_tables(seq)
    for i in range(DEPTH):
        mod = _ada_mod(c, w_ada[i], b_ada[i])
        shift_m, scale_m, gate_m, shift_f, scale_f, gate_f = [
            mod[:, D_MODEL * k:D_MODEL * (k + 1)].reshape(bsz, 1, D_MODEL) for k in range(6)]
        if i % 2 == 0:
            qt, k, vt = _proj_a(x, p["g_mix"][i], scale_m, shift_m, p["wqt_a"], p["wk_a"],
                                p["wvt_a"], p["gq_a"], p["gk_a"], tables, p["bd"])
            a = _flash_a(qt, k, vt)
            x = _oproj(a, p["wo_a"], x, gate_m)
        else:
            outs, lses = [], []
            for gi, (_, dil) in enumerate(DILATED_CONFIGS):
                qkv = _proj_b(x, p["g_mix"][i], scale_m, shift_m, p["w_b"][gi], p["gv_b"][gi],
                              p["bd"], dil)
                o, lse = _dilated_attn(qkv, p["bias_b"][gi])
                outs.append(o)
                lses.append(lse)
            x = _merge_oproj(outs, lses, p["wo_b"], x, gate_m)
        x = _moe(x, p["g_ffn"][i], scale_f, shift_f, gate_f, p["wr_t"][i], p["br"][i],
                 p["wgu"][i], p["bgu"][i], p["wdn"][i], p["bdn"][i])
    return x


def kernel(x_prompt, x_sample, c_prompt, c_sample, norm_mix_g, norm_ffn_g, w_ada, b_ada, w_qkv_a,
           q_norm_a, k_norm_a, w_o_a, w_qkv_b, q_norm_b, k_norm_b, w_o_b, rel_bias, w_router,
           b_router, w_gate_up, b_gate_up, w_down, b_down):
    p = _prepare(norm_mix_g, norm_ffn_g, w_qkv_a, q_norm_a, k_norm_a, w_o_a, w_qkv_b, q_norm_b,
                 k_norm_b, w_o_b, rel_bias, w_router, b_router, w_gate_up, b_gate_up, w_down, b_down)
    y_prompt = _trunk(x_prompt, c_prompt, w_ada, b_ada, p)
    y_sample = _trunk(x_sample, c_sample, w_ada, b_ada, p)
    return (y_prompt, y_sample)
```

```python
import functools

import numpy as np
import jax
import jax.numpy as jnp
from jax import lax
from jax.experimental import pallas as pl
from jax.experimental.pallas import tpu as pltpu

D_MODEL = 1024
HEAD_DIM = 64
A_HEADS = 16
A_KV_HEADS = 4
A_GROUP = A_HEADS // A_KV_HEADS
GRID_W = 64
ROPE_THETA = 10000.0
B_HEADS = 16
DILATED_CONFIGS = ((128, 1), (512, 4), (2048, 16))
NUM_BUCKETS = 32
MAX_DISTANCE = 1024
N_EXPERTS = 32
TOP_K = 4
SWIGLU_ALPHA = 1.702
SWIGLU_LIMIT = 7.0
RMS_EPS = 1e-6
NEG_INF = -1e30
DEPTH = 2

LANES = 128
SUBLANES = 8
VMEM_LIMIT = 56 * 1024 * 1024

ROW_TILE = 512
FA_TQ = 256
FA_TK = 512
FA_UNROLL = 8
V_ROWS = HEAD_DIM + 16
DIL_TQ = 512
DIL_SUB = 128
DIL_RADIUS = 64
ROUTE_TILE = 256
DISPATCH_TILE = 256
EXPERT_BLOCK = 256
ROW_WORDS = D_MODEL // LANES

_F32 = jnp.float32
_BF16 = jnp.bfloat16
_NT = (((1,), (1,)), ((), ()))


def _cparams(sem):
    return pltpu.CompilerParams(dimension_semantics=sem, vmem_limit_bytes=VMEM_LIMIT)


def _rms_mod(x, g, scale, shift):
    var = jnp.mean(x * x, axis=-1, keepdims=True)
    y = x * lax.rsqrt(var + RMS_EPS) * g
    return y * (1.0 + scale) + shift


def _head_rms(y, bd, gvec):
    ss = jnp.dot((y * y).astype(_BF16), bd, preferred_element_type=_F32)
    return y * lax.rsqrt(ss * (1.0 / HEAD_DIM) + RMS_EPS) * gvec


def _ada_kernel(c_ref, w_ref, b_ref, o_ref):
    c = c_ref[...]
    s = c * jax.nn.sigmoid(c)
    o_ref[...] = jnp.dot(s, w_ref[...], preferred_element_type=_F32,
                         precision=lax.Precision.HIGHEST) + b_ref[...]


def _ada_mod(c, w, b):
    bsz = c.shape[0]
    cp = jnp.pad(c, ((0, SUBLANES - bsz), (0, 0)))
    n = w.shape[1]
    out = pl.pallas_call(
        _ada_kernel,
        grid=(n // D_MODEL,),
        in_specs=[pl.BlockSpec((SUBLANES, D_MODEL), lambda j: (0, 0)),
                  pl.BlockSpec((D_MODEL, D_MODEL), lambda j: (0, j)),
                  pl.BlockSpec((1, D_MODEL), lambda j: (0, j))],
        out_specs=pl.BlockSpec((SUBLANES, D_MODEL), lambda j: (0, j)),
        out_shape=jax.ShapeDtypeStruct((SUBLANES, n), _F32),
        compiler_params=_cparams(("arbitrary",)),
        name="ada_mod",
    )(cp, w, b.reshape(1, n))
    return out[:bsz]


def _proj_a_kernel(x_ref, g_ref, sc_ref, sh_ref, wqt_ref, wk_ref, wvt_ref, gq_ref, gk_ref,
                   cos_ref, sina_ref, sinb_ref, cost_ref, sint_ref, bd_ref,
                   qt_ref, k_ref, vt_ref):
    h = _rms_mod(x_ref[0], g_ref[...], sc_ref[0], sh_ref[0]).astype(_BF16)
    acct = lax.dot_general(wqt_ref[...], h, _NT, preferred_element_type=_F32)
    cost = cost_ref[...]
    sint = sint_ref[...]
    gq = gq_ref[...]
    for j in range(A_HEADS):
        slab = acct[HEAD_DIM * j:HEAD_DIM * (j + 1), :]
        ss = jnp.sum(slab * slab, axis=0, keepdims=True)
        y = slab * lax.rsqrt(ss * (1.0 / HEAD_DIM) + RMS_EPS) * gq
        rot = jnp.concatenate([-y[16:32], y[0:16], -y[48:64], y[32:48]], axis=0)
        qt_ref[0, HEAD_DIM * j:HEAD_DIM * (j + 1), :] = (y * cost + rot * sint).astype(_BF16)
    acck = jnp.dot(h, wk_ref[...], preferred_element_type=_F32)
    y = _head_rms(acck, bd_ref[...], gk_ref[...])
    cos = cos_ref[...]
    sina = sina_ref[...]
    sinb = sinb_ref[...]
    for hh in range(2):
        yy = y[:, LANES * hh:LANES * (hh + 1)]
        r = yy * cos + pltpu.roll(yy, LANES - 16, 1) * sina + pltpu.roll(yy, 16, 1) * sinb
        for e in range(2):
            k_ref[0, 2 * hh + e] = r[:, HEAD_DIM * e:HEAD_DIM * (e + 1)].astype(_BF16)
    accv = lax.dot_general(wvt_ref[...], h, _NT, preferred_element_type=_F32)
    ones = jnp.ones((V_ROWS - HEAD_DIM, accv.shape[1]), _BF16)
    for j in range(A_KV_HEADS):
        vt_ref[0, j, 0, 0:HEAD_DIM, :] = accv[HEAD_DIM * j:HEAD_DIM * (j + 1), :].astype(_BF16)
        vt_ref[0, j, 0, HEAD_DIM:V_ROWS, :] = ones


def _rope_tables(seq_len):
    rows = seq_len // GRID_W
    row = jnp.repeat(jnp.arange(rows, dtype=_F32), GRID_W)
    col = jnp.tile(jnp.arange(GRID_W, dtype=_F32), rows)
    n_freq = HEAD_DIM // 4
    inv_freq = 1.0 / (ROPE_THETA ** (jnp.arange(n_freq, dtype=_F32) / n_freq))
    ang_r = row[:, None] * inv_freq[None, :]
    ang_c = col[:, None] * inv_freq[None, :]
    ang = jnp.concatenate([ang_r, ang_r, ang_c, ang_c], axis=-1)
    cos, sin = jnp.cos(ang), jnp.sin(ang)
    first = (np.arange(HEAD_DIM) % 32) < 16
    cos2 = jnp.tile(cos, (1, 2))
    sina = jnp.tile(jnp.where(first[None, :], -sin, 0.0), (1, 2))
    sinb = jnp.tile(jnp.where(first[None, :], 0.0, sin), (1, 2))
    return cos2, sina, sinb, cos.T, sin.T


def _block_diag_ones():
    i = np.arange(256)
    return jnp.asarray((i[:, None] // HEAD_DIM) == (i[None, :] // HEAD_DIM), dtype=_BF16)


def _proj_a(x, g, scale, shift, wqt, wk, wvt, gq_col, gk_t, tables, bd):
    bsz, seq, _ = x.shape
    tm = FA_TK
    assert seq % tm == 0
    cos2, sina, sinb, cost, sint = tables
    nq = A_HEADS * HEAD_DIM
    nk = A_KV_HEADS * HEAD_DIM
    const = lambda b, i: (0, 0)
    return pl.pallas_call(
        _proj_a_kernel,
        grid=(bsz, seq // tm),
        in_specs=[
            pl.BlockSpec((1, tm, D_MODEL), lambda b, i: (b, i, 0)),
            pl.BlockSpec((1, D_MODEL), const),
            pl.BlockSpec((1, 1, D_MODEL), lambda b, i: (b, 0, 0)),
            pl.BlockSpec((1, 1, D_MODEL), lambda b, i: (b, 0, 0)),
            pl.BlockSpec((nq, D_MODEL), const),
            pl.BlockSpec((D_MODEL, nk), const),
            pl.BlockSpec((nk, D_MODEL), const),
            pl.BlockSpec((HEAD_DIM, 1), const),
            pl.BlockSpec((1, nk), const),
            pl.BlockSpec((tm, LANES), lambda b, i: (i, 0)),
            pl.BlockSpec((tm, LANES), lambda b, i: (i, 0)),
            pl.BlockSpec((tm, LANES), lambda b, i: (i, 0)),
            pl.BlockSpec((HEAD_DIM, tm), lambda b, i: (0, i)),
            pl.BlockSpec((HEAD_DIM, tm), lambda b, i: (0, i)),
            pl.BlockSpec((256, 256), const),
        ],
        out_specs=[
            pl.BlockSpec((1, nq, tm), lambda b, i: (b, 0, i)),
            pl.BlockSpec((1, A_KV_HEADS, tm, HEAD_DIM), lambda b, i: (b, 0, i, 0)),
            pl.BlockSpec((1, A_KV_HEADS, 1, V_ROWS, tm), lambda b, i: (b, 0, i, 0, 0)),
        ],
        out_shape=[
            jax.ShapeDtypeStruct((bsz, nq, seq), _BF16),
            jax.ShapeDtypeStruct((bsz, A_KV_HEADS, seq, HEAD_DIM), _BF16),
            jax.ShapeDtypeStruct((bsz, A_KV_HEADS, seq // tm, V_ROWS, tm), _BF16),
        ],
        compiler_params=_cparams(("parallel", "parallel")),
        name="proj_a",
    )(x, g, scale, shift, wqt, wk, wvt, gq_col, gk_t, cos2, sina, sinb, cost, sint, bd)


def _flash_kernel(qt_ref, k_ref, vt_ref, o_ref, s_ref, m_ref, acc_ref, *, tq, tk, nchunks):
    qt4 = jnp.concatenate([qt_ref[0, HEAD_DIM * hq:HEAD_DIM * (hq + 1), :] for hq in range(A_GROUP)],
                          axis=1)
    m_ref[...] = jnp.full(m_ref.shape, -jnp.inf, _F32)
    acc_ref[...] = jnp.zeros(acc_ref.shape, _F32)

    def scores(kc, slot):
        kb = k_ref[0, 0, pl.ds(pl.multiple_of(kc * tk, tk), tk), :]
        s_ref[slot] = jnp.dot(kb, qt4, preferred_element_type=_F32)

    def step(kc, slot, kc_next, slot_next):
        vb = vt_ref[0, 0, kc]
        if kc_next is not None:
            kb = k_ref[0, 0, pl.ds(pl.multiple_of(kc_next * tk, tk), tk), :]
        for hq in range(A_GROUP):
            sl = slice(tq * hq, tq * (hq + 1))
            if kc_next is not None:
                s_ref[slot_next, :, sl] = jnp.dot(kb, qt4[:, sl], preferred_element_type=_F32)
            s = s_ref[slot, :, sl]
            m_prev = m_ref[:, sl]
            m_new = jnp.maximum(m_prev, jnp.max(s, axis=0, keepdims=True))
            alpha = jnp.exp2(m_prev - m_new)
            p = jnp.exp2(s - m_new).astype(_BF16)
            acc_ref[:, sl] = alpha * acc_ref[:, sl] + jnp.dot(vb, p, preferred_element_type=_F32)
            m_ref[:, sl] = m_new

    def run(kc0, last):
        for j in range(FA_UNROLL):
            final = last and j == FA_UNROLL - 1
            step(kc0 + j, j % 2, None if final else kc0 + j + 1, (j + 1) % 2)

    scores(0, 0)

    def body(c, carry):
        run(FA_UNROLL * c, False)
        return carry

    lax.fori_loop(0, nchunks // FA_UNROLL - 1, body, 0)
    run(nchunks - FA_UNROLL, True)
    o = acc_ref[0:HEAD_DIM, :] / acc_ref[HEAD_DIM:HEAD_DIM + 1, :]
    o = jnp.concatenate([o[:, tq * hq:tq * (hq + 1)] for hq in range(A_GROUP)], axis=0)
    o_ref[0] = o.T.astype(_BF16)


def _flash_a(qt, k, vt):
    bsz, nq, seq = qt.shape
    tq = min(FA_TQ, seq)
    tk = vt.shape[-1]
    nchunks = seq // tk
    assert nchunks % FA_UNROLL == 0 and FA_UNROLL % 2 == 0 and seq % tq == 0
    return pl.pallas_call(
        functools.partial(_flash_kernel, tq=tq, tk=tk, nchunks=nchunks),
        grid=(bsz, A_KV_HEADS, seq // tq),
        in_specs=[
            pl.BlockSpec((1, A_GROUP * HEAD_DIM, tq), lambda b, h, i: (b, h, i)),
            pl.BlockSpec((1, 1, seq, HEAD_DIM), lambda b, h, i: (b, h, 0, 0)),
            pl.BlockSpec((1, 1, nchunks, V_ROWS, tk), lambda b, h, i: (b, h, 0, 0, 0)),
        ],
        out_specs=pl.BlockSpec((1, tq, A_GROUP * HEAD_DIM), lambda b, h, i: (b, i, h)),
        out_shape=jax.ShapeDtypeStruct((bsz, seq, nq), _BF16),
        scratch_shapes=[
            pltpu.VMEM((2, tk, A_GROUP * tq), _F32),
            pltpu.VMEM((1, A_GROUP * tq), _F32),
            pltpu.VMEM((V_ROWS, A_GROUP * tq), _F32),
        ],
        compiler_params=_cparams(("parallel", "parallel", "parallel")),
        name="flash_a",
    )(qt, k, vt)


def _oproj_kernel(a_ref, w_ref, x_ref, gate_ref, o_ref):
    y = jnp.dot(a_ref[0], w_ref[...], preferred_element_type=_F32)
    o_ref[0] = x_ref[0] + gate_ref[0] * y


def _oproj(a, w, x, gate):
    bsz, seq, _ = x.shape
    tm = min(ROW_TILE, seq)
    return pl.pallas_call(
        _oproj_kernel,
        grid=(bsz, seq // tm),
        in_specs=[
            pl.BlockSpec((1, tm, D_MODEL), lambda b, i: (b, i, 0)),
            pl.BlockSpec((D_MODEL, D_MODEL), lambda b, i: (0, 0)),
            pl.BlockSpec((1, tm, D_MODEL), lambda b, i: (b, i, 0)),
            pl.BlockSpec((1, 1, D_MODEL), lambda b, i: (b, 0, 0)),
        ],
        out_specs=pl.BlockSpec((1, tm, D_MODEL), lambda b, i: (b, i, 0)),
        out_shape=jax.ShapeDtypeStruct(x.shape, _F32),
        compiler_params=_cparams(("parallel", "parallel")),
        name="oproj",
    )(a, w, x, gate)


def _proj_b_kernel(x_ref, g_ref, sc_ref, sh_ref, w_ref, gv_ref, bd_ref, o_ref, scr_ref, *, dil, tm):
    h = _rms_mod(x_ref[0], g_ref[...], sc_ref[0], sh_ref[0]).astype(_BF16)
    bd = bd_ref[...]
    acc = jnp.dot(h, w_ref[...], preferred_element_type=_F32)
    nqk = 2 * B_HEADS * HEAD_DIM
    n = acc.shape[1]

    def emit(col, y):
        if dil == 1:
            o_ref[0, 0, :, col:col + y.shape[1]] = y.astype(_BF16)
        else:
            for c in range(y.shape[1] // LANES):
                scr_ref[col // LANES + c] = y[:, LANES * c:LANES * (c + 1)]

    for c in range(nqk // 256):
        sl = slice(256 * c, 256 * (c + 1))
        emit(256 * c, _head_rms(acc[:, sl], bd, gv_ref[:, sl]))
    emit(nqk, acc[:, nqk:])
    if dil > 1:
        for r in range(dil):
            for c in range(n // LANES):
                o_ref[0, r, :, LANES * c:LANES * (c + 1)] = (
                    scr_ref[c, pl.ds(r, tm // dil, stride=dil), :].astype(_BF16))


def _proj_b(x, g, scale, shift, w, gi, gvec, bd, dil):
    bsz, seq, _ = x.shape
    tm = min(ROW_TILE, seq)
    n = 3 * B_HEADS * HEAD_DIM
    const = lambda b, i: (0, 0)
    return pl.pallas_call(
        functools.partial(_proj_b_kernel, dil=dil, tm=tm),
        grid=(bsz, seq // tm),
        in_specs=[
            pl.BlockSpec((1, tm, D_MODEL), lambda b, i: (b, i, 0)),
            pl.BlockSpec((1, D_MODEL), const),
            pl.BlockSpec((1, 1, D_MODEL), lambda b, i: (b, 0, 0)),
            pl.BlockSpec((1, 1, D_MODEL), lambda b, i: (b, 0, 0)),
            pl.BlockSpec((D_MODEL, n), lambda b, i: (0, gi)),
            pl.BlockSpec((1, 2 * B_HEADS * HEAD_DIM), const),
            pl.BlockSpec((256, 256), const),
        ],
        out_specs=pl.BlockSpec((1, dil, tm // dil, n), lambda b, i: (b, 0, i, 0)),
        out_shape=jax.ShapeDtypeStruct((bsz, dil, seq // dil, n), _BF16),
        scratch_shapes=[pltpu.VMEM((n // LANES, tm, LANES), _F32)],
        compiler_params=_cparams(("parallel", "parallel")),
        name="proj_b",
    )(x, g, scale, shift, w, gvec, bd)


def _dilated_kernel(main_ref, kp_ref, vp_ref, kn_ref, vn_ref, bias_ref, o_ref, lse_ref,
                    kf_ref, vf_ref, *, tq, length):
    nh = B_HEADS * HEAD_DIM
    l0 = pl.program_id(2) * tq
    kf_ref[0:DIL_RADIUS] = kp_ref[0, 0]
    kf_ref[DIL_RADIUS:DIL_RADIUS + tq] = main_ref[0, 0, :, nh:2 * nh]
    kf_ref[DIL_RADIUS + tq:] = kn_ref[0, 0]
    vf_ref[0:DIL_RADIUS] = vp_ref[0, 0]
    vf_ref[DIL_RADIUS:DIL_RADIUS + tq] = main_ref[0, 0, :, 2 * nh:3 * nh]
    vf_ref[DIL_RADIUS + tq:] = vn_ref[0, 0]
    width = DIL_SUB + 2 * DIL_RADIUS
    low_half = lax.broadcasted_iota(jnp.int32, (DIL_SUB, LANES), 1) < HEAD_DIM
    col = lax.broadcasted_iota(jnp.int32, (DIL_SUB, width), 1)

    def sub_block(u, carry):
        r0 = pl.multiple_of(u * DIL_SUB, DIL_SUB)
        kpos = col + (l0 + r0 - DIL_RADIUS)
        valid = (kpos >= 0) & (kpos < length)
        for m in range(B_HEADS // 2):
            sl = slice(LANES * m, LANES * (m + 1))
            qp = main_ref[0, 0, pl.ds(r0, DIL_SUB), sl]
            kw = kf_ref[pl.ds(r0, width), sl]
            vw = vf_ref[pl.ds(r0, width), sl]
            res = []
            for hh in range(2):
                keep = low_half if hh == 0 else jnp.logical_not(low_half)
                qm = jnp.where(keep, qp, jnp.zeros_like(qp))
                s = lax.dot_general(qm, kw, _NT, preferred_element_type=_F32) + bias_ref[2 * m + hh]
                s = jnp.where(valid, s, NEG_INF)
                mx = jnp.max(s, axis=1, keepdims=True)
                p = jnp.exp(s - mx)
                den = jnp.sum(p, axis=1, keepdims=True)
                pv = jnp.dot(p.astype(_BF16), vw, preferred_element_type=_F32)
                res.append((pv / den, mx + jnp.log(den)))
            o_ref[0, 0, pl.ds(r0, DIL_SUB), sl] = jnp.where(low_half, res[0][0], res[1][0])
            lse_ref[0, 0, pl.ds(r0, DIL_SUB), sl] = jnp.where(
                low_half, jnp.broadcast_to(res[0][1], (DIL_SUB, LANES)),
                jnp.broadcast_to(res[1][1], (DIL_SUB, LANES)))
        return carry

    lax.fori_loop(0, tq // DIL_SUB, sub_block, 0)


def _dilated_attn(qkv, bias):
    bsz, dil, length, n = qkv.shape
    nh = B_HEADS * HEAD_DIM
    tq = min(DIL_TQ, length)
    nb = tq // DIL_RADIUS
    last = length // DIL_RADIUS - 1
    prev_ix = lambda i: jnp.maximum(i * nb - 1, 0)
    next_ix = lambda i: jnp.minimum((i + 1) * nb, last)
    halo = (1, 1, DIL_RADIUS, nh)
    width = DIL_SUB + 2 * DIL_RADIUS
    out = jax.ShapeDtypeStruct((bsz, dil, length, nh), _F32)
    return pl.pallas_call(
        functools.partial(_dilated_kernel, tq=tq, length=length),
        grid=(bsz, dil, length // tq),
        in_specs=[
            pl.BlockSpec((1, 1, tq, n), lambda b, r, i: (b, r, i, 0)),
            pl.BlockSpec(halo, lambda b, r, i: (b, r, prev_ix(i), 1)),
            pl.BlockSpec(halo, lambda b, r, i: (b, r, prev_ix(i), 2)),
            pl.BlockSpec(halo, lambda b, r, i: (b, r, next_ix(i), 1)),
            pl.BlockSpec(halo, lambda b, r, i: (b, r, next_ix(i), 2)),
            pl.BlockSpec((B_HEADS, DIL_SUB, width), lambda b, r, i: (0, 0, 0)),
        ],
        out_specs=[pl.BlockSpec((1, 1, tq, nh), lambda b, r, i: (b, r, i, 0)),
                   pl.BlockSpec((1, 1, tq, nh), lambda b, r, i: (b, r, i, 0))],
        out_shape=[out, out],
        scratch_shapes=[pltpu.VMEM((tq + 2 * DIL_RADIUS, nh), _BF16),
                        pltpu.VMEM((tq + 2 * DIL_RADIUS, nh), _BF16)],
        compiler_params=_cparams(("parallel", "parallel", "parallel")),
        name="dilated_attn",
    )(qkv, qkv, qkv, qkv, qkv, bias)


def _t5_bucket(rel):
    nb = NUM_BUCKETS // 2
    max_exact = nb // 2
    ret = (rel > 0).astype(np.int32) * nb
    n = np.abs(rel)
    large = max_exact + (np.log(np.maximum(n, 1) / max_exact) / np.log(MAX_DISTANCE / max_exact)
                         * (nb - max_exact)).astype(np.int32)
    large = np.minimum(large, nb - 1)
    return (ret + np.where(n < max_exact, n, large)).astype(np.int32)


def _dilated_bias(rel_bias_g, dil):
    width = DIL_SUB + 2 * DIL_RADIUS
    rel = np.arange(width)[None, :] - DIL_RADIUS - np.arange(DIL_SUB)[:, None]
    bucket = _t5_bucket(rel * dil)
    band = np.abs(rel) <= DIL_RADIUS
    onehot = (bucket.reshape(-1, 1) == np.arange(NUM_BUCKETS)[None, :]).astype(np.float32)
    bias = jnp.dot(jnp.asarray(onehot), rel_bias_g.astype(_F32), precision=lax.Precision.HIGHEST)
    bias = bias.reshape(DIL_SUB, width, B_HEADS).transpose(2, 0, 1)
    return jnp.where(jnp.asarray(band)[None], bias, NEG_INF)


def _merge_oproj_kernel(o0_ref, l0_ref, o1_ref, l1_ref, o2_ref, l2_ref, w_ref, x_ref, gate_ref,
                        out_ref, so1, sl1, so2, sl2, *, tm):
    d1 = DILATED_CONFIGS[1][1]
    d2 = DILATED_CONFIGS[2][1]
    nc = B_HEADS * HEAD_DIM // LANES

    def to_token_order(src_ref, scr_ref, dil):
        for r in range(dil):
            for c in range(nc):
                scr_ref[c, pl.ds(r, tm // dil, stride=dil), :] = (
                    src_ref[0, r, :, LANES * c:LANES * (c + 1)])
        return jnp.concatenate([scr_ref[c] for c in range(nc)], axis=1)

    o1 = to_token_order(o1_ref, so1, d1)
    l1 = to_token_order(l1_ref, sl1, d1)
    o2 = to_token_order(o2_ref, so2, d2)
    l2 = to_token_order(l2_ref, sl2, d2)
    l0 = l0_ref[0, 0]
    mx = jnp.maximum(jnp.maximum(l0, l1), l2)
    e0, e1, e2 = jnp.exp(l0 - mx), jnp.exp(l1 - mx), jnp.exp(l2 - mx)
    o = (e0 * o0_ref[0, 0] + e1 * o1 + e2 * o2) / (e0 + e1 + e2)
    y = jnp.dot(o.astype(_BF16), w_ref[...], preferred_element_type=_F32)
    out_ref[0] = x_ref[0] + gate_ref[0] * y


def _merge_oproj(outs, lses, w, x, gate):
    bsz, seq, _ = x.shape
    tm = min(ROUTE_TILE, seq)
    nh = B_HEADS * HEAD_DIM
    specs = []
    for (_, dil) in DILATED_CONFIGS:
        spec = pl.BlockSpec((1, dil, tm // dil, nh), lambda b, i: (b, 0, i, 0))
        specs += [spec, spec]
    return pl.pallas_call(
        functools.partial(_merge_oproj_kernel, tm=tm),
        grid=(bsz, seq // tm),
        in_specs=specs + [
            pl.BlockSpec((nh, D_MODEL), lambda b, i: (0, 0)),
            pl.BlockSpec((1, tm, D_MODEL), lambda b, i: (b, i, 0)),
            pl.BlockSpec((1, 1, D_MODEL), lambda b, i: (b, 0, 0)),
        ],
        out_specs=pl.BlockSpec((1, tm, D_MODEL), lambda b, i: (b, i, 0)),
        out_shape=jax.ShapeDtypeStruct(x.shape, _F32),
        scratch_shapes=[pltpu.VMEM((nh // LANES, tm, LANES), _F32)] * 4,
        compiler_params=_cparams(("parallel", "parallel")),
        name="merge_oproj",
    )(outs[0], lses[0], outs[1], lses[1], outs[2], lses[2], w, x, gate)


def _router_kernel(x_ref, g_ref, sc_ref, sh_ref, wr_ref, br_ref,
                   hf_ref, idx_ref, gate_ref, rank_ref, cnt_ref, carry_ref, tri_ref, *, tm):
    first = (pl.program_id(0) == 0) & (pl.program_id(1) == 0)

    @pl.when(first)
    def _():
        carry_ref[...] = jnp.zeros(carry_ref.shape, _F32)
        r = lax.broadcasted_iota(jnp.int32, (tm, tm), 0)
        c = lax.broadcasted_iota(jnp.int32, (tm, tm), 1)
        tri_ref[...] = (r < c).astype(_BF16)

    hf = _rms_mod(x_ref[0], g_ref[...], sc_ref[0], sh_ref[0])
    for j in range(ROW_WORDS):
        hf_ref[pl.ds(j, tm, stride=ROW_WORDS), :] = hf[:, LANES * j:LANES * (j + 1)]
    logits = lax.dot_general(wr_ref[...], hf, _NT, preferred_element_type=_F32,
                             precision=lax.Precision.HIGHEST) + br_ref[...]
    eio = lax.broadcasted_iota(jnp.int32, logits.shape, 0).astype(_F32)
    cur = logits
    vals, sel = [], []
    for _ in range(TOP_K):
        mx = jnp.max(cur, axis=0, keepdims=True)
        ix = jnp.min(jnp.where(cur == mx, eio, float(N_EXPERTS)), axis=0, keepdims=True)
        vals.append(mx)
        sel.append(ix)
        cur = jnp.where(eio == ix, -jnp.inf, cur)
    ex = [jnp.exp(v - vals[0]) for v in vals]
    den = ex[0] + ex[1] + ex[2] + ex[3]
    member = jnp.zeros(logits.shape, _F32)
    for ix in sel:
        member = member + (eio == ix).astype(_F32)
    before = carry_ref[:, 0:1] + jnp.dot(member.astype(_BF16), tri_ref[...],
                                         preferred_element_type=_F32)
    ranks = [jnp.sum(jnp.where(eio == ix, before, 0.0), axis=0, keepdims=True) for ix in sel]
    idx_ref[...] = jnp.concatenate(sel, axis=0).astype(jnp.int32)
    gate_ref[...] = jnp.concatenate([e / den for e in ex], axis=0)
    rank_ref[...] = jnp.concatenate(ranks, axis=0).astype(jnp.int32)
    carry_ref[...] = carry_ref[...] + jnp.sum(member, axis=1, keepdims=True)
    cnt_ref[...] = carry_ref[...]


def _router(x, g, scale, shift, wr_t, br):
    bsz, seq, _ = x.shape
    tm = min(ROUTE_TILE, seq)
    ns = seq // tm
    tok = bsz * seq
    const = lambda b, i: (0, 0)
    flat = lambda b, i: (0, b * ns + i)
    return pl.pallas_call(
        functools.partial(_router_kernel, tm=tm),
        grid=(bsz, ns),
        in_specs=[
            pl.BlockSpec((1, tm, D_MODEL), lambda b, i: (b, i, 0)),
            pl.BlockSpec((1, D_MODEL), const),
            pl.BlockSpec((1, 1, D_MODEL), lambda b, i: (b, 0, 0)),
            pl.BlockSpec((1, 1, D_MODEL), lambda b, i: (b, 0, 0)),
            pl.BlockSpec((N_EXPERTS, D_MODEL), const),
            pl.BlockSpec((N_EXPERTS, 1), const),
        ],
        out_specs=[
            pl.BlockSpec((tm * ROW_WORDS, LANES), lambda b, i: (b * ns + i, 0)),
            pl.BlockSpec((TOP_K, tm), flat),
            pl.BlockSpec((TOP_K, tm), flat),
            pl.BlockSpec((TOP_K, tm), flat),
            pl.BlockSpec((N_EXPERTS, LANES), const),
        ],
        out_shape=[
            jax.ShapeDtypeStruct((tok * ROW_WORDS, LANES), _F32),
            jax.ShapeDtypeStruct((TOP_K, tok), jnp.int32),
            jax.ShapeDtypeStruct((TOP_K, tok), _F32),
            jax.ShapeDtypeStruct((TOP_K, tok), jnp.int32),
            jax.ShapeDtypeStruct((N_EXPERTS, LANES), _F32),
        ],
        scratch_shapes=[pltpu.VMEM((N_EXPERTS, LANES), _F32), pltpu.VMEM((tm, tm), _BF16)],
        compiler_params=_cparams(("arbitrary", "arbitrary")),
        name="router",
    )(x, g, scale, shift, wr_t, br)


def _row_copy(src_ref, src_row, dst_ref, dst_row, sem):
    return pltpu.make_async_copy(
        src_ref.at[pl.ds(pl.multiple_of(src_row * ROW_WORDS, ROW_WORDS), ROW_WORDS), :],
        dst_ref.at[pl.ds(pl.multiple_of(dst_row * ROW_WORDS, ROW_WORDS), ROW_WORDS), :], sem)


def _dispatch_kernel(dest_hbm, hf_ref, init_ref, xs_ref, dest_smem0, dest_smem1, isem, sem,
                     *, tm, nsteps):
    del init_ref
    g = pl.program_id(0)
    n = tm * TOP_K
    dest_smem = (dest_smem0, dest_smem1)

    def index_copy(step, slot):
        off = pl.multiple_of(step * n, n)
        return pltpu.make_async_copy(dest_hbm.at[pl.ds(off, n)], dest_smem[slot], isem.at[slot])

    @pl.when(g == 0)
    def _():
        index_copy(0, 0).start()

    for slot in range(2):
        @pl.when(g % 2 == slot)
        def _():
            @pl.when(g + 1 < nsteps)
            def _():
                index_copy(g + 1, 1 - slot).start()

            index_copy(g, slot).wait()
            idx_ref = dest_smem[slot]

            def issue(r, carry):
                for k in range(TOP_K):
                    _row_copy(hf_ref, r, xs_ref, idx_ref[r * TOP_K + k], sem).start(priority=k % 2)
                return carry

            lax.fori_loop(0, tm, issue, 0, unroll=2)

    for k in range(TOP_K):
        pltpu.make_async_copy(hf_ref, xs_ref.at[pl.ds(0, tm * ROW_WORDS), :], sem).wait()


def _dispatch(dest_flat, hf, n_rows):
    tok = hf.shape[0] // ROW_WORDS
    tm = min(DISPATCH_TILE, tok)
    assert tok % tm == 0 and (tm * TOP_K) % 1024 == 0
    init = jnp.zeros((n_rows * ROW_WORDS, LANES), _F32)
    return pl.pallas_call(
        functools.partial(_dispatch_kernel, tm=tm, nsteps=tok // tm),
        grid=(tok // tm,),
        in_specs=[
            pl.BlockSpec(memory_space=pl.ANY),
            pl.BlockSpec((tm * ROW_WORDS, LANES), lambda i: (i, 0)),
            pl.BlockSpec(memory_space=pl.ANY),
        ],
        out_specs=pl.BlockSpec(memory_space=pl.ANY),
        out_shape=jax.ShapeDtypeStruct(init.shape, _F32),
        scratch_shapes=[pltpu.SMEM((tm * TOP_K,), jnp.int32),
                        pltpu.SMEM((tm * TOP_K,), jnp.int32),
                        pltpu.SemaphoreType.DMA((2,)), pltpu.SemaphoreType.DMA],
        input_output_aliases={2: 0},
        compiler_params=_cparams(("arbitrary",)),
        name="moe_dispatch",
    )(dest_flat, hf, init)


def _expert_kernel(be_ref, nu_ref, xs_ref, wgu_ref, bgu_ref, wdn_ref, bdn_ref, ys_ref, *, blk):
    del be_ref
    i = pl.program_id(0)
    dff = wdn_ref.shape[1]

    @pl.when(i < nu_ref[0])
    def _():
        x = jnp.concatenate([xs_ref[pl.ds(j, blk, stride=ROW_WORDS), :] for j in range(ROW_WORDS)],
                            axis=1).astype(_BF16)
        gu = jnp.dot(x, wgu_ref[0], preferred_element_type=_F32) + bgu_ref[0]
        g = jnp.minimum(gu[:, :dff], SWIGLU_LIMIT)
        u = jnp.clip(gu[:, dff:], -SWIGLU_LIMIT, SWIGLU_LIMIT)
        act = g * jax.nn.sigmoid(SWIGLU_ALPHA * g)
        mid = ((u + 1.0) * act).astype(_BF16)
        y = jnp.dot(mid, wdn_ref[0], preferred_element_type=_F32) + bdn_ref[0]
        for j in range(ROW_WORDS):
            ys_ref[pl.ds(j, blk, stride=ROW_WORDS), :] = y[:, LANES * j:LANES * (j + 1)]

    @pl.when(i >= nu_ref[0])
    def _():
        ys_ref[...] = jnp.zeros(ys_ref.shape, _F32)


def _experts(blk_e, n_used, xs, wgu, bgu, wdn, bdn):
    blk = EXPERT_BLOCK
    n_blk = xs.shape[0] // (blk * ROW_WORDS)
    dff = wdn.shape[1]
    grid_spec = pltpu.PrefetchScalarGridSpec(
        num_scalar_prefetch=2,
        grid=(n_blk,),
        in_specs=[
            pl.BlockSpec((blk * ROW_WORDS, LANES), lambda i, be, nu: (i, 0)),
            pl.BlockSpec((1, D_MODEL, 2 * dff), lambda i, be, nu: (be[i], 0, 0)),
            pl.BlockSpec((1, 1, 2 * dff), lambda i, be, nu: (be[i], 0, 0)),
            pl.BlockSpec((1, dff, D_MODEL), lambda i, be, nu: (be[i], 0, 0)),
            pl.BlockSpec((1, 1, D_MODEL), lambda i, be, nu: (be[i], 0, 0)),
        ],
        out_specs=pl.BlockSpec((blk * ROW_WORDS, LANES), lambda i, be, nu: (i, 0)),
    )
    return pl.pallas_call(
        functools.partial(_expert_kernel, blk=blk),
        grid_spec=grid_spec,
        out_shape=jax.ShapeDtypeStruct(xs.shape, _F32),
        compiler_params=_cparams(("arbitrary",)),
        name="moe_experts",
    )(blk_e, n_used, xs, wgu, bgu, wdn, bdn)


def _combine_kernel(dest_hbm, ys_ref, gate_ref, x_ref, gf_ref, o_ref,
                    dest_smem0, dest_smem1, buf_ref, isem, sem, *, tm, ns, nsteps):
    g = pl.program_id(0) * ns + pl.program_id(1)
    n = tm * TOP_K
    dest_smem = (dest_smem0, dest_smem1)

    def index_copy(step, slot):
        off = pl.multiple_of(step * n, n)
        return pltpu.make_async_copy(dest_hbm.at[pl.ds(off, n)], dest_smem[slot], isem.at[slot])

    def start_gathers(step, slot):
        index_copy(step, slot).wait()
        idx_ref = dest_smem[slot]

        def issue(r, carry):
            for k in range(TOP_K):
                _row_copy(ys_ref, idx_ref[r * TOP_K + k], buf_ref.at[slot], k * tm + r,
                          sem.at[slot]).start(priority=k % 2)
            return carry

        lax.fori_loop(0, tm, issue, 0, unroll=2)

        @pl.when(step + 1 < nsteps)
        def _():
            index_copy(step + 1, 1 - slot).start()

    def finish(slot):
        pltpu.make_async_copy(ys_ref.at[pl.ds(0, TOP_K * tm * ROW_WORDS), :], buf_ref.at[slot],
                              sem.at[slot]).wait()
        gate = gate_ref[...]
        cols = []
        for j in range(ROW_WORDS):
            acc = None
            for k in range(TOP_K):
                slab = buf_ref[slot, pl.ds(k * tm * ROW_WORDS + j, tm, stride=ROW_WORDS), :]
                term = gate[:, k:k + 1] * slab
                acc = term if acc is None else acc + term
            cols.append(acc)
        moe = jnp.concatenate(cols, axis=1)
        o_ref[0] = x_ref[0] + gf_ref[0] * moe

    @pl.when(g == 0)
    def _():
        index_copy(0, 0).start()
        start_gathers(0, 0)

    for slot in range(2):
        @pl.when(g % 2 == slot)
        def _():
            @pl.when(g + 1 < nsteps)
            def _():
                start_gathers(g + 1, 1 - slot)

            finish(slot)


def _combine(dest_flat, gate_tok, ys, x, gate_f):
    bsz, seq, _ = x.shape
    tm = min(ROUTE_TILE, seq)
    ns = seq // tm
    return pl.pallas_call(
        functools.partial(_combine_kernel, tm=tm, ns=ns, nsteps=bsz * ns),
        grid=(bsz, ns),
        in_specs=[
            pl.BlockSpec(memory_space=pl.ANY),
            pl.BlockSpec(memory_space=pl.ANY),
            pl.BlockSpec((tm, TOP_K), lambda b, i: (b * ns + i, 0)),
            pl.BlockSpec((1, tm, D_MODEL), lambda b, i: (b, i, 0)),
            pl.BlockSpec((1, 1, D_MODEL), lambda b, i: (b, 0, 0)),
        ],
        out_specs=pl.BlockSpec((1, tm, D_MODEL), lambda b, i: (b, i, 0)),
        out_shape=jax.ShapeDtypeStruct(x.shape, _F32),
        scratch_shapes=[
            pltpu.SMEM((tm * TOP_K,), jnp.int32),
            pltpu.SMEM((tm * TOP_K,), jnp.int32),
            pltpu.VMEM((2, TOP_K * tm * ROW_WORDS, LANES), _F32),
            pltpu.SemaphoreType.DMA((2,)),
            pltpu.SemaphoreType.DMA((2,)),
        ],
        compiler_params=_cparams(("arbitrary", "arbitrary")),
        name="moe_combine",
    )(dest_flat, ys, gate_tok, x, gate_f)


def _moe(x, g, scale, shift, gate_f, wr_t, br, wgu, bgu, wdn, bdn):
    bsz, seq, _ = x.shape
    tok = bsz * seq
    blk = EXPERT_BLOCK
    hf, idx, gate, rank, cnt = _router(x, g, scale, shift, wr_t, br)
    counts = cnt[:, 0].astype(jnp.int32)
    padded = (counts + blk - 1) // blk * blk
    pend = jnp.cumsum(padded)
    pstart = pend - padded
    eids = jnp.arange(N_EXPERTS, dtype=jnp.int32)[:, None, None]
    base = jnp.sum(jnp.where(idx[None] == eids, pstart[:, None, None], 0), axis=0)
    dest = (base + rank).T.reshape(tok * TOP_K)
    gate_tok = gate.T
    n_blk = (tok * TOP_K + N_EXPERTS * (blk - 1) + blk - 1) // blk
    blk_start = jnp.arange(n_blk, dtype=jnp.int32) * blk
    blk_e = jnp.minimum(jnp.sum(blk_start[:, None] >= pend[None, :], axis=1),
                        N_EXPERTS - 1).astype(jnp.int32)
    n_used = (pend[-1:] // blk).astype(jnp.int32)
    xs = _dispatch(dest, hf, n_blk * blk)
    ys = _experts(blk_e, n_used, xs, wgu, bgu, wdn, bdn)
    return _combine(dest, gate_tok, ys, x, gate_f)


def _prepare(norm_mix_g, norm_ffn_g, w_qkv_a, q_norm_a, k_norm_a, w_o_a, w_qkv_b, q_norm_b, k_norm_b,
             w_o_b, rel_bias, w_router, b_router, w_gate_up, b_gate_up, w_down, b_down):
    nq = A_HEADS * HEAD_DIM
    nk = A_KV_HEADS * HEAD_DIM
    nh = B_HEADS * HEAD_DIM
    scale = HEAD_DIM ** -0.5
    p = {}
    wa = w_qkv_a[0]
    p["wqt_a"] = wa[:, :nq].T.astype(_BF16)
    p["wk_a"] = wa[:, nq:nq + nk].astype(_BF16)
    p["wvt_a"] = wa[:, nq + nk:].T.astype(_BF16)
    p["gq_a"] = (q_norm_a[0] * (scale * np.log2(np.e))).reshape(HEAD_DIM, 1)
    p["gk_a"] = jnp.tile(k_norm_a[0], A_KV_HEADS).reshape(1, nk)
    p["wo_a"] = w_o_a[0].astype(_BF16)
    wb = w_qkv_b[0].astype(_BF16)
    p["w_b"] = wb
    p["gv_b"] = [jnp.concatenate([jnp.tile(q_norm_b[0, gi], B_HEADS) * scale,
                                  jnp.tile(k_norm_b[0, gi], B_HEADS)]).reshape(1, 2 * nh)
                 for gi in range(len(DILATED_CONFIGS))]
    p["bias_b"] = [_dilated_bias(rel_bias[:, gi * B_HEADS:(gi + 1) * B_HEADS], dil)
                   for gi, (_, dil) in enumerate(DILATED_CONFIGS)]
    p["wo_b"] = w_o_b[0].astype(_BF16)
    p["g_mix"] = [norm_mix_g[i].reshape(1, D_MODEL) for i in range(DEPTH)]
    p["g_ffn"] = [norm_ffn_g[i].reshape(1, D_MODEL) for i in range(DEPTH)]
    p["wr_t"] = [w_router[i].T for i in range(DEPTH)]
    p["br"] = [b_router[i].reshape(N_EXPERTS, 1) for i in range(DEPTH)]
    p["wgu"] = [w_gate_up[i].astype(_BF16) for i in range(DEPTH)]
    p["bgu"] = [b_gate_up[i].reshape(N_EXPERTS, 1, -1) for i in range(DEPTH)]
    p["wdn"] = [w_down[i].astype(_BF16) for i in range(DEPTH)]
    p["bdn"] = [b_down[i].reshape(N_EXPERTS, 1, -1) for i in range(DEPTH)]
    p["bd"] = _block_diag_ones()
    return p


def _trunk(x, c, w_ada, b_ada, p):
    bsz, seq, _ = x.shape
    tables = _rope_tables(seq)
    for i in range(DEPTH):
        mod = _ada_mod(c, w_ada[i], b_ada[i])
        shift_m, scale_m, gate_m, shift_f, scale_f, gate_f = [
            mod[:, D_MODEL * k:D_MODEL * (k + 1)].reshape(bsz, 1, D_MODEL) for k in range(6)]
        if i % 2 == 0:
            qt, k, vt = _proj_a(x, p["g_mix"][i], scale_m, shift_m, p["wqt_a"], p["wk_a"],
                                p["wvt_a"], p["gq_a"], p["gk_a"], tables, p["bd"])
            a = _flash_a(qt, k, vt)
            x = _oproj(a, p["wo_a"], x, gate_m)
        else:
            outs, lses = [], []
            for gi, (_, dil) in enumerate(DILATED_CONFIGS):
                qkv = _proj_b(x, p["g_mix"][i], scale_m, shift_m, p["w_b"], gi, p["gv_b"][gi],
                              p["bd"], dil)
                o, lse = _dilated_attn(qkv, p["bias_b"][gi])
                outs.append(o)
                lses.append(lse)
            x = _merge_oproj(outs, lses, p["wo_b"], x, gate_m)
        x = _moe(x, p["g_ffn"][i], scale_f, shift_f, gate_f, p["wr_t"][i], p["br"][i],
                 p["wgu"][i], p["bgu"][i], p["wdn"][i], p["bdn"][i])
    return x


def kernel(x_prompt, x_sample, c_prompt, c_sample, norm_mix_g, norm_ffn_g, w_ada, b_ada, w_qkv_a,
           q_norm_a, k_norm_a, w_o_a, w_qkv_b, q_norm_b, k_norm_b, w_o_b, rel_bias, w_router,
           b_router, w_gate_up, b_gate_up, w_down, b_down):
    p = _prepare(norm_mix_g, norm_ffn_g, w_qkv_a, q_norm_a, k_norm_a, w_o_a, w_qkv_b, q_norm_b,
                 k_norm_b, w_o_b, rel_bias, w_router, b_router, w_gate_up, b_gate_up, w_down, b_down)
    y_prompt = _trunk(x_prompt, c_prompt, w_ada, b_ada, p)
    y_sample = _trunk(x_sample, c_sample, w_ada, b_ada, p)
    return (y_prompt, y_sample)
```

```python
import functools

import numpy as np
import jax
import jax.numpy as jnp
from jax import lax
from jax.experimental import pallas as pl
from jax.experimental.pallas import tpu as pltpu

D_MODEL = 1024
HEAD_DIM = 64
A_HEADS = 16
A_KV_HEADS = 4
A_GROUP = A_HEADS // A_KV_HEADS
GRID_W = 64
ROPE_THETA = 10000.0
B_HEADS = 16
DILATED_CONFIGS = ((128, 1), (512, 4), (2048, 16))
NUM_BUCKETS = 32
MAX_DISTANCE = 1024
N_EXPERTS = 32
TOP_K = 4
SWIGLU_ALPHA = 1.702
SWIGLU_LIMIT = 7.0
RMS_EPS = 1e-6
NEG_INF = -1e30
DEPTH = 2

LANES = 128
SUBLANES = 8
VMEM_LIMIT = 56 * 1024 * 1024

ROW_TILE = 512
FA_TQ = 256
FA_TK = 512
FA_UNROLL = 8
V_ROWS = HEAD_DIM + 16
DIL_TQ = 512
DIL_SUB = 128
DIL_RADIUS = 64
ROUTE_TILE = 256
DISPATCH_TILE = 256
EXPERT_BLOCK = 256
ROW_WORDS = D_MODEL // LANES

_F32 = jnp.float32
_BF16 = jnp.bfloat16
_NT = (((1,), (1,)), ((), ()))


def _cparams(sem):
    return pltpu.CompilerParams(dimension_semantics=sem, vmem_limit_bytes=VMEM_LIMIT)


def _rms_mod(x, g, scale, shift):
    var = jnp.mean(x * x, axis=-1, keepdims=True)
    y = x * lax.rsqrt(var + RMS_EPS) * g
    return y * (1.0 + scale) + shift


def _head_rms(y, bd, gvec):
    ss = jnp.dot((y * y).astype(_BF16), bd, preferred_element_type=_F32)
    return y * lax.rsqrt(ss * (1.0 / HEAD_DIM) + RMS_EPS) * gvec


def _ada_kernel(c_ref, w_ref, b_ref, o_ref):
    c = c_ref[...]
    s = c * jax.nn.sigmoid(c)
    o_ref[...] = jnp.dot(s, w_ref[...], preferred_element_type=_F32,
                         precision=lax.Precision.HIGHEST) + b_ref[...]


def _ada_mod(c, w, b):
    bsz = c.shape[0]
    cp = jnp.pad(c, ((0, SUBLANES - bsz), (0, 0)))
    n = w.shape[1]
    out = pl.pallas_call(
        _ada_kernel,
        grid=(n // D_MODEL,),
        in_specs=[pl.BlockSpec((SUBLANES, D_MODEL), lambda j: (0, 0)),
                  pl.BlockSpec((D_MODEL, D_MODEL), lambda j: (0, j)),
                  pl.BlockSpec((1, D_MODEL), lambda j: (0, j))],
        out_specs=pl.BlockSpec((SUBLANES, D_MODEL), lambda j: (0, j)),
        out_shape=jax.ShapeDtypeStruct((SUBLANES, n), _F32),
        compiler_params=_cparams(("arbitrary",)),
        name="ada_mod",
    )(cp, w, b.reshape(1, n))
    return out[:bsz]


def _proj_a_kernel(x_ref, g_ref, sc_ref, sh_ref, wqt_ref, wk_ref, wvt_ref, gq_ref, gk_ref,
                   cos_ref, sina_ref, sinb_ref, cost_ref, sint_ref, bd_ref,
                   qt_ref, k_ref, vt_ref):
    h = _rms_mod(x_ref[0], g_ref[...], sc_ref[0], sh_ref[0]).astype(_BF16)
    acct = lax.dot_general(wqt_ref[...], h, _NT, preferred_element_type=_F32)
    cost = cost_ref[...]
    sint = sint_ref[...]
    gq = gq_ref[...]
    for j in range(A_HEADS):
        slab = acct[HEAD_DIM * j:HEAD_DIM * (j + 1), :]
        ss = jnp.sum(slab * slab, axis=0, keepdims=True)
        y = slab * lax.rsqrt(ss * (1.0 / HEAD_DIM) + RMS_EPS) * gq
        rot = jnp.concatenate([-y[16:32], y[0:16], -y[48:64], y[32:48]], axis=0)
        qt_ref[0, HEAD_DIM * j:HEAD_DIM * (j + 1), :] = (y * cost + rot * sint).astype(_BF16)
    acck = jnp.dot(h, wk_ref[...], preferred_element_type=_F32)
    y = _head_rms(acck, bd_ref[...], gk_ref[...])
    cos = cos_ref[...]
    sina = sina_ref[...]
    sinb = sinb_ref[...]
    for hh in range(2):
        yy = y[:, LANES * hh:LANES * (hh + 1)]
        r = yy * cos + pltpu.roll(yy, LANES - 16, 1) * sina + pltpu.roll(yy, 16, 1) * sinb
        for e in range(2):
            k_ref[0, 2 * hh + e] = r[:, HEAD_DIM * e:HEAD_DIM * (e + 1)].astype(_BF16)
    accv = lax.dot_general(wvt_ref[...], h, _NT, preferred_element_type=_F32)
    ones = jnp.ones((V_ROWS - HEAD_DIM, accv.shape[1]), _BF16)
    for j in range(A_KV_HEADS):
        vt_ref[0, j, 0, 0:HEAD_DIM, :] = accv[HEAD_DIM * j:HEAD_DIM * (j + 1), :].astype(_BF16)
        vt_ref[0, j, 0, HEAD_DIM:V_ROWS, :] = ones


def _rope_tables(seq_len):
    rows = seq_len // GRID_W
    row = jnp.repeat(jnp.arange(rows, dtype=_F32), GRID_W)
    col = jnp.tile(jnp.arange(GRID_W, dtype=_F32), rows)
    n_freq = HEAD_DIM // 4
    inv_freq = 1.0 / (ROPE_THETA ** (jnp.arange(n_freq, dtype=_F32) / n_freq))
    ang_r = row[:, None] * inv_freq[None, :]
    ang_c = col[:, None] * inv_freq[None, :]
    ang = jnp.concatenate([ang_r, ang_r, ang_c, ang_c], axis=-1)
    cos, sin = jnp.cos(ang), jnp.sin(ang)
    first = (np.arange(HEAD_DIM) % 32) < 16
    cos2 = jnp.tile(cos, (1, 2))
    sina = jnp.tile(jnp.where(first[None, :], -sin, 0.0), (1, 2))
    sinb = jnp.tile(jnp.where(first[None, :], 0.0, sin), (1, 2))
    return cos2, sina, sinb, cos.T, sin.T


def _block_diag_ones():
    i = np.arange(256)
    return jnp.asarray((i[:, None] // HEAD_DIM) == (i[None, :] // HEAD_DIM), dtype=_BF16)


def _proj_a(x, g, scale, shift, wqt, wk, wvt, gq_col, gk_t, tables, bd):
    bsz, seq, _ = x.shape
    tm = FA_TK
    assert seq % tm == 0
    cos2, sina, sinb, cost, sint = tables
    nq = A_HEADS * HEAD_DIM
    nk = A_KV_HEADS * HEAD_DIM
    const = lambda b, i: (0, 0)
    return pl.pallas_call(
        _proj_a_kernel,
        grid=(bsz, seq // tm),
        in_specs=[
            pl.BlockSpec((1, tm, D_MODEL), lambda b, i: (b, i, 0)),
            pl.BlockSpec((1, D_MODEL), const),
            pl.BlockSpec((1, 1, D_MODEL), lambda b, i: (b, 0, 0)),
            pl.BlockSpec((1, 1, D_MODEL), lambda b, i: (b, 0, 0)),
            pl.BlockSpec((nq, D_MODEL), const),
            pl.BlockSpec((D_MODEL, nk), const),
            pl.BlockSpec((nk, D_MODEL), const),
            pl.BlockSpec((HEAD_DIM, 1), const),
            pl.BlockSpec((1, nk), const),
            pl.BlockSpec((tm, LANES), lambda b, i: (i, 0)),
            pl.BlockSpec((tm, LANES), lambda b, i: (i, 0)),
            pl.BlockSpec((tm, LANES), lambda b, i: (i, 0)),
            pl.BlockSpec((HEAD_DIM, tm), lambda b, i: (0, i)),
            pl.BlockSpec((HEAD_DIM, tm), lambda b, i: (0, i)),
            pl.BlockSpec((256, 256), const),
        ],
        out_specs=[
            pl.BlockSpec((1, nq, tm), lambda b, i: (b, 0, i)),
            pl.BlockSpec((1, A_KV_HEADS, tm, HEAD_DIM), lambda b, i: (b, 0, i, 0)),
            pl.BlockSpec((1, A_KV_HEADS, 1, V_ROWS, tm), lambda b, i: (b, 0, i, 0, 0)),
        ],
        out_shape=[
            jax.ShapeDtypeStruct((bsz, nq, seq), _BF16),
            jax.ShapeDtypeStruct((bsz, A_KV_HEADS, seq, HEAD_DIM), _BF16),
            jax.ShapeDtypeStruct((bsz, A_KV_HEADS, seq // tm, V_ROWS, tm), _BF16),
        ],
        compiler_params=_cparams(("parallel", "parallel")),
        name="proj_a",
    )(x, g, scale, shift, wqt, wk, wvt, gq_col, gk_t, cos2, sina, sinb, cost, sint, bd)


def _flash_kernel(qt_ref, k_ref, vt_ref, o_ref, s_ref, m_ref, acc_ref, *, tq, tk, nchunks):
    qt4 = jnp.concatenate([qt_ref[0, HEAD_DIM * hq:HEAD_DIM * (hq + 1), :] for hq in range(A_GROUP)],
                          axis=1)
    m_ref[...] = jnp.full(m_ref.shape, -jnp.inf, _F32)
    acc_ref[...] = jnp.zeros(acc_ref.shape, _F32)

    def scores(kc, slot):
        kb = k_ref[0, 0, pl.ds(pl.multiple_of(kc * tk, tk), tk), :]
        s_ref[slot] = jnp.dot(kb, qt4, preferred_element_type=_F32)

    def step(kc, slot, kc_next, slot_next):
        vb = vt_ref[0, 0, kc]
        if kc_next is not None:
            kb = k_ref[0, 0, pl.ds(pl.multiple_of(kc_next * tk, tk), tk), :]
        for hq in range(A_GROUP):
            sl = slice(tq * hq, tq * (hq + 1))
            if kc_next is not None:
                s_ref[slot_next, :, sl] = jnp.dot(kb, qt4[:, sl], preferred_element_type=_F32)
            s = s_ref[slot, :, sl]
            m_prev = m_ref[:, sl]
            m_new = jnp.maximum(m_prev, jnp.max(s, axis=0, keepdims=True))
            alpha = jnp.exp2(m_prev - m_new)
            p = jnp.exp2(s - m_new).astype(_BF16)
            acc_ref[:, sl] = alpha * acc_ref[:, sl] + jnp.dot(vb, p, preferred_element_type=_F32)
            m_ref[:, sl] = m_new

    def run(kc0, last):
        for j in range(FA_UNROLL):
            final = last and j == FA_UNROLL - 1
            step(kc0 + j, j % 2, None if final else kc0 + j + 1, (j + 1) % 2)

    scores(0, 0)

    def body(c, carry):
        run(FA_UNROLL * c, False)
        return carry

    lax.fori_loop(0, nchunks // FA_UNROLL - 1, body, 0)
    run(nchunks - FA_UNROLL, True)
    o = acc_ref[0:HEAD_DIM, :] / acc_ref[HEAD_DIM:HEAD_DIM + 1, :]
    o = jnp.concatenate([o[:, tq * hq:tq * (hq + 1)] for hq in range(A_GROUP)], axis=0)
    o_ref[0] = o.T.astype(_BF16)


def _flash_a(qt, k, vt):
    bsz, nq, seq = qt.shape
    tq = min(FA_TQ, seq)
    tk = vt.shape[-1]
    nchunks = seq // tk
    assert nchunks % FA_UNROLL == 0 and FA_UNROLL % 2 == 0 and seq % tq == 0
    return pl.pallas_call(
        functools.partial(_flash_kernel, tq=tq, tk=tk, nchunks=nchunks),
        grid=(bsz, A_KV_HEADS, seq // tq),
        in_specs=[
            pl.BlockSpec((1, A_GROUP * HEAD_DIM, tq), lambda b, h, i: (b, h, i)),
            pl.BlockSpec((1, 1, seq, HEAD_DIM), lambda b, h, i: (b, h, 0, 0)),
            pl.BlockSpec((1, 1, nchunks, V_ROWS, tk), lambda b, h, i: (b, h, 0, 0, 0)),
        ],
        out_specs=pl.BlockSpec((1, tq, A_GROUP * HEAD_DIM), lambda b, h, i: (b, i, h)),
        out_shape=jax.ShapeDtypeStruct((bsz, seq, nq), _BF16),
        scratch_shapes=[
            pltpu.VMEM((2, tk, A_GROUP * tq), _F32),
            pltpu.VMEM((1, A_GROUP * tq), _F32),
            pltpu.VMEM((V_ROWS, A_GROUP * tq), _F32),
        ],
        compiler_params=_cparams(("parallel", "parallel", "parallel")),
        name="flash_a",
    )(qt, k, vt)


def _oproj_kernel(a_ref, w_ref, x_ref, gate_ref, o_ref):
    y = jnp.dot(a_ref[0], w_ref[...], preferred_element_type=_F32)
    o_ref[0] = x_ref[0] + gate_ref[0] * y


def _oproj(a, w, x, gate):
    bsz, seq, _ = x.shape
    tm = min(ROW_TILE, seq)
    return pl.pallas_call(
        _oproj_kernel,
        grid=(bsz, seq // tm),
        in_specs=[
            pl.BlockSpec((1, tm, D_MODEL), lambda b, i: (b, i, 0)),
            pl.BlockSpec((D_MODEL, D_MODEL), lambda b, i: (0, 0)),
            pl.BlockSpec((1, tm, D_MODEL), lambda b, i: (b, i, 0)),
            pl.BlockSpec((1, 1, D_MODEL), lambda b, i: (b, 0, 0)),
        ],
        out_specs=pl.BlockSpec((1, tm, D_MODEL), lambda b, i: (b, i, 0)),
        out_shape=jax.ShapeDtypeStruct(x.shape, _F32),
        compiler_params=_cparams(("parallel", "parallel")),
        name="oproj",
    )(a, w, x, gate)


def _proj_b_kernel(x_ref, g_ref, sc_ref, sh_ref, w_ref, gv_ref, bd_ref, o_ref, scr_ref, *, dil, tm):
    h = _rms_mod(x_ref[0], g_ref[...], sc_ref[0], sh_ref[0])
    rows = tm // dil
    if dil > 1:
        for c in range(D_MODEL // LANES):
            scr_ref[c] = h[:, LANES * c:LANES * (c + 1)]
        h = jnp.concatenate(
            [jnp.concatenate([scr_ref[c, pl.ds(r, rows, stride=dil), :]
                              for c in range(D_MODEL // LANES)], axis=1)
             for r in range(dil)], axis=0)
    bd = bd_ref[...]
    acc = jnp.dot(h.astype(_BF16), w_ref[...], preferred_element_type=_F32)
    nqk = 2 * B_HEADS * HEAD_DIM

    def emit(col, y):
        for r in range(dil):
            o_ref[0, r, :, col:col + y.shape[1]] = y[rows * r:rows * (r + 1)].astype(_BF16)

    for c in range(nqk // 256):
        sl = slice(256 * c, 256 * (c + 1))
        emit(256 * c, _head_rms(acc[:, sl], bd, gv_ref[:, sl]))
    emit(nqk, acc[:, nqk:])


def _proj_b(x, g, scale, shift, w, gi, gvec, bd, dil):
    bsz, seq, _ = x.shape
    tm = min(ROW_TILE, seq)
    n = 3 * B_HEADS * HEAD_DIM
    const = lambda b, i: (0, 0)
    return pl.pallas_call(
        functools.partial(_proj_b_kernel, dil=dil, tm=tm),
        grid=(bsz, seq // tm),
        in_specs=[
            pl.BlockSpec((1, tm, D_MODEL), lambda b, i: (b, i, 0)),
            pl.BlockSpec((1, D_MODEL), const),
            pl.BlockSpec((1, 1, D_MODEL), lambda b, i: (b, 0, 0)),
            pl.BlockSpec((1, 1, D_MODEL), lambda b, i: (b, 0, 0)),
            pl.BlockSpec((D_MODEL, n), lambda b, i: (0, gi)),
            pl.BlockSpec((1, 2 * B_HEADS * HEAD_DIM), const),
            pl.BlockSpec((256, 256), const),
        ],
        out_specs=pl.BlockSpec((1, dil, tm // dil, n), lambda b, i: (b, 0, i, 0)),
        out_shape=jax.ShapeDtypeStruct((bsz, dil, seq // dil, n), _BF16),
        scratch_shapes=[pltpu.VMEM((D_MODEL // LANES, tm, LANES), _F32)],
        compiler_params=_cparams(("parallel", "parallel")),
        name="proj_b",
    )(x, g, scale, shift, w, gvec, bd)


def _dilated_kernel(main_ref, kp_ref, vp_ref, kn_ref, vn_ref, bias_ref, o_ref, lse_ref,
                    kf_ref, vf_ref, *, tq, length):
    nh = B_HEADS * HEAD_DIM
    l0 = pl.program_id(2) * tq
    kf_ref[0:DIL_RADIUS] = kp_ref[0, 0]
    kf_ref[DIL_RADIUS:DIL_RADIUS + tq] = main_ref[0, 0, :, nh:2 * nh]
    kf_ref[DIL_RADIUS + tq:] = kn_ref[0, 0]
    vf_ref[0:DIL_RADIUS] = vp_ref[0, 0]
    vf_ref[DIL_RADIUS:DIL_RADIUS + tq] = main_ref[0, 0, :, 2 * nh:3 * nh]
    vf_ref[DIL_RADIUS + tq:] = vn_ref[0, 0]
    width = DIL_SUB + 2 * DIL_RADIUS
    low_half = lax.broadcasted_iota(jnp.int32, (DIL_SUB, LANES), 1) < HEAD_DIM
    col = lax.broadcasted_iota(jnp.int32, (DIL_SUB, width), 1)

    def sub_block(u, carry):
        r0 = pl.multiple_of(u * DIL_SUB, DIL_SUB)
        kpos = col + (l0 + r0 - DIL_RADIUS)
        valid = (kpos >= 0) & (kpos < length)
        for m in range(B_HEADS // 2):
            sl = slice(LANES * m, LANES * (m + 1))
            qp = main_ref[0, 0, pl.ds(r0, DIL_SUB), sl]
            kw = kf_ref[pl.ds(r0, width), sl]
            vw = vf_ref[pl.ds(r0, width), sl]
            res = []
            for hh in range(2):
                keep = low_half if hh == 0 else jnp.logical_not(low_half)
                qm = jnp.where(keep, qp, jnp.zeros_like(qp))
                s = lax.dot_general(qm, kw, _NT, preferred_element_type=_F32) + bias_ref[2 * m + hh]
                s = jnp.where(valid, s, NEG_INF)
                mx = jnp.max(s, axis=1, keepdims=True)
                p = jnp.exp(s - mx)
                den = jnp.sum(p, axis=1, keepdims=True)
                pv = jnp.dot(p.astype(_BF16), vw, preferred_element_type=_F32)
                res.append((pv / den, mx + jnp.log(den)))
            o_ref[0, 0, pl.ds(r0, DIL_SUB), sl] = jnp.where(low_half, res[0][0], res[1][0])
            lse_ref[0, 0, pl.ds(r0, DIL_SUB), sl] = jnp.where(
                low_half, jnp.broadcast_to(res[0][1], (DIL_SUB, LANES)),
                jnp.broadcast_to(res[1][1], (DIL_SUB, LANES)))
        return carry

    lax.fori_loop(0, tq // DIL_SUB, sub_block, 0)


def _dilated_attn(qkv, bias):
    bsz, dil, length, n = qkv.shape
    nh = B_HEADS * HEAD_DIM
    tq = min(DIL_TQ, length)
    nb = tq // DIL_RADIUS
    last = length // DIL_RADIUS - 1
    prev_ix = lambda i: jnp.maximum(i * nb - 1, 0)
    next_ix = lambda i: jnp.minimum((i + 1) * nb, last)
    halo = (1, 1, DIL_RADIUS, nh)
    width = DIL_SUB + 2 * DIL_RADIUS
    out = jax.ShapeDtypeStruct((bsz, dil, length, nh), _F32)
    return pl.pallas_call(
        functools.partial(_dilated_kernel, tq=tq, length=length),
        grid=(bsz, dil, length // tq),
        in_specs=[
            pl.BlockSpec((1, 1, tq, n), lambda b, r, i: (b, r, i, 0)),
            pl.BlockSpec(halo, lambda b, r, i: (b, r, prev_ix(i), 1)),
            pl.BlockSpec(halo, lambda b, r, i: (b, r, prev_ix(i), 2)),
            pl.BlockSpec(halo, lambda b, r, i: (b, r, next_ix(i), 1)),
            pl.BlockSpec(halo, lambda b, r, i: (b, r, next_ix(i), 2)),
            pl.BlockSpec((B_HEADS, DIL_SUB, width), lambda b, r, i: (0, 0, 0)),
        ],
        out_specs=[pl.BlockSpec((1, 1, tq, nh), lambda b, r, i: (b, r, i, 0)),
                   pl.BlockSpec((1, 1, tq, nh), lambda b, r, i: (b, r, i, 0))],
        out_shape=[out, out],
        scratch_shapes=[pltpu.VMEM((tq + 2 * DIL_RADIUS, nh), _BF16),
                        pltpu.VMEM((tq + 2 * DIL_RADIUS, nh), _BF16)],
        compiler_params=_cparams(("parallel", "parallel", "parallel")),
        name="dilated_attn",
    )(qkv, qkv, qkv, qkv, qkv, bias)


def _t5_bucket(rel):
    nb = NUM_BUCKETS // 2
    max_exact = nb // 2
    ret = (rel > 0).astype(np.int32) * nb
    n = np.abs(rel)
    large = max_exact + (np.log(np.maximum(n, 1) / max_exact) / np.log(MAX_DISTANCE / max_exact)
                         * (nb - max_exact)).astype(np.int32)
    large = np.minimum(large, nb - 1)
    return (ret + np.where(n < max_exact, n, large)).astype(np.int32)


def _dilated_bias(rel_bias_g, dil):
    width = DIL_SUB + 2 * DIL_RADIUS
    rel = np.arange(width)[None, :] - DIL_RADIUS - np.arange(DIL_SUB)[:, None]
    bucket = _t5_bucket(rel * dil)
    band = np.abs(rel) <= DIL_RADIUS
    onehot = (bucket.reshape(-1, 1) == np.arange(NUM_BUCKETS)[None, :]).astype(np.float32)
    bias = jnp.dot(jnp.asarray(onehot), rel_bias_g.astype(_F32), precision=lax.Precision.HIGHEST)
    bias = bias.reshape(DIL_SUB, width, B_HEADS).transpose(2, 0, 1)
    return jnp.where(jnp.asarray(band)[None], bias, NEG_INF)


def _merge_oproj_kernel(o0_ref, l0_ref, o1_ref, l1_ref, o2_ref, l2_ref, w_ref, x_ref, gate_ref,
                        out_ref, so1, sl1, so2, sl2, *, tm):
    d1 = DILATED_CONFIGS[1][1]
    d2 = DILATED_CONFIGS[2][1]
    nc = B_HEADS * HEAD_DIM // LANES

    def to_token_order(src_ref, scr_ref, dil):
        for r in range(dil):
            for c in range(nc):
                scr_ref[c, pl.ds(r, tm // dil, stride=dil), :] = (
                    src_ref[0, r, :, LANES * c:LANES * (c + 1)])
        return jnp.concatenate([scr_ref[c] for c in range(nc)], axis=1)

    o1 = to_token_order(o1_ref, so1, d1)
    l1 = to_token_order(l1_ref, sl1, d1)
    o2 = to_token_order(o2_ref, so2, d2)
    l2 = to_token_order(l2_ref, sl2, d2)
    l0 = l0_ref[0, 0]
    mx = jnp.maximum(jnp.maximum(l0, l1), l2)
    e0, e1, e2 = jnp.exp(l0 - mx), jnp.exp(l1 - mx), jnp.exp(l2 - mx)
    o = (e0 * o0_ref[0, 0] + e1 * o1 + e2 * o2) / (e0 + e1 + e2)
    y = jnp.dot(o.astype(_BF16), w_ref[...], preferred_element_type=_F32)
    out_ref[0] = x_ref[0] + gate_ref[0] * y


def _merge_oproj(outs, lses, w, x, gate):
    bsz, seq, _ = x.shape
    tm = min(ROUTE_TILE, seq)
    nh = B_HEADS * HEAD_DIM
    specs = []
    for (_, dil) in DILATED_CONFIGS:
        spec = pl.BlockSpec((1, dil, tm // dil, nh), lambda b, i: (b, 0, i, 0))
        specs += [spec, spec]
    return pl.pallas_call(
        functools.partial(_merge_oproj_kernel, tm=tm),
        grid=(bsz, seq // tm),
        in_specs=specs + [
            pl.BlockSpec((nh, D_MODEL), lambda b, i: (0, 0)),
            pl.BlockSpec((1, tm, D_MODEL), lambda b, i: (b, i, 0)),
            pl.BlockSpec((1, 1, D_MODEL), lambda b, i: (b, 0, 0)),
        ],
        out_specs=pl.BlockSpec((1, tm, D_MODEL), lambda b, i: (b, i, 0)),
        out_shape=jax.ShapeDtypeStruct(x.shape, _F32),
        scratch_shapes=[pltpu.VMEM((nh // LANES, tm, LANES), _F32)] * 4,
        compiler_params=_cparams(("parallel", "parallel")),
        name="merge_oproj",
    )(outs[0], lses[0], outs[1], lses[1], outs[2], lses[2], w, x, gate)


def _router_kernel(x_ref, g_ref, sc_ref, sh_ref, wr_ref, br_ref,
                   hf_ref, idx_ref, gate_ref, rank_ref, cnt_ref, carry_ref, tri_ref, *, tm):
    first = (pl.program_id(0) == 0) & (pl.program_id(1) == 0)

    @pl.when(first)
    def _():
        carry_ref[...] = jnp.zeros(carry_ref.shape, _F32)
        r = lax.broadcasted_iota(jnp.int32, (tm, tm), 0)
        c = lax.broadcasted_iota(jnp.int32, (tm, tm), 1)
        tri_ref[...] = (r < c).astype(_BF16)

    hf = _rms_mod(x_ref[0], g_ref[...], sc_ref[0], sh_ref[0])
    for j in range(ROW_WORDS):
        hf_ref[pl.ds(j, tm, stride=ROW_WORDS), :] = hf[:, LANES * j:LANES * (j + 1)]
    logits = lax.dot_general(wr_ref[...], hf, _NT, preferred_element_type=_F32,
                             precision=lax.Precision.HIGHEST) + br_ref[...]
    eio = lax.broadcasted_iota(jnp.int32, logits.shape, 0).astype(_F32)
    cur = logits
    vals, sel = [], []
    for _ in range(TOP_K):
        mx = jnp.max(cur, axis=0, keepdims=True)
        ix = jnp.min(jnp.where(cur == mx, eio, float(N_EXPERTS)), axis=0, keepdims=True)
        vals.append(mx)
        sel.append(ix)
        cur = jnp.where(eio == ix, -jnp.inf, cur)
    ex = [jnp.exp(v - vals[0]) for v in vals]
    den = ex[0] + ex[1] + ex[2] + ex[3]
    member = jnp.zeros(logits.shape, _F32)
    for ix in sel:
        member = member + (eio == ix).astype(_F32)
    before = carry_ref[:, 0:1] + jnp.dot(member.astype(_BF16), tri_ref[...],
                                         preferred_element_type=_F32)
    ranks = [jnp.sum(jnp.where(eio == ix, before, 0.0), axis=0, keepdims=True) for ix in sel]
    idx_ref[...] = jnp.concatenate(sel, axis=0).astype(jnp.int32)
    gate_ref[...] = jnp.concatenate([e / den for e in ex], axis=0)
    rank_ref[...] = jnp.concatenate(ranks, axis=0).astype(jnp.int32)
    carry_ref[...] = carry_ref[...] + jnp.sum(member, axis=1, keepdims=True)
    cnt_ref[...] = carry_ref[...]


def _router(x, g, scale, shift, wr_t, br):
    bsz, seq, _ = x.shape
    tm = min(ROUTE_TILE, seq)
    ns = seq // tm
    tok = bsz * seq
    const = lambda b, i: (0, 0)
    flat = lambda b, i: (0, b * ns + i)
    return pl.pallas_call(
        functools.partial(_router_kernel, tm=tm),
        grid=(bsz, ns),
        in_specs=[
            pl.BlockSpec((1, tm, D_MODEL), lambda b, i: (b, i, 0)),
            pl.BlockSpec((1, D_MODEL), const),
            pl.BlockSpec((1, 1, D_MODEL), lambda b, i: (b, 0, 0)),
            pl.BlockSpec((1, 1, D_MODEL), lambda b, i: (b, 0, 0)),
            pl.BlockSpec((N_EXPERTS, D_MODEL), const),
            pl.BlockSpec((N_EXPERTS, 1), const),
        ],
        out_specs=[
            pl.BlockSpec((tm * ROW_WORDS, LANES), lambda b, i: (b * ns + i, 0)),
            pl.BlockSpec((TOP_K, tm), flat),
            pl.BlockSpec((TOP_K, tm), flat),
            pl.BlockSpec((TOP_K, tm), flat),
            pl.BlockSpec((N_EXPERTS, LANES), const),
        ],
        out_shape=[
            jax.ShapeDtypeStruct((tok * ROW_WORDS, LANES), _F32),
            jax.ShapeDtypeStruct((TOP_K, tok), jnp.int32),
            jax.ShapeDtypeStruct((TOP_K, tok), _F32),
            jax.ShapeDtypeStruct((TOP_K, tok), jnp.int32),
            jax.ShapeDtypeStruct((N_EXPERTS, LANES), _F32),
        ],
        scratch_shapes=[pltpu.VMEM((N_EXPERTS, LANES), _F32), pltpu.VMEM((tm, tm), _BF16)],
        compiler_params=_cparams(("arbitrary", "arbitrary")),
        name="router",
    )(x, g, scale, shift, wr_t, br)


def _row_copy(src_ref, src_row, dst_ref, dst_row, sem):
    return pltpu.make_async_copy(
        src_ref.at[pl.ds(pl.multiple_of(src_row * ROW_WORDS, ROW_WORDS), ROW_WORDS), :],
        dst_ref.at[pl.ds(pl.multiple_of(dst_row * ROW_WORDS, ROW_WORDS), ROW_WORDS), :], sem)


def _dispatch_kernel(pad_start_ref, pad_len_ref, tail_ref, dest_hbm, hf_ref, xs_ref,
                     dest_smem0, dest_smem1, zero_ref, isem, sem, zsem, *, tm, nsteps):
    g = pl.program_id(0)
    n = tm * TOP_K
    dest_smem = (dest_smem0, dest_smem1)

    def index_copy(step, slot):
        off = pl.multiple_of(step * n, n)
        return pltpu.make_async_copy(dest_hbm.at[pl.ds(off, n)], dest_smem[slot], isem.at[slot])

    def zero_fill(e, wait):
        start = pad_start_ref[e]
        length = pad_len_ref[e]
        for bit in reversed(range(EXPERT_BLOCK.bit_length() - 1)):
            size = 1 << bit
            row = start + (length & ~(2 * size - 1))
            cp = pltpu.make_async_copy(
                zero_ref.at[pl.ds(0, size * ROW_WORDS), :],
                xs_ref.at[pl.ds(pl.multiple_of(row * ROW_WORDS, ROW_WORDS), size * ROW_WORDS), :],
                zsem)

            @pl.when((length & size) != 0)
            def _():
                if wait:
                    cp.wait()
                else:
                    cp.start()

    @pl.when(g == 0)
    def _():
        index_copy(0, 0).start()
        zero_ref[...] = jnp.zeros(zero_ref.shape, _F32)

        def fill(e, carry):
            zero_fill(e, False)
            return carry

        lax.fori_loop(0, N_EXPERTS, fill, 0)

        def done(e, carry):
            zero_fill(e, True)
            return carry

        lax.fori_loop(0, N_EXPERTS, done, 0)

        def tail_copy(j):
            row = (tail_ref[0] + j) * EXPERT_BLOCK
            return pltpu.make_async_copy(
                zero_ref,
                xs_ref.at[pl.ds(pl.multiple_of(row * ROW_WORDS, ROW_WORDS),
                                EXPERT_BLOCK * ROW_WORDS), :], zsem)

        def tail_fill(j, carry):
            tail_copy(j).start()
            return carry

        lax.fori_loop(0, tail_ref[1], tail_fill, 0)

        def tail_done(j, carry):
            tail_copy(j).wait()
            return carry

        lax.fori_loop(0, tail_ref[1], tail_done, 0)

    for slot in range(2):
        @pl.when(g % 2 == slot)
        def _():
            @pl.when(g + 1 < nsteps)
            def _():
                index_copy(g + 1, 1 - slot).start()

            index_copy(g, slot).wait()
            idx_ref = dest_smem[slot]

            def issue(r, carry):
                for k in range(TOP_K):
                    _row_copy(hf_ref, r, xs_ref, idx_ref[r * TOP_K + k], sem).start(priority=k % 2)
                return carry

            lax.fori_loop(0, tm, issue, 0, unroll=2)

    for k in range(TOP_K):
        pltpu.make_async_copy(hf_ref, xs_ref.at[pl.ds(0, tm * ROW_WORDS), :], sem).wait()


def _dispatch(pad_start, pad_len, tail, dest_flat, hf, n_rows):
    tok = hf.shape[0] // ROW_WORDS
    tm = min(DISPATCH_TILE, tok)
    assert tok % tm == 0 and (tm * TOP_K) % 1024 == 0
    grid_spec = pltpu.PrefetchScalarGridSpec(
        num_scalar_prefetch=3,
        grid=(tok // tm,),
        in_specs=[
            pl.BlockSpec(memory_space=pl.ANY),
            pl.BlockSpec((tm * ROW_WORDS, LANES), lambda i, ps, pn, tl: (i, 0)),
        ],
        out_specs=pl.BlockSpec(memory_space=pl.ANY),
        scratch_shapes=[pltpu.SMEM((tm * TOP_K,), jnp.int32),
                        pltpu.SMEM((tm * TOP_K,), jnp.int32),
                        pltpu.VMEM((EXPERT_BLOCK * ROW_WORDS, LANES), _F32),
                        pltpu.SemaphoreType.DMA((2,)), pltpu.SemaphoreType.DMA,
                        pltpu.SemaphoreType.DMA],
    )
    return pl.pallas_call(
        functools.partial(_dispatch_kernel, tm=tm, nsteps=tok // tm),
        grid_spec=grid_spec,
        out_shape=jax.ShapeDtypeStruct((n_rows * ROW_WORDS, LANES), _F32),
        compiler_params=_cparams(("arbitrary",)),
        name="moe_dispatch",
    )(pad_start, pad_len, tail, dest_flat, hf)


def _expert_kernel(be_ref, nu_ref, xs_ref, wgu_ref, bgu_ref, wdn_ref, bdn_ref, ys_ref, *, blk):
    del be_ref
    i = pl.program_id(0)
    dff = wdn_ref.shape[1]

    @pl.when(i < nu_ref[0])
    def _():
        x = jnp.concatenate([xs_ref[pl.ds(j, blk, stride=ROW_WORDS), :] for j in range(ROW_WORDS)],
                            axis=1).astype(_BF16)
        gu = jnp.dot(x, wgu_ref[0], preferred_element_type=_F32) + bgu_ref[0]
        g = jnp.minimum(gu[:, :dff], SWIGLU_LIMIT)
        u = jnp.clip(gu[:, dff:], -SWIGLU_LIMIT, SWIGLU_LIMIT)
        act = g * jax.nn.sigmoid(SWIGLU_ALPHA * g)
        mid = ((u + 1.0) * act).astype(_BF16)
        y = jnp.dot(mid, wdn_ref[0], preferred_element_type=_F32) + bdn_ref[0]
        for j in range(ROW_WORDS):
            ys_ref[pl.ds(j, blk, stride=ROW_WORDS), :] = y[:, LANES * j:LANES * (j + 1)]

    @pl.when(i >= nu_ref[0])
    def _():
        ys_ref[...] = jnp.zeros(ys_ref.shape, _F32)


def _experts(blk_e, n_used, xs, wgu, bgu, wdn, bdn):
    blk = EXPERT_BLOCK
    n_blk = xs.shape[0] // (blk * ROW_WORDS)
    dff = wdn.shape[1]
    grid_spec = pltpu.PrefetchScalarGridSpec(
        num_scalar_prefetch=2,
        grid=(n_blk,),
        in_specs=[
            pl.BlockSpec((blk * ROW_WORDS, LANES), lambda i, be, nu: (jnp.minimum(i, nu[0] - 1), 0)),
            pl.BlockSpec((1, D_MODEL, 2 * dff), lambda i, be, nu: (be[i], 0, 0)),
            pl.BlockSpec((1, 1, 2 * dff), lambda i, be, nu: (be[i], 0, 0)),
            pl.BlockSpec((1, dff, D_MODEL), lambda i, be, nu: (be[i], 0, 0)),
            pl.BlockSpec((1, 1, D_MODEL), lambda i, be, nu: (be[i], 0, 0)),
        ],
        out_specs=pl.BlockSpec((blk * ROW_WORDS, LANES), lambda i, be, nu: (i, 0)),
    )
    return pl.pallas_call(
        functools.partial(_expert_kernel, blk=blk),
        grid_spec=grid_spec,
        out_shape=jax.ShapeDtypeStruct(xs.shape, _F32),
        compiler_params=_cparams(("arbitrary",)),
        name="moe_experts",
    )(blk_e, n_used, xs, wgu, bgu, wdn, bdn)


def _combine_kernel(dest_hbm, ys_ref, gate_ref, x_ref, gf_ref, o_ref,
                    dest_smem0, dest_smem1, buf_ref, isem, sem, *, tm, ns, nsteps):
    g = pl.program_id(0) * ns + pl.program_id(1)
    n = tm * TOP_K
    dest_smem = (dest_smem0, dest_smem1)

    def index_copy(step, slot):
        off = pl.multiple_of(step * n, n)
        return pltpu.make_async_copy(dest_hbm.at[pl.ds(off, n)], dest_smem[slot], isem.at[slot])

    def start_gathers(step, slot):
        index_copy(step, slot).wait()
        idx_ref = dest_smem[slot]

        def issue(r, carry):
            for k in range(TOP_K):
                _row_copy(ys_ref, idx_ref[r * TOP_K + k], buf_ref.at[slot], k * tm + r,
                          sem.at[slot]).start(priority=k % 2)
            return carry

        lax.fori_loop(0, tm, issue, 0, unroll=2)

        @pl.when(step + 1 < nsteps)
        def _():
            index_copy(step + 1, 1 - slot).start()

    def finish(slot):
        pltpu.make_async_copy(ys_ref.at[pl.ds(0, TOP_K * tm * ROW_WORDS), :], buf_ref.at[slot],
                              sem.at[slot]).wait()
        gate = gate_ref[...]
        cols = []
        for j in range(ROW_WORDS):
            acc = None
            for k in range(TOP_K):
                slab = buf_ref[slot, pl.ds(k * tm * ROW_WORDS + j, tm, stride=ROW_WORDS), :]
                term = gate[:, k:k + 1] * slab
                acc = term if acc is None else acc + term
            cols.append(acc)
        moe = jnp.concatenate(cols, axis=1)
        o_ref[0] = x_ref[0] + gf_ref[0] * moe

    @pl.when(g == 0)
    def _():
        index_copy(0, 0).start()
        start_gathers(0, 0)

    for slot in range(2):
        @pl.when(g % 2 == slot)
        def _():
            @pl.when(g + 1 < nsteps)
            def _():
                start_gathers(g + 1, 1 - slot)

            finish(slot)


def _combine(dest_flat, gate_tok, ys, x, gate_f):
    bsz, seq, _ = x.shape
    tm = min(ROUTE_TILE, seq)
    ns = seq // tm
    return pl.pallas_call(
        functools.partial(_combine_kernel, tm=tm, ns=ns, nsteps=bsz * ns),
        grid=(bsz, ns),
        in_specs=[
            pl.BlockSpec(memory_space=pl.ANY),
            pl.BlockSpec(memory_space=pl.ANY),
            pl.BlockSpec((tm, TOP_K), lambda b, i: (b * ns + i, 0)),
            pl.BlockSpec((1, tm, D_MODEL), lambda b, i: (b, i, 0)),
            pl.BlockSpec((1, 1, D_MODEL), lambda b, i: (b, 0, 0)),
        ],
        out_specs=pl.BlockSpec((1, tm, D_MODEL), lambda b, i: (b, i, 0)),
        out_shape=jax.ShapeDtypeStruct(x.shape, _F32),
        scratch_shapes=[
            pltpu.SMEM((tm * TOP_K,), jnp.int32),
            pltpu.SMEM((tm * TOP_K,), jnp.int32),
            pltpu.VMEM((2, TOP_K * tm * ROW_WORDS, LANES), _F32),
            pltpu.SemaphoreType.DMA((2,)),
            pltpu.SemaphoreType.DMA((2,)),
        ],
        compiler_params=_cparams(("arbitrary", "arbitrary")),
        name="moe_combine",
    )(dest_flat, ys, gate_tok, x, gate_f)


def _moe(x, g, scale, shift, gate_f, wr_t, br, wgu, bgu, wdn, bdn):
    bsz, seq, _ = x.shape
    tok = bsz * seq
    blk = EXPERT_BLOCK
    hf, idx, gate, rank, cnt = _router(x, g, scale, shift, wr_t, br)
    counts = cnt[:, 0].astype(jnp.int32)
    padded = (counts + blk - 1) // blk * blk
    pend = jnp.cumsum(padded)
    pstart = pend - padded
    eids = jnp.arange(N_EXPERTS, dtype=jnp.int32)[:, None, None]
    base = jnp.sum(jnp.where(idx[None] == eids, pstart[:, None, None], 0), axis=0)
    dest = (base + rank).T.reshape(tok * TOP_K)
    gate_tok = gate.T
    n_blk = (tok * TOP_K + N_EXPERTS * (blk - 1) + blk - 1) // blk
    blk_start = jnp.arange(n_blk, dtype=jnp.int32) * blk
    blk_e = jnp.minimum(jnp.sum(blk_start[:, None] >= pend[None, :], axis=1),
                        N_EXPERTS - 1).astype(jnp.int32)
    n_used = (pend[-1:] // blk).astype(jnp.int32)
    tail = jnp.concatenate([n_used, n_blk - n_used])
    xs = _dispatch(pstart + counts, padded - counts, tail, dest, hf, n_blk * blk)
    ys = _experts(blk_e, n_used, xs, wgu, bgu, wdn, bdn)
    return _combine(dest, gate_tok, ys, x, gate_f)


def _prepare(norm_mix_g, norm_ffn_g, w_qkv_a, q_norm_a, k_norm_a, w_o_a, w_qkv_b, q_norm_b, k_norm_b,
             w_o_b, rel_bias, w_router, b_router, w_gate_up, b_gate_up, w_down, b_down):
    nq = A_HEADS * HEAD_DIM
    nk = A_KV_HEADS * HEAD_DIM
    nh = B_HEADS * HEAD_DIM
    scale = HEAD_DIM ** -0.5
    p = {}
    wa = w_qkv_a[0]
    p["wqt_a"] = wa[:, :nq].T.astype(_BF16)
    p["wk_a"] = wa[:, nq:nq + nk].astype(_BF16)
    p["wvt_a"] = wa[:, nq + nk:].T.astype(_BF16)
    p["gq_a"] = (q_norm_a[0] * (scale * np.log2(np.e))).reshape(HEAD_DIM, 1)
    p["gk_a"] = jnp.tile(k_norm_a[0], A_KV_HEADS).reshape(1, nk)
    p["wo_a"] = w_o_a[0].astype(_BF16)
    wb = w_qkv_b[0].astype(_BF16)
    p["w_b"] = wb
    p["gv_b"] = [jnp.concatenate([jnp.tile(q_norm_b[0, gi], B_HEADS) * scale,
                                  jnp.tile(k_norm_b[0, gi], B_HEADS)]).reshape(1, 2 * nh)
                 for gi in range(len(DILATED_CONFIGS))]
    p["bias_b"] = [_dilated_bias(rel_bias[:, gi * B_HEADS:(gi + 1) * B_HEADS], dil)
                   for gi, (_, dil) in enumerate(DILATED_CONFIGS)]
    p["wo_b"] = w_o_b[0].astype(_BF16)
    p["g_mix"] = [norm_mix_g[i].reshape(1, D_MODEL) for i in range(DEPTH)]
    p["g_ffn"] = [norm_ffn_g[i].reshape(1, D_MODEL) for i in range(DEPTH)]
    p["wr_t"] = [w_router[i].T for i in range(DEPTH)]
    p["br"] = [b_router[i].reshape(N_EXPERTS, 1) for i in range(DEPTH)]
    p["wgu"] = [w_gate_up[i].astype(_BF16) for i in range(DEPTH)]
    p["bgu"] = [b_gate_up[i].reshape(N_EXPERTS, 1, -1) for i in range(DEPTH)]
    p["wdn"] = [w_down[i].astype(_BF16) for i in range(DEPTH)]
    p["bdn"] = [b_down[i].reshape(N_EXPERTS, 1, -1) for i in range(DEPTH)]
    p["bd"] = _block_diag_ones()
    return p


def _trunk(x, c, w_ada, b_ada, p):
    bsz, seq, _ = x.shape
    tables = _rope_tables(seq)
    for i in range(DEPTH):
        mod = _ada_mod(c, w_ada[i], b_ada[i])
        shift_m, scale_m, gate_m, shift_f, scale_f, gate_f = [
            mod[:, D_MODEL * k:D_MODEL * (k + 1)].reshape(bsz, 1, D_MODEL) for k in range(6)]
        if i % 2 == 0:
            qt, k, vt = _proj_a(x, p["g_mix"][i], scale_m, shift_m, p["wqt_a"], p["wk_a"],
                                p["wvt_a"], p["gq_a"], p["gk_a"], tables, p["bd"])
            a = _flash_a(qt, k, vt)
            x = _oproj(a, p["wo_a"], x, gate_m)
        else:
            outs, lses = [], []
            for gi, (_, dil) in enumerate(DILATED_CONFIGS):
                qkv = _proj_b(x, p["g_mix"][i], scale_m, shift_m, p["w_b"], gi, p["gv_b"][gi],
                              p["bd"], dil)
                o, lse = _dilated_attn(qkv, p["bias_b"][gi])
                outs.append(o)
                lses.append(lse)
            x = _merge_oproj(outs, lses, p["wo_b"], x, gate_m)
        x = _moe(x, p["g_ffn"][i], scale_f, shift_f, gate_f, p["wr_t"][i], p["br"][i],
                 p["wgu"][i], p["bgu"][i], p["wdn"][i], p["bdn"][i])
    return x


def kernel(x_prompt, x_sample, c_prompt, c_sample, norm_mix_g, norm_ffn_g, w_ada, b_ada, w_qkv_a,
           q_norm_a, k_norm_a, w_o_a, w_qkv_b, q_norm_b, k_norm_b, w_o_b, rel_bias, w_router,
           b_router, w_gate_up, b_gate_up, w_down, b_down):
    p = _prepare(norm_mix_g, norm_ffn_g, w_qkv_a, q_norm_a, k_norm_a, w_o_a, w_qkv_b, q_norm_b,
                 k_norm_b, w_o_b, rel_bias, w_router, b_router, w_gate_up, b_gate_up, w_down, b_down)
    y_prompt = _trunk(x_prompt, c_prompt, w_ada, b_ada, p)
    y_sample = _trunk(x_sample, c_sample, w_ada, b_ada, p)
    return (y_prompt, y_sample)
```

```python
import functools

import numpy as np
import jax
import jax.numpy as jnp
from jax import lax
from jax.experimental import pallas as pl
from jax.experimental.pallas import tpu as pltpu

D_MODEL = 1024
HEAD_DIM = 64
A_HEADS = 16
A_KV_HEADS = 4
A_GROUP = A_HEADS // A_KV_HEADS
GRID_W = 64
ROPE_THETA = 10000.0
B_HEADS = 16
DILATED_CONFIGS = ((128, 1), (512, 4), (2048, 16))
NUM_BUCKETS = 32
MAX_DISTANCE = 1024
N_EXPERTS = 32
TOP_K = 4
SWIGLU_ALPHA = 1.702
SWIGLU_LIMIT = 7.0
RMS_EPS = 1e-6
NEG_INF = -1e30
DEPTH = 2

LANES = 128
SUBLANES = 8
VMEM_LIMIT = 56 * 1024 * 1024

ROW_TILE = 512
FA_TQ = 256
FA_TK = 512
FA_UNROLL = 8
V_ROWS = HEAD_DIM + 16
DIL_TQ = 512
DIL_SUB = 128
DIL_RADIUS = 64
ROUTE_TILE = 256
DISPATCH_TILE = 256
EXPERT_BLOCK = 512
ROW_WORDS = D_MODEL // LANES

_F32 = jnp.float32
_BF16 = jnp.bfloat16
_NT = (((1,), (1,)), ((), ()))


def _cparams(sem):
    return pltpu.CompilerParams(dimension_semantics=sem, vmem_limit_bytes=VMEM_LIMIT)


def _rms_mod(x, g, scale, shift):
    var = jnp.mean(x * x, axis=-1, keepdims=True)
    y = x * lax.rsqrt(var + RMS_EPS) * g
    return y * (1.0 + scale) + shift


def _head_rms(y, bd, gvec):
    ss = jnp.dot((y * y).astype(_BF16), bd, preferred_element_type=_F32)
    return y * lax.rsqrt(ss * (1.0 / HEAD_DIM) + RMS_EPS) * gvec


def _ada_kernel(c_ref, w_ref, b_ref, o_ref):
    c = c_ref[...]
    s = c * jax.nn.sigmoid(c)
    o_ref[...] = jnp.dot(s, w_ref[0], preferred_element_type=_F32,
                         precision=lax.Precision.HIGHEST) + b_ref[0]


def _ada_mod(c, w, b, layer):
    bsz = c.shape[0]
    cp = jnp.pad(c, ((0, SUBLANES - bsz), (0, 0)))
    n = w.shape[2]
    out = pl.pallas_call(
        _ada_kernel,
        grid=(n // D_MODEL,),
        in_specs=[pl.BlockSpec((SUBLANES, D_MODEL), lambda j: (0, 0)),
                  pl.BlockSpec((1, D_MODEL, D_MODEL), lambda j: (layer, 0, j)),
                  pl.BlockSpec((1, 1, D_MODEL), lambda j: (layer, 0, j))],
        out_specs=pl.BlockSpec((SUBLANES, D_MODEL), lambda j: (0, j)),
        out_shape=jax.ShapeDtypeStruct((SUBLANES, n), _F32),
        compiler_params=_cparams(("arbitrary",)),
        name="ada_mod",
    )(cp, w, b.reshape(w.shape[0], 1, n))
    return out[:bsz]


def _proj_a_kernel(x_ref, g_ref, sc_ref, sh_ref, wqt_ref, wk_ref, wvt_ref, gq_ref, gk_ref,
                   cos_ref, sina_ref, sinb_ref, cost_ref, sint_ref, bd_ref,
                   qt_ref, k_ref, vt_ref):
    h = _rms_mod(x_ref[0], g_ref[...], sc_ref[0], sh_ref[0]).astype(_BF16)
    acct = lax.dot_general(wqt_ref[...], h, _NT, preferred_element_type=_F32)
    cost = cost_ref[...]
    sint = sint_ref[...]
    gq = gq_ref[...]
    for j in range(A_HEADS):
        slab = acct[HEAD_DIM * j:HEAD_DIM * (j + 1), :]
        ss = jnp.sum(slab * slab, axis=0, keepdims=True)
        y = slab * lax.rsqrt(ss * (1.0 / HEAD_DIM) + RMS_EPS) * gq
        rot = jnp.concatenate([-y[16:32], y[0:16], -y[48:64], y[32:48]], axis=0)
        qt_ref[0, HEAD_DIM * j:HEAD_DIM * (j + 1), :] = (y * cost + rot * sint).astype(_BF16)
    acck = jnp.dot(h, wk_ref[...], preferred_element_type=_F32)
    y = _head_rms(acck, bd_ref[...], gk_ref[...])
    cos = cos_ref[...]
    sina = sina_ref[...]
    sinb = sinb_ref[...]
    for hh in range(2):
        yy = y[:, LANES * hh:LANES * (hh + 1)]
        r = yy * cos + pltpu.roll(yy, LANES - 16, 1) * sina + pltpu.roll(yy, 16, 1) * sinb
        for e in range(2):
            k_ref[0, 2 * hh + e] = r[:, HEAD_DIM * e:HEAD_DIM * (e + 1)].astype(_BF16)
    accv = lax.dot_general(wvt_ref[...], h, _NT, preferred_element_type=_F32)
    ones = jnp.ones((V_ROWS - HEAD_DIM, accv.shape[1]), _BF16)
    for j in range(A_KV_HEADS):
        vt_ref[0, j, 0, 0:HEAD_DIM, :] = accv[HEAD_DIM * j:HEAD_DIM * (j + 1), :].astype(_BF16)
        vt_ref[0, j, 0, HEAD_DIM:V_ROWS, :] = ones


def _rope_tables(seq_len):
    rows = seq_len // GRID_W
    row = jnp.repeat(jnp.arange(rows, dtype=_F32), GRID_W)
    col = jnp.tile(jnp.arange(GRID_W, dtype=_F32), rows)
    n_freq = HEAD_DIM // 4
    inv_freq = 1.0 / (ROPE_THETA ** (jnp.arange(n_freq, dtype=_F32) / n_freq))
    ang_r = row[:, None] * inv_freq[None, :]
    ang_c = col[:, None] * inv_freq[None, :]
    ang = jnp.concatenate([ang_r, ang_r, ang_c, ang_c], axis=-1)
    cos, sin = jnp.cos(ang), jnp.sin(ang)
    first = (np.arange(HEAD_DIM) % 32) < 16
    cos2 = jnp.tile(cos, (1, 2))
    sina = jnp.tile(jnp.where(first[None, :], -sin, 0.0), (1, 2))
    sinb = jnp.tile(jnp.where(first[None, :], 0.0, sin), (1, 2))
    return cos2, sina, sinb, cos.T, sin.T


def _block_diag_ones():
    i = np.arange(256)
    return jnp.asarray((i[:, None] // HEAD_DIM) == (i[None, :] // HEAD_DIM), dtype=_BF16)


def _proj_a(x, g, scale, shift, wqt, wk, wvt, gq_col, gk_t, tables, bd):
    bsz, seq, _ = x.shape
    tm = FA_TK
    assert seq % tm == 0
    cos2, sina, sinb, cost, sint = tables
    nq = A_HEADS * HEAD_DIM
    nk = A_KV_HEADS * HEAD_DIM
    const = lambda b, i: (0, 0)
    return pl.pallas_call(
        _proj_a_kernel,
        grid=(bsz, seq // tm),
        in_specs=[
            pl.BlockSpec((1, tm, D_MODEL), lambda b, i: (b, i, 0)),
            pl.BlockSpec((1, D_MODEL), const),
            pl.BlockSpec((1, 1, D_MODEL), lambda b, i: (b, 0, 0)),
            pl.BlockSpec((1, 1, D_MODEL), lambda b, i: (b, 0, 0)),
            pl.BlockSpec((nq, D_MODEL), const),
            pl.BlockSpec((D_MODEL, nk), const),
            pl.BlockSpec((nk, D_MODEL), const),
            pl.BlockSpec((HEAD_DIM, 1), const),
            pl.BlockSpec((1, nk), const),
            pl.BlockSpec((tm, LANES), lambda b, i: (i, 0)),
            pl.BlockSpec((tm, LANES), lambda b, i: (i, 0)),
            pl.BlockSpec((tm, LANES), lambda b, i: (i, 0)),
            pl.BlockSpec((HEAD_DIM, tm), lambda b, i: (0, i)),
            pl.BlockSpec((HEAD_DIM, tm), lambda b, i: (0, i)),
            pl.BlockSpec((256, 256), const),
        ],
        out_specs=[
            pl.BlockSpec((1, nq, tm), lambda b, i: (b, 0, i)),
            pl.BlockSpec((1, A_KV_HEADS, tm, HEAD_DIM), lambda b, i: (b, 0, i, 0)),
            pl.BlockSpec((1, A_KV_HEADS, 1, V_ROWS, tm), lambda b, i: (b, 0, i, 0, 0)),
        ],
        out_shape=[
            jax.ShapeDtypeStruct((bsz, nq, seq), _BF16),
            jax.ShapeDtypeStruct((bsz, A_KV_HEADS, seq, HEAD_DIM), _BF16),
            jax.ShapeDtypeStruct((bsz, A_KV_HEADS, seq // tm, V_ROWS, tm), _BF16),
        ],
        compiler_params=_cparams(("parallel", "parallel")),
        name="proj_a",
    )(x, g, scale, shift, wqt, wk, wvt, gq_col, gk_t, cos2, sina, sinb, cost, sint, bd)


def _flash_kernel(qt_ref, k_ref, vt_ref, o_ref, s_ref, m_ref, acc_ref, *, tq, tk, nchunks):
    qt4 = jnp.concatenate([qt_ref[0, HEAD_DIM * hq:HEAD_DIM * (hq + 1), :] for hq in range(A_GROUP)],
                          axis=1)
    m_ref[...] = jnp.full(m_ref.shape, -jnp.inf, _F32)
    acc_ref[...] = jnp.zeros(acc_ref.shape, _F32)

    def scores(kc, slot):
        kb = k_ref[0, 0, pl.ds(pl.multiple_of(kc * tk, tk), tk), :]
        s_ref[slot] = jnp.dot(kb, qt4, preferred_element_type=_F32)

    def step(kc, slot, kc_next, slot_next):
        vb = vt_ref[0, 0, kc]
        if kc_next is not None:
            kb = k_ref[0, 0, pl.ds(pl.multiple_of(kc_next * tk, tk), tk), :]
        for hq in range(A_GROUP):
            sl = slice(tq * hq, tq * (hq + 1))
            if kc_next is not None:
                s_ref[slot_next, :, sl] = jnp.dot(kb, qt4[:, sl], preferred_element_type=_F32)
            s = s_ref[slot, :, sl]
            m_prev = m_ref[:, sl]
            m_new = jnp.maximum(m_prev, jnp.max(s, axis=0, keepdims=True))
            alpha = jnp.exp2(m_prev - m_new)
            p = jnp.exp2(s - m_new).astype(_BF16)
            acc_ref[:, sl] = alpha * acc_ref[:, sl] + jnp.dot(vb, p, preferred_element_type=_F32)
            m_ref[:, sl] = m_new

    def run(kc0, last):
        for j in range(FA_UNROLL):
            final = last and j == FA_UNROLL - 1
            step(kc0 + j, j % 2, None if final else kc0 + j + 1, (j + 1) % 2)

    scores(0, 0)

    def body(c, carry):
        run(FA_UNROLL * c, False)
        return carry

    lax.fori_loop(0, nchunks // FA_UNROLL - 1, body, 0)
    run(nchunks - FA_UNROLL, True)
    o = acc_ref[0:HEAD_DIM, :] / acc_ref[HEAD_DIM:HEAD_DIM + 1, :]
    o = jnp.concatenate([o[:, tq * hq:tq * (hq + 1)] for hq in range(A_GROUP)], axis=0)
    o_ref[0] = o.T.astype(_BF16)


def _flash_a(qt, k, vt):
    bsz, nq, seq = qt.shape
    tq = min(FA_TQ, seq)
    tk = vt.shape[-1]
    nchunks = seq // tk
    assert nchunks % FA_UNROLL == 0 and FA_UNROLL % 2 == 0 and seq % tq == 0
    return pl.pallas_call(
        functools.partial(_flash_kernel, tq=tq, tk=tk, nchunks=nchunks),
        grid=(bsz, A_KV_HEADS, seq // tq),
        in_specs=[
            pl.BlockSpec((1, A_GROUP * HEAD_DIM, tq), lambda b, h, i: (b, h, i)),
            pl.BlockSpec((1, 1, seq, HEAD_DIM), lambda b, h, i: (b, h, 0, 0)),
            pl.BlockSpec((1, 1, nchunks, V_ROWS, tk), lambda b, h, i: (b, h, 0, 0, 0)),
        ],
        out_specs=pl.BlockSpec((1, tq, A_GROUP * HEAD_DIM), lambda b, h, i: (b, i, h)),
        out_shape=jax.ShapeDtypeStruct((bsz, seq, nq), _BF16),
        scratch_shapes=[
            pltpu.VMEM((2, tk, A_GROUP * tq), _F32),
            pltpu.VMEM((1, A_GROUP * tq), _F32),
            pltpu.VMEM((V_ROWS, A_GROUP * tq), _F32),
        ],
        compiler_params=_cparams(("parallel", "parallel", "parallel")),
        name="flash_a",
    )(qt, k, vt)


def _oproj_kernel(a_ref, w_ref, x_ref, gate_ref, o_ref):
    y = jnp.dot(a_ref[0], w_ref[...], preferred_element_type=_F32)
    o_ref[0] = x_ref[0] + gate_ref[0] * y


def _oproj(a, w, x, gate):
    bsz, seq, _ = x.shape
    tm = min(ROW_TILE, seq)
    return pl.pallas_call(
        _oproj_kernel,
        grid=(bsz, seq // tm),
        in_specs=[
            pl.BlockSpec((1, tm, D_MODEL), lambda b, i: (b, i, 0)),
            pl.BlockSpec((D_MODEL, D_MODEL), lambda b, i: (0, 0)),
            pl.BlockSpec((1, tm, D_MODEL), lambda b, i: (b, i, 0)),
            pl.BlockSpec((1, 1, D_MODEL), lambda b, i: (b, 0, 0)),
        ],
        out_specs=pl.BlockSpec((1, tm, D_MODEL), lambda b, i: (b, i, 0)),
        out_shape=jax.ShapeDtypeStruct(x.shape, _F32),
        compiler_params=_cparams(("parallel", "parallel")),
        name="oproj",
    )(a, w, x, gate)


def _proj_b_kernel(x_ref, g_ref, sc_ref, sh_ref, w_ref, gv_ref, bd_ref, o_ref, scr_ref, *, dil, tm):
    h = _rms_mod(x_ref[0], g_ref[...], sc_ref[0], sh_ref[0])
    rows = tm // dil
    if dil > 1:
        for c in range(D_MODEL // LANES):
            scr_ref[c] = h[:, LANES * c:LANES * (c + 1)]
        h = jnp.concatenate(
            [jnp.concatenate([scr_ref[c, pl.ds(r, rows, stride=dil), :]
                              for c in range(D_MODEL // LANES)], axis=1)
             for r in range(dil)], axis=0)
    bd = bd_ref[...]
    acc = jnp.dot(h.astype(_BF16), w_ref[...], preferred_element_type=_F32)
    nqk = 2 * B_HEADS * HEAD_DIM

    def emit(col, y):
        for r in range(dil):
            o_ref[0, r, :, col:col + y.shape[1]] = y[rows * r:rows * (r + 1)].astype(_BF16)

    for c in range(nqk // 256):
        sl = slice(256 * c, 256 * (c + 1))
        emit(256 * c, _head_rms(acc[:, sl], bd, gv_ref[:, sl]))
    emit(nqk, acc[:, nqk:])


def _proj_b(x, g, scale, shift, w, gi, gvec, bd, dil):
    bsz, seq, _ = x.shape
    tm = min(ROW_TILE, seq)
    n = 3 * B_HEADS * HEAD_DIM
    const = lambda b, i: (0, 0)
    return pl.pallas_call(
        functools.partial(_proj_b_kernel, dil=dil, tm=tm),
        grid=(bsz, seq // tm),
        in_specs=[
            pl.BlockSpec((1, tm, D_MODEL), lambda b, i: (b, i, 0)),
            pl.BlockSpec((1, D_MODEL), const),
            pl.BlockSpec((1, 1, D_MODEL), lambda b, i: (b, 0, 0)),
            pl.BlockSpec((1, 1, D_MODEL), lambda b, i: (b, 0, 0)),
            pl.BlockSpec((D_MODEL, n), lambda b, i: (0, gi)),
            pl.BlockSpec((1, 2 * B_HEADS * HEAD_DIM), const),
            pl.BlockSpec((256, 256), const),
        ],
        out_specs=pl.BlockSpec((1, dil, tm // dil, n), lambda b, i: (b, 0, i, 0)),
        out_shape=jax.ShapeDtypeStruct((bsz, dil, seq // dil, n), _BF16),
        scratch_shapes=[pltpu.VMEM((D_MODEL // LANES, tm, LANES), _F32)],
        compiler_params=_cparams(("parallel", "parallel")),
        name="proj_b",
    )(x, g, scale, shift, w, gvec, bd)


def _dilated_kernel(main_ref, kp_ref, vp_ref, kn_ref, vn_ref, bias_ref, o_ref, lse_ref,
                    kf_ref, vf_ref, *, tq, length):
    nh = B_HEADS * HEAD_DIM
    l0 = pl.program_id(2) * tq
    kf_ref[0:DIL_RADIUS] = kp_ref[0, 0]
    kf_ref[DIL_RADIUS:DIL_RADIUS + tq] = main_ref[0, 0, :, nh:2 * nh]
    kf_ref[DIL_RADIUS + tq:] = kn_ref[0, 0]
    vf_ref[0:DIL_RADIUS] = vp_ref[0, 0]
    vf_ref[DIL_RADIUS:DIL_RADIUS + tq] = main_ref[0, 0, :, 2 * nh:3 * nh]
    vf_ref[DIL_RADIUS + tq:] = vn_ref[0, 0]
    width = DIL_SUB + 2 * DIL_RADIUS
    low_half = lax.broadcasted_iota(jnp.int32, (DIL_SUB, LANES), 1) < HEAD_DIM
    col = lax.broadcasted_iota(jnp.int32, (DIL_SUB, width), 1)

    def sub_block(u, carry):
        r0 = pl.multiple_of(u * DIL_SUB, DIL_SUB)
        kpos = col + (l0 + r0 - DIL_RADIUS)
        valid = (kpos >= 0) & (kpos < length)
        for m in range(B_HEADS // 2):
            sl = slice(LANES * m, LANES * (m + 1))
            qp = main_ref[0, 0, pl.ds(r0, DIL_SUB), sl]
            kw = kf_ref[pl.ds(r0, width), sl]
            vw = vf_ref[pl.ds(r0, width), sl]
            res = []
            for hh in range(2):
                keep = low_half if hh == 0 else jnp.logical_not(low_half)
                qm = jnp.where(keep, qp, jnp.zeros_like(qp))
                s = lax.dot_general(qm, kw, _NT, preferred_element_type=_F32) + bias_ref[2 * m + hh]
                s = jnp.where(valid, s, NEG_INF)
                mx = jnp.max(s, axis=1, keepdims=True)
                p = jnp.exp(s - mx)
                den = jnp.sum(p, axis=1, keepdims=True)
                pv = jnp.dot(p.astype(_BF16), vw, preferred_element_type=_F32)
                res.append((pv / den, mx + jnp.log(den)))
            o_ref[0, 0, pl.ds(r0, DIL_SUB), sl] = jnp.where(low_half, res[0][0], res[1][0])
            lse_ref[0, 0, pl.ds(r0, DIL_SUB), sl] = jnp.where(
                low_half, jnp.broadcast_to(res[0][1], (DIL_SUB, LANES)),
                jnp.broadcast_to(res[1][1], (DIL_SUB, LANES)))
        return carry

    lax.fori_loop(0, tq // DIL_SUB, sub_block, 0)


def _dilated_attn(qkv, bias):
    bsz, dil, length, n = qkv.shape
    nh = B_HEADS * HEAD_DIM
    tq = min(DIL_TQ, length)
    nb = tq // DIL_RADIUS
    last = length // DIL_RADIUS - 1
    prev_ix = lambda i: jnp.maximum(i * nb - 1, 0)
    next_ix = lambda i: jnp.minimum((i + 1) * nb, last)
    halo = (1, 1, DIL_RADIUS, nh)
    width = DIL_SUB + 2 * DIL_RADIUS
    out = jax.ShapeDtypeStruct((bsz, dil, length, nh), _F32)
    return pl.pallas_call(
        functools.partial(_dilated_kernel, tq=tq, length=length),
        grid=(bsz, dil, length // tq),
        in_specs=[
            pl.BlockSpec((1, 1, tq, n), lambda b, r, i: (b, r, i, 0)),
            pl.BlockSpec(halo, lambda b, r, i: (b, r, prev_ix(i), 1)),
            pl.BlockSpec(halo, lambda b, r, i: (b, r, prev_ix(i), 2)),
            pl.BlockSpec(halo, lambda b, r, i: (b, r, next_ix(i), 1)),
            pl.BlockSpec(halo, lambda b, r, i: (b, r, next_ix(i), 2)),
            pl.BlockSpec((B_HEADS, DIL_SUB, width), lambda b, r, i: (0, 0, 0)),
        ],
        out_specs=[pl.BlockSpec((1, 1, tq, nh), lambda b, r, i: (b, r, i, 0)),
                   pl.BlockSpec((1, 1, tq, nh), lambda b, r, i: (b, r, i, 0))],
        out_shape=[out, out],
        scratch_shapes=[pltpu.VMEM((tq + 2 * DIL_RADIUS, nh), _BF16),
                        pltpu.VMEM((tq + 2 * DIL_RADIUS, nh), _BF16)],
        compiler_params=_cparams(("parallel", "parallel", "parallel")),
        name="dilated_attn",
    )(qkv, qkv, qkv, qkv, qkv, bias)


def _t5_bucket(rel):
    nb = NUM_BUCKETS // 2
    max_exact = nb // 2
    ret = (rel > 0).astype(np.int32) * nb
    n = np.abs(rel)
    large = max_exact + (np.log(np.maximum(n, 1) / max_exact) / np.log(MAX_DISTANCE / max_exact)
                         * (nb - max_exact)).astype(np.int32)
    large = np.minimum(large, nb - 1)
    return (ret + np.where(n < max_exact, n, large)).astype(np.int32)


def _dilated_bias(rel_bias_g, dil):
    width = DIL_SUB + 2 * DIL_RADIUS
    rel = np.arange(width)[None, :] - DIL_RADIUS - np.arange(DIL_SUB)[:, None]
    bucket = _t5_bucket(rel * dil)
    band = np.abs(rel) <= DIL_RADIUS
    onehot = (bucket.reshape(-1, 1) == np.arange(NUM_BUCKETS)[None, :]).astype(np.float32)
    bias = jnp.dot(jnp.asarray(onehot), rel_bias_g.astype(_F32), precision=lax.Precision.HIGHEST)
    bias = bias.reshape(DIL_SUB, width, B_HEADS).transpose(2, 0, 1)
    return jnp.where(jnp.asarray(band)[None], bias, NEG_INF)


def _merge_oproj_kernel(o0_ref, l0_ref, o1_ref, l1_ref, o2_ref, l2_ref, w_ref, x_ref, gate_ref,
                        out_ref, so1, sl1, so2, sl2, *, tm):
    d1 = DILATED_CONFIGS[1][1]
    d2 = DILATED_CONFIGS[2][1]
    nc = B_HEADS * HEAD_DIM // LANES

    def to_token_order(src_ref, scr_ref, dil):
        for r in range(dil):
            for c in range(nc):
                scr_ref[c, pl.ds(r, tm // dil, stride=dil), :] = (
                    src_ref[0, r, :, LANES * c:LANES * (c + 1)])
        return jnp.concatenate([scr_ref[c] for c in range(nc)], axis=1)

    o1 = to_token_order(o1_ref, so1, d1)
    l1 = to_token_order(l1_ref, sl1, d1)
    o2 = to_token_order(o2_ref, so2, d2)
    l2 = to_token_order(l2_ref, sl2, d2)
    l0 = l0_ref[0, 0]
    mx = jnp.maximum(jnp.maximum(l0, l1), l2)
    e0, e1, e2 = jnp.exp(l0 - mx), jnp.exp(l1 - mx), jnp.exp(l2 - mx)
    o = (e0 * o0_ref[0, 0] + e1 * o1 + e2 * o2) / (e0 + e1 + e2)
    y = jnp.dot(o.astype(_BF16), w_ref[...], preferred_element_type=_F32)
    out_ref[0] = x_ref[0] + gate_ref[0] * y


def _merge_oproj(outs, lses, w, x, gate):
    bsz, seq, _ = x.shape
    tm = min(ROUTE_TILE, seq)
    nh = B_HEADS * HEAD_DIM
    specs = []
    for (_, dil) in DILATED_CONFIGS:
        spec = pl.BlockSpec((1, dil, tm // dil, nh), lambda b, i: (b, 0, i, 0))
        specs += [spec, spec]
    return pl.pallas_call(
        functools.partial(_merge_oproj_kernel, tm=tm),
        grid=(bsz, seq // tm),
        in_specs=specs + [
            pl.BlockSpec((nh, D_MODEL), lambda b, i: (0, 0)),
            pl.BlockSpec((1, tm, D_MODEL), lambda b, i: (b, i, 0)),
            pl.BlockSpec((1, 1, D_MODEL), lambda b, i: (b, 0, 0)),
        ],
        out_specs=pl.BlockSpec((1, tm, D_MODEL), lambda b, i: (b, i, 0)),
        out_shape=jax.ShapeDtypeStruct(x.shape, _F32),
        scratch_shapes=[pltpu.VMEM((nh // LANES, tm, LANES), _F32)] * 4,
        compiler_params=_cparams(("parallel", "parallel")),
        name="merge_oproj",
    )(outs[0], lses[0], outs[1], lses[1], outs[2], lses[2], w, x, gate)


def _router_kernel(x_ref, g_ref, sc_ref, sh_ref, wr_ref, br_ref,
                   hf_ref, idx_ref, gate_ref, rank_ref, cnt_ref, carry_ref, tri_ref, *, tm):
    first = (pl.program_id(0) == 0) & (pl.program_id(1) == 0)

    @pl.when(first)
    def _():
        carry_ref[...] = jnp.zeros(carry_ref.shape, _F32)
        r = lax.broadcasted_iota(jnp.int32, (tm, tm), 0)
        c = lax.broadcasted_iota(jnp.int32, (tm, tm), 1)
        tri_ref[...] = (r < c).astype(_BF16)

    hf = _rms_mod(x_ref[0], g_ref[...], sc_ref[0], sh_ref[0])
    for j in range(ROW_WORDS):
        hf_ref[pl.ds(j, tm, stride=ROW_WORDS), :] = hf[:, LANES * j:LANES * (j + 1)]
    logits = lax.dot_general(wr_ref[...], hf, _NT, preferred_element_type=_F32,
                             precision=lax.Precision.HIGHEST) + br_ref[...]
    eio = lax.broadcasted_iota(jnp.int32, logits.shape, 0).astype(_F32)
    cur = logits
    vals, sel = [], []
    for _ in range(TOP_K):
        mx = jnp.max(cur, axis=0, keepdims=True)
        ix = jnp.min(jnp.where(cur == mx, eio, float(N_EXPERTS)), axis=0, keepdims=True)
        vals.append(mx)
        sel.append(ix)
        cur = jnp.where(eio == ix, -jnp.inf, cur)
    ex = [jnp.exp(v - vals[0]) for v in vals]
    den = ex[0] + ex[1] + ex[2] + ex[3]
    member = jnp.zeros(logits.shape, _F32)
    for ix in sel:
        member = member + (eio == ix).astype(_F32)
    before = carry_ref[:, 0:1] + jnp.dot(member.astype(_BF16), tri_ref[...],
                                         preferred_element_type=_F32)
    ranks = [jnp.sum(jnp.where(eio == ix, before, 0.0), axis=0, keepdims=True) for ix in sel]
    idx_ref[...] = jnp.concatenate(sel, axis=0).astype(jnp.int32)
    gate_ref[...] = jnp.concatenate([e / den for e in ex], axis=0)
    rank_ref[...] = jnp.concatenate(ranks, axis=0).astype(jnp.int32)
    carry_ref[...] = carry_ref[...] + jnp.sum(member, axis=1, keepdims=True)
    cnt_ref[...] = carry_ref[...]


def _router(x, g, scale, shift, wr_t, br):
    bsz, seq, _ = x.shape
    tm = min(ROUTE_TILE, seq)
    ns = seq // tm
    tok = bsz * seq
    const = lambda b, i: (0, 0)
    flat = lambda b, i: (0, b * ns + i)
    return pl.pallas_call(
        functools.partial(_router_kernel, tm=tm),
        grid=(bsz, ns),
        in_specs=[
            pl.BlockSpec((1, tm, D_MODEL), lambda b, i: (b, i, 0)),
            pl.BlockSpec((1, D_MODEL), const),
            pl.BlockSpec((1, 1, D_MODEL), lambda b, i: (b, 0, 0)),
            pl.BlockSpec((1, 1, D_MODEL), lambda b, i: (b, 0, 0)),
            pl.BlockSpec((N_EXPERTS, D_MODEL), const),
            pl.BlockSpec((N_EXPERTS, 1), const),
        ],
        out_specs=[
            pl.BlockSpec((tm * ROW_WORDS, LANES), lambda b, i: (b * ns + i, 0)),
            pl.BlockSpec((TOP_K, tm), flat),
            pl.BlockSpec((TOP_K, tm), flat),
            pl.BlockSpec((TOP_K, tm), flat),
            pl.BlockSpec((N_EXPERTS, LANES), const),
        ],
        out_shape=[
            jax.ShapeDtypeStruct((tok * ROW_WORDS, LANES), _F32),
            jax.ShapeDtypeStruct((TOP_K, tok), jnp.int32),
            jax.ShapeDtypeStruct((TOP_K, tok), _F32),
            jax.ShapeDtypeStruct((TOP_K, tok), jnp.int32),
            jax.ShapeDtypeStruct((N_EXPERTS, LANES), _F32),
        ],
        scratch_shapes=[pltpu.VMEM((N_EXPERTS, LANES), _F32), pltpu.VMEM((tm, tm), _BF16)],
        compiler_params=_cparams(("arbitrary", "arbitrary")),
        name="router",
    )(x, g, scale, shift, wr_t, br)


def _row_copy(src_ref, src_row, dst_ref, dst_row, sem):
    return pltpu.make_async_copy(
        src_ref.at[pl.ds(pl.multiple_of(src_row * ROW_WORDS, ROW_WORDS), ROW_WORDS), :],
        dst_ref.at[pl.ds(pl.multiple_of(dst_row * ROW_WORDS, ROW_WORDS), ROW_WORDS), :], sem)


def _dispatch_kernel(pad_start_ref, pad_len_ref, tail_ref, dest_hbm, hf_ref, xs_ref,
                     dest_smem0, dest_smem1, zero_ref, isem, sem, zsem, *, tm, nsteps):
    g = pl.program_id(0)
    n = tm * TOP_K
    dest_smem = (dest_smem0, dest_smem1)

    def index_copy(step, slot):
        off = pl.multiple_of(step * n, n)
        return pltpu.make_async_copy(dest_hbm.at[pl.ds(off, n)], dest_smem[slot], isem.at[slot])

    def zero_fill(e, wait):
        start = pad_start_ref[e]
        length = pad_len_ref[e]
        for bit in reversed(range(EXPERT_BLOCK.bit_length() - 1)):
            size = 1 << bit
            row = start + (length & ~(2 * size - 1))
            cp = pltpu.make_async_copy(
                zero_ref.at[pl.ds(0, size * ROW_WORDS), :],
                xs_ref.at[pl.ds(pl.multiple_of(row * ROW_WORDS, ROW_WORDS), size * ROW_WORDS), :],
                zsem)

            @pl.when((length & size) != 0)
            def _():
                if wait:
                    cp.wait()
                else:
                    cp.start()

    @pl.when(g == 0)
    def _():
        index_copy(0, 0).start()
        zero_ref[...] = jnp.zeros(zero_ref.shape, _F32)

        def fill(e, carry):
            zero_fill(e, False)
            return carry

        lax.fori_loop(0, N_EXPERTS, fill, 0)

        def done(e, carry):
            zero_fill(e, True)
            return carry

        lax.fori_loop(0, N_EXPERTS, done, 0)

        def tail_copy(j):
            row = (tail_ref[0] + j) * EXPERT_BLOCK
            return pltpu.make_async_copy(
                zero_ref,
                xs_ref.at[pl.ds(pl.multiple_of(row * ROW_WORDS, ROW_WORDS),
                                EXPERT_BLOCK * ROW_WORDS), :], zsem)

        def tail_fill(j, carry):
            tail_copy(j).start()
            return carry

        lax.fori_loop(0, tail_ref[1], tail_fill, 0)

        def tail_done(j, carry):
            tail_copy(j).wait()
            return carry

        lax.fori_loop(0, tail_ref[1], tail_done, 0)

    for slot in range(2):
        @pl.when(g % 2 == slot)
        def _():
            @pl.when(g + 1 < nsteps)
            def _():
                index_copy(g + 1, 1 - slot).start()

            index_copy(g, slot).wait()
            idx_ref = dest_smem[slot]

            def issue(r, carry):
                for k in range(TOP_K):
                    _row_copy(hf_ref, r, xs_ref, idx_ref[r * TOP_K + k], sem).start(priority=k % 2)
                return carry

            lax.fori_loop(0, tm, issue, 0, unroll=2)

    for k in range(TOP_K):
        pltpu.make_async_copy(hf_ref, xs_ref.at[pl.ds(0, tm * ROW_WORDS), :], sem).wait()


def _dispatch(pad_start, pad_len, tail, dest_flat, hf, n_rows):
    tok = hf.shape[0] // ROW_WORDS
    tm = min(DISPATCH_TILE, tok)
    assert tok % tm == 0 and (tm * TOP_K) % 1024 == 0
    grid_spec = pltpu.PrefetchScalarGridSpec(
        num_scalar_prefetch=3,
        grid=(tok // tm,),
        in_specs=[
            pl.BlockSpec(memory_space=pl.ANY),
            pl.BlockSpec((tm * ROW_WORDS, LANES), lambda i, ps, pn, tl: (i, 0)),
        ],
        out_specs=pl.BlockSpec(memory_space=pl.ANY),
        scratch_shapes=[pltpu.SMEM((tm * TOP_K,), jnp.int32),
                        pltpu.SMEM((tm * TOP_K,), jnp.int32),
                        pltpu.VMEM((EXPERT_BLOCK * ROW_WORDS, LANES), _F32),
                        pltpu.SemaphoreType.DMA((2,)), pltpu.SemaphoreType.DMA,
                        pltpu.SemaphoreType.DMA],
    )
    return pl.pallas_call(
        functools.partial(_dispatch_kernel, tm=tm, nsteps=tok // tm),
        grid_spec=grid_spec,
        out_shape=jax.ShapeDtypeStruct((n_rows * ROW_WORDS, LANES), _F32),
        compiler_params=_cparams(("arbitrary",)),
        name="moe_dispatch",
    )(pad_start, pad_len, tail, dest_flat, hf)


def _expert_kernel(be_ref, nu_ref, xs_ref, wgu_ref, bgu_ref, wdn_ref, bdn_ref, ys_ref, *, blk):
    del be_ref
    i = pl.program_id(0)
    dff = wdn_ref.shape[2]

    @pl.when(i < nu_ref[0])
    def _():
        x = jnp.concatenate([xs_ref[pl.ds(j, blk, stride=ROW_WORDS), :] for j in range(ROW_WORDS)],
                            axis=1).astype(_BF16)
        gu = jnp.dot(x, wgu_ref[0, 0], preferred_element_type=_F32) + bgu_ref[0, 0]
        g = jnp.minimum(gu[:, :dff], SWIGLU_LIMIT)
        u = jnp.clip(gu[:, dff:], -SWIGLU_LIMIT, SWIGLU_LIMIT)
        act = g * jax.nn.sigmoid(SWIGLU_ALPHA * g)
        mid = ((u + 1.0) * act).astype(_BF16)
        y = jnp.dot(mid, wdn_ref[0, 0], preferred_element_type=_F32) + bdn_ref[0, 0]
        for j in range(ROW_WORDS):
            ys_ref[pl.ds(j, blk, stride=ROW_WORDS), :] = y[:, LANES * j:LANES * (j + 1)]

    @pl.when(i >= nu_ref[0])
    def _():
        ys_ref[...] = jnp.zeros(ys_ref.shape, _F32)


def _experts(blk_e, n_used, xs, wgu, bgu, wdn, bdn, layer):
    blk = EXPERT_BLOCK
    n_blk = xs.shape[0] // (blk * ROW_WORDS)
    dff = wdn.shape[2]
    grid_spec = pltpu.PrefetchScalarGridSpec(
        num_scalar_prefetch=2,
        grid=(n_blk,),
        in_specs=[
            pl.BlockSpec((blk * ROW_WORDS, LANES), lambda i, be, nu: (jnp.minimum(i, nu[0] - 1), 0)),
            pl.BlockSpec((1, 1, D_MODEL, 2 * dff), lambda i, be, nu: (layer, be[i], 0, 0)),
            pl.BlockSpec((1, 1, 1, 2 * dff), lambda i, be, nu: (layer, be[i], 0, 0)),
            pl.BlockSpec((1, 1, dff, D_MODEL), lambda i, be, nu: (layer, be[i], 0, 0)),
            pl.BlockSpec((1, 1, 1, D_MODEL), lambda i, be, nu: (layer, be[i], 0, 0)),
        ],
        out_specs=pl.BlockSpec((blk * ROW_WORDS, LANES), lambda i, be, nu: (i, 0)),
    )
    return pl.pallas_call(
        functools.partial(_expert_kernel, blk=blk),
        grid_spec=grid_spec,
        out_shape=jax.ShapeDtypeStruct(xs.shape, _F32),
        compiler_params=_cparams(("arbitrary",)),
        name="moe_experts",
    )(blk_e, n_used, xs, wgu, bgu, wdn, bdn)


def _combine_kernel(dest_hbm, ys_ref, gate_ref, x_ref, gf_ref, o_ref,
                    dest_smem0, dest_smem1, buf_ref, isem, sem, *, tm, ns, nsteps):
    g = pl.program_id(0) * ns + pl.program_id(1)
    n = tm * TOP_K
    dest_smem = (dest_smem0, dest_smem1)

    def index_copy(step, slot):
        off = pl.multiple_of(step * n, n)
        return pltpu.make_async_copy(dest_hbm.at[pl.ds(off, n)], dest_smem[slot], isem.at[slot])

    def start_gathers(step, slot):
        index_copy(step, slot).wait()
        idx_ref = dest_smem[slot]

        def issue(r, carry):
            for k in range(TOP_K):
                _row_copy(ys_ref, idx_ref[r * TOP_K + k], buf_ref.at[slot], k * tm + r,
                          sem.at[slot]).start(priority=k % 2)
            return carry

        lax.fori_loop(0, tm, issue, 0, unroll=2)

        @pl.when(step + 1 < nsteps)
        def _():
            index_copy(step + 1, 1 - slot).start()

    def finish(slot):
        pltpu.make_async_copy(ys_ref.at[pl.ds(0, TOP_K * tm * ROW_WORDS), :], buf_ref.at[slot],
                              sem.at[slot]).wait()
        gate = gate_ref[...]
        cols = []
        for j in range(ROW_WORDS):
            acc = None
            for k in range(TOP_K):
                slab = buf_ref[slot, pl.ds(k * tm * ROW_WORDS + j, tm, stride=ROW_WORDS), :]
                term = gate[:, k:k + 1] * slab
                acc = term if acc is None else acc + term
            cols.append(acc)
        moe = jnp.concatenate(cols, axis=1)
        o_ref[0] = x_ref[0] + gf_ref[0] * moe

    @pl.when(g == 0)
    def _():
        index_copy(0, 0).start()
        start_gathers(0, 0)

    for slot in range(2):
        @pl.when(g % 2 == slot)
        def _():
            @pl.when(g + 1 < nsteps)
            def _():
                start_gathers(g + 1, 1 - slot)

            finish(slot)


def _combine(dest_flat, gate_tok, ys, x, gate_f):
    bsz, seq, _ = x.shape
    tm = min(ROUTE_TILE, seq)
    ns = seq // tm
    return pl.pallas_call(
        functools.partial(_combine_kernel, tm=tm, ns=ns, nsteps=bsz * ns),
        grid=(bsz, ns),
        in_specs=[
            pl.BlockSpec(memory_space=pl.ANY),
            pl.BlockSpec(memory_space=pl.ANY),
            pl.BlockSpec((tm, TOP_K), lambda b, i: (b * ns + i, 0)),
            pl.BlockSpec((1, tm, D_MODEL), lambda b, i: (b, i, 0)),
            pl.BlockSpec((1, 1, D_MODEL), lambda b, i: (b, 0, 0)),
        ],
        out_specs=pl.BlockSpec((1, tm, D_MODEL), lambda b, i: (b, i, 0)),
        out_shape=jax.ShapeDtypeStruct(x.shape, _F32),
        scratch_shapes=[
            pltpu.SMEM((tm * TOP_K,), jnp.int32),
            pltpu.SMEM((tm * TOP_K,), jnp.int32),
            pltpu.VMEM((2, TOP_K * tm * ROW_WORDS, LANES), _F32),
            pltpu.SemaphoreType.DMA((2,)),
            pltpu.SemaphoreType.DMA((2,)),
        ],
        compiler_params=_cparams(("arbitrary", "arbitrary")),
        name="moe_combine",
    )(dest_flat, ys, gate_tok, x, gate_f)


def _moe(x, g, scale, shift, gate_f, wr_t, br, wgu, bgu, wdn, bdn, layer):
    bsz, seq, _ = x.shape
    tok = bsz * seq
    blk = EXPERT_BLOCK
    hf, idx, gate, rank, cnt = _router(x, g, scale, shift, wr_t, br)
    counts = cnt[:, 0].astype(jnp.int32)
    padded = (counts + blk - 1) // blk * blk
    pend = jnp.cumsum(padded)
    pstart = pend - padded
    eids = jnp.arange(N_EXPERTS, dtype=jnp.int32)[:, None, None]
    base = jnp.sum(jnp.where(idx[None] == eids, pstart[:, None, None], 0), axis=0)
    dest = (base + rank).T.reshape(tok * TOP_K)
    gate_tok = gate.T
    n_blk = (tok * TOP_K + N_EXPERTS * (blk - 1) + blk - 1) // blk
    blk_start = jnp.arange(n_blk, dtype=jnp.int32) * blk
    blk_e = jnp.minimum(jnp.sum(blk_start[:, None] >= pend[None, :], axis=1),
                        N_EXPERTS - 1).astype(jnp.int32)
    n_used = (pend[-1:] // blk).astype(jnp.int32)
    tail = jnp.concatenate([n_used, n_blk - n_used])
    xs = _dispatch(pstart + counts, padded - counts, tail, dest, hf, n_blk * blk)
    ys = _experts(blk_e, n_used, xs, wgu, bgu, wdn, bdn, layer)
    return _combine(dest, gate_tok, ys, x, gate_f)


def _prepare(norm_mix_g, norm_ffn_g, w_qkv_a, q_norm_a, k_norm_a, w_o_a, w_qkv_b, q_norm_b, k_norm_b,
             w_o_b, rel_bias, w_router, b_router, w_gate_up, b_gate_up, w_down, b_down):
    nq = A_HEADS * HEAD_DIM
    nk = A_KV_HEADS * HEAD_DIM
    nh = B_HEADS * HEAD_DIM
    scale = HEAD_DIM ** -0.5
    p = {}
    wa = w_qkv_a[0]
    p["wqt_a"] = wa[:, :nq].T.astype(_BF16)
    p["wk_a"] = wa[:, nq:nq + nk].astype(_BF16)
    p["wvt_a"] = wa[:, nq + nk:].T.astype(_BF16)
    p["gq_a"] = (q_norm_a[0] * (scale * np.log2(np.e))).reshape(HEAD_DIM, 1)
    p["gk_a"] = jnp.tile(k_norm_a[0], A_KV_HEADS).reshape(1, nk)
    p["wo_a"] = w_o_a[0].astype(_BF16)
    wb = w_qkv_b[0].astype(_BF16)
    p["w_b"] = wb
    p["gv_b"] = [jnp.concatenate([jnp.tile(q_norm_b[0, gi], B_HEADS) * scale,
                                  jnp.tile(k_norm_b[0, gi], B_HEADS)]).reshape(1, 2 * nh)
                 for gi in range(len(DILATED_CONFIGS))]
    p["bias_b"] = [_dilated_bias(rel_bias[:, gi * B_HEADS:(gi + 1) * B_HEADS], dil)
                   for gi, (_, dil) in enumerate(DILATED_CONFIGS)]
    p["wo_b"] = w_o_b[0].astype(_BF16)
    p["g_mix"] = [norm_mix_g[i].reshape(1, D_MODEL) for i in range(DEPTH)]
    p["g_ffn"] = [norm_ffn_g[i].reshape(1, D_MODEL) for i in range(DEPTH)]
    p["wr_t"] = [w_router[i].T for i in range(DEPTH)]
    p["br"] = [b_router[i].reshape(N_EXPERTS, 1) for i in range(DEPTH)]
    p["wgu"] = w_gate_up.astype(_BF16)
    p["bgu"] = b_gate_up.reshape(DEPTH, N_EXPERTS, 1, -1)
    p["wdn"] = w_down.astype(_BF16)
    p["bdn"] = b_down.reshape(DEPTH, N_EXPERTS, 1, -1)
    p["bd"] = _block_diag_ones()
    return p


def _trunk(x, c, w_ada, b_ada, p):
    bsz, seq, _ = x.shape
    tables = _rope_tables(seq)
    for i in range(DEPTH):
        mod = _ada_mod(c, w_ada, b_ada, i)
        shift_m, scale_m, gate_m, shift_f, scale_f, gate_f = [
            mod[:, D_MODEL * k:D_MODEL * (k + 1)].reshape(bsz, 1, D_MODEL) for k in range(6)]
        if i % 2 == 0:
            qt, k, vt = _proj_a(x, p["g_mix"][i], scale_m, shift_m, p["wqt_a"], p["wk_a"],
                                p["wvt_a"], p["gq_a"], p["gk_a"], tables, p["bd"])
            a = _flash_a(qt, k, vt)
            x = _oproj(a, p["wo_a"], x, gate_m)
        else:
            outs, lses = [], []
            for gi, (_, dil) in enumerate(DILATED_CONFIGS):
                qkv = _proj_b(x, p["g_mix"][i], scale_m, shift_m, p["w_b"], gi, p["gv_b"][gi],
                              p["bd"], dil)
                o, lse = _dilated_attn(qkv, p["bias_b"][gi])
                outs.append(o)
                lses.append(lse)
            x = _merge_oproj(outs, lses, p["wo_b"], x, gate_m)
        x = _moe(x, p["g_ffn"][i], scale_f, shift_f, gate_f, p["wr_t"][i], p["br"][i],
                 p["wgu"], p["bgu"], p["wdn"], p["bdn"], i)
    return x


def kernel(x_prompt, x_sample, c_prompt, c_sample, norm_mix_g, norm_ffn_g, w_ada, b_ada, w_qkv_a,
           q_norm_a, k_norm_a, w_o_a, w_qkv_b, q_norm_b, k_norm_b, w_o_b, rel_bias, w_router,
           b_router, w_gate_up, b_gate_up, w_down, b_down):
    p = _prepare(norm_mix_g, norm_ffn_g, w_qkv_a, q_norm_a, k_norm_a, w_o_a, w_qkv_b, q_norm_b,
                 k_norm_b, w_o_b, rel_bias, w_router, b_router, w_gate_up, b_gate_up, w_down, b_down)
    y_prompt = _trunk(x_prompt, c_prompt, w_ada, b_ada, p)
    y_sample = _trunk(x_sample, c_sample, w_ada, b_ada, p)
    return (y_prompt, y_sample)
```

```python
import functools

import numpy as np
import jax
import jax.numpy as jnp
from jax import lax
from jax.experimental import pallas as pl
from jax.experimental.pallas import tpu as pltpu

D_MODEL = 1024
HEAD_DIM = 64
A_HEADS = 16
A_KV_HEADS = 4
A_GROUP = A_HEADS // A_KV_HEADS
GRID_W = 64
ROPE_THETA = 10000.0
B_HEADS = 16
DILATED_CONFIGS = ((128, 1), (512, 4), (2048, 16))
NUM_BUCKETS = 32
MAX_DISTANCE = 1024
N_EXPERTS = 32
TOP_K = 4
SWIGLU_ALPHA = 1.702
SWIGLU_LIMIT = 7.0
RMS_EPS = 1e-6
NEG_INF = -1e30
DEPTH = 2

LANES = 128
SUBLANES = 8
VMEM_LIMIT = 56 * 1024 * 1024

ROW_TILE = 512
FA_TQ = 256
FA_TK = 512
FA_UNROLL = 8
V_ROWS = HEAD_DIM + 16
DIL_TQ = 512
DIL_SUB = 128
DIL_RADIUS = 64
ROUTE_TILE = 256
DISPATCH_TILE = 256
EXPERT_BLOCK = 512
ROW_WORDS = D_MODEL // LANES

_F32 = jnp.float32
_BF16 = jnp.bfloat16
_NT = (((1,), (1,)), ((), ()))


def _cparams(sem):
    return pltpu.CompilerParams(dimension_semantics=sem, vmem_limit_bytes=VMEM_LIMIT)


def _rms_mod(x, g, scale, shift):
    var = jnp.mean(x * x, axis=-1, keepdims=True)
    y = x * lax.rsqrt(var + RMS_EPS) * g
    return y * (1.0 + scale) + shift


def _head_rms(y, bd, gvec):
    ss = jnp.dot((y * y).astype(_BF16), bd, preferred_element_type=_F32)
    return y * lax.rsqrt(ss * (1.0 / HEAD_DIM) + RMS_EPS) * gvec


def _ada_kernel(c_ref, w_ref, b_ref, o_ref):
    c = c_ref[...]
    s = c * jax.nn.sigmoid(c)
    o_ref[...] = jnp.dot(s, w_ref[0], preferred_element_type=_F32,
                         precision=lax.Precision.HIGHEST) + b_ref[0]


def _ada_mod(c, w, b, layer):
    bsz = c.shape[0]
    cp = jnp.pad(c, ((0, SUBLANES - bsz), (0, 0)))
    n = w.shape[2]
    out = pl.pallas_call(
        _ada_kernel,
        grid=(n // D_MODEL,),
        in_specs=[pl.BlockSpec((SUBLANES, D_MODEL), lambda j: (0, 0)),
                  pl.BlockSpec((1, D_MODEL, D_MODEL), lambda j: (layer, 0, j)),
                  pl.BlockSpec((1, 1, D_MODEL), lambda j: (layer, 0, j))],
        out_specs=pl.BlockSpec((SUBLANES, D_MODEL), lambda j: (0, j)),
        out_shape=jax.ShapeDtypeStruct((SUBLANES, n), _F32),
        compiler_params=_cparams(("arbitrary",)),
        name="ada_mod",
    )(cp, w, b.reshape(w.shape[0], 1, n))
    return out[:bsz]


def _proj_a_kernel(x_ref, g_ref, sc_ref, sh_ref, wqt_ref, wk_ref, wvt_ref, gq_ref, gk_ref,
                   cos_ref, sina_ref, sinb_ref, cost_ref, sint_ref, bd_ref,
                   qt_ref, k_ref, vt_ref):
    h = _rms_mod(x_ref[0], g_ref[...], sc_ref[0], sh_ref[0]).astype(_BF16)
    acct = lax.dot_general(wqt_ref[...], h, _NT, preferred_element_type=_F32)
    cost = cost_ref[...]
    sint = sint_ref[...]
    gq = gq_ref[...]
    for j in range(A_HEADS):
        slab = acct[HEAD_DIM * j:HEAD_DIM * (j + 1), :]
        ss = jnp.sum(slab * slab, axis=0, keepdims=True)
        y = slab * lax.rsqrt(ss * (1.0 / HEAD_DIM) + RMS_EPS) * gq
        rot = jnp.concatenate([-y[16:32], y[0:16], -y[48:64], y[32:48]], axis=0)
        qt_ref[0, HEAD_DIM * j:HEAD_DIM * (j + 1), :] = (y * cost + rot * sint).astype(_BF16)
    acck = jnp.dot(h, wk_ref[...], preferred_element_type=_F32)
    y = _head_rms(acck, bd_ref[...], gk_ref[...])
    cos = cos_ref[...]
    sina = sina_ref[...]
    sinb = sinb_ref[...]
    for hh in range(2):
        yy = y[:, LANES * hh:LANES * (hh + 1)]
        r = yy * cos + pltpu.roll(yy, LANES - 16, 1) * sina + pltpu.roll(yy, 16, 1) * sinb
        for e in range(2):
            k_ref[0, 2 * hh + e] = r[:, HEAD_DIM * e:HEAD_DIM * (e + 1)].astype(_BF16)
    accv = lax.dot_general(wvt_ref[...], h, _NT, preferred_element_type=_F32)
    ones = jnp.ones((V_ROWS - HEAD_DIM, accv.shape[1]), _BF16)
    for j in range(A_KV_HEADS):
        vt_ref[0, j, 0, 0:HEAD_DIM, :] = accv[HEAD_DIM * j:HEAD_DIM * (j + 1), :].astype(_BF16)
        vt_ref[0, j, 0, HEAD_DIM:V_ROWS, :] = ones


def _rope_tables(seq_len):
    rows = seq_len // GRID_W
    row = jnp.repeat(jnp.arange(rows, dtype=_F32), GRID_W)
    col = jnp.tile(jnp.arange(GRID_W, dtype=_F32), rows)
    n_freq = HEAD_DIM // 4
    inv_freq = 1.0 / (ROPE_THETA ** (jnp.arange(n_freq, dtype=_F32) / n_freq))
    ang_r = row[:, None] * inv_freq[None, :]
    ang_c = col[:, None] * inv_freq[None, :]
    ang = jnp.concatenate([ang_r, ang_r, ang_c, ang_c], axis=-1)
    cos, sin = jnp.cos(ang), jnp.sin(ang)
    first = (np.arange(HEAD_DIM) % 32) < 16
    cos2 = jnp.tile(cos, (1, 2))
    sina = jnp.tile(jnp.where(first[None, :], -sin, 0.0), (1, 2))
    sinb = jnp.tile(jnp.where(first[None, :], 0.0, sin), (1, 2))
    return cos2, sina, sinb, cos.T, sin.T


def _block_diag_ones():
    i = np.arange(256)
    return jnp.asarray((i[:, None] // HEAD_DIM) == (i[None, :] // HEAD_DIM), dtype=_BF16)


def _proj_a(x, g, scale, shift, wqt, wk, wvt, gq_col, gk_t, tables, bd):
    bsz, seq, _ = x.shape
    tm = FA_TK
    assert seq % tm == 0
    cos2, sina, sinb, cost, sint = tables
    nq = A_HEADS * HEAD_DIM
    nk = A_KV_HEADS * HEAD_DIM
    const = lambda b, i: (0, 0)
    return pl.pallas_call(
        _proj_a_kernel,
        grid=(bsz, seq // tm),
        in_specs=[
            pl.BlockSpec((1, tm, D_MODEL), lambda b, i: (b, i, 0)),
            pl.BlockSpec((1, D_MODEL), const),
            pl.BlockSpec((1, 1, D_MODEL), lambda b, i: (b, 0, 0)),
            pl.BlockSpec((1, 1, D_MODEL), lambda b, i: (b, 0, 0)),
            pl.BlockSpec((nq, D_MODEL), const),
            pl.BlockSpec((D_MODEL, nk), const),
            pl.BlockSpec((nk, D_MODEL), const),
            pl.BlockSpec((HEAD_DIM, 1), const),
            pl.BlockSpec((1, nk), const),
            pl.BlockSpec((tm, LANES), lambda b, i: (i, 0)),
            pl.BlockSpec((tm, LANES), lambda b, i: (i, 0)),
            pl.BlockSpec((tm, LANES), lambda b, i: (i, 0)),
            pl.BlockSpec((HEAD_DIM, tm), lambda b, i: (0, i)),
            pl.BlockSpec((HEAD_DIM, tm), lambda b, i: (0, i)),
            pl.BlockSpec((256, 256), const),
        ],
        out_specs=[
            pl.BlockSpec((1, nq, tm), lambda b, i: (b, 0, i)),
            pl.BlockSpec((1, A_KV_HEADS, tm, HEAD_DIM), lambda b, i: (b, 0, i, 0)),
            pl.BlockSpec((1, A_KV_HEADS, 1, V_ROWS, tm), lambda b, i: (b, 0, i, 0, 0)),
        ],
        out_shape=[
            jax.ShapeDtypeStruct((bsz, nq, seq), _BF16),
            jax.ShapeDtypeStruct((bsz, A_KV_HEADS, seq, HEAD_DIM), _BF16),
            jax.ShapeDtypeStruct((bsz, A_KV_HEADS, seq // tm, V_ROWS, tm), _BF16),
        ],
        compiler_params=_cparams(("parallel", "parallel")),
        name="proj_a",
    )(x, g, scale, shift, wqt, wk, wvt, gq_col, gk_t, cos2, sina, sinb, cost, sint, bd)


def _flash_kernel(qt_ref, k_ref, vt_ref, o_ref, s_ref, m_ref, acc_ref, *, tq, tk, nchunks):
    qt4 = jnp.concatenate([qt_ref[0, HEAD_DIM * hq:HEAD_DIM * (hq + 1), :] for hq in range(A_GROUP)],
                          axis=1)
    m_ref[...] = jnp.full(m_ref.shape, -jnp.inf, _F32)
    acc_ref[...] = jnp.zeros(acc_ref.shape, _F32)

    def scores(kc, slot):
        kb = k_ref[0, 0, pl.ds(pl.multiple_of(kc * tk, tk), tk), :]
        s_ref[slot] = jnp.dot(kb, qt4, preferred_element_type=_F32)

    def step(kc, slot, kc_next, slot_next):
        vb = vt_ref[0, 0, kc]
        if kc_next is not None:
            kb = k_ref[0, 0, pl.ds(pl.multiple_of(kc_next * tk, tk), tk), :]
        for hq in range(A_GROUP):
            sl = slice(tq * hq, tq * (hq + 1))
            if kc_next is not None:
                s_ref[slot_next, :, sl] = jnp.dot(kb, qt4[:, sl], preferred_element_type=_F32)
            s = s_ref[slot, :, sl]
            m_prev = m_ref[:, sl]
            m_new = jnp.maximum(m_prev, jnp.max(s, axis=0, keepdims=True))
            alpha = jnp.exp2(m_prev - m_new)
            p = jnp.exp2(s - m_new).astype(_BF16)
            acc_ref[:, sl] = alpha * acc_ref[:, sl] + jnp.dot(vb, p, preferred_element_type=_F32)
            m_ref[:, sl] = m_new

    def run(kc0, last):
        for j in range(FA_UNROLL):
            final = last and j == FA_UNROLL - 1
            step(kc0 + j, j % 2, None if final else kc0 + j + 1, (j + 1) % 2)

    scores(0, 0)

    def body(c, carry):
        run(FA_UNROLL * c, False)
        return carry

    lax.fori_loop(0, nchunks // FA_UNROLL - 1, body, 0)
    run(nchunks - FA_UNROLL, True)
    o = acc_ref[0:HEAD_DIM, :] / acc_ref[HEAD_DIM:HEAD_DIM + 1, :]
    o = jnp.concatenate([o[:, tq * hq:tq * (hq + 1)] for hq in range(A_GROUP)], axis=0)
    o_ref[0] = o.T.astype(_BF16)


def _flash_a(qt, k, vt):
    bsz, nq, seq = qt.shape
    tq = min(FA_TQ, seq)
    tk = vt.shape[-1]
    nchunks = seq // tk
    assert nchunks % FA_UNROLL == 0 and FA_UNROLL % 2 == 0 and seq % tq == 0
    return pl.pallas_call(
        functools.partial(_flash_kernel, tq=tq, tk=tk, nchunks=nchunks),
        grid=(bsz, A_KV_HEADS, seq // tq),
        in_specs=[
            pl.BlockSpec((1, A_GROUP * HEAD_DIM, tq), lambda b, h, i: (b, h, i)),
            pl.BlockSpec((1, 1, seq, HEAD_DIM), lambda b, h, i: (b, h, 0, 0)),
            pl.BlockSpec((1, 1, nchunks, V_ROWS, tk), lambda b, h, i: (b, h, 0, 0, 0)),
        ],
        out_specs=pl.BlockSpec((1, tq, A_GROUP * HEAD_DIM), lambda b, h, i: (b, i, h)),
        out_shape=jax.ShapeDtypeStruct((bsz, seq, nq), _BF16),
        scratch_shapes=[
            pltpu.VMEM((2, tk, A_GROUP * tq), _F32),
            pltpu.VMEM((1, A_GROUP * tq), _F32),
            pltpu.VMEM((V_ROWS, A_GROUP * tq), _F32),
        ],
        compiler_params=_cparams(("parallel", "parallel", "parallel")),
        name="flash_a",
    )(qt, k, vt)


def _oproj_kernel(a_ref, w_ref, x_ref, gate_ref, o_ref):
    y = jnp.dot(a_ref[0], w_ref[...], preferred_element_type=_F32)
    o_ref[0] = x_ref[0] + gate_ref[0] * y


def _oproj(a, w, x, gate):
    bsz, seq, _ = x.shape
    tm = min(ROW_TILE, seq)
    return pl.pallas_call(
        _oproj_kernel,
        grid=(bsz, seq // tm),
        in_specs=[
            pl.BlockSpec((1, tm, D_MODEL), lambda b, i: (b, i, 0)),
            pl.BlockSpec((D_MODEL, D_MODEL), lambda b, i: (0, 0)),
            pl.BlockSpec((1, tm, D_MODEL), lambda b, i: (b, i, 0)),
            pl.BlockSpec((1, 1, D_MODEL), lambda b, i: (b, 0, 0)),
        ],
        out_specs=pl.BlockSpec((1, tm, D_MODEL), lambda b, i: (b, i, 0)),
        out_shape=jax.ShapeDtypeStruct(x.shape, _F32),
        compiler_params=_cparams(("parallel", "parallel")),
        name="oproj",
    )(a, w, x, gate)


def _proj_b_kernel(x_ref, g_ref, sc_ref, sh_ref, w_ref, gv_ref, bd_ref, o_ref, scr_ref, *, dil, tm):
    h = _rms_mod(x_ref[0], g_ref[...], sc_ref[0], sh_ref[0])
    rows = tm // dil
    if dil > 1:
        for c in range(D_MODEL // LANES):
            scr_ref[c] = h[:, LANES * c:LANES * (c + 1)]
        h = jnp.concatenate(
            [jnp.concatenate([scr_ref[c, pl.ds(r, rows, stride=dil), :]
                              for c in range(D_MODEL // LANES)], axis=1)
             for r in range(dil)], axis=0)
    bd = bd_ref[...]
    acc = jnp.dot(h.astype(_BF16), w_ref[...], preferred_element_type=_F32)
    nqk = 2 * B_HEADS * HEAD_DIM

    def emit(col, y):
        for r in range(dil):
            o_ref[0, r, :, col:col + y.shape[1]] = y[rows * r:rows * (r + 1)].astype(_BF16)

    for c in range(nqk // 256):
        sl = slice(256 * c, 256 * (c + 1))
        emit(256 * c, _head_rms(acc[:, sl], bd, gv_ref[:, sl]))
    emit(nqk, acc[:, nqk:])


def _proj_b(x, g, scale, shift, w, gi, gvec, bd, dil):
    bsz, seq, _ = x.shape
    tm = min(ROW_TILE, seq)
    n = 3 * B_HEADS * HEAD_DIM
    const = lambda b, i: (0, 0)
    return pl.pallas_call(
        functools.partial(_proj_b_kernel, dil=dil, tm=tm),
        grid=(bsz, seq // tm),
        in_specs=[
            pl.BlockSpec((1, tm, D_MODEL), lambda b, i: (b, i, 0)),
            pl.BlockSpec((1, D_MODEL), const),
            pl.BlockSpec((1, 1, D_MODEL), lambda b, i: (b, 0, 0)),
            pl.BlockSpec((1, 1, D_MODEL), lambda b, i: (b, 0, 0)),
            pl.BlockSpec((D_MODEL, n), lambda b, i: (0, gi)),
            pl.BlockSpec((1, 2 * B_HEADS * HEAD_DIM), const),
            pl.BlockSpec((256, 256), const),
        ],
        out_specs=pl.BlockSpec((1, dil, tm // dil, n), lambda b, i: (b, 0, i, 0)),
        out_shape=jax.ShapeDtypeStruct((bsz, dil, seq // dil, n), _BF16),
        scratch_shapes=[pltpu.VMEM((D_MODEL // LANES, tm, LANES), _F32)],
        compiler_params=_cparams(("parallel", "parallel")),
        name="proj_b",
    )(x, g, scale, shift, w, gvec, bd)


def _dilated_kernel(main_ref, kp_ref, vp_ref, kn_ref, vn_ref, bias_ref, o_ref, lse_ref,
                    kf_ref, vf_ref, *, tq, length):
    nh = B_HEADS * HEAD_DIM
    l0 = pl.program_id(2) * tq
    kf_ref[0:DIL_RADIUS] = kp_ref[0, 0]
    kf_ref[DIL_RADIUS:DIL_RADIUS + tq] = main_ref[0, 0, :, nh:2 * nh]
    kf_ref[DIL_RADIUS + tq:] = kn_ref[0, 0]
    vf_ref[0:DIL_RADIUS] = vp_ref[0, 0]
    vf_ref[DIL_RADIUS:DIL_RADIUS + tq] = main_ref[0, 0, :, 2 * nh:3 * nh]
    vf_ref[DIL_RADIUS + tq:] = vn_ref[0, 0]
    width = DIL_SUB + 2 * DIL_RADIUS
    low_half = lax.broadcasted_iota(jnp.int32, (DIL_SUB, LANES), 1) < HEAD_DIM
    col = lax.broadcasted_iota(jnp.int32, (DIL_SUB, width), 1)

    def sub_block(u, carry):
        r0 = pl.multiple_of(u * DIL_SUB, DIL_SUB)
        kpos = col + (l0 + r0 - DIL_RADIUS)
        valid = (kpos >= 0) & (kpos < length)
        for m in range(B_HEADS // 2):
            sl = slice(LANES * m, LANES * (m + 1))
            qp = main_ref[0, 0, pl.ds(r0, DIL_SUB), sl]
            kw = kf_ref[pl.ds(r0, width), sl]
            vw = vf_ref[pl.ds(r0, width), sl]
            res = []
            for hh in range(2):
                keep = low_half if hh == 0 else jnp.logical_not(low_half)
                qm = jnp.where(keep, qp, jnp.zeros_like(qp))
                s = lax.dot_general(qm, kw, _NT, preferred_element_type=_F32) + bias_ref[2 * m + hh]
                s = jnp.where(valid, s, NEG_INF)
                mx = jnp.max(s, axis=1, keepdims=True)
                p = jnp.exp(s - mx)
                den = jnp.sum(p, axis=1, keepdims=True)
                pv = jnp.dot(p.astype(_BF16), vw, preferred_element_type=_F32)
                res.append((pv / den, mx + jnp.log(den)))
            o_ref[0, 0, pl.ds(r0, DIL_SUB), sl] = jnp.where(low_half, res[0][0], res[1][0])
            lse_ref[0, 0, pl.ds(r0, DIL_SUB), sl] = jnp.where(
                low_half, jnp.broadcast_to(res[0][1], (DIL_SUB, LANES)),
                jnp.broadcast_to(res[1][1], (DIL_SUB, LANES)))
        return carry

    lax.fori_loop(0, tq // DIL_SUB, sub_block, 0)


def _dilated_attn(qkv, bias):
    bsz, dil, length, n = qkv.shape
    nh = B_HEADS * HEAD_DIM
    tq = min(DIL_TQ, length)
    nb = tq // DIL_RADIUS
    last = length // DIL_RADIUS - 1
    prev_ix = lambda i: jnp.maximum(i * nb - 1, 0)
    next_ix = lambda i: jnp.minimum((i + 1) * nb, last)
    halo = (1, 1, DIL_RADIUS, nh)
    width = DIL_SUB + 2 * DIL_RADIUS
    out = jax.ShapeDtypeStruct((bsz, dil, length, nh), _F32)
    return pl.pallas_call(
        functools.partial(_dilated_kernel, tq=tq, length=length),
        grid=(bsz, dil, length // tq),
        in_specs=[
            pl.BlockSpec((1, 1, tq, n), lambda b, r, i: (b, r, i, 0)),
            pl.BlockSpec(halo, lambda b, r, i: (b, r, prev_ix(i), 1)),
            pl.BlockSpec(halo, lambda b, r, i: (b, r, prev_ix(i), 2)),
            pl.BlockSpec(halo, lambda b, r, i: (b, r, next_ix(i), 1)),
            pl.BlockSpec(halo, lambda b, r, i: (b, r, next_ix(i), 2)),
            pl.BlockSpec((B_HEADS, DIL_SUB, width), lambda b, r, i: (0, 0, 0)),
        ],
        out_specs=[pl.BlockSpec((1, 1, tq, nh), lambda b, r, i: (b, r, i, 0)),
                   pl.BlockSpec((1, 1, tq, nh), lambda b, r, i: (b, r, i, 0))],
        out_shape=[out, out],
        scratch_shapes=[pltpu.VMEM((tq + 2 * DIL_RADIUS, nh), _BF16),
                        pltpu.VMEM((tq + 2 * DIL_RADIUS, nh), _BF16)],
        compiler_params=_cparams(("parallel", "parallel", "parallel")),
        name="dilated_attn",
    )(qkv, qkv, qkv, qkv, qkv, bias)


def _t5_bucket(rel):
    nb = NUM_BUCKETS // 2
    max_exact = nb // 2
    ret = (rel > 0).astype(np.int32) * nb
    n = np.abs(rel)
    large = max_exact + (np.log(np.maximum(n, 1) / max_exact) / np.log(MAX_DISTANCE / max_exact)
                         * (nb - max_exact)).astype(np.int32)
    large = np.minimum(large, nb - 1)
    return (ret + np.where(n < max_exact, n, large)).astype(np.int32)


def _dilated_bias(rel_bias_g, dil):
    width = DIL_SUB + 2 * DIL_RADIUS
    rel = np.arange(width)[None, :] - DIL_RADIUS - np.arange(DIL_SUB)[:, None]
    bucket = _t5_bucket(rel * dil)
    band = np.abs(rel) <= DIL_RADIUS
    onehot = (bucket.reshape(-1, 1) == np.arange(NUM_BUCKETS)[None, :]).astype(np.float32)
    bias = jnp.dot(jnp.asarray(onehot), rel_bias_g.astype(_F32), precision=lax.Precision.HIGHEST)
    bias = bias.reshape(DIL_SUB, width, B_HEADS).transpose(2, 0, 1)
    return jnp.where(jnp.asarray(band)[None], bias, NEG_INF)


def _merge_oproj_kernel(o0_ref, l0_ref, o1_ref, l1_ref, o2_ref, l2_ref, w_ref, x_ref, gate_ref,
                        out_ref, so1, sl1, so2, sl2, *, tm):
    d1 = DILATED_CONFIGS[1][1]
    d2 = DILATED_CONFIGS[2][1]
    nc = B_HEADS * HEAD_DIM // LANES

    def to_token_order(src_ref, scr_ref, dil):
        for r in range(dil):
            for c in range(nc):
                scr_ref[c, pl.ds(r, tm // dil, stride=dil), :] = (
                    src_ref[0, r, :, LANES * c:LANES * (c + 1)])
        return jnp.concatenate([scr_ref[c] for c in range(nc)], axis=1)

    o1 = to_token_order(o1_ref, so1, d1)
    l1 = to_token_order(l1_ref, sl1, d1)
    o2 = to_token_order(o2_ref, so2, d2)
    l2 = to_token_order(l2_ref, sl2, d2)
    l0 = l0_ref[0, 0]
    mx = jnp.maximum(jnp.maximum(l0, l1), l2)
    e0, e1, e2 = jnp.exp(l0 - mx), jnp.exp(l1 - mx), jnp.exp(l2 - mx)
    o = (e0 * o0_ref[0, 0] + e1 * o1 + e2 * o2) / (e0 + e1 + e2)
    y = jnp.dot(o.astype(_BF16), w_ref[...], preferred_element_type=_F32)
    out_ref[0] = x_ref[0] + gate_ref[0] * y


def _merge_oproj(outs, lses, w, x, gate):
    bsz, seq, _ = x.shape
    tm = min(ROUTE_TILE, seq)
    nh = B_HEADS * HEAD_DIM
    specs = []
    for (_, dil) in DILATED_CONFIGS:
        spec = pl.BlockSpec((1, dil, tm // dil, nh), lambda b, i: (b, 0, i, 0))
        specs += [spec, spec]
    return pl.pallas_call(
        functools.partial(_merge_oproj_kernel, tm=tm),
        grid=(bsz, seq // tm),
        in_specs=specs + [
            pl.BlockSpec((nh, D_MODEL), lambda b, i: (0, 0)),
            pl.BlockSpec((1, tm, D_MODEL), lambda b, i: (b, i, 0)),
            pl.BlockSpec((1, 1, D_MODEL), lambda b, i: (b, 0, 0)),
        ],
        out_specs=pl.BlockSpec((1, tm, D_MODEL), lambda b, i: (b, i, 0)),
        out_shape=jax.ShapeDtypeStruct(x.shape, _F32),
        scratch_shapes=[pltpu.VMEM((nh // LANES, tm, LANES), _F32)] * 4,
        compiler_params=_cparams(("parallel", "parallel")),
        name="merge_oproj",
    )(outs[0], lses[0], outs[1], lses[1], outs[2], lses[2], w, x, gate)


def _router_kernel(x_ref, g_ref, sc_ref, sh_ref, wr_ref, br_ref,
                   hf_ref, idx_ref, gate_ref, rank_ref, cnt_ref, carry_ref, tri_ref, *, tm):
    first = (pl.program_id(0) == 0) & (pl.program_id(1) == 0)

    @pl.when(first)
    def _():
        carry_ref[...] = jnp.zeros(carry_ref.shape, _F32)
        r = lax.broadcasted_iota(jnp.int32, (tm, tm), 0)
        c = lax.broadcasted_iota(jnp.int32, (tm, tm), 1)
        tri_ref[...] = (r < c).astype(_BF16)

    hf = _rms_mod(x_ref[0], g_ref[...], sc_ref[0], sh_ref[0])
    for j in range(ROW_WORDS):
        hf_ref[pl.ds(j, tm, stride=ROW_WORDS), :] = hf[:, LANES * j:LANES * (j + 1)]
    logits = lax.dot_general(wr_ref[...], hf, _NT, preferred_element_type=_F32,
                             precision=lax.Precision.HIGHEST) + br_ref[...]
    eio = lax.broadcasted_iota(jnp.int32, logits.shape, 0).astype(_F32)
    cur = logits
    vals, sel = [], []
    for _ in range(TOP_K):
        mx = jnp.max(cur, axis=0, keepdims=True)
        ix = jnp.min(jnp.where(cur == mx, eio, float(N_EXPERTS)), axis=0, keepdims=True)
        vals.append(mx)
        sel.append(ix)
        cur = jnp.where(eio == ix, -jnp.inf, cur)
    ex = [jnp.exp(v - vals[0]) for v in vals]
    den = ex[0] + ex[1] + ex[2] + ex[3]
    member = jnp.zeros(logits.shape, _F32)
    for ix in sel:
        member = member + (eio == ix).astype(_F32)
    before = carry_ref[:, 0:1] + jnp.dot(member.astype(_BF16), tri_ref[...],
                                         preferred_element_type=_F32)
    ranks = [jnp.sum(jnp.where(eio == ix, before, 0.0), axis=0, keepdims=True) for ix in sel]
    idx_ref[...] = jnp.concatenate(sel, axis=0).astype(jnp.int32)
    gate_ref[...] = jnp.concatenate([e / den for e in ex], axis=0)
    rank_ref[...] = jnp.concatenate(ranks, axis=0).astype(jnp.int32)
    carry_ref[...] = carry_ref[...] + jnp.sum(member, axis=1, keepdims=True)
    cnt_ref[...] = carry_ref[...]


def _router(x, g, scale, shift, wr_t, br):
    bsz, seq, _ = x.shape
    tm = min(ROUTE_TILE, seq)
    ns = seq // tm
    tok = bsz * seq
    const = lambda b, i: (0, 0)
    flat = lambda b, i: (0, b * ns + i)
    return pl.pallas_call(
        functools.partial(_router_kernel, tm=tm),
        grid=(bsz, ns),
        in_specs=[
            pl.BlockSpec((1, tm, D_MODEL), lambda b, i: (b, i, 0)),
            pl.BlockSpec((1, D_MODEL), const),
            pl.BlockSpec((1, 1, D_MODEL), lambda b, i: (b, 0, 0)),
            pl.BlockSpec((1, 1, D_MODEL), lambda b, i: (b, 0, 0)),
            pl.BlockSpec((N_EXPERTS, D_MODEL), const),
            pl.BlockSpec((N_EXPERTS, 1), const),
        ],
        out_specs=[
            pl.BlockSpec((tm * ROW_WORDS, LANES), lambda b, i: (b * ns + i, 0)),
            pl.BlockSpec((TOP_K, tm), flat),
            pl.BlockSpec((TOP_K, tm), flat),
            pl.BlockSpec((TOP_K, tm), flat),
            pl.BlockSpec((N_EXPERTS, LANES), const),
        ],
        out_shape=[
            jax.ShapeDtypeStruct((tok * ROW_WORDS, LANES), _F32),
            jax.ShapeDtypeStruct((TOP_K, tok), jnp.int32),
            jax.ShapeDtypeStruct((TOP_K, tok), _F32),
            jax.ShapeDtypeStruct((TOP_K, tok), jnp.int32),
            jax.ShapeDtypeStruct((N_EXPERTS, LANES), _F32),
        ],
        scratch_shapes=[pltpu.VMEM((N_EXPERTS, LANES), _F32), pltpu.VMEM((tm, tm), _BF16)],
        compiler_params=_cparams(("arbitrary", "arbitrary")),
        name="router",
    )(x, g, scale, shift, wr_t, br)


def _row_copy(src_ref, src_row, dst_ref, dst_row, sem):
    return pltpu.make_async_copy(
        src_ref.at[pl.ds(pl.multiple_of(src_row * ROW_WORDS, ROW_WORDS), ROW_WORDS), :],
        dst_ref.at[pl.ds(pl.multiple_of(dst_row * ROW_WORDS, ROW_WORDS), ROW_WORDS), :], sem)


def _dispatch_kernel(pad_start_ref, pad_len_ref, tail_ref, dest_hbm, hf_ref, xs_ref,
                     dest_smem0, dest_smem1, zero_ref, isem, sem, zsem, *, tm, nsteps):
    g = pl.program_id(0)
    n = tm * TOP_K
    dest_smem = (dest_smem0, dest_smem1)

    def index_copy(step, slot):
        off = pl.multiple_of(step * n, n)
        return pltpu.make_async_copy(dest_hbm.at[pl.ds(off, n)], dest_smem[slot], isem.at[slot])

    def zero_fill(e, wait):
        start = pad_start_ref[e]
        length = pad_len_ref[e]
        for bit in reversed(range(EXPERT_BLOCK.bit_length() - 1)):
            size = 1 << bit
            row = start + (length & ~(2 * size - 1))
            cp = pltpu.make_async_copy(
                zero_ref.at[pl.ds(0, size * ROW_WORDS), :],
                xs_ref.at[pl.ds(pl.multiple_of(row * ROW_WORDS, ROW_WORDS), size * ROW_WORDS), :],
                zsem)

            @pl.when((length & size) != 0)
            def _():
                if wait:
                    cp.wait()
                else:
                    cp.start()

    @pl.when(g == 0)
    def _():
        index_copy(0, 0).start()
        zero_ref[...] = jnp.zeros(zero_ref.shape, _F32)

        def fill(e, carry):
            zero_fill(e, False)
            return carry

        lax.fori_loop(0, N_EXPERTS, fill, 0)

        def done(e, carry):
            zero_fill(e, True)
            return carry

        lax.fori_loop(0, N_EXPERTS, done, 0)

        def tail_copy(j):
            row = (tail_ref[0] + j) * EXPERT_BLOCK
            return pltpu.make_async_copy(
                zero_ref,
                xs_ref.at[pl.ds(pl.multiple_of(row * ROW_WORDS, ROW_WORDS),
                                EXPERT_BLOCK * ROW_WORDS), :], zsem)

        def tail_fill(j, carry):
            tail_copy(j).start()
            return carry

        lax.fori_loop(0, tail_ref[1], tail_fill, 0)

        def tail_done(j, carry):
            tail_copy(j).wait()
            return carry

        lax.fori_loop(0, tail_ref[1], tail_done, 0)

    for slot in range(2):
        @pl.when(g % 2 == slot)
        def _():
            @pl.when(g + 1 < nsteps)
            def _():
                index_copy(g + 1, 1 - slot).start()

            index_copy(g, slot).wait()
            idx_ref = dest_smem[slot]

            def issue(r, carry):
                for k in range(TOP_K):
                    _row_copy(hf_ref, r, xs_ref, idx_ref[r * TOP_K + k], sem).start(priority=k % 2)
                return carry

            lax.fori_loop(0, tm, issue, 0, unroll=2)

    for k in range(TOP_K):
        pltpu.make_async_copy(hf_ref, xs_ref.at[pl.ds(0, tm * ROW_WORDS), :], sem).wait()


def _dispatch(pad_start, pad_len, tail, dest_flat, hf, n_rows):
    tok = hf.shape[0] // ROW_WORDS
    tm = min(DISPATCH_TILE, tok)
    assert tok % tm == 0 and (tm * TOP_K) % 1024 == 0
    grid_spec = pltpu.PrefetchScalarGridSpec(
        num_scalar_prefetch=3,
        grid=(tok // tm,),
        in_specs=[
            pl.BlockSpec(memory_space=pl.ANY),
            pl.BlockSpec((tm * ROW_WORDS, LANES), lambda i, ps, pn, tl: (i, 0)),
        ],
        out_specs=pl.BlockSpec(memory_space=pl.ANY),
        scratch_shapes=[pltpu.SMEM((tm * TOP_K,), jnp.int32),
                        pltpu.SMEM((tm * TOP_K,), jnp.int32),
                        pltpu.VMEM((EXPERT_BLOCK * ROW_WORDS, LANES), _F32),
                        pltpu.SemaphoreType.DMA((2,)), pltpu.SemaphoreType.DMA,
                        pltpu.SemaphoreType.DMA],
    )
    return pl.pallas_call(
        functools.partial(_dispatch_kernel, tm=tm, nsteps=tok // tm),
        grid_spec=grid_spec,
        out_shape=jax.ShapeDtypeStruct((n_rows * ROW_WORDS, LANES), _F32),
        compiler_params=_cparams(("arbitrary",)),
        name="moe_dispatch",
    )(pad_start, pad_len, tail, dest_flat, hf)


def _expert_kernel(be_ref, nu_ref, xs_ref, wgu_ref, bgu_ref, wdn_ref, bdn_ref, ys_ref,
                   wgu_bf, wdn_bf, *, blk):
    i = pl.program_id(0)
    dff = wdn_ref.shape[2]
    used = i < nu_ref[0]
    new_expert = (i == 0) | (be_ref[i] != be_ref[jnp.maximum(i - 1, 0)])

    @pl.when(used & new_expert)
    def _():
        wgu_bf[...] = wgu_ref[0, 0].astype(_BF16)
        wdn_bf[...] = wdn_ref[0, 0].astype(_BF16)

    @pl.when(used)
    def _():
        x = jnp.concatenate([xs_ref[pl.ds(j, blk, stride=ROW_WORDS), :] for j in range(ROW_WORDS)],
                            axis=1).astype(_BF16)
        gu = jnp.dot(x, wgu_bf[...], preferred_element_type=_F32) + bgu_ref[0, 0]
        g = jnp.minimum(gu[:, :dff], SWIGLU_LIMIT)
        u = jnp.clip(gu[:, dff:], -SWIGLU_LIMIT, SWIGLU_LIMIT)
        act = g * jax.nn.sigmoid(SWIGLU_ALPHA * g)
        mid = ((u + 1.0) * act).astype(_BF16)
        y = jnp.dot(mid, wdn_bf[...], preferred_element_type=_F32) + bdn_ref[0, 0]
        for j in range(ROW_WORDS):
            ys_ref[pl.ds(j, blk, stride=ROW_WORDS), :] = y[:, LANES * j:LANES * (j + 1)]

    @pl.when(i >= nu_ref[0])
    def _():
        ys_ref[...] = jnp.zeros(ys_ref.shape, _F32)


def _experts(blk_e, n_used, xs, wgu, bgu, wdn, bdn, layer):
    blk = EXPERT_BLOCK
    n_blk = xs.shape[0] // (blk * ROW_WORDS)
    dff = wdn.shape[2]
    grid_spec = pltpu.PrefetchScalarGridSpec(
        num_scalar_prefetch=2,
        grid=(n_blk,),
        in_specs=[
            pl.BlockSpec((blk * ROW_WORDS, LANES), lambda i, be, nu: (jnp.minimum(i, nu[0] - 1), 0)),
            pl.BlockSpec((1, 1, D_MODEL, 2 * dff), lambda i, be, nu: (layer, be[i], 0, 0)),
            pl.BlockSpec((1, 1, 1, 2 * dff), lambda i, be, nu: (layer, be[i], 0, 0)),
            pl.BlockSpec((1, 1, dff, D_MODEL), lambda i, be, nu: (layer, be[i], 0, 0)),
            pl.BlockSpec((1, 1, 1, D_MODEL), lambda i, be, nu: (layer, be[i], 0, 0)),
        ],
        out_specs=pl.BlockSpec((blk * ROW_WORDS, LANES), lambda i, be, nu: (i, 0)),
        scratch_shapes=[pltpu.VMEM((D_MODEL, 2 * dff), _BF16), pltpu.VMEM((dff, D_MODEL), _BF16)],
    )
    return pl.pallas_call(
        functools.partial(_expert_kernel, blk=blk),
        grid_spec=grid_spec,
        out_shape=jax.ShapeDtypeStruct(xs.shape, _F32),
        compiler_params=_cparams(("arbitrary",)),
        name="moe_experts",
    )(blk_e, n_used, xs, wgu, bgu, wdn, bdn)


def _combine_kernel(dest_hbm, ys_ref, gate_ref, x_ref, gf_ref, o_ref,
                    dest_smem0, dest_smem1, buf_ref, isem, sem, *, tm, ns, nsteps):
    g = pl.program_id(0) * ns + pl.program_id(1)
    n = tm * TOP_K
    dest_smem = (dest_smem0, dest_smem1)

    def index_copy(step, slot):
        off = pl.multiple_of(step * n, n)
        return pltpu.make_async_copy(dest_hbm.at[pl.ds(off, n)], dest_smem[slot], isem.at[slot])

    def start_gathers(step, slot):
        index_copy(step, slot).wait()
        idx_ref = dest_smem[slot]

        def issue(r, carry):
            for k in range(TOP_K):
                _row_copy(ys_ref, idx_ref[r * TOP_K + k], buf_ref.at[slot], k * tm + r,
                          sem.at[slot]).start(priority=k % 2)
            return carry

        lax.fori_loop(0, tm, issue, 0, unroll=2)

        @pl.when(step + 1 < nsteps)
        def _():
            index_copy(step + 1, 1 - slot).start()

    def finish(slot):
        pltpu.make_async_copy(ys_ref.at[pl.ds(0, TOP_K * tm * ROW_WORDS), :], buf_ref.at[slot],
                              sem.at[slot]).wait()
        gate = gate_ref[...]
        cols = []
        for j in range(ROW_WORDS):
            acc = None
            for k in range(TOP_K):
                slab = buf_ref[slot, pl.ds(k * tm * ROW_WORDS + j, tm, stride=ROW_WORDS), :]
                term = gate[:, k:k + 1] * slab
                acc = term if acc is None else acc + term
            cols.append(acc)
        moe = jnp.concatenate(cols, axis=1)
        o_ref[0] = x_ref[0] + gf_ref[0] * moe

    @pl.when(g == 0)
    def _():
        index_copy(0, 0).start()
        start_gathers(0, 0)

    for slot in range(2):
        @pl.when(g % 2 == slot)
        def _():
            @pl.when(g + 1 < nsteps)
            def _():
                start_gathers(g + 1, 1 - slot)

            finish(slot)


def _combine(dest_flat, gate_tok, ys, x, gate_f):
    bsz, seq, _ = x.shape
    tm = min(ROUTE_TILE, seq)
    ns = seq // tm
    return pl.pallas_call(
        functools.partial(_combine_kernel, tm=tm, ns=ns, nsteps=bsz * ns),
        grid=(bsz, ns),
        in_specs=[
            pl.BlockSpec(memory_space=pl.ANY),
            pl.BlockSpec(memory_space=pl.ANY),
            pl.BlockSpec((tm, TOP_K), lambda b, i: (b * ns + i, 0)),
            pl.BlockSpec((1, tm, D_MODEL), lambda b, i: (b, i, 0)),
            pl.BlockSpec((1, 1, D_MODEL), lambda b, i: (b, 0, 0)),
        ],
        out_specs=pl.BlockSpec((1, tm, D_MODEL), lambda b, i: (b, i, 0)),
        out_shape=jax.ShapeDtypeStruct(x.shape, _F32),
        scratch_shapes=[
            pltpu.SMEM((tm * TOP_K,), jnp.int32),
            pltpu.SMEM((tm * TOP_K,), jnp.int32),
            pltpu.VMEM((2, TOP_K * tm * ROW_WORDS, LANES), _F32),
            pltpu.SemaphoreType.DMA((2,)),
            pltpu.SemaphoreType.DMA((2,)),
        ],
        compiler_params=_cparams(("arbitrary", "arbitrary")),
        name="moe_combine",
    )(dest_flat, ys, gate_tok, x, gate_f)


def _moe(x, g, scale, shift, gate_f, wr_t, br, wgu, bgu, wdn, bdn, layer):
    bsz, seq, _ = x.shape
    tok = bsz * seq
    blk = EXPERT_BLOCK
    hf, idx, gate, rank, cnt = _router(x, g, scale, shift, wr_t, br)
    counts = cnt[:, 0].astype(jnp.int32)
    padded = (counts + blk - 1) // blk * blk
    pend = jnp.cumsum(padded)
    pstart = pend - padded
    eids = jnp.arange(N_EXPERTS, dtype=jnp.int32)[:, None, None]
    base = jnp.sum(jnp.where(idx[None] == eids, pstart[:, None, None], 0), axis=0)
    dest = (base + rank).T.reshape(tok * TOP_K)
    gate_tok = gate.T
    n_blk = (tok * TOP_K + N_EXPERTS * (blk - 1) + blk - 1) // blk
    blk_start = jnp.arange(n_blk, dtype=jnp.int32) * blk
    blk_e = jnp.minimum(jnp.sum(blk_start[:, None] >= pend[None, :], axis=1),
                        N_EXPERTS - 1).astype(jnp.int32)
    n_used = (pend[-1:] // blk).astype(jnp.int32)
    tail = jnp.concatenate([n_used, n_blk - n_used])
    xs = _dispatch(pstart + counts, padded - counts, tail, dest, hf, n_blk * blk)
    ys = _experts(blk_e, n_used, xs, wgu, bgu, wdn, bdn, layer)
    return _combine(dest, gate_tok, ys, x, gate_f)


def _prepare(norm_mix_g, norm_ffn_g, w_qkv_a, q_norm_a, k_norm_a, w_o_a, w_qkv_b, q_norm_b, k_norm_b,
             w_o_b, rel_bias, w_router, b_router, w_gate_up, b_gate_up, w_down, b_down):
    nq = A_HEADS * HEAD_DIM
    nk = A_KV_HEADS * HEAD_DIM
    nh = B_HEADS * HEAD_DIM
    scale = HEAD_DIM ** -0.5
    p = {}
    wa = w_qkv_a[0]
    p["wqt_a"] = wa[:, :nq].T.astype(_BF16)
    p["wk_a"] = wa[:, nq:nq + nk].astype(_BF16)
    p["wvt_a"] = wa[:, nq + nk:].T.astype(_BF16)
    p["gq_a"] = (q_norm_a[0] * (scale * np.log2(np.e))).reshape(HEAD_DIM, 1)
    p["gk_a"] = jnp.tile(k_norm_a[0], A_KV_HEADS).reshape(1, nk)
    p["wo_a"] = w_o_a[0].astype(_BF16)
    wb = w_qkv_b[0].astype(_BF16)
    p["w_b"] = wb
    p["gv_b"] = [jnp.concatenate([jnp.tile(q_norm_b[0, gi], B_HEADS) * scale,
                                  jnp.tile(k_norm_b[0, gi], B_HEADS)]).reshape(1, 2 * nh)
                 for gi in range(len(DILATED_CONFIGS))]
    p["bias_b"] = [_dilated_bias(rel_bias[:, gi * B_HEADS:(gi + 1) * B_HEADS], dil)
                   for gi, (_, dil) in enumerate(DILATED_CONFIGS)]
    p["wo_b"] = w_o_b[0].astype(_BF16)
    p["g_mix"] = [norm_mix_g[i].reshape(1, D_MODEL) for i in range(DEPTH)]
    p["g_ffn"] = [norm_ffn_g[i].reshape(1, D_MODEL) for i in range(DEPTH)]
    p["wr_t"] = [w_router[i].T for i in range(DEPTH)]
    p["br"] = [b_router[i].reshape(N_EXPERTS, 1) for i in range(DEPTH)]
    p["wgu"] = w_gate_up
    p["bgu"] = b_gate_up.reshape(DEPTH, N_EXPERTS, 1, -1)
    p["wdn"] = w_down
    p["bdn"] = b_down.reshape(DEPTH, N_EXPERTS, 1, -1)
    p["bd"] = _block_diag_ones()
    return p


def _trunk(x, c, w_ada, b_ada, p):
    bsz, seq, _ = x.shape
    tables = _rope_tables(seq)
    for i in range(DEPTH):
        mod = _ada_mod(c, w_ada, b_ada, i)
        shift_m, scale_m, gate_m, shift_f, scale_f, gate_f = [
            mod[:, D_MODEL * k:D_MODEL * (k + 1)].reshape(bsz, 1, D_MODEL) for k in range(6)]
        if i % 2 == 0:
            qt, k, vt = _proj_a(x, p["g_mix"][i], scale_m, shift_m, p["wqt_a"], p["wk_a"],
                                p["wvt_a"], p["gq_a"], p["gk_a"], tables, p["bd"])
            a = _flash_a(qt, k, vt)
            x = _oproj(a, p["wo_a"], x, gate_m)
        else:
            outs, lses = [], []
            for gi, (_, dil) in enumerate(DILATED_CONFIGS):
                qkv = _proj_b(x, p["g_mix"][i], scale_m, shift_m, p["w_b"], gi, p["gv_b"][gi],
                              p["bd"], dil)
                o, lse = _dilated_attn(qkv, p["bias_b"][gi])
                outs.append(o)
                lses.append(lse)
            x = _merge_oproj(outs, lses, p["wo_b"], x, gate_m)
        x = _moe(x, p["g_ffn"][i], scale_f, shift_f, gate_f, p["wr_t"][i], p["br"][i],
                 p["wgu"], p["bgu"], p["wdn"], p["bdn"], i)
    return x


def kernel(x_prompt, x_sample, c_prompt, c_sample, norm_mix_g, norm_ffn_g, w_ada, b_ada, w_qkv_a,
           q_norm_a, k_norm_a, w_o_a, w_qkv_b, q_norm_b, k_norm_b, w_o_b, rel_bias, w_router,
           b_router, w_gate_up, b_gate_up, w_down, b_down):
    p = _prepare(norm_mix_g, norm_ffn_g, w_qkv_a, q_norm_a, k_norm_a, w_o_a, w_qkv_b, q_norm_b,
                 k_norm_b, w_o_b, rel_bias, w_router, b_router, w_gate_up, b_gate_up, w_down, b_down)
    y_prompt = _trunk(x_prompt, c_prompt, w_ada, b_ada, p)
    y_sample = _trunk(x_sample, c_sample, w_ada, b_ada, p)
    return (y_prompt, y_sample)
```

```python
import functools

import numpy as np
import jax
import jax.numpy as jnp
from jax import lax
from jax.experimental import pallas as pl
from jax.experimental.pallas import tpu as pltpu

D_MODEL = 1024
HEAD_DIM = 64
A_HEADS = 16
A_KV_HEADS = 4
A_GROUP = A_HEADS // A_KV_HEADS
GRID_W = 64
ROPE_THETA = 10000.0
B_HEADS = 16
DILATED_CONFIGS = ((128, 1), (512, 4), (2048, 16))
NUM_BUCKETS = 32
MAX_DISTANCE = 1024
N_EXPERTS = 32
TOP_K = 4
SWIGLU_ALPHA = 1.702
SWIGLU_LIMIT = 7.0
RMS_EPS = 1e-6
NEG_INF = -1e30
DEPTH = 2

LANES = 128
SUBLANES = 8
VMEM_LIMIT = 56 * 1024 * 1024

ROW_TILE = 512
FA_TQ = 256
FA_TK = 512
FA_UNROLL = 8
V_ROWS = HEAD_DIM + 16
DIL_TQ = 512
DIL_SUB = 128
DIL_RADIUS = 64
LSE_LANES = LANES // B_HEADS
ROUTE_TILE = 256
DISPATCH_TILE = 256
EXPERT_BLOCK = 512
ROW_WORDS = D_MODEL // LANES

_F32 = jnp.float32
_BF16 = jnp.bfloat16
_NT = (((1,), (1,)), ((), ()))


def _cparams(sem):
    return pltpu.CompilerParams(dimension_semantics=sem, vmem_limit_bytes=VMEM_LIMIT)


def _rms_mod(x, g, scale, shift):
    var = jnp.mean(x * x, axis=-1, keepdims=True)
    y = x * lax.rsqrt(var + RMS_EPS) * g
    return y * (1.0 + scale) + shift


def _head_rms(y, bd, gvec):
    ss = jnp.dot((y * y).astype(_BF16), bd, preferred_element_type=_F32)
    return y * lax.rsqrt(ss * (1.0 / HEAD_DIM) + RMS_EPS) * gvec


def _ada_kernel(c_ref, w_ref, b_ref, o_ref):
    c = c_ref[...]
    s = c * jax.nn.sigmoid(c)
    o_ref[...] = jnp.dot(s, w_ref[0], preferred_element_type=_F32,
                         precision=lax.Precision.HIGHEST) + b_ref[0]


def _ada_mod(c, w, b, layer):
    bsz = c.shape[0]
    cp = jnp.pad(c, ((0, SUBLANES - bsz), (0, 0)))
    n = w.shape[2]
    out = pl.pallas_call(
        _ada_kernel,
        grid=(n // D_MODEL,),
        in_specs=[pl.BlockSpec((SUBLANES, D_MODEL), lambda j: (0, 0)),
                  pl.BlockSpec((1, D_MODEL, D_MODEL), lambda j: (layer, 0, j)),
                  pl.BlockSpec((1, 1, D_MODEL), lambda j: (layer, 0, j))],
        out_specs=pl.BlockSpec((SUBLANES, D_MODEL), lambda j: (0, j)),
        out_shape=jax.ShapeDtypeStruct((SUBLANES, n), _F32),
        compiler_params=_cparams(("arbitrary",)),
        name="ada_mod",
    )(cp, w, b.reshape(w.shape[0], 1, n))
    return out[:bsz]


def _proj_a_kernel(x_ref, g_ref, sc_ref, sh_ref, wqt_ref, wk_ref, wvt_ref, gq_ref, gk_ref,
                   cos_ref, sina_ref, sinb_ref, cost_ref, sint_ref, bd_ref,
                   qt_ref, k_ref, vt_ref):
    h = _rms_mod(x_ref[0], g_ref[...], sc_ref[0], sh_ref[0]).astype(_BF16)
    acct = lax.dot_general(wqt_ref[...], h, _NT, preferred_element_type=_F32)
    cost = cost_ref[...]
    sint = sint_ref[...]
    gq = gq_ref[...]
    for j in range(A_HEADS):
        slab = acct[HEAD_DIM * j:HEAD_DIM * (j + 1), :]
        ss = jnp.sum(slab * slab, axis=0, keepdims=True)
        y = slab * lax.rsqrt(ss * (1.0 / HEAD_DIM) + RMS_EPS) * gq
        rot = jnp.concatenate([-y[16:32], y[0:16], -y[48:64], y[32:48]], axis=0)
        qt_ref[0, HEAD_DIM * j:HEAD_DIM * (j + 1), :] = (y * cost + rot * sint).astype(_BF16)
    acck = jnp.dot(h, wk_ref[...], preferred_element_type=_F32)
    y = _head_rms(acck, bd_ref[...], gk_ref[...])
    cos = cos_ref[...]
    sina = sina_ref[...]
    sinb = sinb_ref[...]
    for hh in range(2):
        yy = y[:, LANES * hh:LANES * (hh + 1)]
        r = yy * cos + pltpu.roll(yy, LANES - 16, 1) * sina + pltpu.roll(yy, 16, 1) * sinb
        for e in range(2):
            k_ref[0, 2 * hh + e] = r[:, HEAD_DIM * e:HEAD_DIM * (e + 1)].astype(_BF16)
    accv = lax.dot_general(wvt_ref[...], h, _NT, preferred_element_type=_F32)
    ones = jnp.ones((V_ROWS - HEAD_DIM, accv.shape[1]), _BF16)
    for j in range(A_KV_HEADS):
        vt_ref[0, j, 0, 0:HEAD_DIM, :] = accv[HEAD_DIM * j:HEAD_DIM * (j + 1), :].astype(_BF16)
        vt_ref[0, j, 0, HEAD_DIM:V_ROWS, :] = ones


def _rope_tables(seq_len):
    rows = seq_len // GRID_W
    row = jnp.repeat(jnp.arange(rows, dtype=_F32), GRID_W)
    col = jnp.tile(jnp.arange(GRID_W, dtype=_F32), rows)
    n_freq = HEAD_DIM // 4
    inv_freq = 1.0 / (ROPE_THETA ** (jnp.arange(n_freq, dtype=_F32) / n_freq))
    ang_r = row[:, None] * inv_freq[None, :]
    ang_c = col[:, None] * inv_freq[None, :]
    ang = jnp.concatenate([ang_r, ang_r, ang_c, ang_c], axis=-1)
    cos, sin = jnp.cos(ang), jnp.sin(ang)
    first = (np.arange(HEAD_DIM) % 32) < 16
    cos2 = jnp.tile(cos, (1, 2))
    sina = jnp.tile(jnp.where(first[None, :], -sin, 0.0), (1, 2))
    sinb = jnp.tile(jnp.where(first[None, :], 0.0, sin), (1, 2))
    return cos2, sina, sinb, cos.T, sin.T


def _block_diag_ones():
    i = np.arange(256)
    return jnp.asarray((i[:, None] // HEAD_DIM) == (i[None, :] // HEAD_DIM), dtype=_BF16)


def _proj_a(x, g, scale, shift, wqt, wk, wvt, gq_col, gk_t, tables, bd):
    bsz, seq, _ = x.shape
    tm = FA_TK
    assert seq % tm == 0
    cos2, sina, sinb, cost, sint = tables
    nq = A_HEADS * HEAD_DIM
    nk = A_KV_HEADS * HEAD_DIM
    const = lambda b, i: (0, 0)
    return pl.pallas_call(
        _proj_a_kernel,
        grid=(bsz, seq // tm),
        in_specs=[
            pl.BlockSpec((1, tm, D_MODEL), lambda b, i: (b, i, 0)),
            pl.BlockSpec((1, D_MODEL), const),
            pl.BlockSpec((1, 1, D_MODEL), lambda b, i: (b, 0, 0)),
            pl.BlockSpec((1, 1, D_MODEL), lambda b, i: (b, 0, 0)),
            pl.BlockSpec((nq, D_MODEL), const),
            pl.BlockSpec((D_MODEL, nk), const),
            pl.BlockSpec((nk, D_MODEL), const),
            pl.BlockSpec((HEAD_DIM, 1), const),
            pl.BlockSpec((1, nk), const),
            pl.BlockSpec((tm, LANES), lambda b, i: (i, 0)),
            pl.BlockSpec((tm, LANES), lambda b, i: (i, 0)),
            pl.BlockSpec((tm, LANES), lambda b, i: (i, 0)),
            pl.BlockSpec((HEAD_DIM, tm), lambda b, i: (0, i)),
            pl.BlockSpec((HEAD_DIM, tm), lambda b, i: (0, i)),
            pl.BlockSpec((256, 256), const),
        ],
        out_specs=[
            pl.BlockSpec((1, nq, tm), lambda b, i: (b, 0, i)),
            pl.BlockSpec((1, A_KV_HEADS, tm, HEAD_DIM), lambda b, i: (b, 0, i, 0)),
            pl.BlockSpec((1, A_KV_HEADS, 1, V_ROWS, tm), lambda b, i: (b, 0, i, 0, 0)),
        ],
        out_shape=[
            jax.ShapeDtypeStruct((bsz, nq, seq), _BF16),
            jax.ShapeDtypeStruct((bsz, A_KV_HEADS, seq, HEAD_DIM), _BF16),
            jax.ShapeDtypeStruct((bsz, A_KV_HEADS, seq // tm, V_ROWS, tm), _BF16),
        ],
        compiler_params=_cparams(("parallel", "parallel")),
        name="proj_a",
    )(x, g, scale, shift, wqt, wk, wvt, gq_col, gk_t, cos2, sina, sinb, cost, sint, bd)


def _flash_kernel(qt_ref, k_ref, vt_ref, o_ref, s_ref, m_ref, acc_ref, *, tq, tk, nchunks):
    qt4 = jnp.concatenate([qt_ref[0, HEAD_DIM * hq:HEAD_DIM * (hq + 1), :] for hq in range(A_GROUP)],
                          axis=1)
    m_ref[...] = jnp.full(m_ref.shape, -jnp.inf, _F32)
    acc_ref[...] = jnp.zeros(acc_ref.shape, _F32)

    def scores(kc, slot):
        kb = k_ref[0, 0, pl.ds(pl.multiple_of(kc * tk, tk), tk), :]
        s_ref[slot] = jnp.dot(kb, qt4, preferred_element_type=_F32)

    def step(kc, slot, kc_next, slot_next):
        vb = vt_ref[0, 0, kc]
        if kc_next is not None:
            kb = k_ref[0, 0, pl.ds(pl.multiple_of(kc_next * tk, tk), tk), :]
        for hq in range(A_GROUP):
            sl = slice(tq * hq, tq * (hq + 1))
            if kc_next is not None:
                s_ref[slot_next, :, sl] = jnp.dot(kb, qt4[:, sl], preferred_element_type=_F32)
            s = s_ref[slot, :, sl]
            m_prev = m_ref[:, sl]
            m_new = jnp.maximum(m_prev, jnp.max(s, axis=0, keepdims=True))
            alpha = jnp.exp2(m_prev - m_new)
            p = jnp.exp2(s - m_new).astype(_BF16)
            acc_ref[:, sl] = alpha * acc_ref[:, sl] + jnp.dot(vb, p, preferred_element_type=_F32)
            m_ref[:, sl] = m_new

    def run(kc0, last):
        for j in range(FA_UNROLL):
            final = last and j == FA_UNROLL - 1
            step(kc0 + j, j % 2, None if final else kc0 + j + 1, (j + 1) % 2)

    scores(0, 0)

    def body(c, carry):
        run(FA_UNROLL * c, False)
        return carry

    lax.fori_loop(0, nchunks // FA_UNROLL - 1, body, 0)
    run(nchunks - FA_UNROLL, True)
    o = acc_ref[0:HEAD_DIM, :] / acc_ref[HEAD_DIM:HEAD_DIM + 1, :]
    o = jnp.concatenate([o[:, tq * hq:tq * (hq + 1)] for hq in range(A_GROUP)], axis=0)
    o_ref[0] = o.T.astype(_BF16)


def _flash_a(qt, k, vt):
    bsz, nq, seq = qt.shape
    tq = min(FA_TQ, seq)
    tk = vt.shape[-1]
    nchunks = seq // tk
    assert nchunks % FA_UNROLL == 0 and FA_UNROLL % 2 == 0 and seq % tq == 0
    return pl.pallas_call(
        functools.partial(_flash_kernel, tq=tq, tk=tk, nchunks=nchunks),
        grid=(bsz, A_KV_HEADS, seq // tq),
        in_specs=[
            pl.BlockSpec((1, A_GROUP * HEAD_DIM, tq), lambda b, h, i: (b, h, i)),
            pl.BlockSpec((1, 1, seq, HEAD_DIM), lambda b, h, i: (b, h, 0, 0)),
            pl.BlockSpec((1, 1, nchunks, V_ROWS, tk), lambda b, h, i: (b, h, 0, 0, 0)),
        ],
        out_specs=pl.BlockSpec((1, tq, A_GROUP * HEAD_DIM), lambda b, h, i: (b, i, h)),
        out_shape=jax.ShapeDtypeStruct((bsz, seq, nq), _BF16),
        scratch_shapes=[
            pltpu.VMEM((2, tk, A_GROUP * tq), _F32),
            pltpu.VMEM((1, A_GROUP * tq), _F32),
            pltpu.VMEM((V_ROWS, A_GROUP * tq), _F32),
        ],
        compiler_params=_cparams(("parallel", "parallel", "parallel")),
        name="flash_a",
    )(qt, k, vt)


def _oproj_kernel(a_ref, w_ref, x_ref, gate_ref, o_ref):
    y = jnp.dot(a_ref[0], w_ref[...], preferred_element_type=_F32)
    o_ref[0] = x_ref[0] + gate_ref[0] * y


def _oproj(a, w, x, gate):
    bsz, seq, _ = x.shape
    tm = min(ROW_TILE, seq)
    return pl.pallas_call(
        _oproj_kernel,
        grid=(bsz, seq // tm),
        in_specs=[
            pl.BlockSpec((1, tm, D_MODEL), lambda b, i: (b, i, 0)),
            pl.BlockSpec((D_MODEL, D_MODEL), lambda b, i: (0, 0)),
            pl.BlockSpec((1, tm, D_MODEL), lambda b, i: (b, i, 0)),
            pl.BlockSpec((1, 1, D_MODEL), lambda b, i: (b, 0, 0)),
        ],
        out_specs=pl.BlockSpec((1, tm, D_MODEL), lambda b, i: (b, i, 0)),
        out_shape=jax.ShapeDtypeStruct(x.shape, _F32),
        compiler_params=_cparams(("parallel", "parallel")),
        name="oproj",
    )(a, w, x, gate)


def _proj_b_kernel(x_ref, g_ref, sc_ref, sh_ref, w_ref, gv_ref, bd_ref, o_ref, scr_ref, *, dil, tm):
    h = _rms_mod(x_ref[0], g_ref[...], sc_ref[0], sh_ref[0])
    rows = tm // dil
    if dil > 1:
        for c in range(D_MODEL // LANES):
            scr_ref[c] = h[:, LANES * c:LANES * (c + 1)]
        h = jnp.concatenate(
            [jnp.concatenate([scr_ref[c, pl.ds(r, rows, stride=dil), :]
                              for c in range(D_MODEL // LANES)], axis=1)
             for r in range(dil)], axis=0)
    bd = bd_ref[...]
    acc = jnp.dot(h.astype(_BF16), w_ref[...], preferred_element_type=_F32)
    nqk = 2 * B_HEADS * HEAD_DIM

    def emit(col, y):
        for r in range(dil):
            o_ref[0, r, :, col:col + y.shape[1]] = y[rows * r:rows * (r + 1)].astype(_BF16)

    for c in range(nqk // 256):
        sl = slice(256 * c, 256 * (c + 1))
        emit(256 * c, _head_rms(acc[:, sl], bd, gv_ref[:, sl]))
    emit(nqk, acc[:, nqk:])


def _proj_b(x, g, scale, shift, w, gi, gvec, bd, dil):
    bsz, seq, _ = x.shape
    tm = min(ROW_TILE, seq)
    n = 3 * B_HEADS * HEAD_DIM
    const = lambda b, i: (0, 0)
    return pl.pallas_call(
        functools.partial(_proj_b_kernel, dil=dil, tm=tm),
        grid=(bsz, seq // tm),
        in_specs=[
            pl.BlockSpec((1, tm, D_MODEL), lambda b, i: (b, i, 0)),
            pl.BlockSpec((1, D_MODEL), const),
            pl.BlockSpec((1, 1, D_MODEL), lambda b, i: (b, 0, 0)),
            pl.BlockSpec((1, 1, D_MODEL), lambda b, i: (b, 0, 0)),
            pl.BlockSpec((D_MODEL, n), lambda b, i: (0, gi)),
            pl.BlockSpec((1, 2 * B_HEADS * HEAD_DIM), const),
            pl.BlockSpec((256, 256), const),
        ],
        out_specs=pl.BlockSpec((1, dil, tm // dil, n), lambda b, i: (b, 0, i, 0)),
        out_shape=jax.ShapeDtypeStruct((bsz, dil, seq // dil, n), _BF16),
        scratch_shapes=[pltpu.VMEM((D_MODEL // LANES, tm, LANES), _F32)],
        compiler_params=_cparams(("parallel", "parallel")),
        name="proj_b",
    )(x, g, scale, shift, w, gvec, bd)


def _dilated_kernel(main_ref, kp_ref, vp_ref, kn_ref, vn_ref, bias_ref, o_ref, lse_ref,
                    kf_ref, vf_ref, *, tq, length):
    nh = B_HEADS * HEAD_DIM
    l0 = pl.program_id(2) * tq
    kf_ref[0:DIL_RADIUS] = kp_ref[0, 0]
    kf_ref[DIL_RADIUS:DIL_RADIUS + tq] = main_ref[0, 0, :, nh:2 * nh]
    kf_ref[DIL_RADIUS + tq:] = kn_ref[0, 0]
    vf_ref[0:DIL_RADIUS] = vp_ref[0, 0]
    vf_ref[DIL_RADIUS:DIL_RADIUS + tq] = main_ref[0, 0, :, 2 * nh:3 * nh]
    vf_ref[DIL_RADIUS + tq:] = vn_ref[0, 0]
    width = DIL_SUB + 2 * DIL_RADIUS
    lane = lax.broadcasted_iota(jnp.int32, (DIL_SUB, LANES), 1)
    low_half = lane < HEAD_DIM
    lse_head = lane // LSE_LANES
    col = lax.broadcasted_iota(jnp.int32, (DIL_SUB, width), 1)

    def sub_block(u, carry):
        r0 = pl.multiple_of(u * DIL_SUB, DIL_SUB)
        kpos = col + (l0 + r0 - DIL_RADIUS)
        valid = (kpos >= 0) & (kpos < length)
        lse_tile = jnp.zeros((DIL_SUB, LANES), _F32)
        for m in range(B_HEADS // 2):
            sl = slice(LANES * m, LANES * (m + 1))
            qp = main_ref[0, 0, pl.ds(r0, DIL_SUB), sl]
            kw = kf_ref[pl.ds(r0, width), sl]
            vw = vf_ref[pl.ds(r0, width), sl]
            res = []
            for hh in range(2):
                keep = low_half if hh == 0 else jnp.logical_not(low_half)
                qm = jnp.where(keep, qp, jnp.zeros_like(qp))
                s = lax.dot_general(qm, kw, _NT, preferred_element_type=_F32) + bias_ref[2 * m + hh]
                s = jnp.where(valid, s, NEG_INF)
                mx = jnp.max(s, axis=1, keepdims=True)
                p = jnp.exp(s - mx)
                den = jnp.sum(p, axis=1, keepdims=True)
                pv = jnp.dot(p.astype(_BF16), vw, preferred_element_type=_F32)
                res.append((pv / den, mx + jnp.log(den)))
            o_ref[0, 0, pl.ds(r0, DIL_SUB), sl] = jnp.where(low_half, res[0][0],
                                                             res[1][0]).astype(_BF16)
            for hh in range(2):
                lse_tile = jnp.where(lse_head == 2 * m + hh,
                                     jnp.broadcast_to(res[hh][1], (DIL_SUB, LANES)), lse_tile)
        lse_ref[0, 0, pl.ds(r0, DIL_SUB), :] = lse_tile
        return carry

    lax.fori_loop(0, tq // DIL_SUB, sub_block, 0)


def _dilated_attn(qkv, bias):
    bsz, dil, length, n = qkv.shape
    nh = B_HEADS * HEAD_DIM
    tq = min(DIL_TQ, length)
    nb = tq // DIL_RADIUS
    last = length // DIL_RADIUS - 1
    prev_ix = lambda i: jnp.maximum(i * nb - 1, 0)
    next_ix = lambda i: jnp.minimum((i + 1) * nb, last)
    halo = (1, 1, DIL_RADIUS, nh)
    width = DIL_SUB + 2 * DIL_RADIUS
    assert tq % DIL_SUB == 0 and length % tq == 0
    return pl.pallas_call(
        functools.partial(_dilated_kernel, tq=tq, length=length),
        grid=(bsz, dil, length // tq),
        in_specs=[
            pl.BlockSpec((1, 1, tq, n), lambda b, r, i: (b, r, i, 0)),
            pl.BlockSpec(halo, lambda b, r, i: (b, r, prev_ix(i), 1)),
            pl.BlockSpec(halo, lambda b, r, i: (b, r, prev_ix(i), 2)),
            pl.BlockSpec(halo, lambda b, r, i: (b, r, next_ix(i), 1)),
            pl.BlockSpec(halo, lambda b, r, i: (b, r, next_ix(i), 2)),
            pl.BlockSpec((B_HEADS, DIL_SUB, width), lambda b, r, i: (0, 0, 0)),
        ],
        out_specs=[pl.BlockSpec((1, 1, tq, nh), lambda b, r, i: (b, r, i, 0)),
                   pl.BlockSpec((1, 1, tq, LANES), lambda b, r, i: (b, r, i, 0))],
        out_shape=[jax.ShapeDtypeStruct((bsz, dil, length, nh), _BF16),
                   jax.ShapeDtypeStruct((bsz, dil, length, LANES), _F32)],
        scratch_shapes=[pltpu.VMEM((tq + 2 * DIL_RADIUS, nh), _BF16),
                        pltpu.VMEM((tq + 2 * DIL_RADIUS, nh), _BF16)],
        compiler_params=_cparams(("parallel", "parallel", "parallel")),
        name="dilated_attn",
    )(qkv, qkv, qkv, qkv, qkv, bias)


def _t5_bucket(rel):
    nb = NUM_BUCKETS // 2
    max_exact = nb // 2
    ret = (rel > 0).astype(np.int32) * nb
    n = np.abs(rel)
    large = max_exact + (np.log(np.maximum(n, 1) / max_exact) / np.log(MAX_DISTANCE / max_exact)
                         * (nb - max_exact)).astype(np.int32)
    large = np.minimum(large, nb - 1)
    return (ret + np.where(n < max_exact, n, large)).astype(np.int32)


def _dilated_bias(rel_bias_g, dil):
    width = DIL_SUB + 2 * DIL_RADIUS
    rel = np.arange(width)[None, :] - DIL_RADIUS - np.arange(DIL_SUB)[:, None]
    bucket = _t5_bucket(rel * dil)
    band = np.abs(rel) <= DIL_RADIUS
    onehot = (bucket.reshape(-1, 1) == np.arange(NUM_BUCKETS)[None, :]).astype(np.float32)
    bias = jnp.dot(jnp.asarray(onehot), rel_bias_g.astype(_F32), precision=lax.Precision.HIGHEST)
    bias = bias.reshape(DIL_SUB, width, B_HEADS).transpose(2, 0, 1)
    return jnp.where(jnp.asarray(band)[None], bias, NEG_INF)


def _merge_oproj_kernel(o0_ref, l0_ref, o1_ref, l1_ref, o2_ref, l2_ref, ex_ref, w_ref, x_ref,
                        gate_ref, out_ref, so1, sl1, so2, sl2, *, tm):
    d1 = DILATED_CONFIGS[1][1]
    d2 = DILATED_CONFIGS[2][1]

    def to_token_order(src_ref, scr_ref, dil):
        nc = src_ref.shape[3] // LANES
        for r in range(dil):
            for c in range(nc):
                scr_ref[c, pl.ds(r, tm // dil, stride=dil), :] = (
                    src_ref[0, r, :, LANES * c:LANES * (c + 1)].astype(_F32))
        return jnp.concatenate([scr_ref[c] for c in range(nc)], axis=1)

    o1 = to_token_order(o1_ref, so1, d1)
    l1 = to_token_order(l1_ref, sl1, d1)
    o2 = to_token_order(o2_ref, so2, d2)
    l2 = to_token_order(l2_ref, sl2, d2)
    l0 = l0_ref[0, 0]
    mx = jnp.maximum(jnp.maximum(l0, l1), l2)
    e0, e1, e2 = jnp.exp(l0 - mx), jnp.exp(l1 - mx), jnp.exp(l2 - mx)
    inv = 1.0 / (e0 + e1 + e2)
    ex = ex_ref[...]

    def expand(wgt):
        hi = wgt.astype(_BF16)
        lo = (wgt - hi.astype(_F32)).astype(_BF16)
        return jnp.dot(jnp.concatenate([hi, lo], axis=1), ex, preferred_element_type=_F32)

    o = (expand(e0 * inv) * o0_ref[0, 0].astype(_F32) + expand(e1 * inv) * o1
         + expand(e2 * inv) * o2)
    y = jnp.dot(o.astype(_BF16), w_ref[...], preferred_element_type=_F32)
    out_ref[0] = x_ref[0] + gate_ref[0] * y


def _merge_oproj(outs, lses, w, x, gate):
    bsz, seq, _ = x.shape
    tm = min(ROUTE_TILE, seq)
    nh = B_HEADS * HEAD_DIM
    specs = []
    for (_, dil) in DILATED_CONFIGS:
        specs += [pl.BlockSpec((1, dil, tm // dil, nh), lambda b, i: (b, 0, i, 0)),
                  pl.BlockSpec((1, dil, tm // dil, LANES), lambda b, i: (b, 0, i, 0))]
    rows = np.arange(2 * LANES)[:, None] % LANES
    cols = np.arange(nh)[None, :]
    ex = jnp.asarray(rows == LSE_LANES * (cols // HEAD_DIM), dtype=_BF16)
    o_scr = pltpu.VMEM((nh // LANES, tm, LANES), _F32)
    l_scr = pltpu.VMEM((1, tm, LANES), _F32)
    return pl.pallas_call(
        functools.partial(_merge_oproj_kernel, tm=tm),
        grid=(bsz, seq // tm),
        in_specs=specs + [
            pl.BlockSpec((2 * LANES, nh), lambda b, i: (0, 0)),
            pl.BlockSpec((nh, D_MODEL), lambda b, i: (0, 0)),
            pl.BlockSpec((1, tm, D_MODEL), lambda b, i: (b, i, 0)),
            pl.BlockSpec((1, 1, D_MODEL), lambda b, i: (b, 0, 0)),
        ],
        out_specs=pl.BlockSpec((1, tm, D_MODEL), lambda b, i: (b, i, 0)),
        out_shape=jax.ShapeDtypeStruct(x.shape, _F32),
        scratch_shapes=[o_scr, l_scr, o_scr, l_scr],
        compiler_params=_cparams(("parallel", "parallel")),
        name="merge_oproj",
    )(outs[0], lses[0], outs[1], lses[1], outs[2], lses[2], ex, w, x, gate)


def _router_kernel(x_ref, g_ref, sc_ref, sh_ref, wr_ref, br_ref,
                   hf_ref, idx_ref, gate_ref, rank_ref, cnt_ref, carry_ref, tri_ref, *, tm):
    first = (pl.program_id(0) == 0) & (pl.program_id(1) == 0)

    @pl.when(first)
    def _():
        carry_ref[...] = jnp.zeros(carry_ref.shape, _F32)
        r = lax.broadcasted_iota(jnp.int32, (tm, tm), 0)
        c = lax.broadcasted_iota(jnp.int32, (tm, tm), 1)
        tri_ref[...] = (r < c).astype(_BF16)

    hf = _rms_mod(x_ref[0], g_ref[...], sc_ref[0], sh_ref[0])
    for j in range(ROW_WORDS):
        hf_ref[pl.ds(j, tm, stride=ROW_WORDS), :] = hf[:, LANES * j:LANES * (j + 1)]
    logits = lax.dot_general(wr_ref[...], hf, _NT, preferred_element_type=_F32,
                             precision=lax.Precision.HIGHEST) + br_ref[...]
    eio = lax.broadcasted_iota(jnp.int32, logits.shape, 0).astype(_F32)
    cur = logits
    vals, sel = [], []
    for _ in range(TOP_K):
        mx = jnp.max(cur, axis=0, keepdims=True)
        ix = jnp.min(jnp.where(cur == mx, eio, float(N_EXPERTS)), axis=0, keepdims=True)
        vals.append(mx)
        sel.append(ix)
        cur = jnp.where(eio == ix, -jnp.inf, cur)
    ex = [jnp.exp(v - vals[0]) for v in vals]
    den = ex[0] + ex[1] + ex[2] + ex[3]
    member = jnp.zeros(logits.shape, _F32)
    for ix in sel:
        member = member + (eio == ix).astype(_F32)
    before = carry_ref[:, 0:1] + jnp.dot(member.astype(_BF16), tri_ref[...],
                                         preferred_element_type=_F32)
    ranks = [jnp.sum(jnp.where(eio == ix, before, 0.0), axis=0, keepdims=True) for ix in sel]
    idx_ref[...] = jnp.concatenate(sel, axis=0).astype(jnp.int32)
    gate_ref[...] = jnp.concatenate([e / den for e in ex], axis=0)
    rank_ref[...] = jnp.concatenate(ranks, axis=0).astype(jnp.int32)
    carry_ref[...] = carry_ref[...] + jnp.sum(member, axis=1, keepdims=True)
    cnt_ref[...] = carry_ref[...]


def _router(x, g, scale, shift, wr_t, br):
    bsz, seq, _ = x.shape
    tm = min(ROUTE_TILE, seq)
    ns = seq // tm
    tok = bsz * seq
    const = lambda b, i: (0, 0)
    flat = lambda b, i: (0, b * ns + i)
    return pl.pallas_call(
        functools.partial(_router_kernel, tm=tm),
        grid=(bsz, ns),
        in_specs=[
            pl.BlockSpec((1, tm, D_MODEL), lambda b, i: (b, i, 0)),
            pl.BlockSpec((1, D_MODEL), const),
            pl.BlockSpec((1, 1, D_MODEL), lambda b, i: (b, 0, 0)),
            pl.BlockSpec((1, 1, D_MODEL), lambda b, i: (b, 0, 0)),
            pl.BlockSpec((N_EXPERTS, D_MODEL), const),
            pl.BlockSpec((N_EXPERTS, 1), const),
        ],
        out_specs=[
            pl.BlockSpec((tm * ROW_WORDS, LANES), lambda b, i: (b * ns + i, 0)),
            pl.BlockSpec((TOP_K, tm), flat),
            pl.BlockSpec((TOP_K, tm), flat),
            pl.BlockSpec((TOP_K, tm), flat),
            pl.BlockSpec((N_EXPERTS, LANES), const),
        ],
        out_shape=[
            jax.ShapeDtypeStruct((tok * ROW_WORDS, LANES), _F32),
            jax.ShapeDtypeStruct((TOP_K, tok), jnp.int32),
            jax.ShapeDtypeStruct((TOP_K, tok), _F32),
            jax.ShapeDtypeStruct((TOP_K, tok), jnp.int32),
            jax.ShapeDtypeStruct((N_EXPERTS, LANES), _F32),
        ],
        scratch_shapes=[pltpu.VMEM((N_EXPERTS, LANES), _F32), pltpu.VMEM((tm, tm), _BF16)],
        compiler_params=_cparams(("arbitrary", "arbitrary")),
        name="router",
    )(x, g, scale, shift, wr_t, br)


def _row_copy(src_ref, src_row, dst_ref, dst_row, sem):
    return pltpu.make_async_copy(
        src_ref.at[pl.ds(pl.multiple_of(src_row * ROW_WORDS, ROW_WORDS), ROW_WORDS), :],
        dst_ref.at[pl.ds(pl.multiple_of(dst_row * ROW_WORDS, ROW_WORDS), ROW_WORDS), :], sem)


def _dispatch_kernel(pad_start_ref, pad_len_ref, tail_ref, dest_hbm, hf_ref, xs_ref,
                     dest_smem0, dest_smem1, zero_ref, isem, sem, zsem, *, tm, nsteps):
    g = pl.program_id(0)
    n = tm * TOP_K
    dest_smem = (dest_smem0, dest_smem1)

    def index_copy(step, slot):
        off = pl.multiple_of(step * n, n)
        return pltpu.make_async_copy(dest_hbm.at[pl.ds(off, n)], dest_smem[slot], isem.at[slot])

    def zero_fill(e, wait):
        start = pad_start_ref[e]
        length = pad_len_ref[e]
        for bit in reversed(range(EXPERT_BLOCK.bit_length() - 1)):
            size = 1 << bit
            row = start + (length & ~(2 * size - 1))
            cp = pltpu.make_async_copy(
                zero_ref.at[pl.ds(0, size * ROW_WORDS), :],
                xs_ref.at[pl.ds(pl.multiple_of(row * ROW_WORDS, ROW_WORDS), size * ROW_WORDS), :],
                zsem)

            @pl.when((length & size) != 0)
            def _():
                if wait:
                    cp.wait()
                else:
                    cp.start()

    @pl.when(g == 0)
    def _():
        index_copy(0, 0).start()
        zero_ref[...] = jnp.zeros(zero_ref.shape, _F32)

        def fill(e, carry):
            zero_fill(e, False)
            return carry

        lax.fori_loop(0, N_EXPERTS, fill, 0)

        def done(e, carry):
            zero_fill(e, True)
            return carry

        lax.fori_loop(0, N_EXPERTS, done, 0)

        def tail_copy(j):
            row = (tail_ref[0] + j) * EXPERT_BLOCK
            return pltpu.make_async_copy(
                zero_ref,
                xs_ref.at[pl.ds(pl.multiple_of(row * ROW_WORDS, ROW_WORDS),
                                EXPERT_BLOCK * ROW_WORDS), :], zsem)

        def tail_fill(j, carry):
            tail_copy(j).start()
            return carry

        lax.fori_loop(0, tail_ref[1], tail_fill, 0)

        def tail_done(j, carry):
            tail_copy(j).wait()
            return carry

        lax.fori_loop(0, tail_ref[1], tail_done, 0)

    for slot in range(2):
        @pl.when(g % 2 == slot)
        def _():
            @pl.when(g + 1 < nsteps)
            def _():
                index_copy(g + 1, 1 - slot).start()

            index_copy(g, slot).wait()
            idx_ref = dest_smem[slot]

            def issue(r, carry):
                for k in range(TOP_K):
                    _row_copy(hf_ref, r, xs_ref, idx_ref[r * TOP_K + k], sem).start(priority=k % 2)
                return carry

            lax.fori_loop(0, tm, issue, 0, unroll=2)

    for k in range(TOP_K):
        pltpu.make_async_copy(hf_ref, xs_ref.at[pl.ds(0, tm * ROW_WORDS), :], sem).wait()


def _dispatch(pad_start, pad_len, tail, dest_flat, hf, n_rows):
    tok = hf.shape[0] // ROW_WORDS
    tm = min(DISPATCH_TILE, tok)
    assert tok % tm == 0 and (tm * TOP_K) % 1024 == 0
    grid_spec = pltpu.PrefetchScalarGridSpec(
        num_scalar_prefetch=3,
        grid=(tok // tm,),
        in_specs=[
            pl.BlockSpec(memory_space=pl.ANY),
            pl.BlockSpec((tm * ROW_WORDS, LANES), lambda i, ps, pn, tl: (i, 0)),
        ],
        out_specs=pl.BlockSpec(memory_space=pl.ANY),
        scratch_shapes=[pltpu.SMEM((tm * TOP_K,), jnp.int32),
                        pltpu.SMEM((tm * TOP_K,), jnp.int32),
                        pltpu.VMEM((EXPERT_BLOCK * ROW_WORDS, LANES), _F32),
                        pltpu.SemaphoreType.DMA((2,)), pltpu.SemaphoreType.DMA,
                        pltpu.SemaphoreType.DMA],
    )
    return pl.pallas_call(
        functools.partial(_dispatch_kernel, tm=tm, nsteps=tok // tm),
        grid_spec=grid_spec,
        out_shape=jax.ShapeDtypeStruct((n_rows * ROW_WORDS, LANES), _F32),
        compiler_params=_cparams(("arbitrary",)),
        name="moe_dispatch",
    )(pad_start, pad_len, tail, dest_flat, hf)


def _expert_kernel(be_ref, nu_ref, xs_ref, wgu_ref, bgu_ref, wdn_ref, bdn_ref, ys_ref, *, blk):
    del be_ref
    i = pl.program_id(0)
    dff = wdn_ref.shape[2]

    @pl.when(i < nu_ref[0])
    def _():
        x = jnp.concatenate([xs_ref[pl.ds(j, blk, stride=ROW_WORDS), :] for j in range(ROW_WORDS)],
                            axis=1).astype(_BF16)
        gu = jnp.dot(x, wgu_ref[0, 0], preferred_element_type=_F32) + bgu_ref[0, 0]
        g = jnp.minimum(gu[:, :dff], SWIGLU_LIMIT)
        u = jnp.clip(gu[:, dff:], -SWIGLU_LIMIT, SWIGLU_LIMIT)
        act = g * jax.nn.sigmoid(SWIGLU_ALPHA * g)
        mid = ((u + 1.0) * act).astype(_BF16)
        y = jnp.dot(mid, wdn_ref[0, 0], preferred_element_type=_F32) + bdn_ref[0, 0]
        for j in range(ROW_WORDS):
            ys_ref[pl.ds(j, blk, stride=ROW_WORDS), :] = y[:, LANES * j:LANES * (j + 1)]

    @pl.when(i >= nu_ref[0])
    def _():
        ys_ref[...] = jnp.zeros(ys_ref.shape, _F32)


def _experts(blk_e, n_used, xs, wgu, bgu, wdn, bdn, layer):
    blk = EXPERT_BLOCK
    n_blk = xs.shape[0] // (blk * ROW_WORDS)
    dff = wdn.shape[2]
    grid_spec = pltpu.PrefetchScalarGridSpec(
        num_scalar_prefetch=2,
        grid=(n_blk,),
        in_specs=[
            pl.BlockSpec((blk * ROW_WORDS, LANES), lambda i, be, nu: (jnp.minimum(i, nu[0] - 1), 0)),
            pl.BlockSpec((1, 1, D_MODEL, 2 * dff), lambda i, be, nu: (layer, be[i], 0, 0)),
            pl.BlockSpec((1, 1, 1, 2 * dff), lambda i, be, nu: (layer, be[i], 0, 0)),
            pl.BlockSpec((1, 1, dff, D_MODEL), lambda i, be, nu: (layer, be[i], 0, 0)),
            pl.BlockSpec((1, 1, 1, D_MODEL), lambda i, be, nu: (layer, be[i], 0, 0)),
        ],
        out_specs=pl.BlockSpec((blk * ROW_WORDS, LANES), lambda i, be, nu: (i, 0)),
    )
    return pl.pallas_call(
        functools.partial(_expert_kernel, blk=blk),
        grid_spec=grid_spec,
        out_shape=jax.ShapeDtypeStruct(xs.shape, _F32),
        compiler_params=_cparams(("arbitrary",)),
        name="moe_experts",
    )(blk_e, n_used, xs, wgu, bgu, wdn, bdn)


def _combine_kernel(dest_hbm, ys_ref, gate_ref, x_ref, gf_ref, o_ref,
                    dest_smem0, dest_smem1, buf_ref, isem, sem, *, tm, ns, nsteps):
    g = pl.program_id(0) * ns + pl.program_id(1)
    n = tm * TOP_K
    dest_smem = (dest_smem0, dest_smem1)

    def index_copy(step, slot):
        off = pl.multiple_of(step * n, n)
        return pltpu.make_async_copy(dest_hbm.at[pl.ds(off, n)], dest_smem[slot], isem.at[slot])

    def start_gathers(step, slot):
        index_copy(step, slot).wait()
        idx_ref = dest_smem[slot]

        def issue(r, carry):
            for k in range(TOP_K):
                _row_copy(ys_ref, idx_ref[r * TOP_K + k], buf_ref.at[slot], k * tm + r,
                          sem.at[slot]).start(priority=k % 2)
            return carry

        lax.fori_loop(0, tm, issue, 0, unroll=2)

        @pl.when(step + 1 < nsteps)
        def _():
            index_copy(step + 1, 1 - slot).start()

    def finish(slot):
        pltpu.make_async_copy(ys_ref.at[pl.ds(0, TOP_K * tm * ROW_WORDS), :], buf_ref.at[slot],
                              sem.at[slot]).wait()
        gate = gate_ref[...]
        cols = []
        for j in range(ROW_WORDS):
            acc = None
            for k in range(TOP_K):
                slab = buf_ref[slot, pl.ds(k * tm * ROW_WORDS + j, tm, stride=ROW_WORDS), :]
                term = gate[:, k:k + 1] * slab
                acc = term if acc is None else acc + term
            cols.append(acc)
        moe = jnp.concatenate(cols, axis=1)
        o_ref[0] = x_ref[0] + gf_ref[0] * moe

    @pl.when(g == 0)
    def _():
        index_copy(0, 0).start()
        start_gathers(0, 0)

    for slot in range(2):
        @pl.when(g % 2 == slot)
        def _():
            @pl.when(g + 1 < nsteps)
            def _():
                start_gathers(g + 1, 1 - slot)

            finish(slot)


def _combine(dest_flat, gate_tok, ys, x, gate_f):
    bsz, seq, _ = x.shape
    tm = min(ROUTE_TILE, seq)
    ns = seq // tm
    return pl.pallas_call(
        functools.partial(_combine_kernel, tm=tm, ns=ns, nsteps=bsz * ns),
        grid=(bsz, ns),
        in_specs=[
            pl.BlockSpec(memory_space=pl.ANY),
            pl.BlockSpec(memory_space=pl.ANY),
            pl.BlockSpec((tm, TOP_K), lambda b, i: (b * ns + i, 0)),
            pl.BlockSpec((1, tm, D_MODEL), lambda b, i: (b, i, 0)),
            pl.BlockSpec((1, 1, D_MODEL), lambda b, i: (b, 0, 0)),
        ],
        out_specs=pl.BlockSpec((1, tm, D_MODEL), lambda b, i: (b, i, 0)),
        out_shape=jax.ShapeDtypeStruct(x.shape, _F32),
        scratch_shapes=[
            pltpu.SMEM((tm * TOP_K,), jnp.int32),
            pltpu.SMEM((tm * TOP_K,), jnp.int32),
            pltpu.VMEM((2, TOP_K * tm * ROW_WORDS, LANES), _F32),
            pltpu.SemaphoreType.DMA((2,)),
            pltpu.SemaphoreType.DMA((2,)),
        ],
        compiler_params=_cparams(("arbitrary", "arbitrary")),
        name="moe_combine",
    )(dest_flat, ys, gate_tok, x, gate_f)


def _moe(x, g, scale, shift, gate_f, wr_t, br, wgu, bgu, wdn, bdn, layer):
    bsz, seq, _ = x.shape
    tok = bsz * seq
    blk = EXPERT_BLOCK
    hf, idx, gate, rank, cnt = _router(x, g, scale, shift, wr_t, br)
    counts = cnt[:, 0].astype(jnp.int32)
    padded = (counts + blk - 1) // blk * blk
    pend = jnp.cumsum(padded)
    pstart = pend - padded
    eids = jnp.arange(N_EXPERTS, dtype=jnp.int32)[:, None, None]
    base = jnp.sum(jnp.where(idx[None] == eids, pstart[:, None, None], 0), axis=0)
    dest = (base + rank).T.reshape(tok * TOP_K)
    gate_tok = gate.T
    n_blk = (tok * TOP_K + N_EXPERTS * (blk - 1) + blk - 1) // blk
    blk_start = jnp.arange(n_blk, dtype=jnp.int32) * blk
    blk_e = jnp.minimum(jnp.sum(blk_start[:, None] >= pend[None, :], axis=1),
                        N_EXPERTS - 1).astype(jnp.int32)
    n_used = (pend[-1:] // blk).astype(jnp.int32)
    tail = jnp.concatenate([n_used, n_blk - n_used])
    xs = _dispatch(pstart + counts, padded - counts, tail, dest, hf, n_blk * blk)
    ys = _experts(blk_e, n_used, xs, wgu, bgu, wdn, bdn, layer)
    return _combine(dest, gate_tok, ys, x, gate_f)


def _prepare(norm_mix_g, norm_ffn_g, w_qkv_a, q_norm_a, k_norm_a, w_o_a, w_qkv_b, q_norm_b, k_norm_b,
             w_o_b, rel_bias, w_router, b_router, w_gate_up, b_gate_up, w_down, b_down):
    nq = A_HEADS * HEAD_DIM
    nk = A_KV_HEADS * HEAD_DIM
    nh = B_HEADS * HEAD_DIM
    scale = HEAD_DIM ** -0.5
    p = {}
    wa = w_qkv_a[0]
    p["wqt_a"] = wa[:, :nq].T.astype(_BF16)
    p["wk_a"] = wa[:, nq:nq + nk].astype(_BF16)
    p["wvt_a"] = wa[:, nq + nk:].T.astype(_BF16)
    p["gq_a"] = (q_norm_a[0] * (scale * np.log2(np.e))).reshape(HEAD_DIM, 1)
    p["gk_a"] = jnp.tile(k_norm_a[0], A_KV_HEADS).reshape(1, nk)
    p["wo_a"] = w_o_a[0].astype(_BF16)
    wb = w_qkv_b[0].astype(_BF16)
    p["w_b"] = wb
    p["gv_b"] = [jnp.concatenate([jnp.tile(q_norm_b[0, gi], B_HEADS) * scale,
                                  jnp.tile(k_norm_b[0, gi], B_HEADS)]).reshape(1, 2 * nh)
                 for gi in range(len(DILATED_CONFIGS))]
    p["bias_b"] = [_dilated_bias(rel_bias[:, gi * B_HEADS:(gi + 1) * B_HEADS], dil)
                   for gi, (_, dil) in enumerate(DILATED_CONFIGS)]
    p["wo_b"] = w_o_b[0].astype(_BF16)
    p["g_mix"] = [norm_mix_g[i].reshape(1, D_MODEL) for i in range(DEPTH)]
    p["g_ffn"] = [norm_ffn_g[i].reshape(1, D_MODEL) for i in range(DEPTH)]
    p["wr_t"] = [w_router[i].T for i in range(DEPTH)]
    p["br"] = [b_router[i].reshape(N_EXPERTS, 1) for i in range(DEPTH)]
    p["wgu"] = w_gate_up.astype(_BF16)
    p["bgu"] = b_gate_up.reshape(DEPTH, N_EXPERTS, 1, -1)
    p["wdn"] = w_down.astype(_BF16)
    p["bdn"] = b_down.reshape(DEPTH, N_EXPERTS, 1, -1)
    p["bd"] = _block_diag_ones()
    return p


def _trunk(x, c, w_ada, b_ada, p):
    bsz, seq, _ = x.shape
    tables = _rope_tables(seq)
    for i in range(DEPTH):
        mod = _ada_mod(c, w_ada, b_ada, i)
        shift_m, scale_m, gate_m, shift_f, scale_f, gate_f = [
            mod[:, D_MODEL * k:D_MODEL * (k + 1)].reshape(bsz, 1, D_MODEL) for k in range(6)]
        if i % 2 == 0:
            qt, k, vt = _proj_a(x, p["g_mix"][i], scale_m, shift_m, p["wqt_a"], p["wk_a"],
                                p["wvt_a"], p["gq_a"], p["gk_a"], tables, p["bd"])
            a = _flash_a(qt, k, vt)
            x = _oproj(a, p["wo_a"], x, gate_m)
        else:
            outs, lses = [], []
            for gi, (_, dil) in enumerate(DILATED_CONFIGS):
                qkv = _proj_b(x, p["g_mix"][i], scale_m, shift_m, p["w_b"], gi, p["gv_b"][gi],
                              p["bd"], dil)
                o, lse = _dilated_attn(qkv, p["bias_b"][gi])
                outs.append(o)
                lses.append(lse)
            x = _merge_oproj(outs, lses, p["wo_b"], x, gate_m)
        x = _moe(x, p["g_ffn"][i], scale_f, shift_f, gate_f, p["wr_t"][i], p["br"][i],
                 p["wgu"], p["bgu"], p["wdn"], p["bdn"], i)
    return x


def kernel(x_prompt, x_sample, c_prompt, c_sample, norm_mix_g, norm_ffn_g, w_ada, b_ada, w_qkv_a,
           q_norm_a, k_norm_a, w_o_a, w_qkv_b, q_norm_b, k_norm_b, w_o_b, rel_bias, w_router,
           b_router, w_gate_up, b_gate_up, w_down, b_down):
    p = _prepare(norm_mix_g, norm_ffn_g, w_qkv_a, q_norm_a, k_norm_a, w_o_a, w_qkv_b, q_norm_b,
                 k_norm_b, w_o_b, rel_bias, w_router, b_router, w_gate_up, b_gate_up, w_down, b_down)
    y_prompt = _trunk(x_prompt, c_prompt, w_ada, b_ada, p)
    y_sample = _trunk(x_sample, c_sample, w_ada, b_ada, p)
    return (y_prompt, y_sample)
```

```python
import functools

import numpy as np
import jax
import jax.numpy as jnp
from jax import lax
from jax.experimental import pallas as pl
from jax.experimental.pallas import tpu as pltpu

D_MODEL = 1024
HEAD_DIM = 64
A_HEADS = 16
A_KV_HEADS = 4
A_GROUP = A_HEADS // A_KV_HEADS
GRID_W = 64
ROPE_THETA = 10000.0
B_HEADS = 16
DILATED_CONFIGS = ((128, 1), (512, 4), (2048, 16))
NUM_BUCKETS = 32
MAX_DISTANCE = 1024
N_EXPERTS = 32
TOP_K = 4
SWIGLU_ALPHA = 1.702
SWIGLU_LIMIT = 7.0
RMS_EPS = 1e-6
NEG_INF = -1e30
DEPTH = 2
LOG2E = float(np.log2(np.e))
LN2 = float(np.log(2.0))

LANES = 128
SUBLANES = 8
VMEM_LIMIT = 56 * 1024 * 1024

ROW_TILE = 512
FA_TQ = 256
FA_TK = 512
FA_UNROLL = 8
V_ROWS = HEAD_DIM + 16
DIL_TQ = 512
DIL_SUB = 128
DIL_RADIUS = 64
LSE_LANES = LANES // B_HEADS
ROUTE_TILE = 256
DISPATCH_TILE = 256
EXPERT_BLOCK = 512
ROW_WORDS = D_MODEL // LANES

_F32 = jnp.float32
_BF16 = jnp.bfloat16
_NT = (((1,), (1,)), ((), ()))


def _cparams(sem):
    return pltpu.CompilerParams(dimension_semantics=sem, vmem_limit_bytes=VMEM_LIMIT)


def _rms_mod(x, g, scale, shift):
    var = jnp.mean(x * x, axis=-1, keepdims=True)
    y = x * lax.rsqrt(var + RMS_EPS) * g
    return y * (1.0 + scale) + shift


def _head_rms(y, bd, gvec):
    ss = jnp.dot((y * y).astype(_BF16), bd, preferred_element_type=_F32)
    return y * lax.rsqrt(ss * (1.0 / HEAD_DIM) + RMS_EPS) * gvec


def _ada_kernel(c_ref, w_ref, b_ref, o_ref):
    c = c_ref[...]
    s = c * jax.nn.sigmoid(c)
    o_ref[...] = jnp.dot(s, w_ref[0], preferred_element_type=_F32,
                         precision=lax.Precision.HIGHEST) + b_ref[0]


def _ada_mod(c, w, b, layer):
    bsz = c.shape[0]
    cp = jnp.pad(c, ((0, SUBLANES - bsz), (0, 0)))
    n = w.shape[2]
    out = pl.pallas_call(
        _ada_kernel,
        grid=(n // D_MODEL,),
        in_specs=[pl.BlockSpec((SUBLANES, D_MODEL), lambda j: (0, 0)),
                  pl.BlockSpec((1, D_MODEL, D_MODEL), lambda j: (layer, 0, j)),
                  pl.BlockSpec((1, 1, D_MODEL), lambda j: (layer, 0, j))],
        out_specs=pl.BlockSpec((SUBLANES, D_MODEL), lambda j: (0, j)),
        out_shape=jax.ShapeDtypeStruct((SUBLANES, n), _F32),
        compiler_params=_cparams(("arbitrary",)),
        name="ada_mod",
    )(cp, w, b.reshape(w.shape[0], 1, n))
    return out[:bsz]


def _proj_a_kernel(x_ref, g_ref, sc_ref, sh_ref, wqt_ref, wk_ref, wvt_ref, gq_ref, gk_ref,
                   cos_ref, sina_ref, sinb_ref, cost_ref, sint_ref, bd_ref,
                   qt_ref, k_ref, vt_ref):
    h = _rms_mod(x_ref[0], g_ref[...], sc_ref[0], sh_ref[0]).astype(_BF16)
    acct = lax.dot_general(wqt_ref[...], h, _NT, preferred_element_type=_F32)
    cost = cost_ref[...]
    sint = sint_ref[...]
    gq = gq_ref[...]
    for j in range(A_HEADS):
        slab = acct[HEAD_DIM * j:HEAD_DIM * (j + 1), :]
        ss = jnp.sum(slab * slab, axis=0, keepdims=True)
        y = slab * lax.rsqrt(ss * (1.0 / HEAD_DIM) + RMS_EPS) * gq
        rot = jnp.concatenate([-y[16:32], y[0:16], -y[48:64], y[32:48]], axis=0)
        qt_ref[0, HEAD_DIM * j:HEAD_DIM * (j + 1), :] = (y * cost + rot * sint).astype(_BF16)
    acck = jnp.dot(h, wk_ref[...], preferred_element_type=_F32)
    y = _head_rms(acck, bd_ref[...], gk_ref[...])
    cos = cos_ref[...]
    sina = sina_ref[...]
    sinb = sinb_ref[...]
    for hh in range(2):
        yy = y[:, LANES * hh:LANES * (hh + 1)]
        r = yy * cos + pltpu.roll(yy, LANES - 16, 1) * sina + pltpu.roll(yy, 16, 1) * sinb
        for e in range(2):
            k_ref[0, 2 * hh + e] = r[:, HEAD_DIM * e:HEAD_DIM * (e + 1)].astype(_BF16)
    accv = lax.dot_general(wvt_ref[...], h, _NT, preferred_element_type=_F32)
    ones = jnp.ones((V_ROWS - HEAD_DIM, accv.shape[1]), _BF16)
    for j in range(A_KV_HEADS):
        vt_ref[0, j, 0, 0:HEAD_DIM, :] = accv[HEAD_DIM * j:HEAD_DIM * (j + 1), :].astype(_BF16)
        vt_ref[0, j, 0, HEAD_DIM:V_ROWS, :] = ones


def _rope_tables(seq_len):
    rows = seq_len // GRID_W
    row = jnp.repeat(jnp.arange(rows, dtype=_F32), GRID_W)
    col = jnp.tile(jnp.arange(GRID_W, dtype=_F32), rows)
    n_freq = HEAD_DIM // 4
    inv_freq = 1.0 / (ROPE_THETA ** (jnp.arange(n_freq, dtype=_F32) / n_freq))
    ang_r = row[:, None] * inv_freq[None, :]
    ang_c = col[:, None] * inv_freq[None, :]
    ang = jnp.concatenate([ang_r, ang_r, ang_c, ang_c], axis=-1)
    cos, sin = jnp.cos(ang), jnp.sin(ang)
    first = (np.arange(HEAD_DIM) % 32) < 16
    cos2 = jnp.tile(cos, (1, 2))
    sina = jnp.tile(jnp.where(first[None, :], -sin, 0.0), (1, 2))
    sinb = jnp.tile(jnp.where(first[None, :], 0.0, sin), (1, 2))
    return cos2, sina, sinb, cos.T, sin.T


def _block_diag_ones():
    i = np.arange(256)
    return jnp.asarray((i[:, None] // HEAD_DIM) == (i[None, :] // HEAD_DIM), dtype=_BF16)


def _proj_a(x, g, scale, shift, wqt, wk, wvt, gq_col, gk_t, tables, bd):
    bsz, seq, _ = x.shape
    tm = FA_TK
    assert seq % tm == 0
    cos2, sina, sinb, cost, sint = tables
    nq = A_HEADS * HEAD_DIM
    nk = A_KV_HEADS * HEAD_DIM
    const = lambda b, i: (0, 0)
    return pl.pallas_call(
        _proj_a_kernel,
        grid=(bsz, seq // tm),
        in_specs=[
            pl.BlockSpec((1, tm, D_MODEL), lambda b, i: (b, i, 0)),
            pl.BlockSpec((1, D_MODEL), const),
            pl.BlockSpec((1, 1, D_MODEL), lambda b, i: (b, 0, 0)),
            pl.BlockSpec((1, 1, D_MODEL), lambda b, i: (b, 0, 0)),
            pl.BlockSpec((nq, D_MODEL), const),
            pl.BlockSpec((D_MODEL, nk), const),
            pl.BlockSpec((nk, D_MODEL), const),
            pl.BlockSpec((HEAD_DIM, 1), const),
            pl.BlockSpec((1, nk), const),
            pl.BlockSpec((tm, LANES), lambda b, i: (i, 0)),
            pl.BlockSpec((tm, LANES), lambda b, i: (i, 0)),
            pl.BlockSpec((tm, LANES), lambda b, i: (i, 0)),
            pl.BlockSpec((HEAD_DIM, tm), lambda b, i: (0, i)),
            pl.BlockSpec((HEAD_DIM, tm), lambda b, i: (0, i)),
            pl.BlockSpec((256, 256), const),
        ],
        out_specs=[
            pl.BlockSpec((1, nq, tm), lambda b, i: (b, 0, i)),
            pl.BlockSpec((1, A_KV_HEADS, tm, HEAD_DIM), lambda b, i: (b, 0, i, 0)),
            pl.BlockSpec((1, A_KV_HEADS, 1, V_ROWS, tm), lambda b, i: (b, 0, i, 0, 0)),
        ],
        out_shape=[
            jax.ShapeDtypeStruct((bsz, nq, seq), _BF16),
            jax.ShapeDtypeStruct((bsz, A_KV_HEADS, seq, HEAD_DIM), _BF16),
            jax.ShapeDtypeStruct((bsz, A_KV_HEADS, seq // tm, V_ROWS, tm), _BF16),
        ],
        compiler_params=_cparams(("parallel", "parallel")),
        name="proj_a",
    )(x, g, scale, shift, wqt, wk, wvt, gq_col, gk_t, cos2, sina, sinb, cost, sint, bd)


def _flash_kernel(qt_ref, k_ref, vt_ref, o_ref, s_ref, m_ref, acc_ref, *, tq, tk, nchunks):
    qt4 = jnp.concatenate([qt_ref[0, HEAD_DIM * hq:HEAD_DIM * (hq + 1), :] for hq in range(A_GROUP)],
                          axis=1)
    m_ref[...] = jnp.full(m_ref.shape, -jnp.inf, _F32)
    acc_ref[...] = jnp.zeros(acc_ref.shape, _F32)

    def scores(kc, slot):
        kb = k_ref[0, 0, pl.ds(pl.multiple_of(kc * tk, tk), tk), :]
        s_ref[slot] = jnp.dot(kb, qt4, preferred_element_type=_F32)

    def step(kc, slot, kc_next, slot_next):
        vb = vt_ref[0, 0, kc]
        if kc_next is not None:
            kb = k_ref[0, 0, pl.ds(pl.multiple_of(kc_next * tk, tk), tk), :]
        for hq in range(A_GROUP):
            sl = slice(tq * hq, tq * (hq + 1))
            if kc_next is not None:
                s_ref[slot_next, :, sl] = jnp.dot(kb, qt4[:, sl], preferred_element_type=_F32)
            s = s_ref[slot, :, sl]
            m_prev = m_ref[:, sl]
            m_new = jnp.maximum(m_prev, jnp.max(s, axis=0, keepdims=True))
            alpha = jnp.exp2(m_prev - m_new)
            p = jnp.exp2(s - m_new).astype(_BF16)
            acc_ref[:, sl] = alpha * acc_ref[:, sl] + jnp.dot(vb, p, preferred_element_type=_F32)
            m_ref[:, sl] = m_new

    def run(kc0, last):
        for j in range(FA_UNROLL):
            final = last and j == FA_UNROLL - 1
            step(kc0 + j, j % 2, None if final else kc0 + j + 1, (j + 1) % 2)

    scores(0, 0)

    def body(c, carry):
        run(FA_UNROLL * c, False)
        return carry

    lax.fori_loop(0, nchunks // FA_UNROLL - 1, body, 0)
    run(nchunks - FA_UNROLL, True)
    o = acc_ref[0:HEAD_DIM, :] / acc_ref[HEAD_DIM:HEAD_DIM + 1, :]
    o = jnp.concatenate([o[:, tq * hq:tq * (hq + 1)] for hq in range(A_GROUP)], axis=0)
    o_ref[0] = o.T.astype(_BF16)


def _flash_a(qt, k, vt):
    bsz, nq, seq = qt.shape
    tq = min(FA_TQ, seq)
    tk = vt.shape[-1]
    nchunks = seq // tk
    assert nchunks % FA_UNROLL == 0 and FA_UNROLL % 2 == 0 and seq % tq == 0
    return pl.pallas_call(
        functools.partial(_flash_kernel, tq=tq, tk=tk, nchunks=nchunks),
        grid=(bsz, A_KV_HEADS, seq // tq),
        in_specs=[
            pl.BlockSpec((1, A_GROUP * HEAD_DIM, tq), lambda b, h, i: (b, h, i)),
            pl.BlockSpec((1, 1, seq, HEAD_DIM), lambda b, h, i: (b, h, 0, 0)),
            pl.BlockSpec((1, 1, nchunks, V_ROWS, tk), lambda b, h, i: (b, h, 0, 0, 0)),
        ],
        out_specs=pl.BlockSpec((1, tq, A_GROUP * HEAD_DIM), lambda b, h, i: (b, i, h)),
        out_shape=jax.ShapeDtypeStruct((bsz, seq, nq), _BF16),
        scratch_shapes=[
            pltpu.VMEM((2, tk, A_GROUP * tq), _F32),
            pltpu.VMEM((1, A_GROUP * tq), _F32),
            pltpu.VMEM((V_ROWS, A_GROUP * tq), _F32),
        ],
        compiler_params=_cparams(("parallel", "parallel", "parallel")),
        name="flash_a",
    )(qt, k, vt)


def _oproj_kernel(a_ref, w_ref, x_ref, gate_ref, o_ref):
    y = jnp.dot(a_ref[0], w_ref[...], preferred_element_type=_F32)
    o_ref[0] = x_ref[0] + gate_ref[0] * y


def _oproj(a, w, x, gate):
    bsz, seq, _ = x.shape
    tm = min(ROW_TILE, seq)
    return pl.pallas_call(
        _oproj_kernel,
        grid=(bsz, seq // tm),
        in_specs=[
            pl.BlockSpec((1, tm, D_MODEL), lambda b, i: (b, i, 0)),
            pl.BlockSpec((D_MODEL, D_MODEL), lambda b, i: (0, 0)),
            pl.BlockSpec((1, tm, D_MODEL), lambda b, i: (b, i, 0)),
            pl.BlockSpec((1, 1, D_MODEL), lambda b, i: (b, 0, 0)),
        ],
        out_specs=pl.BlockSpec((1, tm, D_MODEL), lambda b, i: (b, i, 0)),
        out_shape=jax.ShapeDtypeStruct(x.shape, _F32),
        compiler_params=_cparams(("parallel", "parallel")),
        name="oproj",
    )(a, w, x, gate)


def _proj_b_kernel(x_ref, g_ref, sc_ref, sh_ref, w_ref, gv_ref, bd_ref, o_ref, scr_ref, *, dil, tm):
    h = _rms_mod(x_ref[0], g_ref[...], sc_ref[0], sh_ref[0])
    rows = tm // dil
    if dil > 1:
        for c in range(D_MODEL // LANES):
            scr_ref[c] = h[:, LANES * c:LANES * (c + 1)]
        h = jnp.concatenate(
            [jnp.concatenate([scr_ref[c, pl.ds(r, rows, stride=dil), :]
                              for c in range(D_MODEL // LANES)], axis=1)
             for r in range(dil)], axis=0)
    bd = bd_ref[...]
    acc = jnp.dot(h.astype(_BF16), w_ref[...], preferred_element_type=_F32)
    nqk = 2 * B_HEADS * HEAD_DIM

    def emit(col, y):
        for r in range(dil):
            o_ref[0, r, :, col:col + y.shape[1]] = y[rows * r:rows * (r + 1)].astype(_BF16)

    for c in range(nqk // 256):
        sl = slice(256 * c, 256 * (c + 1))
        emit(256 * c, _head_rms(acc[:, sl], bd, gv_ref[:, sl]))
    emit(nqk, acc[:, nqk:])


def _proj_b(x, g, scale, shift, w, gi, gvec, bd, dil):
    bsz, seq, _ = x.shape
    tm = min(ROW_TILE, seq)
    n = 3 * B_HEADS * HEAD_DIM
    const = lambda b, i: (0, 0)
    return pl.pallas_call(
        functools.partial(_proj_b_kernel, dil=dil, tm=tm),
        grid=(bsz, seq // tm),
        in_specs=[
            pl.BlockSpec((1, tm, D_MODEL), lambda b, i: (b, i, 0)),
            pl.BlockSpec((1, D_MODEL), const),
            pl.BlockSpec((1, 1, D_MODEL), lambda b, i: (b, 0, 0)),
            pl.BlockSpec((1, 1, D_MODEL), lambda b, i: (b, 0, 0)),
            pl.BlockSpec((D_MODEL, n), lambda b, i: (0, gi)),
            pl.BlockSpec((1, 2 * B_HEADS * HEAD_DIM), const),
            pl.BlockSpec((256, 256), const),
        ],
        out_specs=pl.BlockSpec((1, dil, tm // dil, n), lambda b, i: (b, 0, i, 0)),
        out_shape=jax.ShapeDtypeStruct((bsz, dil, seq // dil, n), _BF16),
        scratch_shapes=[pltpu.VMEM((D_MODEL // LANES, tm, LANES), _F32)],
        compiler_params=_cparams(("parallel", "parallel")),
        name="proj_b",
    )(x, g, scale, shift, w, gvec, bd)


def _dilated_kernel(main_ref, kp_ref, vp_ref, kn_ref, vn_ref, bias_ref, o_ref, lse_ref,
                    kf_ref, vf_ref, *, tq, length):
    nh = B_HEADS * HEAD_DIM
    l0 = pl.program_id(2) * tq
    kf_ref[0:DIL_RADIUS] = kp_ref[0, 0]
    kf_ref[DIL_RADIUS:DIL_RADIUS + tq] = main_ref[0, 0, :, nh:2 * nh]
    kf_ref[DIL_RADIUS + tq:] = kn_ref[0, 0]
    vf_ref[0:DIL_RADIUS] = vp_ref[0, 0]
    vf_ref[DIL_RADIUS:DIL_RADIUS + tq] = main_ref[0, 0, :, 2 * nh:3 * nh]
    vf_ref[DIL_RADIUS + tq:] = vn_ref[0, 0]
    width = DIL_SUB + 2 * DIL_RADIUS
    lane = lax.broadcasted_iota(jnp.int32, (DIL_SUB, LANES), 1)
    low_half = lane < HEAD_DIM
    lse_head = lane // LSE_LANES
    col = lax.broadcasted_iota(jnp.int32, (DIL_SUB, width), 1)

    def windows(r0, at_edge):
        if at_edge:
            kpos = col + (l0 + r0 - DIL_RADIUS)
            valid = (kpos >= 0) & (kpos < length)
        lse_tile = jnp.zeros((DIL_SUB, LANES), _F32)
        for m in range(B_HEADS // 2):
            sl = slice(LANES * m, LANES * (m + 1))
            qp = main_ref[0, 0, pl.ds(r0, DIL_SUB), sl]
            kw = kf_ref[pl.ds(r0, width), sl]
            vw = vf_ref[pl.ds(r0, width), sl]
            res = []
            for hh in range(2):
                keep = low_half if hh == 0 else jnp.logical_not(low_half)
                qm = jnp.where(keep, qp, jnp.zeros_like(qp))
                s = lax.dot_general(qm, kw, _NT, preferred_element_type=_F32) + bias_ref[2 * m + hh]
                if at_edge:
                    s = jnp.where(valid, s, NEG_INF)
                mx = jnp.max(s, axis=1, keepdims=True)
                p = jnp.exp2(s - mx)
                den = jnp.sum(p, axis=1, keepdims=True)
                pv = jnp.dot(p.astype(_BF16), vw, preferred_element_type=_F32)
                res.append((pv / den, mx * LN2 + jnp.log(den)))
            o_ref[0, 0, pl.ds(r0, DIL_SUB), sl] = jnp.where(low_half, res[0][0],
                                                             res[1][0]).astype(_BF16)
            for hh in range(2):
                lse_tile = jnp.where(lse_head == 2 * m + hh,
                                     jnp.broadcast_to(res[hh][1], (DIL_SUB, LANES)), lse_tile)
        lse_ref[0, 0, pl.ds(r0, DIL_SUB), :] = lse_tile

    def sub_block(u, carry):
        r0 = pl.multiple_of(u * DIL_SUB, DIL_SUB)
        start = l0 + r0
        edge = (start < DIL_RADIUS) | (start + DIL_SUB + DIL_RADIUS > length)

        @pl.when(edge)
        def _():
            windows(r0, True)

        @pl.when(jnp.logical_not(edge))
        def _():
            windows(r0, False)

        return carry

    lax.fori_loop(0, tq // DIL_SUB, sub_block, 0)


def _dilated_attn(qkv, bias):
    bsz, dil, length, n = qkv.shape
    nh = B_HEADS * HEAD_DIM
    tq = min(DIL_TQ, length)
    nb = tq // DIL_RADIUS
    last = length // DIL_RADIUS - 1
    prev_ix = lambda i: jnp.maximum(i * nb - 1, 0)
    next_ix = lambda i: jnp.minimum((i + 1) * nb, last)
    halo = (1, 1, DIL_RADIUS, nh)
    width = DIL_SUB + 2 * DIL_RADIUS
    assert tq % DIL_SUB == 0 and length % tq == 0
    return pl.pallas_call(
        functools.partial(_dilated_kernel, tq=tq, length=length),
        grid=(bsz, dil, length // tq),
        in_specs=[
            pl.BlockSpec((1, 1, tq, n), lambda b, r, i: (b, r, i, 0)),
            pl.BlockSpec(halo, lambda b, r, i: (b, r, prev_ix(i), 1)),
            pl.BlockSpec(halo, lambda b, r, i: (b, r, prev_ix(i), 2)),
            pl.BlockSpec(halo, lambda b, r, i: (b, r, next_ix(i), 1)),
            pl.BlockSpec(halo, lambda b, r, i: (b, r, next_ix(i), 2)),
            pl.BlockSpec((B_HEADS, DIL_SUB, width), lambda b, r, i: (0, 0, 0)),
        ],
        out_specs=[pl.BlockSpec((1, 1, tq, nh), lambda b, r, i: (b, r, i, 0)),
                   pl.BlockSpec((1, 1, tq, LANES), lambda b, r, i: (b, r, i, 0))],
        out_shape=[jax.ShapeDtypeStruct((bsz, dil, length, nh), _BF16),
                   jax.ShapeDtypeStruct((bsz, dil, length, LANES), _F32)],
        scratch_shapes=[pltpu.VMEM((tq + 2 * DIL_RADIUS, nh), _BF16),
                        pltpu.VMEM((tq + 2 * DIL_RADIUS, nh), _BF16)],
        compiler_params=_cparams(("parallel", "parallel", "parallel")),
        name="dilated_attn",
    )(qkv, qkv, qkv, qkv, qkv, bias)


def _t5_bucket(rel):
    nb = NUM_BUCKETS // 2
    max_exact = nb // 2
    ret = (rel > 0).astype(np.int32) * nb
    n = np.abs(rel)
    large = max_exact + (np.log(np.maximum(n, 1) / max_exact) / np.log(MAX_DISTANCE / max_exact)
                         * (nb - max_exact)).astype(np.int32)
    large = np.minimum(large, nb - 1)
    return (ret + np.where(n < max_exact, n, large)).astype(np.int32)


def _dilated_bias(rel_bias_g, dil):
    width = DIL_SUB + 2 * DIL_RADIUS
    rel = np.arange(width)[None, :] - DIL_RADIUS - np.arange(DIL_SUB)[:, None]
    bucket = _t5_bucket(rel * dil)
    band = np.abs(rel) <= DIL_RADIUS
    onehot = (bucket.reshape(-1, 1) == np.arange(NUM_BUCKETS)[None, :]).astype(np.float32)
    bias = jnp.dot(jnp.asarray(onehot), rel_bias_g.astype(_F32), precision=lax.Precision.HIGHEST)
    bias = bias.reshape(DIL_SUB, width, B_HEADS).transpose(2, 0, 1) * LOG2E
    return jnp.where(jnp.asarray(band)[None], bias, NEG_INF)


def _merge_oproj_kernel(o0_ref, l0_ref, o1_ref, l1_ref, o2_ref, l2_ref, ex_ref, w_ref, x_ref,
                        gate_ref, out_ref, so1, sl1, so2, sl2, *, tm):
    d1 = DILATED_CONFIGS[1][1]
    d2 = DILATED_CONFIGS[2][1]

    def to_token_order(src_ref, scr_ref, dil):
        nc = src_ref.shape[3] // LANES
        for r in range(dil):
            for c in range(nc):
                scr_ref[c, pl.ds(r, tm // dil, stride=dil), :] = (
                    src_ref[0, r, :, LANES * c:LANES * (c + 1)].astype(_F32))
        return jnp.concatenate([scr_ref[c] for c in range(nc)], axis=1)

    o1 = to_token_order(o1_ref, so1, d1)
    l1 = to_token_order(l1_ref, sl1, d1)
    o2 = to_token_order(o2_ref, so2, d2)
    l2 = to_token_order(l2_ref, sl2, d2)
    l0 = l0_ref[0, 0]
    mx = jnp.maximum(jnp.maximum(l0, l1), l2)
    e0, e1, e2 = jnp.exp(l0 - mx), jnp.exp(l1 - mx), jnp.exp(l2 - mx)
    inv = 1.0 / (e0 + e1 + e2)
    ex = ex_ref[...]

    def expand(wgt):
        hi = wgt.astype(_BF16)
        lo = (wgt - hi.astype(_F32)).astype(_BF16)
        return jnp.dot(jnp.concatenate([hi, lo], axis=1), ex, preferred_element_type=_F32)

    o = (expand(e0 * inv) * o0_ref[0, 0].astype(_F32) + expand(e1 * inv) * o1
         + expand(e2 * inv) * o2)
    y = jnp.dot(o.astype(_BF16), w_ref[...], preferred_element_type=_F32)
    out_ref[0] = x_ref[0] + gate_ref[0] * y


def _merge_oproj(outs, lses, w, x, gate):
    bsz, seq, _ = x.shape
    tm = min(ROUTE_TILE, seq)
    nh = B_HEADS * HEAD_DIM
    specs = []
    for (_, dil) in DILATED_CONFIGS:
        specs += [pl.BlockSpec((1, dil, tm // dil, nh), lambda b, i: (b, 0, i, 0)),
                  pl.BlockSpec((1, dil, tm // dil, LANES), lambda b, i: (b, 0, i, 0))]
    rows = np.arange(2 * LANES)[:, None] % LANES
    cols = np.arange(nh)[None, :]
    ex = jnp.asarray(rows == LSE_LANES * (cols // HEAD_DIM), dtype=_BF16)
    o_scr = pltpu.VMEM((nh // LANES, tm, LANES), _F32)
    l_scr = pltpu.VMEM((1, tm, LANES), _F32)
    return pl.pallas_call(
        functools.partial(_merge_oproj_kernel, tm=tm),
        grid=(bsz, seq // tm),
        in_specs=specs + [
            pl.BlockSpec((2 * LANES, nh), lambda b, i: (0, 0)),
            pl.BlockSpec((nh, D_MODEL), lambda b, i: (0, 0)),
            pl.BlockSpec((1, tm, D_MODEL), lambda b, i: (b, i, 0)),
            pl.BlockSpec((1, 1, D_MODEL), lambda b, i: (b, 0, 0)),
        ],
        out_specs=pl.BlockSpec((1, tm, D_MODEL), lambda b, i: (b, i, 0)),
        out_shape=jax.ShapeDtypeStruct(x.shape, _F32),
        scratch_shapes=[o_scr, l_scr, o_scr, l_scr],
        compiler_params=_cparams(("parallel", "parallel")),
        name="merge_oproj",
    )(outs[0], lses[0], outs[1], lses[1], outs[2], lses[2], ex, w, x, gate)


def _router_kernel(x_ref, g_ref, sc_ref, sh_ref, wr_ref, br_ref,
                   hf_ref, idx_ref, gate_ref, rank_ref, cnt_ref, carry_ref, tri_ref, *, tm):
    first = (pl.program_id(0) == 0) & (pl.program_id(1) == 0)

    @pl.when(first)
    def _():
        carry_ref[...] = jnp.zeros(carry_ref.shape, _F32)
        r = lax.broadcasted_iota(jnp.int32, (tm, tm), 0)
        c = lax.broadcasted_iota(jnp.int32, (tm, tm), 1)
        tri_ref[...] = (r < c).astype(_BF16)

    hf = _rms_mod(x_ref[0], g_ref[...], sc_ref[0], sh_ref[0])
    for j in range(ROW_WORDS):
        hf_ref[pl.ds(j, tm, stride=ROW_WORDS), :] = hf[:, LANES * j:LANES * (j + 1)]
    logits = lax.dot_general(wr_ref[...], hf, _NT, preferred_element_type=_F32,
                             precision=lax.Precision.HIGHEST) + br_ref[...]
    eio = lax.broadcasted_iota(jnp.int32, logits.shape, 0).astype(_F32)
    cur = logits
    vals, sel = [], []
    for _ in range(TOP_K):
        mx = jnp.max(cur, axis=0, keepdims=True)
        ix = jnp.min(jnp.where(cur == mx, eio, float(N_EXPERTS)), axis=0, keepdims=True)
        vals.append(mx)
        sel.append(ix)
        cur = jnp.where(eio == ix, -jnp.inf, cur)
    ex = [jnp.exp(v - vals[0]) for v in vals]
    den = ex[0] + ex[1] + ex[2] + ex[3]
    member = jnp.zeros(logits.shape, _F32)
    for ix in sel:
        member = member + (eio == ix).astype(_F32)
    before = carry_ref[:, 0:1] + jnp.dot(member.astype(_BF16), tri_ref[...],
                                         preferred_element_type=_F32)
    ranks = [jnp.sum(jnp.where(eio == ix, before, 0.0), axis=0, keepdims=True) for ix in sel]
    idx_ref[...] = jnp.concatenate(sel, axis=0).astype(jnp.int32)
    gate_ref[...] = jnp.concatenate([e / den for e in ex], axis=0)
    rank_ref[...] = jnp.concatenate(ranks, axis=0).astype(jnp.int32)
    carry_ref[...] = carry_ref[...] + jnp.sum(member, axis=1, keepdims=True)
    cnt_ref[...] = carry_ref[...]


def _router(x, g, scale, shift, wr_t, br):
    bsz, seq, _ = x.shape
    tm = min(ROUTE_TILE, seq)
    ns = seq // tm
    tok = bsz * seq
    const = lambda b, i: (0, 0)
    flat = lambda b, i: (0, b * ns + i)
    return pl.pallas_call(
        functools.partial(_router_kernel, tm=tm),
        grid=(bsz, ns),
        in_specs=[
            pl.BlockSpec((1, tm, D_MODEL), lambda b, i: (b, i, 0)),
            pl.BlockSpec((1, D_MODEL), const),
            pl.BlockSpec((1, 1, D_MODEL), lambda b, i: (b, 0, 0)),
            pl.BlockSpec((1, 1, D_MODEL), lambda b, i: (b, 0, 0)),
            pl.BlockSpec((N_EXPERTS, D_MODEL), const),
            pl.BlockSpec((N_EXPERTS, 1), const),
        ],
        out_specs=[
            pl.BlockSpec((tm * ROW_WORDS, LANES), lambda b, i: (b * ns + i, 0)),
            pl.BlockSpec((TOP_K, tm), flat),
            pl.BlockSpec((TOP_K, tm), flat),
            pl.BlockSpec((TOP_K, tm), flat),
            pl.BlockSpec((N_EXPERTS, LANES), const),
        ],
        out_shape=[
            jax.ShapeDtypeStruct((tok * ROW_WORDS, LANES), _F32),
            jax.ShapeDtypeStruct((TOP_K, tok), jnp.int32),
            jax.ShapeDtypeStruct((TOP_K, tok), _F32),
            jax.ShapeDtypeStruct((TOP_K, tok), jnp.int32),
            jax.ShapeDtypeStruct((N_EXPERTS, LANES), _F32),
        ],
        scratch_shapes=[pltpu.VMEM((N_EXPERTS, LANES), _F32), pltpu.VMEM((tm, tm), _BF16)],
        compiler_params=_cparams(("arbitrary", "arbitrary")),
        name="router",
    )(x, g, scale, shift, wr_t, br)


def _row_copy(src_ref, src_row, dst_ref, dst_row, sem):
    return pltpu.make_async_copy(
        src_ref.at[pl.ds(pl.multiple_of(src_row * ROW_WORDS, ROW_WORDS), ROW_WORDS), :],
        dst_ref.at[pl.ds(pl.multiple_of(dst_row * ROW_WORDS, ROW_WORDS), ROW_WORDS), :], sem)


def _dispatch_kernel(pad_start_ref, pad_len_ref, tail_ref, dest_hbm, hf_ref, xs_ref,
                     dest_smem0, dest_smem1, zero_ref, isem, sem, zsem, *, tm, nsteps):
    g = pl.program_id(0)
    n = tm * TOP_K
    dest_smem = (dest_smem0, dest_smem1)

    def index_copy(step, slot):
        off = pl.multiple_of(step * n, n)
        return pltpu.make_async_copy(dest_hbm.at[pl.ds(off, n)], dest_smem[slot], isem.at[slot])

    def zero_fill(e, wait):
        start = pad_start_ref[e]
        length = pad_len_ref[e]
        for bit in reversed(range(EXPERT_BLOCK.bit_length() - 1)):
            size = 1 << bit
            row = start + (length & ~(2 * size - 1))
            cp = pltpu.make_async_copy(
                zero_ref.at[pl.ds(0, size * ROW_WORDS), :],
                xs_ref.at[pl.ds(pl.multiple_of(row * ROW_WORDS, ROW_WORDS), size * ROW_WORDS), :],
                zsem)

            @pl.when((length & size) != 0)
            def _():
                if wait:
                    cp.wait()
                else:
                    cp.start()

    @pl.when(g == 0)
    def _():
        index_copy(0, 0).start()
        zero_ref[...] = jnp.zeros(zero_ref.shape, _F32)

        def fill(e, carry):
            zero_fill(e, False)
            return carry

        lax.fori_loop(0, N_EXPERTS, fill, 0)

        def done(e, carry):
            zero_fill(e, True)
            return carry

        lax.fori_loop(0, N_EXPERTS, done, 0)

        def tail_copy(j):
            row = (tail_ref[0] + j) * EXPERT_BLOCK
            return pltpu.make_async_copy(
                zero_ref,
                xs_ref.at[pl.ds(pl.multiple_of(row * ROW_WORDS, ROW_WORDS),
                                EXPERT_BLOCK * ROW_WORDS), :], zsem)

        def tail_fill(j, carry):
            tail_copy(j).start()
            return carry

        lax.fori_loop(0, tail_ref[1], tail_fill, 0)

        def tail_done(j, carry):
            tail_copy(j).wait()
            return carry

        lax.fori_loop(0, tail_ref[1], tail_done, 0)

    for slot in range(2):
        @pl.when(g % 2 == slot)
        def _():
            @pl.when(g + 1 < nsteps)
            def _():
                index_copy(g + 1, 1 - slot).start()

            index_copy(g, slot).wait()
            idx_ref = dest_smem[slot]

            def issue(r, carry):
                for k in range(TOP_K):
                    _row_copy(hf_ref, r, xs_ref, idx_ref[r * TOP_K + k], sem).start(priority=k % 2)
                return carry

            lax.fori_loop(0, tm, issue, 0, unroll=2)

    for k in range(TOP_K):
        pltpu.make_async_copy(hf_ref, xs_ref.at[pl.ds(0, tm * ROW_WORDS), :], sem).wait()


def _dispatch(pad_start, pad_len, tail, dest_flat, hf, n_rows):
    tok = hf.shape[0] // ROW_WORDS
    tm = min(DISPATCH_TILE, tok)
    assert tok % tm == 0 and (tm * TOP_K) % 1024 == 0
    grid_spec = pltpu.PrefetchScalarGridSpec(
        num_scalar_prefetch=3,
        grid=(tok // tm,),
        in_specs=[
            pl.BlockSpec(memory_space=pl.ANY),
            pl.BlockSpec((tm * ROW_WORDS, LANES), lambda i, ps, pn, tl: (i, 0)),
        ],
        out_specs=pl.BlockSpec(memory_space=pl.ANY),
        scratch_shapes=[pltpu.SMEM((tm * TOP_K,), jnp.int32),
                        pltpu.SMEM((tm * TOP_K,), jnp.int32),
                        pltpu.VMEM((EXPERT_BLOCK * ROW_WORDS, LANES), _F32),
                        pltpu.SemaphoreType.DMA((2,)), pltpu.SemaphoreType.DMA,
                        pltpu.SemaphoreType.DMA],
    )
    return pl.pallas_call(
        functools.partial(_dispatch_kernel, tm=tm, nsteps=tok // tm),
        grid_spec=grid_spec,
        out_shape=jax.ShapeDtypeStruct((n_rows * ROW_WORDS, LANES), _F32),
        compiler_params=_cparams(("arbitrary",)),
        name="moe_dispatch",
    )(pad_start, pad_len, tail, dest_flat, hf)


def _expert_kernel(be_ref, nu_ref, xs_ref, wgu_ref, bgu_ref, wdn_ref, bdn_ref, ys_ref, *, blk):
    del be_ref
    i = pl.program_id(0)
    dff = wdn_ref.shape[2]

    @pl.when(i < nu_ref[0])
    def _():
        x = jnp.concatenate([xs_ref[pl.ds(j, blk, stride=ROW_WORDS), :] for j in range(ROW_WORDS)],
                            axis=1).astype(_BF16)
        gu = jnp.dot(x, wgu_ref[0, 0], preferred_element_type=_F32) + bgu_ref[0, 0]
        g = jnp.minimum(gu[:, :dff], SWIGLU_LIMIT)
        u = jnp.clip(gu[:, dff:], -SWIGLU_LIMIT, SWIGLU_LIMIT)
        act = g * jax.nn.sigmoid(SWIGLU_ALPHA * g)
        mid = ((u + 1.0) * act).astype(_BF16)
        y = jnp.dot(mid, wdn_ref[0, 0], preferred_element_type=_F32) + bdn_ref[0, 0]
        for j in range(ROW_WORDS):
            ys_ref[pl.ds(j, blk, stride=ROW_WORDS), :] = y[:, LANES * j:LANES * (j + 1)]

    @pl.when(i >= nu_ref[0])
    def _():
        ys_ref[...] = jnp.zeros(ys_ref.shape, _F32)


def _experts(blk_e, n_used, xs, wgu, bgu, wdn, bdn, layer):
    blk = EXPERT_BLOCK
    n_blk = xs.shape[0] // (blk * ROW_WORDS)
    dff = wdn.shape[2]
    grid_spec = pltpu.PrefetchScalarGridSpec(
        num_scalar_prefetch=2,
        grid=(n_blk,),
        in_specs=[
            pl.BlockSpec((blk * ROW_WORDS, LANES), lambda i, be, nu: (jnp.minimum(i, nu[0] - 1), 0)),
            pl.BlockSpec((1, 1, D_MODEL, 2 * dff), lambda i, be, nu: (layer, be[i], 0, 0)),
            pl.BlockSpec((1, 1, 1, 2 * dff), lambda i, be, nu: (layer, be[i], 0, 0)),
            pl.BlockSpec((1, 1, dff, D_MODEL), lambda i, be, nu: (layer, be[i], 0, 0)),
            pl.BlockSpec((1, 1, 1, D_MODEL), lambda i, be, nu: (layer, be[i], 0, 0)),
        ],
        out_specs=pl.BlockSpec((blk * ROW_WORDS, LANES), lambda i, be, nu: (i, 0)),
    )
    return pl.pallas_call(
        functools.partial(_expert_kernel, blk=blk),
        grid_spec=grid_spec,
        out_shape=jax.ShapeDtypeStruct(xs.shape, _F32),
        compiler_params=_cparams(("arbitrary",)),
        name="moe_experts",
    )(blk_e, n_used, xs, wgu, bgu, wdn, bdn)


def _combine_kernel(dest_hbm, ys_ref, gate_ref, x_ref, gf_ref, o_ref,
                    dest_smem0, dest_smem1, buf_ref, isem, sem, *, tm, ns, nsteps):
    g = pl.program_id(0) * ns + pl.program_id(1)
    n = tm * TOP_K
    dest_smem = (dest_smem0, dest_smem1)

    def index_copy(step, slot):
        off = pl.multiple_of(step * n, n)
        return pltpu.make_async_copy(dest_hbm.at[pl.ds(off, n)], dest_smem[slot], isem.at[slot])

    def start_gathers(step, slot):
        index_copy(step, slot).wait()
        idx_ref = dest_smem[slot]

        def issue(r, carry):
            for k in range(TOP_K):
                _row_copy(ys_ref, idx_ref[r * TOP_K + k], buf_ref.at[slot], k * tm + r,
                          sem.at[slot]).start(priority=k % 2)
            return carry

        lax.fori_loop(0, tm, issue, 0, unroll=2)

        @pl.when(step + 1 < nsteps)
        def _():
            index_copy(step + 1, 1 - slot).start()

    def finish(slot):
        pltpu.make_async_copy(ys_ref.at[pl.ds(0, TOP_K * tm * ROW_WORDS), :], buf_ref.at[slot],
                              sem.at[slot]).wait()
        gate = gate_ref[...]
        cols = []
        for j in range(ROW_WORDS):
            acc = None
            for k in range(TOP_K):
                slab = buf_ref[slot, pl.ds(k * tm * ROW_WORDS + j, tm, stride=ROW_WORDS), :]
                term = gate[:, k:k + 1] * slab
                acc = term if acc is None else acc + term
            cols.append(acc)
        moe = jnp.concatenate(cols, axis=1)
        o_ref[0] = x_ref[0] + gf_ref[0] * moe

    @pl.when(g == 0)
    def _():
        index_copy(0, 0).start()
        start_gathers(0, 0)

    for slot in range(2):
        @pl.when(g % 2 == slot)
        def _():
            @pl.when(g + 1 < nsteps)
            def _():
                start_gathers(g + 1, 1 - slot)

            finish(slot)


def _combine(dest_flat, gate_tok, ys, x, gate_f):
    bsz, seq, _ = x.shape
    tm = min(ROUTE_TILE, seq)
    ns = seq // tm
    return pl.pallas_call(
        functools.partial(_combine_kernel, tm=tm, ns=ns, nsteps=bsz * ns),
        grid=(bsz, ns),
        in_specs=[
            pl.BlockSpec(memory_space=pl.ANY),
            pl.BlockSpec(memory_space=pl.ANY),
            pl.BlockSpec((tm, TOP_K), lambda b, i: (b * ns + i, 0)),
            pl.BlockSpec((1, tm, D_MODEL), lambda b, i: (b, i, 0)),
            pl.BlockSpec((1, 1, D_MODEL), lambda b, i: (b, 0, 0)),
        ],
        out_specs=pl.BlockSpec((1, tm, D_MODEL), lambda b, i: (b, i, 0)),
        out_shape=jax.ShapeDtypeStruct(x.shape, _F32),
        scratch_shapes=[
            pltpu.SMEM((tm * TOP_K,), jnp.int32),
            pltpu.SMEM((tm * TOP_K,), jnp.int32),
            pltpu.VMEM((2, TOP_K * tm * ROW_WORDS, LANES), _F32),
            pltpu.SemaphoreType.DMA((2,)),
            pltpu.SemaphoreType.DMA((2,)),
        ],
        compiler_params=_cparams(("arbitrary", "arbitrary")),
        name="moe_combine",
    )(dest_flat, ys, gate_tok, x, gate_f)


def _moe(x, g, scale, shift, gate_f, wr_t, br, wgu, bgu, wdn, bdn, layer):
    bsz, seq, _ = x.shape
    tok = bsz * seq
    blk = EXPERT_BLOCK
    hf, idx, gate, rank, cnt = _router(x, g, scale, shift, wr_t, br)
    counts = cnt[:, 0].astype(jnp.int32)
    padded = (counts + blk - 1) // blk * blk
    pend = jnp.cumsum(padded)
    pstart = pend - padded
    eids = jnp.arange(N_EXPERTS, dtype=jnp.int32)[:, None, None]
    base = jnp.sum(jnp.where(idx[None] == eids, pstart[:, None, None], 0), axis=0)
    dest = (base + rank).T.reshape(tok * TOP_K)
    gate_tok = gate.T
    n_blk = (tok * TOP_K + N_EXPERTS * (blk - 1) + blk - 1) // blk
    blk_start = jnp.arange(n_blk, dtype=jnp.int32) * blk
    blk_e = jnp.minimum(jnp.sum(blk_start[:, None] >= pend[None, :], axis=1),
                        N_EXPERTS - 1).astype(jnp.int32)
    n_used = (pend[-1:] // blk).astype(jnp.int32)
    tail = jnp.concatenate([n_used, n_blk - n_used])
    xs = _dispatch(pstart + counts, padded - counts, tail, dest, hf, n_blk * blk)
    ys = _experts(blk_e, n_used, xs, wgu, bgu, wdn, bdn, layer)
    return _combine(dest, gate_tok, ys, x, gate_f)


def _prepare(norm_mix_g, norm_ffn_g, w_qkv_a, q_norm_a, k_norm_a, w_o_a, w_qkv_b, q_norm_b, k_norm_b,
             w_o_b, rel_bias, w_router, b_router, w_gate_up, b_gate_up, w_down, b_down):
    nq = A_HEADS * HEAD_DIM
    nk = A_KV_HEADS * HEAD_DIM
    nh = B_HEADS * HEAD_DIM
    scale = HEAD_DIM ** -0.5
    p = {}
    wa = w_qkv_a[0]
    p["wqt_a"] = wa[:, :nq].T.astype(_BF16)
    p["wk_a"] = wa[:, nq:nq + nk].astype(_BF16)
    p["wvt_a"] = wa[:, nq + nk:].T.astype(_BF16)
    p["gq_a"] = (q_norm_a[0] * (scale * LOG2E)).reshape(HEAD_DIM, 1)
    p["gk_a"] = jnp.tile(k_norm_a[0], A_KV_HEADS).reshape(1, nk)
    p["wo_a"] = w_o_a[0].astype(_BF16)
    wb = w_qkv_b[0].astype(_BF16)
    p["w_b"] = wb
    p["gv_b"] = [jnp.concatenate([jnp.tile(q_norm_b[0, gi], B_HEADS) * (scale * LOG2E),
                                  jnp.tile(k_norm_b[0, gi], B_HEADS)]).reshape(1, 2 * nh)
                 for gi in range(len(DILATED_CONFIGS))]
    p["bias_b"] = [_dilated_bias(rel_bias[:, gi * B_HEADS:(gi + 1) * B_HEADS], dil)
                   for gi, (_, dil) in enumerate(DILATED_CONFIGS)]
    p["wo_b"] = w_o_b[0].astype(_BF16)
    p["g_mix"] = [norm_mix_g[i].reshape(1, D_MODEL) for i in range(DEPTH)]
    p["g_ffn"] = [norm_ffn_g[i].reshape(1, D_MODEL) for i in range(DEPTH)]
    p["wr_t"] = [w_router[i].T for i in range(DEPTH)]
    p["br"] = [b_router[i].reshape(N_EXPERTS, 1) for i in range(DEPTH)]
    p["wgu"] = w_gate_up.astype(_BF16)
    p["bgu"] = b_gate_up.reshape(DEPTH, N_EXPERTS, 1, -1)
    p["wdn"] = w_down.astype(_BF16)
    p["bdn"] = b_down.reshape(DEPTH, N_EXPERTS, 1, -1)
    p["bd"] = _block_diag_ones()
    return p


def _trunk(x, c, w_ada, b_ada, p):
    bsz, seq, _ = x.shape
    tables = _rope_tables(seq)
    for i in range(DEPTH):
        mod = _ada_mod(c, w_ada, b_ada, i)
        shift_m, scale_m, gate_m, shift_f, scale_f, gate_f = [
            mod[:, D_MODEL * k:D_MODEL * (k + 1)].reshape(bsz, 1, D_MODEL) for k in range(6)]
        if i % 2 == 0:
            qt, k, vt = _proj_a(x, p["g_mix"][i], scale_m, shift_m, p["wqt_a"], p["wk_a"],
                                p["wvt_a"], p["gq_a"], p["gk_a"], tables, p["bd"])
            a = _flash_a(qt, k, vt)
            x = _oproj(a, p["wo_a"], x, gate_m)
        else:
            outs, lses = [], []
            for gi, (_, dil) in enumerate(DILATED_CONFIGS):
                qkv = _proj_b(x, p["g_mix"][i], scale_m, shift_m, p["w_b"], gi, p["gv_b"][gi],
                              p["bd"], dil)
                o, lse = _dilated_attn(qkv, p["bias_b"][gi])
                outs.append(o)
                lses.append(lse)
            x = _merge_oproj(outs, lses, p["wo_b"], x, gate_m)
        x = _moe(x, p["g_ffn"][i], scale_f, shift_f, gate_f, p["wr_t"][i], p["br"][i],
                 p["wgu"], p["bgu"], p["wdn"], p["bdn"], i)
    return x


def kernel(x_prompt, x_sample, c_prompt, c_sample, norm_mix_g, norm_ffn_g, w_ada, b_ada, w_qkv_a,
           q_norm_a, k_norm_a, w_o_a, w_qkv_b, q_norm_b, k_norm_b, w_o_b, rel_bias, w_router,
           b_router, w_gate_up, b_gate_up, w_down, b_down):
    p = _prepare(norm_mix_g, norm_ffn_g, w_qkv_a, q_norm_a, k_norm_a, w_o_a, w_qkv_b, q_norm_b,
                 k_norm_b, w_o_b, rel_bias, w_router, b_router, w_gate_up, b_gate_up, w_down, b_down)
    y_prompt = _trunk(x_prompt, c_prompt, w_ada, b_ada, p)
    y_sample = _trunk(x_sample, c_sample, w_ada, b_ada, p)
    return (y_prompt, y_sample)
```

```python
import functools

import numpy as np
import jax
import jax.numpy as jnp
from jax import lax
from jax.experimental import pallas as pl
from jax.experimental.pallas import tpu as pltpu

D_MODEL = 1024
HEAD_DIM = 64
A_HEADS = 16
A_KV_HEADS = 4
A_GROUP = A_HEADS // A_KV_HEADS
GRID_W = 64
ROPE_THETA = 10000.0
B_HEADS = 16
DILATED_CONFIGS = ((128, 1), (512, 4), (2048, 16))
NUM_BUCKETS = 32
MAX_DISTANCE = 1024
N_EXPERTS = 32
TOP_K = 4
SWIGLU_ALPHA = 1.702
SWIGLU_LIMIT = 7.0
RMS_EPS = 1e-6
NEG_INF = -1e30
DEPTH = 2
LOG2E = float(np.log2(np.e))
LN2 = float(np.log(2.0))

LANES = 128
SUBLANES = 8
VMEM_LIMIT = 56 * 1024 * 1024

ROW_TILE = 512
FA_TQ = 256
FA_TK = 512
FA_UNROLL = 8
V_ROWS = HEAD_DIM + 16
DIL_TQ = 512
DIL_SUB = 128
DIL_RADIUS = 64
LSE_LANES = LANES // B_HEADS
ROUTE_TILE = 256
DISPATCH_TILE = 512
EXPERT_BLOCK = 512
ROW_WORDS = D_MODEL // LANES

_F32 = jnp.float32
_BF16 = jnp.bfloat16
_NT = (((1,), (1,)), ((), ()))


def _cparams(sem):
    return pltpu.CompilerParams(dimension_semantics=sem, vmem_limit_bytes=VMEM_LIMIT)


def _rms_mod(x, g, scale, shift):
    var = jnp.mean(x * x, axis=-1, keepdims=True)
    y = x * lax.rsqrt(var + RMS_EPS) * g
    return y * (1.0 + scale) + shift


def _head_rms(y, bd, gvec):
    ss = jnp.dot((y * y).astype(_BF16), bd, preferred_element_type=_F32)
    return y * lax.rsqrt(ss * (1.0 / HEAD_DIM) + RMS_EPS) * gvec


def _ada_kernel(c_ref, w_ref, b_ref, o_ref):
    c = c_ref[...]
    s = c * jax.nn.sigmoid(c)
    o_ref[...] = jnp.dot(s, w_ref[0], preferred_element_type=_F32,
                         precision=lax.Precision.HIGHEST) + b_ref[0]


def _ada_mod(c, w, b, layer):
    bsz = c.shape[0]
    cp = jnp.pad(c, ((0, SUBLANES - bsz), (0, 0)))
    n = w.shape[2]
    out = pl.pallas_call(
        _ada_kernel,
        grid=(n // D_MODEL,),
        in_specs=[pl.BlockSpec((SUBLANES, D_MODEL), lambda j: (0, 0)),
                  pl.BlockSpec((1, D_MODEL, D_MODEL), lambda j: (layer, 0, j)),
                  pl.BlockSpec((1, 1, D_MODEL), lambda j: (layer, 0, j))],
        out_specs=pl.BlockSpec((SUBLANES, D_MODEL), lambda j: (0, j)),
        out_shape=jax.ShapeDtypeStruct((SUBLANES, n), _F32),
        compiler_params=_cparams(("arbitrary",)),
        name="ada_mod",
    )(cp, w, b.reshape(w.shape[0], 1, n))
    return out[:bsz]


def _proj_a_kernel(x_ref, g_ref, sc_ref, sh_ref, wqt_ref, wk_ref, wvt_ref, gq_ref, gk_ref,
                   cos_ref, sina_ref, sinb_ref, cost_ref, sint_ref, bd_ref,
                   qt_ref, k_ref, vt_ref):
    h = _rms_mod(x_ref[0], g_ref[...], sc_ref[0], sh_ref[0]).astype(_BF16)
    acct = lax.dot_general(wqt_ref[...], h, _NT, preferred_element_type=_F32)
    cost = cost_ref[...]
    sint = sint_ref[...]
    gq = gq_ref[...]
    for j in range(A_HEADS):
        slab = acct[HEAD_DIM * j:HEAD_DIM * (j + 1), :]
        ss = jnp.sum(slab * slab, axis=0, keepdims=True)
        y = slab * lax.rsqrt(ss * (1.0 / HEAD_DIM) + RMS_EPS) * gq
        rot = jnp.concatenate([-y[16:32], y[0:16], -y[48:64], y[32:48]], axis=0)
        qt_ref[0, HEAD_DIM * j:HEAD_DIM * (j + 1), :] = (y * cost + rot * sint).astype(_BF16)
    acck = jnp.dot(h, wk_ref[...], preferred_element_type=_F32)
    y = _head_rms(acck, bd_ref[...], gk_ref[...])
    cos = cos_ref[...]
    sina = sina_ref[...]
    sinb = sinb_ref[...]
    for hh in range(2):
        yy = y[:, LANES * hh:LANES * (hh + 1)]
        r = yy * cos + pltpu.roll(yy, LANES - 16, 1) * sina + pltpu.roll(yy, 16, 1) * sinb
        for e in range(2):
            k_ref[0, 2 * hh + e] = r[:, HEAD_DIM * e:HEAD_DIM * (e + 1)].astype(_BF16)
    accv = lax.dot_general(wvt_ref[...], h, _NT, preferred_element_type=_F32)
    ones = jnp.ones((V_ROWS - HEAD_DIM, accv.shape[1]), _BF16)
    for j in range(A_KV_HEADS):
        vt_ref[0, j, 0, 0:HEAD_DIM, :] = accv[HEAD_DIM * j:HEAD_DIM * (j + 1), :].astype(_BF16)
        vt_ref[0, j, 0, HEAD_DIM:V_ROWS, :] = ones


def _rope_tables(seq_len):
    rows = seq_len // GRID_W
    row = jnp.repeat(jnp.arange(rows, dtype=_F32), GRID_W)
    col = jnp.tile(jnp.arange(GRID_W, dtype=_F32), rows)
    n_freq = HEAD_DIM // 4
    inv_freq = 1.0 / (ROPE_THETA ** (jnp.arange(n_freq, dtype=_F32) / n_freq))
    ang_r = row[:, None] * inv_freq[None, :]
    ang_c = col[:, None] * inv_freq[None, :]
    ang = jnp.concatenate([ang_r, ang_r, ang_c, ang_c], axis=-1)
    cos, sin = jnp.cos(ang), jnp.sin(ang)
    first = (np.arange(HEAD_DIM) % 32) < 16
    cos2 = jnp.tile(cos, (1, 2))
    sina = jnp.tile(jnp.where(first[None, :], -sin, 0.0), (1, 2))
    sinb = jnp.tile(jnp.where(first[None, :], 0.0, sin), (1, 2))
    return cos2, sina, sinb, cos.T, sin.T


def _block_diag_ones():
    i = np.arange(256)
    return jnp.asarray((i[:, None] // HEAD_DIM) == (i[None, :] // HEAD_DIM), dtype=_BF16)


def _proj_a(x, g, scale, shift, wqt, wk, wvt, gq_col, gk_t, tables, bd):
    bsz, seq, _ = x.shape
    tm = FA_TK
    assert seq % tm == 0
    cos2, sina, sinb, cost, sint = tables
    nq = A_HEADS * HEAD_DIM
    nk = A_KV_HEADS * HEAD_DIM
    const = lambda b, i: (0, 0)
    return pl.pallas_call(
        _proj_a_kernel,
        grid=(bsz, seq // tm),
        in_specs=[
            pl.BlockSpec((1, tm, D_MODEL), lambda b, i: (b, i, 0)),
            pl.BlockSpec((1, D_MODEL), const),
            pl.BlockSpec((1, 1, D_MODEL), lambda b, i: (b, 0, 0)),
            pl.BlockSpec((1, 1, D_MODEL), lambda b, i: (b, 0, 0)),
            pl.BlockSpec((nq, D_MODEL), const),
            pl.BlockSpec((D_MODEL, nk), const),
            pl.BlockSpec((nk, D_MODEL), const),
            pl.BlockSpec((HEAD_DIM, 1), const),
            pl.BlockSpec((1, nk), const),
            pl.BlockSpec((tm, LANES), lambda b, i: (i, 0)),
            pl.BlockSpec((tm, LANES), lambda b, i: (i, 0)),
            pl.BlockSpec((tm, LANES), lambda b, i: (i, 0)),
            pl.BlockSpec((HEAD_DIM, tm), lambda b, i: (0, i)),
            pl.BlockSpec((HEAD_DIM, tm), lambda b, i: (0, i)),
            pl.BlockSpec((256, 256), const),
        ],
        out_specs=[
            pl.BlockSpec((1, nq, tm), lambda b, i: (b, 0, i)),
            pl.BlockSpec((1, A_KV_HEADS, tm, HEAD_DIM), lambda b, i: (b, 0, i, 0)),
            pl.BlockSpec((1, A_KV_HEADS, 1, V_ROWS, tm), lambda b, i: (b, 0, i, 0, 0)),
        ],
        out_shape=[
            jax.ShapeDtypeStruct((bsz, nq, seq), _BF16),
            jax.ShapeDtypeStruct((bsz, A_KV_HEADS, seq, HEAD_DIM), _BF16),
            jax.ShapeDtypeStruct((bsz, A_KV_HEADS, seq // tm, V_ROWS, tm), _BF16),
        ],
        compiler_params=_cparams(("parallel", "parallel")),
        name="proj_a",
    )(x, g, scale, shift, wqt, wk, wvt, gq_col, gk_t, cos2, sina, sinb, cost, sint, bd)


def _flash_kernel(qt_ref, k_ref, vt_ref, o_ref, s_ref, m_ref, acc_ref, *, tq, tk, nchunks):
    qt4 = jnp.concatenate([qt_ref[0, HEAD_DIM * hq:HEAD_DIM * (hq + 1), :] for hq in range(A_GROUP)],
                          axis=1)
    m_ref[...] = jnp.full(m_ref.shape, -jnp.inf, _F32)
    acc_ref[...] = jnp.zeros(acc_ref.shape, _F32)

    def scores(kc, slot):
        kb = k_ref[0, 0, pl.ds(pl.multiple_of(kc * tk, tk), tk), :]
        s_ref[slot] = jnp.dot(kb, qt4, preferred_element_type=_F32)

    def step(kc, slot, kc_next, slot_next):
        vb = vt_ref[0, 0, kc]
        if kc_next is not None:
            kb = k_ref[0, 0, pl.ds(pl.multiple_of(kc_next * tk, tk), tk), :]
        for hq in range(A_GROUP):
            sl = slice(tq * hq, tq * (hq + 1))
            if kc_next is not None:
                s_ref[slot_next, :, sl] = jnp.dot(kb, qt4[:, sl], preferred_element_type=_F32)
            s = s_ref[slot, :, sl]
            m_prev = m_ref[:, sl]
            m_new = jnp.maximum(m_prev, jnp.max(s, axis=0, keepdims=True))
            alpha = jnp.exp2(m_prev - m_new)
            p = jnp.exp2(s - m_new).astype(_BF16)
            acc_ref[:, sl] = alpha * acc_ref[:, sl] + jnp.dot(vb, p, preferred_element_type=_F32)
            m_ref[:, sl] = m_new

    def run(kc0, last):
        for j in range(FA_UNROLL):
            final = last and j == FA_UNROLL - 1
            step(kc0 + j, j % 2, None if final else kc0 + j + 1, (j + 1) % 2)

    scores(0, 0)

    def body(c, carry):
        run(FA_UNROLL * c, False)
        return carry

    lax.fori_loop(0, nchunks // FA_UNROLL - 1, body, 0)
    run(nchunks - FA_UNROLL, True)
    o = acc_ref[0:HEAD_DIM, :] / acc_ref[HEAD_DIM:HEAD_DIM + 1, :]
    o = jnp.concatenate([o[:, tq * hq:tq * (hq + 1)] for hq in range(A_GROUP)], axis=0)
    o_ref[0] = o.T.astype(_BF16)


def _flash_a(qt, k, vt):
    bsz, nq, seq = qt.shape
    tq = min(FA_TQ, seq)
    tk = vt.shape[-1]
    nchunks = seq // tk
    assert nchunks % FA_UNROLL == 0 and FA_UNROLL % 2 == 0 and seq % tq == 0
    return pl.pallas_call(
        functools.partial(_flash_kernel, tq=tq, tk=tk, nchunks=nchunks),
        grid=(bsz, A_KV_HEADS, seq // tq),
        in_specs=[
            pl.BlockSpec((1, A_GROUP * HEAD_DIM, tq), lambda b, h, i: (b, h, i)),
            pl.BlockSpec((1, 1, seq, HEAD_DIM), lambda b, h, i: (b, h, 0, 0)),
            pl.BlockSpec((1, 1, nchunks, V_ROWS, tk), lambda b, h, i: (b, h, 0, 0, 0)),
        ],
        out_specs=pl.BlockSpec((1, tq, A_GROUP * HEAD_DIM), lambda b, h, i: (b, i, h)),
        out_shape=jax.ShapeDtypeStruct((bsz, seq, nq), _BF16),
        scratch_shapes=[
            pltpu.VMEM((2, tk, A_GROUP * tq), _F32),
            pltpu.VMEM((1, A_GROUP * tq), _F32),
            pltpu.VMEM((V_ROWS, A_GROUP * tq), _F32),
        ],
        compiler_params=_cparams(("parallel", "parallel", "parallel")),
        name="flash_a",
    )(qt, k, vt)


def _oproj_kernel(a_ref, w_ref, x_ref, gate_ref, o_ref):
    y = jnp.dot(a_ref[0], w_ref[...], preferred_element_type=_F32)
    o_ref[0] = x_ref[0] + gate_ref[0] * y


def _oproj(a, w, x, gate):
    bsz, seq, _ = x.shape
    tm = min(ROW_TILE, seq)
    return pl.pallas_call(
        _oproj_kernel,
        grid=(bsz, seq // tm),
        in_specs=[
            pl.BlockSpec((1, tm, D_MODEL), lambda b, i: (b, i, 0)),
            pl.BlockSpec((D_MODEL, D_MODEL), lambda b, i: (0, 0)),
            pl.BlockSpec((1, tm, D_MODEL), lambda b, i: (b, i, 0)),
            pl.BlockSpec((1, 1, D_MODEL), lambda b, i: (b, 0, 0)),
        ],
        out_specs=pl.BlockSpec((1, tm, D_MODEL), lambda b, i: (b, i, 0)),
        out_shape=jax.ShapeDtypeStruct(x.shape, _F32),
        compiler_params=_cparams(("parallel", "parallel")),
        name="oproj",
    )(a, w, x, gate)


def _proj_b_kernel(x_ref, g_ref, sc_ref, sh_ref, w_ref, gv_ref, bd_ref, o_ref, scr_ref, *, dil, tm):
    h = _rms_mod(x_ref[0], g_ref[...], sc_ref[0], sh_ref[0])
    rows = tm // dil
    if dil > 1:
        for c in range(D_MODEL // LANES):
            scr_ref[c] = h[:, LANES * c:LANES * (c + 1)]
        h = jnp.concatenate(
            [jnp.concatenate([scr_ref[c, pl.ds(r, rows, stride=dil), :]
                              for c in range(D_MODEL // LANES)], axis=1)
             for r in range(dil)], axis=0)
    bd = bd_ref[...]
    acc = jnp.dot(h.astype(_BF16), w_ref[...], preferred_element_type=_F32)
    nqk = 2 * B_HEADS * HEAD_DIM

    def emit(col, y):
        for r in range(dil):
            o_ref[0, r, :, col:col + y.shape[1]] = y[rows * r:rows * (r + 1)].astype(_BF16)

    for c in range(nqk // 256):
        sl = slice(256 * c, 256 * (c + 1))
        emit(256 * c, _head_rms(acc[:, sl], bd, gv_ref[:, sl]))
    emit(nqk, acc[:, nqk:])


def _proj_b(x, g, scale, shift, w, gi, gvec, bd, dil):
    bsz, seq, _ = x.shape
    tm = min(ROW_TILE, seq)
    n = 3 * B_HEADS * HEAD_DIM
    const = lambda b, i: (0, 0)
    return pl.pallas_call(
        functools.partial(_proj_b_kernel, dil=dil, tm=tm),
        grid=(bsz, seq // tm),
        in_specs=[
            pl.BlockSpec((1, tm, D_MODEL), lambda b, i: (b, i, 0)),
            pl.BlockSpec((1, D_MODEL), const),
            pl.BlockSpec((1, 1, D_MODEL), lambda b, i: (b, 0, 0)),
            pl.BlockSpec((1, 1, D_MODEL), lambda b, i: (b, 0, 0)),
            pl.BlockSpec((D_MODEL, n), lambda b, i: (0, gi)),
            pl.BlockSpec((1, 2 * B_HEADS * HEAD_DIM), const),
            pl.BlockSpec((256, 256), const),
        ],
        out_specs=pl.BlockSpec((1, dil, tm // dil, n), lambda b, i: (b, 0, i, 0)),
        out_shape=jax.ShapeDtypeStruct((bsz, dil, seq // dil, n), _BF16),
        scratch_shapes=[pltpu.VMEM((D_MODEL // LANES, tm, LANES), _F32)],
        compiler_params=_cparams(("parallel", "parallel")),
        name="proj_b",
    )(x, g, scale, shift, w, gvec, bd)


def _dilated_kernel(main_ref, kp_ref, vp_ref, kn_ref, vn_ref, bias_ref, o_ref, lse_ref,
                    kf_ref, vf_ref, *, tq, length):
    nh = B_HEADS * HEAD_DIM
    l0 = pl.program_id(2) * tq
    kf_ref[0:DIL_RADIUS] = kp_ref[0, 0]
    kf_ref[DIL_RADIUS:DIL_RADIUS + tq] = main_ref[0, 0, :, nh:2 * nh]
    kf_ref[DIL_RADIUS + tq:] = kn_ref[0, 0]
    vf_ref[0:DIL_RADIUS] = vp_ref[0, 0]
    vf_ref[DIL_RADIUS:DIL_RADIUS + tq] = main_ref[0, 0, :, 2 * nh:3 * nh]
    vf_ref[DIL_RADIUS + tq:] = vn_ref[0, 0]
    width = DIL_SUB + 2 * DIL_RADIUS
    lane = lax.broadcasted_iota(jnp.int32, (DIL_SUB, LANES), 1)
    low_half = lane < HEAD_DIM
    lse_head = lane // LSE_LANES
    col = lax.broadcasted_iota(jnp.int32, (DIL_SUB, width), 1)

    def windows(r0, at_edge):
        if at_edge:
            kpos = col + (l0 + r0 - DIL_RADIUS)
            valid = (kpos >= 0) & (kpos < length)
        lse_tile = jnp.zeros((DIL_SUB, LANES), _F32)
        for m in range(B_HEADS // 2):
            sl = slice(LANES * m, LANES * (m + 1))
            qp = main_ref[0, 0, pl.ds(r0, DIL_SUB), sl]
            kw = kf_ref[pl.ds(r0, width), sl]
            vw = vf_ref[pl.ds(r0, width), sl]
            res = []
            for hh in range(2):
                keep = low_half if hh == 0 else jnp.logical_not(low_half)
                qm = jnp.where(keep, qp, jnp.zeros_like(qp))
                s = lax.dot_general(qm, kw, _NT, preferred_element_type=_F32) + bias_ref[2 * m + hh]
                if at_edge:
                    s = jnp.where(valid, s, NEG_INF)
                mx = jnp.max(s, axis=1, keepdims=True)
                p = jnp.exp2(s - mx)
                den = jnp.sum(p, axis=1, keepdims=True)
                pv = jnp.dot(p.astype(_BF16), vw, preferred_element_type=_F32)
                res.append((pv / den, mx * LN2 + jnp.log(den)))
            o_ref[0, 0, pl.ds(r0, DIL_SUB), sl] = jnp.where(low_half, res[0][0],
                                                             res[1][0]).astype(_BF16)
            for hh in range(2):
                lse_tile = jnp.where(lse_head == 2 * m + hh,
                                     jnp.broadcast_to(res[hh][1], (DIL_SUB, LANES)), lse_tile)
        lse_ref[0, 0, pl.ds(r0, DIL_SUB), :] = lse_tile

    def sub_block(u, carry):
        r0 = pl.multiple_of(u * DIL_SUB, DIL_SUB)
        start = l0 + r0
        edge = (start < DIL_RADIUS) | (start + DIL_SUB + DIL_RADIUS > length)

        @pl.when(edge)
        def _():
            windows(r0, True)

        @pl.when(jnp.logical_not(edge))
        def _():
            windows(r0, False)

        return carry

    lax.fori_loop(0, tq // DIL_SUB, sub_block, 0)


def _dilated_attn(qkv, bias):
    bsz, dil, length, n = qkv.shape
    nh = B_HEADS * HEAD_DIM
    tq = min(DIL_TQ, length)
    nb = tq // DIL_RADIUS
    last = length // DIL_RADIUS - 1
    prev_ix = lambda i: jnp.maximum(i * nb - 1, 0)
    next_ix = lambda i: jnp.minimum((i + 1) * nb, last)
    halo = (1, 1, DIL_RADIUS, nh)
    width = DIL_SUB + 2 * DIL_RADIUS
    assert tq % DIL_SUB == 0 and length % tq == 0
    return pl.pallas_call(
        functools.partial(_dilated_kernel, tq=tq, length=length),
        grid=(bsz, dil, length // tq),
        in_specs=[
            pl.BlockSpec((1, 1, tq, n), lambda b, r, i: (b, r, i, 0)),
            pl.BlockSpec(halo, lambda b, r, i: (b, r, prev_ix(i), 1)),
            pl.BlockSpec(halo, lambda b, r, i: (b, r, prev_ix(i), 2)),
            pl.BlockSpec(halo, lambda b, r, i: (b, r, next_ix(i), 1)),
            pl.BlockSpec(halo, lambda b, r, i: (b, r, next_ix(i), 2)),
            pl.BlockSpec((B_HEADS, DIL_SUB, width), lambda b, r, i: (0, 0, 0)),
        ],
        out_specs=[pl.BlockSpec((1, 1, tq, nh), lambda b, r, i: (b, r, i, 0)),
                   pl.BlockSpec((1, 1, tq, LANES), lambda b, r, i: (b, r, i, 0))],
        out_shape=[jax.ShapeDtypeStruct((bsz, dil, length, nh), _BF16),
                   jax.ShapeDtypeStruct((bsz, dil, length, LANES), _F32)],
        scratch_shapes=[pltpu.VMEM((tq + 2 * DIL_RADIUS, nh), _BF16),
                        pltpu.VMEM((tq + 2 * DIL_RADIUS, nh), _BF16)],
        compiler_params=_cparams(("parallel", "parallel", "parallel")),
        name="dilated_attn",
    )(qkv, qkv, qkv, qkv, qkv, bias)


def _t5_bucket(rel):
    nb = NUM_BUCKETS // 2
    max_exact = nb // 2
    ret = (rel > 0).astype(np.int32) * nb
    n = np.abs(rel)
    large = max_exact + (np.log(np.maximum(n, 1) / max_exact) / np.log(MAX_DISTANCE / max_exact)
                         * (nb - max_exact)).astype(np.int32)
    large = np.minimum(large, nb - 1)
    return (ret + np.where(n < max_exact, n, large)).astype(np.int32)


def _dilated_bias(rel_bias_g, dil):
    width = DIL_SUB + 2 * DIL_RADIUS
    rel = np.arange(width)[None, :] - DIL_RADIUS - np.arange(DIL_SUB)[:, None]
    bucket = _t5_bucket(rel * dil)
    band = np.abs(rel) <= DIL_RADIUS
    onehot = (bucket.reshape(-1, 1) == np.arange(NUM_BUCKETS)[None, :]).astype(np.float32)
    bias = jnp.dot(jnp.asarray(onehot), rel_bias_g.astype(_F32), precision=lax.Precision.HIGHEST)
    bias = bias.reshape(DIL_SUB, width, B_HEADS).transpose(2, 0, 1) * LOG2E
    return jnp.where(jnp.asarray(band)[None], bias, NEG_INF)


def _merge_oproj_kernel(o0_ref, l0_ref, o1_ref, l1_ref, o2_ref, l2_ref, ex_ref, w_ref, x_ref,
                        gate_ref, out_ref, so1, sl1, so2, sl2, *, tm):
    d1 = DILATED_CONFIGS[1][1]
    d2 = DILATED_CONFIGS[2][1]

    def to_token_order(src_ref, scr_ref, dil):
        nc = src_ref.shape[3] // LANES
        for r in range(dil):
            for c in range(nc):
                scr_ref[c, pl.ds(r, tm // dil, stride=dil), :] = (
                    src_ref[0, r, :, LANES * c:LANES * (c + 1)].astype(_F32))
        return jnp.concatenate([scr_ref[c] for c in range(nc)], axis=1)

    o1 = to_token_order(o1_ref, so1, d1)
    l1 = to_token_order(l1_ref, sl1, d1)
    o2 = to_token_order(o2_ref, so2, d2)
    l2 = to_token_order(l2_ref, sl2, d2)
    l0 = l0_ref[0, 0]
    mx = jnp.maximum(jnp.maximum(l0, l1), l2)
    e0, e1, e2 = jnp.exp(l0 - mx), jnp.exp(l1 - mx), jnp.exp(l2 - mx)
    inv = 1.0 / (e0 + e1 + e2)
    ex = ex_ref[...]

    def expand(wgt):
        hi = wgt.astype(_BF16)
        lo = (wgt - hi.astype(_F32)).astype(_BF16)
        return jnp.dot(jnp.concatenate([hi, lo], axis=1), ex, preferred_element_type=_F32)

    o = (expand(e0 * inv) * o0_ref[0, 0].astype(_F32) + expand(e1 * inv) * o1
         + expand(e2 * inv) * o2)
    y = jnp.dot(o.astype(_BF16), w_ref[...], preferred_element_type=_F32)
    out_ref[0] = x_ref[0] + gate_ref[0] * y


def _merge_oproj(outs, lses, w, x, gate):
    bsz, seq, _ = x.shape
    tm = min(ROUTE_TILE, seq)
    nh = B_HEADS * HEAD_DIM
    specs = []
    for (_, dil) in DILATED_CONFIGS:
        specs += [pl.BlockSpec((1, dil, tm // dil, nh), lambda b, i: (b, 0, i, 0)),
                  pl.BlockSpec((1, dil, tm // dil, LANES), lambda b, i: (b, 0, i, 0))]
    rows = np.arange(2 * LANES)[:, None] % LANES
    cols = np.arange(nh)[None, :]
    ex = jnp.asarray(rows == LSE_LANES * (cols // HEAD_DIM), dtype=_BF16)
    o_scr = pltpu.VMEM((nh // LANES, tm, LANES), _F32)
    l_scr = pltpu.VMEM((1, tm, LANES), _F32)
    return pl.pallas_call(
        functools.partial(_merge_oproj_kernel, tm=tm),
        grid=(bsz, seq // tm),
        in_specs=specs + [
            pl.BlockSpec((2 * LANES, nh), lambda b, i: (0, 0)),
            pl.BlockSpec((nh, D_MODEL), lambda b, i: (0, 0)),
            pl.BlockSpec((1, tm, D_MODEL), lambda b, i: (b, i, 0)),
            pl.BlockSpec((1, 1, D_MODEL), lambda b, i: (b, 0, 0)),
        ],
        out_specs=pl.BlockSpec((1, tm, D_MODEL), lambda b, i: (b, i, 0)),
        out_shape=jax.ShapeDtypeStruct(x.shape, _F32),
        scratch_shapes=[o_scr, l_scr, o_scr, l_scr],
        compiler_params=_cparams(("parallel", "parallel")),
        name="merge_oproj",
    )(outs[0], lses[0], outs[1], lses[1], outs[2], lses[2], ex, w, x, gate)


def _router_kernel(x_ref, g_ref, sc_ref, sh_ref, wr_ref, br_ref,
                   hf_ref, idx_ref, gate_ref, rank_ref, cnt_ref, carry_ref, tri_ref, *, tm):
    first = (pl.program_id(0) == 0) & (pl.program_id(1) == 0)

    @pl.when(first)
    def _():
        carry_ref[...] = jnp.zeros(carry_ref.shape, _F32)
        r = lax.broadcasted_iota(jnp.int32, (tm, tm), 0)
        c = lax.broadcasted_iota(jnp.int32, (tm, tm), 1)
        tri_ref[...] = (r < c).astype(_BF16)

    hf = _rms_mod(x_ref[0], g_ref[...], sc_ref[0], sh_ref[0])
    for j in range(ROW_WORDS):
        hf_ref[pl.ds(j, tm, stride=ROW_WORDS), :] = hf[:, LANES * j:LANES * (j + 1)]
    logits = lax.dot_general(wr_ref[...], hf, _NT, preferred_element_type=_F32,
                             precision=lax.Precision.HIGHEST) + br_ref[...]
    eio = lax.broadcasted_iota(jnp.int32, logits.shape, 0).astype(_F32)
    cur = logits
    vals, sel = [], []
    for _ in range(TOP_K):
        mx = jnp.max(cur, axis=0, keepdims=True)
        ix = jnp.min(jnp.where(cur == mx, eio, float(N_EXPERTS)), axis=0, keepdims=True)
        vals.append(mx)
        sel.append(ix)
        cur = jnp.where(eio == ix, -jnp.inf, cur)
    ex = [jnp.exp(v - vals[0]) for v in vals]
    den = ex[0] + ex[1] + ex[2] + ex[3]
    member = jnp.zeros(logits.shape, _F32)
    for ix in sel:
        member = member + (eio == ix).astype(_F32)
    before = carry_ref[:, 0:1] + jnp.dot(member.astype(_BF16), tri_ref[...],
                                         preferred_element_type=_F32)
    ranks = [jnp.sum(jnp.where(eio == ix, before, 0.0), axis=0, keepdims=True) for ix in sel]
    idx_ref[...] = jnp.concatenate(sel, axis=0).astype(jnp.int32)
    gate_ref[...] = jnp.concatenate([e / den for e in ex], axis=0)
    rank_ref[...] = jnp.concatenate(ranks, axis=0).astype(jnp.int32)
    carry_ref[...] = carry_ref[...] + jnp.sum(member, axis=1, keepdims=True)
    cnt_ref[...] = carry_ref[...]


def _router(x, g, scale, shift, wr_t, br):
    bsz, seq, _ = x.shape
    tm = min(ROUTE_TILE, seq)
    ns = seq // tm
    tok = bsz * seq
    const = lambda b, i: (0, 0)
    flat = lambda b, i: (0, b * ns + i)
    return pl.pallas_call(
        functools.partial(_router_kernel, tm=tm),
        grid=(bsz, ns),
        in_specs=[
            pl.BlockSpec((1, tm, D_MODEL), lambda b, i: (b, i, 0)),
            pl.BlockSpec((1, D_MODEL), const),
            pl.BlockSpec((1, 1, D_MODEL), lambda b, i: (b, 0, 0)),
            pl.BlockSpec((1, 1, D_MODEL), lambda b, i: (b, 0, 0)),
            pl.BlockSpec((N_EXPERTS, D_MODEL), const),
            pl.BlockSpec((N_EXPERTS, 1), const),
        ],
        out_specs=[
            pl.BlockSpec((tm * ROW_WORDS, LANES), lambda b, i: (b * ns + i, 0)),
            pl.BlockSpec((TOP_K, tm), flat),
            pl.BlockSpec((TOP_K, tm), flat),
            pl.BlockSpec((TOP_K, tm), flat),
            pl.BlockSpec((N_EXPERTS, LANES), const),
        ],
        out_shape=[
            jax.ShapeDtypeStruct((tok * ROW_WORDS, LANES), _F32),
            jax.ShapeDtypeStruct((TOP_K, tok), jnp.int32),
            jax.ShapeDtypeStruct((TOP_K, tok), _F32),
            jax.ShapeDtypeStruct((TOP_K, tok), jnp.int32),
            jax.ShapeDtypeStruct((N_EXPERTS, LANES), _F32),
        ],
        scratch_shapes=[pltpu.VMEM((N_EXPERTS, LANES), _F32), pltpu.VMEM((tm, tm), _BF16)],
        compiler_params=_cparams(("arbitrary", "arbitrary")),
        name="router",
    )(x, g, scale, shift, wr_t, br)


def _row_copy(src_ref, src_row, dst_ref, dst_row, sem):
    return pltpu.make_async_copy(
        src_ref.at[pl.ds(pl.multiple_of(src_row * ROW_WORDS, ROW_WORDS), ROW_WORDS), :],
        dst_ref.at[pl.ds(pl.multiple_of(dst_row * ROW_WORDS, ROW_WORDS), ROW_WORDS), :], sem)


def _dispatch_kernel(pad_start_ref, pad_len_ref, tail_ref, dest_hbm, hf_ref, xs_ref,
                     dest_smem0, dest_smem1, zero_ref, isem, sem, zsem, *, tm, nsteps):
    g = pl.program_id(0)
    n = tm * TOP_K
    dest_smem = (dest_smem0, dest_smem1)

    def index_copy(step, slot):
        off = pl.multiple_of(step * n, n)
        return pltpu.make_async_copy(dest_hbm.at[pl.ds(off, n)], dest_smem[slot], isem.at[slot])

    def zero_fill(e, wait):
        start = pad_start_ref[e]
        length = pad_len_ref[e]
        for bit in reversed(range(EXPERT_BLOCK.bit_length() - 1)):
            size = 1 << bit
            row = start + (length & ~(2 * size - 1))
            cp = pltpu.make_async_copy(
                zero_ref.at[pl.ds(0, size * ROW_WORDS), :],
                xs_ref.at[pl.ds(pl.multiple_of(row * ROW_WORDS, ROW_WORDS), size * ROW_WORDS), :],
                zsem)

            @pl.when((length & size) != 0)
            def _():
                if wait:
                    cp.wait()
                else:
                    cp.start()

    @pl.when(g == 0)
    def _():
        index_copy(0, 0).start()
        zero_ref[...] = jnp.zeros(zero_ref.shape, _F32)

        def fill(e, carry):
            zero_fill(e, False)
            return carry

        lax.fori_loop(0, N_EXPERTS, fill, 0)

        def done(e, carry):
            zero_fill(e, True)
            return carry

        lax.fori_loop(0, N_EXPERTS, done, 0)

        def tail_copy(j):
            row = (tail_ref[0] + j) * EXPERT_BLOCK
            return pltpu.make_async_copy(
                zero_ref,
                xs_ref.at[pl.ds(pl.multiple_of(row * ROW_WORDS, ROW_WORDS),
                                EXPERT_BLOCK * ROW_WORDS), :], zsem)

        def tail_fill(j, carry):
            tail_copy(j).start()
            return carry

        lax.fori_loop(0, tail_ref[1], tail_fill, 0)

        def tail_done(j, carry):
            tail_copy(j).wait()
            return carry

        lax.fori_loop(0, tail_ref[1], tail_done, 0)

    for slot in range(2):
        @pl.when(g % 2 == slot)
        def _():
            @pl.when(g + 1 < nsteps)
            def _():
                index_copy(g + 1, 1 - slot).start()

            index_copy(g, slot).wait()
            idx_ref = dest_smem[slot]

            def issue(r, carry):
                for k in range(TOP_K):
                    _row_copy(hf_ref, r, xs_ref, idx_ref[r * TOP_K + k], sem).start(priority=k % 2)
                return carry

            lax.fori_loop(0, tm, issue, 0, unroll=2)

    for k in range(TOP_K):
        pltpu.make_async_copy(hf_ref, xs_ref.at[pl.ds(0, tm * ROW_WORDS), :], sem).wait()


def _dispatch(pad_start, pad_len, tail, dest_flat, hf, n_rows):
    tok = hf.shape[0] // ROW_WORDS
    tm = min(DISPATCH_TILE, tok)
    assert tok % tm == 0 and (tm * TOP_K) % 1024 == 0
    grid_spec = pltpu.PrefetchScalarGridSpec(
        num_scalar_prefetch=3,
        grid=(tok // tm,),
        in_specs=[
            pl.BlockSpec(memory_space=pl.ANY),
            pl.BlockSpec((tm * ROW_WORDS, LANES), lambda i, ps, pn, tl: (i, 0)),
        ],
        out_specs=pl.BlockSpec(memory_space=pl.ANY),
        scratch_shapes=[pltpu.SMEM((tm * TOP_K,), jnp.int32),
                        pltpu.SMEM((tm * TOP_K,), jnp.int32),
                        pltpu.VMEM((EXPERT_BLOCK * ROW_WORDS, LANES), _F32),
                        pltpu.SemaphoreType.DMA((2,)), pltpu.SemaphoreType.DMA,
                        pltpu.SemaphoreType.DMA],
    )
    return pl.pallas_call(
        functools.partial(_dispatch_kernel, tm=tm, nsteps=tok // tm),
        grid_spec=grid_spec,
        out_shape=jax.ShapeDtypeStruct((n_rows * ROW_WORDS, LANES), _F32),
        compiler_params=_cparams(("arbitrary",)),
        name="moe_dispatch",
    )(pad_start, pad_len, tail, dest_flat, hf)


def _expert_kernel(be_ref, nu_ref, xs_ref, wgu_ref, bgu_ref, wdn_ref, bdn_ref, ys_ref, *, blk):
    del be_ref
    i = pl.program_id(0)
    dff = wdn_ref.shape[2]

    @pl.when(i < nu_ref[0])
    def _():
        x = jnp.concatenate([xs_ref[pl.ds(j, blk, stride=ROW_WORDS), :] for j in range(ROW_WORDS)],
                            axis=1).astype(_BF16)
        gu = jnp.dot(x, wgu_ref[0, 0], preferred_element_type=_F32) + bgu_ref[0, 0]
        g = jnp.minimum(gu[:, :dff], SWIGLU_LIMIT)
        u = jnp.clip(gu[:, dff:], -SWIGLU_LIMIT, SWIGLU_LIMIT)
        act = g * jax.nn.sigmoid(SWIGLU_ALPHA * g)
        mid = ((u + 1.0) * act).astype(_BF16)
        y = jnp.dot(mid, wdn_ref[0, 0], preferred_element_type=_F32) + bdn_ref[0, 0]
        for j in range(ROW_WORDS):
            ys_ref[pl.ds(j, blk, stride=ROW_WORDS), :] = y[:, LANES * j:LANES * (j + 1)]

    @pl.when(i >= nu_ref[0])
    def _():
        ys_ref[...] = jnp.zeros(ys_ref.shape, _F32)


def _experts(blk_e, n_used, xs, wgu, bgu, wdn, bdn, layer):
    blk = EXPERT_BLOCK
    n_blk = xs.shape[0] // (blk * ROW_WORDS)
    dff = wdn.shape[2]
    grid_spec = pltpu.PrefetchScalarGridSpec(
        num_scalar_prefetch=2,
        grid=(n_blk,),
        in_specs=[
            pl.BlockSpec((blk * ROW_WORDS, LANES), lambda i, be, nu: (jnp.minimum(i, nu[0] - 1), 0)),
            pl.BlockSpec((1, 1, D_MODEL, 2 * dff), lambda i, be, nu: (layer, be[i], 0, 0)),
            pl.BlockSpec((1, 1, 1, 2 * dff), lambda i, be, nu: (layer, be[i], 0, 0)),
            pl.BlockSpec((1, 1, dff, D_MODEL), lambda i, be, nu: (layer, be[i], 0, 0)),
            pl.BlockSpec((1, 1, 1, D_MODEL), lambda i, be, nu: (layer, be[i], 0, 0)),
        ],
        out_specs=pl.BlockSpec((blk * ROW_WORDS, LANES), lambda i, be, nu: (i, 0)),
    )
    return pl.pallas_call(
        functools.partial(_expert_kernel, blk=blk),
        grid_spec=grid_spec,
        out_shape=jax.ShapeDtypeStruct(xs.shape, _F32),
        compiler_params=_cparams(("arbitrary",)),
        name="moe_experts",
    )(blk_e, n_used, xs, wgu, bgu, wdn, bdn)


def _combine_kernel(dest_hbm, ys_ref, gate_ref, x_ref, gf_ref, o_ref,
                    dest_smem0, dest_smem1, buf_ref, isem, sem, *, tm, ns, nsteps):
    g = pl.program_id(0) * ns + pl.program_id(1)
    n = tm * TOP_K
    dest_smem = (dest_smem0, dest_smem1)

    def index_copy(step, slot):
        off = pl.multiple_of(step * n, n)
        return pltpu.make_async_copy(dest_hbm.at[pl.ds(off, n)], dest_smem[slot], isem.at[slot])

    def start_gathers(step, slot):
        index_copy(step, slot).wait()
        idx_ref = dest_smem[slot]

        def issue(r, carry):
            for k in range(TOP_K):
                _row_copy(ys_ref, idx_ref[r * TOP_K + k], buf_ref.at[slot], k * tm + r,
                          sem.at[slot]).start(priority=k % 2)
            return carry

        lax.fori_loop(0, tm, issue, 0, unroll=2)

        @pl.when(step + 1 < nsteps)
        def _():
            index_copy(step + 1, 1 - slot).start()

    def finish(slot):
        pltpu.make_async_copy(ys_ref.at[pl.ds(0, TOP_K * tm * ROW_WORDS), :], buf_ref.at[slot],
                              sem.at[slot]).wait()
        gate = gate_ref[...]
        cols = []
        for j in range(ROW_WORDS):
            acc = None
            for k in range(TOP_K):
                slab = buf_ref[slot, pl.ds(k * tm * ROW_WORDS + j, tm, stride=ROW_WORDS), :]
                term = gate[:, k:k + 1] * slab
                acc = term if acc is None else acc + term
            cols.append(acc)
        moe = jnp.concatenate(cols, axis=1)
        o_ref[0] = x_ref[0] + gf_ref[0] * moe

    @pl.when(g == 0)
    def _():
        index_copy(0, 0).start()
        start_gathers(0, 0)

    for slot in range(2):
        @pl.when(g % 2 == slot)
        def _():
            @pl.when(g + 1 < nsteps)
            def _():
                start_gathers(g + 1, 1 - slot)

            finish(slot)


def _combine(dest_flat, gate_tok, ys, x, gate_f):
    bsz, seq, _ = x.shape
    tm = min(ROUTE_TILE, seq)
    ns = seq // tm
    return pl.pallas_call(
        functools.partial(_combine_kernel, tm=tm, ns=ns, nsteps=bsz * ns),
        grid=(bsz, ns),
        in_specs=[
            pl.BlockSpec(memory_space=pl.ANY),
            pl.BlockSpec(memory_space=pl.ANY),
            pl.BlockSpec((tm, TOP_K), lambda b, i: (b * ns + i, 0)),
            pl.BlockSpec((1, tm, D_MODEL), lambda b, i: (b, i, 0)),
            pl.BlockSpec((1, 1, D_MODEL), lambda b, i: (b, 0, 0)),
        ],
        out_specs=pl.BlockSpec((1, tm, D_MODEL), lambda b, i: (b, i, 0)),
        out_shape=jax.ShapeDtypeStruct(x.shape, _F32),
        scratch_shapes=[
            pltpu.SMEM((tm * TOP_K,), jnp.int32),
            pltpu.SMEM((tm * TOP_K,), jnp.int32),
            pltpu.VMEM((2, TOP_K * tm * ROW_WORDS, LANES), _F32),
            pltpu.SemaphoreType.DMA((2,)),
            pltpu.SemaphoreType.DMA((2,)),
        ],
        compiler_params=_cparams(("arbitrary", "arbitrary")),
        name="moe_combine",
    )(dest_flat, ys, gate_tok, x, gate_f)


def _moe(x, g, scale, shift, gate_f, wr_t, br, wgu, bgu, wdn, bdn, layer):
    bsz, seq, _ = x.shape
    tok = bsz * seq
    blk = EXPERT_BLOCK
    hf, idx, gate, rank, cnt = _router(x, g, scale, shift, wr_t, br)
    counts = cnt[:, 0].astype(jnp.int32)
    padded = (counts + blk - 1) // blk * blk
    pend = jnp.cumsum(padded)
    pstart = pend - padded
    eids = jnp.arange(N_EXPERTS, dtype=jnp.int32)[:, None, None]
    base = jnp.sum(jnp.where(idx[None] == eids, pstart[:, None, None], 0), axis=0)
    dest = (base + rank).T.reshape(tok * TOP_K)
    gate_tok = gate.T
    n_blk = (tok * TOP_K + N_EXPERTS * (blk - 1) + blk - 1) // blk
    blk_start = jnp.arange(n_blk, dtype=jnp.int32) * blk
    blk_e = jnp.minimum(jnp.sum(blk_start[:, None] >= pend[None, :], axis=1),
                        N_EXPERTS - 1).astype(jnp.int32)
    n_used = (pend[-1:] // blk).astype(jnp.int32)
    tail = jnp.concatenate([n_used, n_blk - n_used])
    xs = _dispatch(pstart + counts, padded - counts, tail, dest, hf, n_blk * blk)
    ys = _experts(blk_e, n_used, xs, wgu, bgu, wdn, bdn, layer)
    return _combine(dest, gate_tok, ys, x, gate_f)


def _prepare(norm_mix_g, norm_ffn_g, w_qkv_a, q_norm_a, k_norm_a, w_o_a, w_qkv_b, q_norm_b, k_norm_b,
             w_o_b, rel_bias, w_router, b_router, w_gate_up, b_gate_up, w_down, b_down):
    nq = A_HEADS * HEAD_DIM
    nk = A_KV_HEADS * HEAD_DIM
    nh = B_HEADS * HEAD_DIM
    scale = HEAD_DIM ** -0.5
    p = {}
    wa = w_qkv_a[0]
    p["wqt_a"] = wa[:, :nq].T.astype(_BF16)
    p["wk_a"] = wa[:, nq:nq + nk].astype(_BF16)
    p["wvt_a"] = wa[:, nq + nk:].T.astype(_BF16)
    p["gq_a"] = (q_norm_a[0] * (scale * LOG2E)).reshape(HEAD_DIM, 1)
    p["gk_a"] = jnp.tile(k_norm_a[0], A_KV_HEADS).reshape(1, nk)
    p["wo_a"] = w_o_a[0].astype(_BF16)
    wb = w_qkv_b[0].astype(_BF16)
    p["w_b"] = wb
    p["gv_b"] = [jnp.concatenate([jnp.tile(q_norm_b[0, gi], B_HEADS) * (scale * LOG2E),
                                  jnp.tile(k_norm_b[0, gi], B_HEADS)]).reshape(1, 2 * nh)
                 for gi in range(len(DILATED_CONFIGS))]
    p["bias_b"] = [_dilated_bias(rel_bias[:, gi * B_HEADS:(gi + 1) * B_HEADS], dil)
                   for gi, (_, dil) in enumerate(DILATED_CONFIGS)]
    p["wo_b"] = w_o_b[0].astype(_BF16)
    p["g_mix"] = [norm_mix_g[i].reshape(1, D_MODEL) for i in range(DEPTH)]
    p["g_ffn"] = [norm_ffn_g[i].reshape(1, D_MODEL) for i in range(DEPTH)]
    p["wr_t"] = [w_router[i].T for i in range(DEPTH)]
    p["br"] = [b_router[i].reshape(N_EXPERTS, 1) for i in range(DEPTH)]
    p["wgu"] = w_gate_up.astype(_BF16)
    p["bgu"] = b_gate_up.reshape(DEPTH, N_EXPERTS, 1, -1)
    p["wdn"] = w_down.astype(_BF16)
    p["bdn"] = b_down.reshape(DEPTH, N_EXPERTS, 1, -1)
    p["bd"] = _block_diag_ones()
    return p


def _trunk(x, c, w_ada, b_ada, p):
    bsz, seq, _ = x.shape
    tables = _rope_tables(seq)
    for i in range(DEPTH):
        mod = _ada_mod(c, w_ada, b_ada, i)
        shift_m, scale_m, gate_m, shift_f, scale_f, gate_f = [
            mod[:, D_MODEL * k:D_MODEL * (k + 1)].reshape(bsz, 1, D_MODEL) for k in range(6)]
        if i % 2 == 0:
            qt, k, vt = _proj_a(x, p["g_mix"][i], scale_m, shift_m, p["wqt_a"], p["wk_a"],
                                p["wvt_a"], p["gq_a"], p["gk_a"], tables, p["bd"])
            a = _flash_a(qt, k, vt)
            x = _oproj(a, p["wo_a"], x, gate_m)
        else:
            outs, lses = [], []
            for gi, (_, dil) in enumerate(DILATED_CONFIGS):
                qkv = _proj_b(x, p["g_mix"][i], scale_m, shift_m, p["w_b"], gi, p["gv_b"][gi],
                              p["bd"], dil)
                o, lse = _dilated_attn(qkv, p["bias_b"][gi])
                outs.append(o)
                lses.append(lse)
            x = _merge_oproj(outs, lses, p["wo_b"], x, gate_m)
        x = _moe(x, p["g_ffn"][i], scale_f, shift_f, gate_f, p["wr_t"][i], p["br"][i],
                 p["wgu"], p["bgu"], p["wdn"], p["bdn"], i)
    return x


def kernel(x_prompt, x_sample, c_prompt, c_sample, norm_mix_g, norm_ffn_g, w_ada, b_ada, w_qkv_a,
           q_norm_a, k_norm_a, w_o_a, w_qkv_b, q_norm_b, k_norm_b, w_o_b, rel_bias, w_router,
           b_router, w_gate_up, b_gate_up, w_down, b_down):
    p = _prepare(norm_mix_g, norm_ffn_g, w_qkv_a, q_norm_a, k_norm_a, w_o_a, w_qkv_b, q_norm_b,
                 k_norm_b, w_o_b, rel_bias, w_router, b_router, w_gate_up, b_gate_up, w_down, b_down)
    y_prompt = _trunk(x_prompt, c_prompt, w_ada, b_ada, p)
    y_sample = _trunk(x_sample, c_sample, w_ada, b_ada, p)
    return (y_prompt, y_sample)
```

```python
import functools

import numpy as np
import jax
import jax.numpy as jnp
from jax import lax
from jax.experimental import pallas as pl
from jax.experimental.pallas import tpu as pltpu

D_MODEL = 1024
HEAD_DIM = 64
A_HEADS = 16
A_KV_HEADS = 4
A_GROUP = A_HEADS // A_KV_HEADS
GRID_W = 64
ROPE_THETA = 10000.0
B_HEADS = 16
DILATED_CONFIGS = ((128, 1), (512, 4), (2048, 16))
NUM_BUCKETS = 32
MAX_DISTANCE = 1024
N_EXPERTS = 32
TOP_K = 4
SWIGLU_ALPHA = 1.702
SWIGLU_LIMIT = 7.0
RMS_EPS = 1e-6
NEG_INF = -1e30
DEPTH = 2
LOG2E = float(np.log2(np.e))
LN2 = float(np.log(2.0))

LANES = 128
SUBLANES = 8
VMEM_LIMIT = 56 * 1024 * 1024

ROW_TILE = 512
FA_TQ = 256
FA_TK = 512
FA_UNROLL = 8
V_ROWS = HEAD_DIM + 16
DIL_TQ = 512
DIL_SUB = 128
DIL_RADIUS = 64
LSE_LANES = LANES // B_HEADS
ROUTE_TILE = 256
DISPATCH_TILE = 512
EXPERT_BLOCK = 512
ROW_WORDS = D_MODEL // LANES

_F32 = jnp.float32
_BF16 = jnp.bfloat16
_NT = (((1,), (1,)), ((), ()))


def _cparams(sem):
    return pltpu.CompilerParams(dimension_semantics=sem, vmem_limit_bytes=VMEM_LIMIT)


def _rms_mod(x, g, scale, shift):
    var = jnp.mean(x * x, axis=-1, keepdims=True)
    y = x * lax.rsqrt(var + RMS_EPS) * g
    return y * (1.0 + scale) + shift


def _head_rms(y, bd, gvec):
    ss = jnp.dot((y * y).astype(_BF16), bd, preferred_element_type=_F32)
    return y * lax.rsqrt(ss * (1.0 / HEAD_DIM) + RMS_EPS) * gvec


def _ada_kernel(c_ref, w_ref, b_ref, o_ref):
    c = c_ref[...]
    s = c * jax.nn.sigmoid(c)
    o_ref[...] = jnp.dot(s, w_ref[0], preferred_element_type=_F32,
                         precision=lax.Precision.HIGHEST) + b_ref[0]


def _ada_mod(c, w, b, layer):
    bsz = c.shape[0]
    cp = jnp.pad(c, ((0, SUBLANES - bsz), (0, 0)))
    n = w.shape[2]
    out = pl.pallas_call(
        _ada_kernel,
        grid=(n // D_MODEL,),
        in_specs=[pl.BlockSpec((SUBLANES, D_MODEL), lambda j: (0, 0)),
                  pl.BlockSpec((1, D_MODEL, D_MODEL), lambda j: (layer, 0, j)),
                  pl.BlockSpec((1, 1, D_MODEL), lambda j: (layer, 0, j))],
        out_specs=pl.BlockSpec((SUBLANES, D_MODEL), lambda j: (0, j)),
        out_shape=jax.ShapeDtypeStruct((SUBLANES, n), _F32),
        compiler_params=_cparams(("arbitrary",)),
        name="ada_mod",
    )(cp, w, b.reshape(w.shape[0], 1, n))
    return out[:bsz]


def _proj_a_kernel(x_ref, g_ref, sc_ref, sh_ref, wqt_ref, wk_ref, wvt_ref, gq_ref, gk_ref,
                   cos_ref, sina_ref, sinb_ref, cost_ref, sint_ref, bd_ref,
                   qt_ref, k_ref, vt_ref):
    h = _rms_mod(x_ref[0], g_ref[...], sc_ref[0], sh_ref[0]).astype(_BF16)
    acct = lax.dot_general(wqt_ref[...], h, _NT, preferred_element_type=_F32)
    cost = cost_ref[...]
    sint = sint_ref[...]
    gq = gq_ref[...]
    for j in range(A_HEADS):
        slab = acct[HEAD_DIM * j:HEAD_DIM * (j + 1), :]
        ss = jnp.sum(slab * slab, axis=0, keepdims=True)
        y = slab * lax.rsqrt(ss * (1.0 / HEAD_DIM) + RMS_EPS) * gq
        rot = jnp.concatenate([-y[16:32], y[0:16], -y[48:64], y[32:48]], axis=0)
        qt_ref[0, HEAD_DIM * j:HEAD_DIM * (j + 1), :] = (y * cost + rot * sint).astype(_BF16)
    acck = jnp.dot(h, wk_ref[...], preferred_element_type=_F32)
    y = _head_rms(acck, bd_ref[...], gk_ref[...])
    cos = cos_ref[...]
    sina = sina_ref[...]
    sinb = sinb_ref[...]
    for hh in range(2):
        yy = y[:, LANES * hh:LANES * (hh + 1)]
        r = yy * cos + pltpu.roll(yy, LANES - 16, 1) * sina + pltpu.roll(yy, 16, 1) * sinb
        for e in range(2):
            k_ref[0, 2 * hh + e] = r[:, HEAD_DIM * e:HEAD_DIM * (e + 1)].astype(_BF16)
    accv = lax.dot_general(wvt_ref[...], h, _NT, preferred_element_type=_F32)
    ones = jnp.ones((V_ROWS - HEAD_DIM, accv.shape[1]), _BF16)
    for j in range(A_KV_HEADS):
        vt_ref[0, j, 0, 0:HEAD_DIM, :] = accv[HEAD_DIM * j:HEAD_DIM * (j + 1), :].astype(_BF16)
        vt_ref[0, j, 0, HEAD_DIM:V_ROWS, :] = ones


def _rope_tables(seq_len):
    rows = seq_len // GRID_W
    row = jnp.repeat(jnp.arange(rows, dtype=_F32), GRID_W)
    col = jnp.tile(jnp.arange(GRID_W, dtype=_F32), rows)
    n_freq = HEAD_DIM // 4
    inv_freq = 1.0 / (ROPE_THETA ** (jnp.arange(n_freq, dtype=_F32) / n_freq))
    ang_r = row[:, None] * inv_freq[None, :]
    ang_c = col[:, None] * inv_freq[None, :]
    ang = jnp.concatenate([ang_r, ang_r, ang_c, ang_c], axis=-1)
    cos, sin = jnp.cos(ang), jnp.sin(ang)
    first = (np.arange(HEAD_DIM) % 32) < 16
    cos2 = jnp.tile(cos, (1, 2))
    sina = jnp.tile(jnp.where(first[None, :], -sin, 0.0), (1, 2))
    sinb = jnp.tile(jnp.where(first[None, :], 0.0, sin), (1, 2))
    return cos2, sina, sinb, cos.T, sin.T


def _block_diag_ones():
    i = np.arange(256)
    return jnp.asarray((i[:, None] // HEAD_DIM) == (i[None, :] // HEAD_DIM), dtype=_BF16)


def _proj_a(x, g, scale, shift, wqt, wk, wvt, gq_col, gk_t, tables, bd):
    bsz, seq, _ = x.shape
    tm = FA_TK
    assert seq % tm == 0
    cos2, sina, sinb, cost, sint = tables
    nq = A_HEADS * HEAD_DIM
    nk = A_KV_HEADS * HEAD_DIM
    const = lambda b, i: (0, 0)
    return pl.pallas_call(
        _proj_a_kernel,
        grid=(bsz, seq // tm),
        in_specs=[
            pl.BlockSpec((1, tm, D_MODEL), lambda b, i: (b, i, 0)),
            pl.BlockSpec((1, D_MODEL), const),
            pl.BlockSpec((1, 1, D_MODEL), lambda b, i: (b, 0, 0)),
            pl.BlockSpec((1, 1, D_MODEL), lambda b, i: (b, 0, 0)),
            pl.BlockSpec((nq, D_MODEL), const),
            pl.BlockSpec((D_MODEL, nk), const),
            pl.BlockSpec((nk, D_MODEL), const),
            pl.BlockSpec((HEAD_DIM, 1), const),
            pl.BlockSpec((1, nk), const),
            pl.BlockSpec((tm, LANES), lambda b, i: (i, 0)),
            pl.BlockSpec((tm, LANES), lambda b, i: (i, 0)),
            pl.BlockSpec((tm, LANES), lambda b, i: (i, 0)),
            pl.BlockSpec((HEAD_DIM, tm), lambda b, i: (0, i)),
            pl.BlockSpec((HEAD_DIM, tm), lambda b, i: (0, i)),
            pl.BlockSpec((256, 256), const),
        ],
        out_specs=[
            pl.BlockSpec((1, nq, tm), lambda b, i: (b, 0, i)),
            pl.BlockSpec((1, A_KV_HEADS, tm, HEAD_DIM), lambda b, i: (b, 0, i, 0)),
            pl.BlockSpec((1, A_KV_HEADS, 1, V_ROWS, tm), lambda b, i: (b, 0, i, 0, 0)),
        ],
        out_shape=[
            jax.ShapeDtypeStruct((bsz, nq, seq), _BF16),
            jax.ShapeDtypeStruct((bsz, A_KV_HEADS, seq, HEAD_DIM), _BF16),
            jax.ShapeDtypeStruct((bsz, A_KV_HEADS, seq // tm, V_ROWS, tm), _BF16),
        ],
        compiler_params=_cparams(("parallel", "parallel")),
        name="proj_a",
    )(x, g, scale, shift, wqt, wk, wvt, gq_col, gk_t, cos2, sina, sinb, cost, sint, bd)


def _flash_kernel(qt_ref, k_ref, vt_ref, o_ref, s_ref, m_ref, acc_ref, *, tq, tk, nchunks):
    qt4 = jnp.concatenate([qt_ref[0, HEAD_DIM * hq:HEAD_DIM * (hq + 1), :] for hq in range(A_GROUP)],
                          axis=1)
    m_ref[...] = jnp.full(m_ref.shape, -jnp.inf, _F32)
    acc_ref[...] = jnp.zeros(acc_ref.shape, _F32)

    def scores(kc, slot):
        kb = k_ref[0, 0, pl.ds(pl.multiple_of(kc * tk, tk), tk), :]
        s_ref[slot] = jnp.dot(kb, qt4, preferred_element_type=_F32)

    def step(kc, slot, kc_next, slot_next):
        vb = vt_ref[0, 0, kc]
        if kc_next is not None:
            kb = k_ref[0, 0, pl.ds(pl.multiple_of(kc_next * tk, tk), tk), :]
        for hq in range(A_GROUP):
            sl = slice(tq * hq, tq * (hq + 1))
            if kc_next is not None:
                s_ref[slot_next, :, sl] = jnp.dot(kb, qt4[:, sl], preferred_element_type=_F32)
            s = s_ref[slot, :, sl]
            m_prev = m_ref[:, sl]
            m_new = jnp.maximum(m_prev, jnp.max(s, axis=0, keepdims=True))
            alpha = jnp.exp2(m_prev - m_new)
            p = jnp.exp2(s - m_new).astype(_BF16)
            acc_ref[:, sl] = alpha * acc_ref[:, sl] + jnp.dot(vb, p, preferred_element_type=_F32)
            m_ref[:, sl] = m_new

    def run(kc0, last):
        for j in range(FA_UNROLL):
            final = last and j == FA_UNROLL - 1
            step(kc0 + j, j % 2, None if final else kc0 + j + 1, (j + 1) % 2)

    scores(0, 0)

    def body(c, carry):
        run(FA_UNROLL * c, False)
        return carry

    lax.fori_loop(0, nchunks // FA_UNROLL - 1, body, 0)
    run(nchunks - FA_UNROLL, True)
    o = acc_ref[0:HEAD_DIM, :] / acc_ref[HEAD_DIM:HEAD_DIM + 1, :]
    o = jnp.concatenate([o[:, tq * hq:tq * (hq + 1)] for hq in range(A_GROUP)], axis=0)
    o_ref[0] = o.T.astype(_BF16)


def _flash_a(qt, k, vt):
    bsz, nq, seq = qt.shape
    tq = min(FA_TQ, seq)
    tk = vt.shape[-1]
    nchunks = seq // tk
    assert nchunks % FA_UNROLL == 0 and FA_UNROLL % 2 == 0 and seq % tq == 0
    return pl.pallas_call(
        functools.partial(_flash_kernel, tq=tq, tk=tk, nchunks=nchunks),
        grid=(bsz, A_KV_HEADS, seq // tq),
        in_specs=[
            pl.BlockSpec((1, A_GROUP * HEAD_DIM, tq), lambda b, h, i: (b, h, i)),
            pl.BlockSpec((1, 1, seq, HEAD_DIM), lambda b, h, i: (b, h, 0, 0)),
            pl.BlockSpec((1, 1, nchunks, V_ROWS, tk), lambda b, h, i: (b, h, 0, 0, 0)),
        ],
        out_specs=pl.BlockSpec((1, tq, A_GROUP * HEAD_DIM), lambda b, h, i: (b, i, h)),
        out_shape=jax.ShapeDtypeStruct((bsz, seq, nq), _BF16),
        scratch_shapes=[
            pltpu.VMEM((2, tk, A_GROUP * tq), _F32),
            pltpu.VMEM((1, A_GROUP * tq), _F32),
            pltpu.VMEM((V_ROWS, A_GROUP * tq), _F32),
        ],
        compiler_params=_cparams(("parallel", "parallel", "parallel")),
        name="flash_a",
    )(qt, k, vt)


def _oproj_kernel(a_ref, w_ref, x_ref, gate_ref, o_ref):
    y = jnp.dot(a_ref[0], w_ref[...], preferred_element_type=_F32)
    o_ref[0] = x_ref[0] + gate_ref[0] * y


def _oproj(a, w, x, gate):
    bsz, seq, _ = x.shape
    tm = min(ROW_TILE, seq)
    return pl.pallas_call(
        _oproj_kernel,
        grid=(bsz, seq // tm),
        in_specs=[
            pl.BlockSpec((1, tm, D_MODEL), lambda b, i: (b, i, 0)),
            pl.BlockSpec((D_MODEL, D_MODEL), lambda b, i: (0, 0)),
            pl.BlockSpec((1, tm, D_MODEL), lambda b, i: (b, i, 0)),
            pl.BlockSpec((1, 1, D_MODEL), lambda b, i: (b, 0, 0)),
        ],
        out_specs=pl.BlockSpec((1, tm, D_MODEL), lambda b, i: (b, i, 0)),
        out_shape=jax.ShapeDtypeStruct(x.shape, _F32),
        compiler_params=_cparams(("parallel", "parallel")),
        name="oproj",
    )(a, w, x, gate)


def _proj_b_kernel(x_ref, g_ref, sc_ref, sh_ref, w_ref, gv_ref, bd_ref, o_ref, scr_ref, *, dil, tm):
    h = _rms_mod(x_ref[0], g_ref[...], sc_ref[0], sh_ref[0])
    rows = tm // dil
    if dil > 1:
        for c in range(D_MODEL // LANES):
            scr_ref[c] = h[:, LANES * c:LANES * (c + 1)]
        h = jnp.concatenate(
            [jnp.concatenate([scr_ref[c, pl.ds(r, rows, stride=dil), :]
                              for c in range(D_MODEL // LANES)], axis=1)
             for r in range(dil)], axis=0)
    bd = bd_ref[...]
    acc = jnp.dot(h.astype(_BF16), w_ref[...], preferred_element_type=_F32)
    nqk = 2 * B_HEADS * HEAD_DIM

    def emit(col, y):
        for r in range(dil):
            o_ref[0, r, :, col:col + y.shape[1]] = y[rows * r:rows * (r + 1)].astype(_BF16)

    for c in range(nqk // 256):
        sl = slice(256 * c, 256 * (c + 1))
        emit(256 * c, _head_rms(acc[:, sl], bd, gv_ref[:, sl]))
    emit(nqk, acc[:, nqk:])


def _proj_b(x, g, scale, shift, w, gi, gvec, bd, dil):
    bsz, seq, _ = x.shape
    tm = min(ROW_TILE, seq)
    n = 3 * B_HEADS * HEAD_DIM
    const = lambda b, i: (0, 0)
    return pl.pallas_call(
        functools.partial(_proj_b_kernel, dil=dil, tm=tm),
        grid=(bsz, seq // tm),
        in_specs=[
            pl.BlockSpec((1, tm, D_MODEL), lambda b, i: (b, i, 0)),
            pl.BlockSpec((1, D_MODEL), const),
            pl.BlockSpec((1, 1, D_MODEL), lambda b, i: (b, 0, 0)),
            pl.BlockSpec((1, 1, D_MODEL), lambda b, i: (b, 0, 0)),
            pl.BlockSpec((D_MODEL, n), lambda b, i: (0, gi)),
            pl.BlockSpec((1, 2 * B_HEADS * HEAD_DIM), const),
            pl.BlockSpec((256, 256), const),
        ],
        out_specs=pl.BlockSpec((1, dil, tm // dil, n), lambda b, i: (b, 0, i, 0)),
        out_shape=jax.ShapeDtypeStruct((bsz, dil, seq // dil, n), _BF16),
        scratch_shapes=[pltpu.VMEM((D_MODEL // LANES, tm, LANES), _F32)],
        compiler_params=_cparams(("parallel", "parallel")),
        name="proj_b",
    )(x, g, scale, shift, w, gvec, bd)


def _dilated_kernel(main_ref, kp_ref, vp_ref, kn_ref, vn_ref, bias_ref, o_ref, lse_ref,
                    kf_ref, vf_ref, *, tq, length):
    nh = B_HEADS * HEAD_DIM
    l0 = pl.program_id(2) * tq
    kf_ref[0:DIL_RADIUS] = kp_ref[0, 0]
    kf_ref[DIL_RADIUS:DIL_RADIUS + tq] = main_ref[0, 0, :, nh:2 * nh]
    kf_ref[DIL_RADIUS + tq:] = kn_ref[0, 0]
    vf_ref[0:DIL_RADIUS] = vp_ref[0, 0]
    vf_ref[DIL_RADIUS:DIL_RADIUS + tq] = main_ref[0, 0, :, 2 * nh:3 * nh]
    vf_ref[DIL_RADIUS + tq:] = vn_ref[0, 0]
    width = DIL_SUB + 2 * DIL_RADIUS
    lane = lax.broadcasted_iota(jnp.int32, (DIL_SUB, LANES), 1)
    low_half = lane < HEAD_DIM
    lse_head = lane // LSE_LANES
    col = lax.broadcasted_iota(jnp.int32, (DIL_SUB, width), 1)

    def windows(r0, at_edge):
        if at_edge:
            kpos = col + (l0 + r0 - DIL_RADIUS)
            valid = (kpos >= 0) & (kpos < length)
        lse_tile = jnp.zeros((DIL_SUB, LANES), _F32)
        for m in range(B_HEADS // 2):
            sl = slice(LANES * m, LANES * (m + 1))
            qp = main_ref[0, 0, pl.ds(r0, DIL_SUB), sl]
            kw = kf_ref[pl.ds(r0, width), sl]
            vw = vf_ref[pl.ds(r0, width), sl]
            res = []
            for hh in range(2):
                keep = low_half if hh == 0 else jnp.logical_not(low_half)
                qm = jnp.where(keep, qp, jnp.zeros_like(qp))
                s = lax.dot_general(qm, kw, _NT, preferred_element_type=_F32) + bias_ref[2 * m + hh]
                if at_edge:
                    s = jnp.where(valid, s, NEG_INF)
                mx = jnp.max(s, axis=1, keepdims=True)
                p = jnp.exp2(s - mx)
                den = jnp.sum(p, axis=1, keepdims=True)
                pv = jnp.dot(p.astype(_BF16), vw, preferred_element_type=_F32)
                res.append((pv / den, mx * LN2 + jnp.log(den)))
            o_ref[0, 0, pl.ds(r0, DIL_SUB), sl] = jnp.where(low_half, res[0][0],
                                                             res[1][0]).astype(_BF16)
            for hh in range(2):
                lse_tile = jnp.where(lse_head == 2 * m + hh,
                                     jnp.broadcast_to(res[hh][1], (DIL_SUB, LANES)), lse_tile)
        lse_ref[0, 0, pl.ds(r0, DIL_SUB), :] = lse_tile

    def sub_block(u, carry):
        r0 = pl.multiple_of(u * DIL_SUB, DIL_SUB)
        start = l0 + r0
        edge = (start < DIL_RADIUS) | (start + DIL_SUB + DIL_RADIUS > length)

        @pl.when(edge)
        def _():
            windows(r0, True)

        @pl.when(jnp.logical_not(edge))
        def _():
            windows(r0, False)

        return carry

    lax.fori_loop(0, tq // DIL_SUB, sub_block, 0)


def _dilated_attn(qkv, bias):
    bsz, dil, length, n = qkv.shape
    nh = B_HEADS * HEAD_DIM
    tq = min(DIL_TQ, length)
    nb = tq // DIL_RADIUS
    last = length // DIL_RADIUS - 1
    prev_ix = lambda i: jnp.maximum(i * nb - 1, 0)
    next_ix = lambda i: jnp.minimum((i + 1) * nb, last)
    halo = (1, 1, DIL_RADIUS, nh)
    width = DIL_SUB + 2 * DIL_RADIUS
    assert tq % DIL_SUB == 0 and length % tq == 0
    return pl.pallas_call(
        functools.partial(_dilated_kernel, tq=tq, length=length),
        grid=(bsz, dil, length // tq),
        in_specs=[
            pl.BlockSpec((1, 1, tq, n), lambda b, r, i: (b, r, i, 0)),
            pl.BlockSpec(halo, lambda b, r, i: (b, r, prev_ix(i), 1)),
            pl.BlockSpec(halo, lambda b, r, i: (b, r, prev_ix(i), 2)),
            pl.BlockSpec(halo, lambda b, r, i: (b, r, next_ix(i), 1)),
            pl.BlockSpec(halo, lambda b, r, i: (b, r, next_ix(i), 2)),
            pl.BlockSpec((B_HEADS, DIL_SUB, width), lambda b, r, i: (0, 0, 0)),
        ],
        out_specs=[pl.BlockSpec((1, 1, tq, nh), lambda b, r, i: (b, r, i, 0)),
                   pl.BlockSpec((1, 1, tq, LANES), lambda b, r, i: (b, r, i, 0))],
        out_shape=[jax.ShapeDtypeStruct((bsz, dil, length, nh), _BF16),
                   jax.ShapeDtypeStruct((bsz, dil, length, LANES), _F32)],
        scratch_shapes=[pltpu.VMEM((tq + 2 * DIL_RADIUS, nh), _BF16),
                        pltpu.VMEM((tq + 2 * DIL_RADIUS, nh), _BF16)],
        compiler_params=_cparams(("parallel", "parallel", "parallel")),
        name="dilated_attn",
    )(qkv, qkv, qkv, qkv, qkv, bias)


def _t5_bucket(rel):
    nb = NUM_BUCKETS // 2
    max_exact = nb // 2
    ret = (rel > 0).astype(np.int32) * nb
    n = np.abs(rel)
    large = max_exact + (np.log(np.maximum(n, 1) / max_exact) / np.log(MAX_DISTANCE / max_exact)
                         * (nb - max_exact)).astype(np.int32)
    large = np.minimum(large, nb - 1)
    return (ret + np.where(n < max_exact, n, large)).astype(np.int32)


def _dilated_bias(rel_bias_g, dil):
    width = DIL_SUB + 2 * DIL_RADIUS
    rel = np.arange(width)[None, :] - DIL_RADIUS - np.arange(DIL_SUB)[:, None]
    bucket = _t5_bucket(rel * dil)
    band = np.abs(rel) <= DIL_RADIUS
    onehot = (bucket.reshape(-1, 1) == np.arange(NUM_BUCKETS)[None, :]).astype(np.float32)
    bias = jnp.dot(jnp.asarray(onehot), rel_bias_g.astype(_F32), precision=lax.Precision.HIGHEST)
    bias = bias.reshape(DIL_SUB, width, B_HEADS).transpose(2, 0, 1) * LOG2E
    return jnp.where(jnp.asarray(band)[None], bias, NEG_INF)


def _merge_oproj_kernel(o0_ref, l0_ref, o1_ref, l1_ref, o2_ref, l2_ref, ex_ref, w_ref, x_ref,
                        gate_ref, out_ref, so1, sl1, so2, sl2, *, tm):
    d1 = DILATED_CONFIGS[1][1]
    d2 = DILATED_CONFIGS[2][1]

    def to_token_order(src_ref, scr_ref, dil):
        nc = src_ref.shape[3] // LANES
        for r in range(dil):
            for c in range(nc):
                scr_ref[c, pl.ds(r, tm // dil, stride=dil), :] = (
                    src_ref[0, r, :, LANES * c:LANES * (c + 1)].astype(_F32))
        return jnp.concatenate([scr_ref[c] for c in range(nc)], axis=1)

    o1 = to_token_order(o1_ref, so1, d1)
    l1 = to_token_order(l1_ref, sl1, d1)
    o2 = to_token_order(o2_ref, so2, d2)
    l2 = to_token_order(l2_ref, sl2, d2)
    l0 = l0_ref[0, 0]
    mx = jnp.maximum(jnp.maximum(l0, l1), l2)
    e0, e1, e2 = jnp.exp(l0 - mx), jnp.exp(l1 - mx), jnp.exp(l2 - mx)
    inv = 1.0 / (e0 + e1 + e2)
    ex = ex_ref[...]

    def expand(wgt):
        hi = wgt.astype(_BF16)
        lo = (wgt - hi.astype(_F32)).astype(_BF16)
        return jnp.dot(jnp.concatenate([hi, lo], axis=1), ex, preferred_element_type=_F32)

    o = (expand(e0 * inv) * o0_ref[0, 0].astype(_F32) + expand(e1 * inv) * o1
         + expand(e2 * inv) * o2)
    y = jnp.dot(o.astype(_BF16), w_ref[...], preferred_element_type=_F32)
    out_ref[0] = x_ref[0] + gate_ref[0] * y


def _merge_oproj(outs, lses, w, x, gate):
    bsz, seq, _ = x.shape
    tm = min(ROUTE_TILE, seq)
    nh = B_HEADS * HEAD_DIM
    specs = []
    for (_, dil) in DILATED_CONFIGS:
        specs += [pl.BlockSpec((1, dil, tm // dil, nh), lambda b, i: (b, 0, i, 0)),
                  pl.BlockSpec((1, dil, tm // dil, LANES), lambda b, i: (b, 0, i, 0))]
    rows = np.arange(2 * LANES)[:, None] % LANES
    cols = np.arange(nh)[None, :]
    ex = jnp.asarray(rows == LSE_LANES * (cols // HEAD_DIM), dtype=_BF16)
    o_scr = pltpu.VMEM((nh // LANES, tm, LANES), _F32)
    l_scr = pltpu.VMEM((1, tm, LANES), _F32)
    return pl.pallas_call(
        functools.partial(_merge_oproj_kernel, tm=tm),
        grid=(bsz, seq // tm),
        in_specs=specs + [
            pl.BlockSpec((2 * LANES, nh), lambda b, i: (0, 0)),
            pl.BlockSpec((nh, D_MODEL), lambda b, i: (0, 0)),
            pl.BlockSpec((1, tm, D_MODEL), lambda b, i: (b, i, 0)),
            pl.BlockSpec((1, 1, D_MODEL), lambda b, i: (b, 0, 0)),
        ],
        out_specs=pl.BlockSpec((1, tm, D_MODEL), lambda b, i: (b, i, 0)),
        out_shape=jax.ShapeDtypeStruct(x.shape, _F32),
        scratch_shapes=[o_scr, l_scr, o_scr, l_scr],
        compiler_params=_cparams(("parallel", "parallel")),
        name="merge_oproj",
    )(outs[0], lses[0], outs[1], lses[1], outs[2], lses[2], ex, w, x, gate)


def _router_kernel(x_ref, g_ref, sc_ref, sh_ref, wr_ref, br_ref,
                   hf_ref, idx_ref, gate_ref, rank_ref, cnt_ref, carry_ref, tri_ref, *, tm):
    first = (pl.program_id(0) == 0) & (pl.program_id(1) == 0)

    @pl.when(first)
    def _():
        carry_ref[...] = jnp.zeros(carry_ref.shape, _F32)
        r = lax.broadcasted_iota(jnp.int32, (tm, tm), 0)
        c = lax.broadcasted_iota(jnp.int32, (tm, tm), 1)
        tri_ref[...] = (r < c).astype(_BF16)

    hf = _rms_mod(x_ref[0], g_ref[...], sc_ref[0], sh_ref[0])
    for j in range(ROW_WORDS):
        hf_ref[pl.ds(j, tm, stride=ROW_WORDS), :] = hf[:, LANES * j:LANES * (j + 1)]
    logits = lax.dot_general(wr_ref[...], hf, _NT, preferred_element_type=_F32,
                             precision=lax.Precision.HIGHEST) + br_ref[...]
    eio = lax.broadcasted_iota(jnp.int32, logits.shape, 0).astype(_F32)
    cur = logits
    vals, sel = [], []
    for _ in range(TOP_K):
        mx = jnp.max(cur, axis=0, keepdims=True)
        ix = jnp.min(jnp.where(cur == mx, eio, float(N_EXPERTS)), axis=0, keepdims=True)
        vals.append(mx)
        sel.append(ix)
        cur = jnp.where(eio == ix, -jnp.inf, cur)
    ex = [jnp.exp(v - vals[0]) for v in vals]
    den = ex[0] + ex[1] + ex[2] + ex[3]
    member = jnp.zeros(logits.shape, _F32)
    for ix in sel:
        member = member + (eio == ix).astype(_F32)
    before = carry_ref[:, 0:1] + jnp.dot(member.astype(_BF16), tri_ref[...],
                                         preferred_element_type=_F32)
    ranks = [jnp.sum(jnp.where(eio == ix, before, 0.0), axis=0, keepdims=True) for ix in sel]
    idx_ref[...] = jnp.concatenate(sel, axis=0).astype(jnp.int32)
    gate_ref[...] = jnp.concatenate([e / den for e in ex], axis=0)
    rank_ref[...] = jnp.concatenate(ranks, axis=0).astype(jnp.int32)
    carry_ref[...] = carry_ref[...] + jnp.sum(member, axis=1, keepdims=True)
    cnt_ref[...] = carry_ref[...]


def _router(x, g, scale, shift, wr_t, br):
    bsz, seq, _ = x.shape
    tm = min(ROUTE_TILE, seq)
    ns = seq // tm
    tok = bsz * seq
    const = lambda b, i: (0, 0)
    flat = lambda b, i: (0, b * ns + i)
    return pl.pallas_call(
        functools.partial(_router_kernel, tm=tm),
        grid=(bsz, ns),
        in_specs=[
            pl.BlockSpec((1, tm, D_MODEL), lambda b, i: (b, i, 0)),
            pl.BlockSpec((1, D_MODEL), const),
            pl.BlockSpec((1, 1, D_MODEL), lambda b, i: (b, 0, 0)),
            pl.BlockSpec((1, 1, D_MODEL), lambda b, i: (b, 0, 0)),
            pl.BlockSpec((N_EXPERTS, D_MODEL), const),
            pl.BlockSpec((N_EXPERTS, 1), const),
        ],
        out_specs=[
            pl.BlockSpec((tm * ROW_WORDS, LANES), lambda b, i: (b * ns + i, 0)),
            pl.BlockSpec((TOP_K, tm), flat),
            pl.BlockSpec((TOP_K, tm), flat),
            pl.BlockSpec((TOP_K, tm), flat),
            pl.BlockSpec((N_EXPERTS, LANES), const),
        ],
        out_shape=[
            jax.ShapeDtypeStruct((tok * ROW_WORDS, LANES), _F32),
            jax.ShapeDtypeStruct((TOP_K, tok), jnp.int32),
            jax.ShapeDtypeStruct((TOP_K, tok), _F32),
            jax.ShapeDtypeStruct((TOP_K, tok), jnp.int32),
            jax.ShapeDtypeStruct((N_EXPERTS, LANES), _F32),
        ],
        scratch_shapes=[pltpu.VMEM((N_EXPERTS, LANES), _F32), pltpu.VMEM((tm, tm), _BF16)],
        compiler_params=_cparams(("arbitrary", "arbitrary")),
        name="router",
    )(x, g, scale, shift, wr_t, br)


def _row_copy(src_ref, src_row, dst_ref, dst_row, sem):
    return pltpu.make_async_copy(
        src_ref.at[pl.ds(pl.multiple_of(src_row * ROW_WORDS, ROW_WORDS), ROW_WORDS), :],
        dst_ref.at[pl.ds(pl.multiple_of(dst_row * ROW_WORDS, ROW_WORDS), ROW_WORDS), :], sem)


def _dispatch_kernel(pad_start_ref, pad_len_ref, tail_ref, dest_hbm, hf_ref, xs_ref,
                     dest_smem0, dest_smem1, zero_ref, isem, sem, zsem, *, tm, nsteps):
    g = pl.program_id(0)
    n = tm * TOP_K
    dest_smem = (dest_smem0, dest_smem1)

    def index_copy(step, slot):
        off = pl.multiple_of(step * n, n)
        return pltpu.make_async_copy(dest_hbm.at[pl.ds(off, n)], dest_smem[slot], isem.at[slot])

    def zero_fill(e, wait):
        start = pad_start_ref[e]
        length = pad_len_ref[e]
        for bit in reversed(range(EXPERT_BLOCK.bit_length() - 1)):
            size = 1 << bit
            row = start + (length & ~(2 * size - 1))
            cp = pltpu.make_async_copy(
                zero_ref.at[pl.ds(0, size * ROW_WORDS), :],
                xs_ref.at[pl.ds(pl.multiple_of(row * ROW_WORDS, ROW_WORDS), size * ROW_WORDS), :],
                zsem)

            @pl.when((length & size) != 0)
            def _():
                if wait:
                    cp.wait()
                else:
                    cp.start()

    @pl.when(g == 0)
    def _():
        index_copy(0, 0).start()
        zero_ref[...] = jnp.zeros(zero_ref.shape, _F32)

        def fill(e, carry):
            zero_fill(e, False)
            return carry

        lax.fori_loop(0, N_EXPERTS, fill, 0)

        def done(e, carry):
            zero_fill(e, True)
            return carry

        lax.fori_loop(0, N_EXPERTS, done, 0)

        def tail_copy(j):
            row = (tail_ref[0] + j) * EXPERT_BLOCK
            return pltpu.make_async_copy(
                zero_ref,
                xs_ref.at[pl.ds(pl.multiple_of(row * ROW_WORDS, ROW_WORDS),
                                EXPERT_BLOCK * ROW_WORDS), :], zsem)

        def tail_fill(j, carry):
            tail_copy(j).start()
            return carry

        lax.fori_loop(0, tail_ref[1], tail_fill, 0)

        def tail_done(j, carry):
            tail_copy(j).wait()
            return carry

        lax.fori_loop(0, tail_ref[1], tail_done, 0)

    for slot in range(2):
        @pl.when(g % 2 == slot)
        def _():
            @pl.when(g + 1 < nsteps)
            def _():
                index_copy(g + 1, 1 - slot).start()

            index_copy(g, slot).wait()
            idx_ref = dest_smem[slot]

            def issue(r, carry):
                for k in range(TOP_K):
                    _row_copy(hf_ref, r, xs_ref, idx_ref[r * TOP_K + k], sem).start(priority=k % 2)
                return carry

            lax.fori_loop(0, tm, issue, 0, unroll=4)

    for k in range(TOP_K):
        pltpu.make_async_copy(hf_ref, xs_ref.at[pl.ds(0, tm * ROW_WORDS), :], sem).wait()


def _dispatch(pad_start, pad_len, tail, dest_flat, hf, n_rows):
    tok = hf.shape[0] // ROW_WORDS
    tm = min(DISPATCH_TILE, tok)
    assert tok % tm == 0 and (tm * TOP_K) % 1024 == 0
    grid_spec = pltpu.PrefetchScalarGridSpec(
        num_scalar_prefetch=3,
        grid=(tok // tm,),
        in_specs=[
            pl.BlockSpec(memory_space=pl.ANY),
            pl.BlockSpec((tm * ROW_WORDS, LANES), lambda i, ps, pn, tl: (i, 0)),
        ],
        out_specs=pl.BlockSpec(memory_space=pl.ANY),
        scratch_shapes=[pltpu.SMEM((tm * TOP_K,), jnp.int32),
                        pltpu.SMEM((tm * TOP_K,), jnp.int32),
                        pltpu.VMEM((EXPERT_BLOCK * ROW_WORDS, LANES), _F32),
                        pltpu.SemaphoreType.DMA((2,)), pltpu.SemaphoreType.DMA,
                        pltpu.SemaphoreType.DMA],
    )
    return pl.pallas_call(
        functools.partial(_dispatch_kernel, tm=tm, nsteps=tok // tm),
        grid_spec=grid_spec,
        out_shape=jax.ShapeDtypeStruct((n_rows * ROW_WORDS, LANES), _F32),
        compiler_params=_cparams(("arbitrary",)),
        name="moe_dispatch",
    )(pad_start, pad_len, tail, dest_flat, hf)


def _expert_kernel(be_ref, nu_ref, xs_ref, wgu_ref, bgu_ref, wdn_ref, bdn_ref, ys_ref, *, blk):
    del be_ref
    i = pl.program_id(0)
    dff = wdn_ref.shape[2]

    @pl.when(i < nu_ref[0])
    def _():
        x = jnp.concatenate([xs_ref[pl.ds(j, blk, stride=ROW_WORDS), :] for j in range(ROW_WORDS)],
                            axis=1).astype(_BF16)
        gu = jnp.dot(x, wgu_ref[0, 0], preferred_element_type=_F32) + bgu_ref[0, 0]
        g = jnp.minimum(gu[:, :dff], SWIGLU_LIMIT)
        u = jnp.clip(gu[:, dff:], -SWIGLU_LIMIT, SWIGLU_LIMIT)
        act = g * jax.nn.sigmoid(SWIGLU_ALPHA * g)
        mid = ((u + 1.0) * act).astype(_BF16)
        y = jnp.dot(mid, wdn_ref[0, 0], preferred_element_type=_F32) + bdn_ref[0, 0]
        for j in range(ROW_WORDS):
            ys_ref[pl.ds(j, blk, stride=ROW_WORDS), :] = y[:, LANES * j:LANES * (j + 1)]

    @pl.when(i >= nu_ref[0])
    def _():
        ys_ref[...] = jnp.zeros(ys_ref.shape, _F32)


def _experts(blk_e, n_used, xs, wgu, bgu, wdn, bdn, layer):
    blk = EXPERT_BLOCK
    n_blk = xs.shape[0] // (blk * ROW_WORDS)
    dff = wdn.shape[2]
    grid_spec = pltpu.PrefetchScalarGridSpec(
        num_scalar_prefetch=2,
        grid=(n_blk,),
        in_specs=[
            pl.BlockSpec((blk * ROW_WORDS, LANES), lambda i, be, nu: (jnp.minimum(i, nu[0] - 1), 0)),
            pl.BlockSpec((1, 1, D_MODEL, 2 * dff), lambda i, be, nu: (layer, be[i], 0, 0)),
            pl.BlockSpec((1, 1, 1, 2 * dff), lambda i, be, nu: (layer, be[i], 0, 0)),
            pl.BlockSpec((1, 1, dff, D_MODEL), lambda i, be, nu: (layer, be[i], 0, 0)),
            pl.BlockSpec((1, 1, 1, D_MODEL), lambda i, be, nu: (layer, be[i], 0, 0)),
        ],
        out_specs=pl.BlockSpec((blk * ROW_WORDS, LANES), lambda i, be, nu: (i, 0)),
    )
    return pl.pallas_call(
        functools.partial(_expert_kernel, blk=blk),
        grid_spec=grid_spec,
        out_shape=jax.ShapeDtypeStruct(xs.shape, _F32),
        compiler_params=_cparams(("arbitrary",)),
        name="moe_experts",
    )(blk_e, n_used, xs, wgu, bgu, wdn, bdn)


def _combine_kernel(dest_hbm, ys_ref, gate_ref, x_ref, gf_ref, o_ref,
                    dest_smem0, dest_smem1, buf_ref, isem, sem, *, tm, ns, nsteps):
    g = pl.program_id(0) * ns + pl.program_id(1)
    n = tm * TOP_K
    dest_smem = (dest_smem0, dest_smem1)

    def index_copy(step, slot):
        off = pl.multiple_of(step * n, n)
        return pltpu.make_async_copy(dest_hbm.at[pl.ds(off, n)], dest_smem[slot], isem.at[slot])

    def start_gathers(step, slot):
        index_copy(step, slot).wait()
        idx_ref = dest_smem[slot]

        def issue(r, carry):
            for k in range(TOP_K):
                _row_copy(ys_ref, idx_ref[r * TOP_K + k], buf_ref.at[slot], k * tm + r,
                          sem.at[slot]).start(priority=k % 2)
            return carry

        lax.fori_loop(0, tm, issue, 0, unroll=4)

        @pl.when(step + 1 < nsteps)
        def _():
            index_copy(step + 1, 1 - slot).start()

    def finish(slot):
        pltpu.make_async_copy(ys_ref.at[pl.ds(0, TOP_K * tm * ROW_WORDS), :], buf_ref.at[slot],
                              sem.at[slot]).wait()
        gate = gate_ref[...]
        cols = []
        for j in range(ROW_WORDS):
            acc = None
            for k in range(TOP_K):
                slab = buf_ref[slot, pl.ds(k * tm * ROW_WORDS + j, tm, stride=ROW_WORDS), :]
                term = gate[:, k:k + 1] * slab
                acc = term if acc is None else acc + term
            cols.append(acc)
        moe = jnp.concatenate(cols, axis=1)
        o_ref[0] = x_ref[0] + gf_ref[0] * moe

    @pl.when(g == 0)
    def _():
        index_copy(0, 0).start()
        start_gathers(0, 0)

    for slot in range(2):
        @pl.when(g % 2 == slot)
        def _():
            @pl.when(g + 1 < nsteps)
            def _():
                start_gathers(g + 1, 1 - slot)

            finish(slot)


def _combine(dest_flat, gate_tok, ys, x, gate_f):
    bsz, seq, _ = x.shape
    tm = min(ROUTE_TILE, seq)
    ns = seq // tm
    return pl.pallas_call(
        functools.partial(_combine_kernel, tm=tm, ns=ns, nsteps=bsz * ns),
        grid=(bsz, ns),
        in_specs=[
            pl.BlockSpec(memory_space=pl.ANY),
            pl.BlockSpec(memory_space=pl.ANY),
            pl.BlockSpec((tm, TOP_K), lambda b, i: (b * ns + i, 0)),
            pl.BlockSpec((1, tm, D_MODEL), lambda b, i: (b, i, 0)),
            pl.BlockSpec((1, 1, D_MODEL), lambda b, i: (b, 0, 0)),
        ],
        out_specs=pl.BlockSpec((1, tm, D_MODEL), lambda b, i: (b, i, 0)),
        out_shape=jax.ShapeDtypeStruct(x.shape, _F32),
        scratch_shapes=[
            pltpu.SMEM((tm * TOP_K,), jnp.int32),
            pltpu.SMEM((tm * TOP_K,), jnp.int32),
            pltpu.VMEM((2, TOP_K * tm * ROW_WORDS, LANES), _F32),
            pltpu.SemaphoreType.DMA((2,)),
            pltpu.SemaphoreType.DMA((2,)),
        ],
        compiler_params=_cparams(("arbitrary", "arbitrary")),
        name="moe_combine",
    )(dest_flat, ys, gate_tok, x, gate_f)


def _moe(x, g, scale, shift, gate_f, wr_t, br, wgu, bgu, wdn, bdn, layer):
    bsz, seq, _ = x.shape
    tok = bsz * seq
    blk = EXPERT_BLOCK
    hf, idx, gate, rank, cnt = _router(x, g, scale, shift, wr_t, br)
    counts = cnt[:, 0].astype(jnp.int32)
    padded = (counts + blk - 1) // blk * blk
    pend = jnp.cumsum(padded)
    pstart = pend - padded
    eids = jnp.arange(N_EXPERTS, dtype=jnp.int32)[:, None, None]
    base = jnp.sum(jnp.where(idx[None] == eids, pstart[:, None, None], 0), axis=0)
    dest = (base + rank).T.reshape(tok * TOP_K)
    gate_tok = gate.T
    n_blk = (tok * TOP_K + N_EXPERTS * (blk - 1) + blk - 1) // blk
    blk_start = jnp.arange(n_blk, dtype=jnp.int32) * blk
    blk_e = jnp.minimum(jnp.sum(blk_start[:, None] >= pend[None, :], axis=1),
                        N_EXPERTS - 1).astype(jnp.int32)
    n_used = (pend[-1:] // blk).astype(jnp.int32)
    tail = jnp.concatenate([n_used, n_blk - n_used])
    xs = _dispatch(pstart + counts, padded - counts, tail, dest, hf, n_blk * blk)
    ys = _experts(blk_e, n_used, xs, wgu, bgu, wdn, bdn, layer)
    return _combine(dest, gate_tok, ys, x, gate_f)


def _prepare(norm_mix_g, norm_ffn_g, w_qkv_a, q_norm_a, k_norm_a, w_o_a, w_qkv_b, q_norm_b, k_norm_b,
             w_o_b, rel_bias, w_router, b_router, w_gate_up, b_gate_up, w_down, b_down):
    nq = A_HEADS * HEAD_DIM
    nk = A_KV_HEADS * HEAD_DIM
    nh = B_HEADS * HEAD_DIM
    scale = HEAD_DIM ** -0.5
    p = {}
    wa = w_qkv_a[0]
    p["wqt_a"] = wa[:, :nq].T.astype(_BF16)
    p["wk_a"] = wa[:, nq:nq + nk].astype(_BF16)
    p["wvt_a"] = wa[:, nq + nk:].T.astype(_BF16)
    p["gq_a"] = (q_norm_a[0] * (scale * LOG2E)).reshape(HEAD_DIM, 1)
    p["gk_a"] = jnp.tile(k_norm_a[0], A_KV_HEADS).reshape(1, nk)
    p["wo_a"] = w_o_a[0].astype(_BF16)
    wb = w_qkv_b[0].astype(_BF16)
    p["w_b"] = wb
    p["gv_b"] = [jnp.concatenate([jnp.tile(q_norm_b[0, gi], B_HEADS) * (scale * LOG2E),
                                  jnp.tile(k_norm_b[0, gi], B_HEADS)]).reshape(1, 2 * nh)
                 for gi in range(len(DILATED_CONFIGS))]
    p["bias_b"] = [_dilated_bias(rel_bias[:, gi * B_HEADS:(gi + 1) * B_HEADS], dil)
                   for gi, (_, dil) in enumerate(DILATED_CONFIGS)]
    p["wo_b"] = w_o_b[0].astype(_BF16)
    p["g_mix"] = [norm_mix_g[i].reshape(1, D_MODEL) for i in range(DEPTH)]
    p["g_ffn"] = [norm_ffn_g[i].reshape(1, D_MODEL) for i in range(DEPTH)]
    p["wr_t"] = [w_router[i].T for i in range(DEPTH)]
    p["br"] = [b_router[i].reshape(N_EXPERTS, 1) for i in range(DEPTH)]
    p["wgu"] = w_gate_up.astype(_BF16)
    p["bgu"] = b_gate_up.reshape(DEPTH, N_EXPERTS, 1, -1)
    p["wdn"] = w_down.astype(_BF16)
    p["bdn"] = b_down.reshape(DEPTH, N_EXPERTS, 1, -1)
    p["bd"] = _block_diag_ones()
    return p


def _trunk(x, c, w_ada, b_ada, p):
    bsz, seq, _ = x.shape
    tables = _rope_tables(seq)
    for i in range(DEPTH):
        mod = _ada_mod(c, w_ada, b_ada, i)
        shift_m, scale_m, gate_m, shift_f, scale_f, gate_f = [
            mod[:, D_MODEL * k:D_MODEL * (k + 1)].reshape(bsz, 1, D_MODEL) for k in range(6)]
        if i % 2 == 0:
            qt, k, vt = _proj_a(x, p["g_mix"][i], scale_m, shift_m, p["wqt_a"], p["wk_a"],
                                p["wvt_a"], p["gq_a"], p["gk_a"], tables, p["bd"])
            a = _flash_a(qt, k, vt)
            x = _oproj(a, p["wo_a"], x, gate_m)
        else:
            outs, lses = [], []
            for gi, (_, dil) in enumerate(DILATED_CONFIGS):
                qkv = _proj_b(x, p["g_mix"][i], scale_m, shift_m, p["w_b"], gi, p["gv_b"][gi],
                              p["bd"], dil)
                o, lse = _dilated_attn(qkv, p["bias_b"][gi])
                outs.append(o)
                lses.append(lse)
            x = _merge_oproj(outs, lses, p["wo_b"], x, gate_m)
        x = _moe(x, p["g_ffn"][i], scale_f, shift_f, gate_f, p["wr_t"][i], p["br"][i],
                 p["wgu"], p["bgu"], p["wdn"], p["bdn"], i)
    return x


def kernel(x_prompt, x_sample, c_prompt, c_sample, norm_mix_g, norm_ffn_g, w_ada, b_ada, w_qkv_a,
           q_norm_a, k_norm_a, w_o_a, w_qkv_b, q_norm_b, k_norm_b, w_o_b, rel_bias, w_router,
           b_router, w_gate_up, b_gate_up, w_down, b_down):
    p = _prepare(norm_mix_g, norm_ffn_g, w_qkv_a, q_norm_a, k_norm_a, w_o_a, w_qkv_b, q_norm_b,
                 k_norm_b, w_o_b, rel_bias, w_router, b_router, w_gate_up, b_gate_up, w_down, b_down)
    y_prompt = _trunk(x_prompt, c_prompt, w_ada, b_ada, p)
    y_sample = _trunk(x_sample, c_sample, w_ada, b_ada, p)
    return (y_prompt, y_sample)
```
